```python
import jax, jax.numpy as jnp
from jax import lax
import numpy as np


D_MODEL = 4096
BATCH = 8
SEQ = 4096
DEPTH = 4

N_MIXERS = 4
MLA_HEADS = 32
QK_NOPE_DIM = 128
QK_ROPE_DIM = 64
V_HEAD_DIM = 128
Q_LORA_RANK = 1024
KV_LORA_RANK = 512
ROPE_THETA = 10000.0
Q_BLOCK = 128
SC_WIDTH = D_MODEL
SC_CONV_WIDTH = 3
GM_WIDTH = D_MODEL
GM_GROUPS = 8
GM_CHUNK = 128
CF_WIDTH = D_MODEL
CF_CONV_WIDTH = 31
NORM_EPS = 1e-6
LN_EPS = 1e-5
N_MLA_LAYERS = len(range(0, DEPTH, N_MIXERS))
N_SC_LAYERS = len(range(1, DEPTH, N_MIXERS))
N_GM_LAYERS = len(range(2, DEPTH, N_MIXERS))
N_CF_LAYERS = len(range(3, DEPTH, N_MIXERS))

kernel_name = "hybrid_interleaved_mla_conv_gmlp_conformer"


def rms_norm(x, g):
    xf = x.astype(jnp.float32)
    y = xf * lax.rsqrt(jnp.mean(xf * xf, axis=-1, keepdims=True) + NORM_EPS)
    return (y * g.astype(jnp.float32)).astype(x.dtype)


def layer_norm(x, g, b):
    xf = x.astype(jnp.float32)
    mu = jnp.mean(xf, axis=-1, keepdims=True)
    var = jnp.mean(jnp.square(xf - mu), axis=-1, keepdims=True)
    y = (xf - mu) * lax.rsqrt(var + LN_EPS) * g.astype(jnp.float32) + b.astype(jnp.float32)
    return y.astype(x.dtype)


def causal_depthwise_conv(x, w):
    width = w.shape[0]
    return lax.conv_general_dilated(
        x, w[:, None, :].astype(x.dtype), window_strides=(1,), padding=[(width - 1, 0)],
        dimension_numbers=("NWC", "WIO", "NWC"), feature_group_count=x.shape[-1])


def apply_rope(t, positions):
    half = QK_ROPE_DIM // 2
    inv_freq = ROPE_THETA ** (-jnp.arange(half, dtype=jnp.float32) / half)
    ang = positions.astype(jnp.float32)[..., None] * inv_freq
    cos, sin = jnp.cos(ang), jnp.sin(ang)
    if t.ndim == 4:
        cos, sin = cos[:, :, None, :], sin[:, :, None, :]
    tf = t.astype(jnp.float32)
    t1, t2 = tf[..., :half], tf[..., half:]
    return jnp.concatenate([t1 * cos - t2 * sin, t2 * cos + t1 * sin], axis=-1).astype(t.dtype)


def causal_block_attention(q, k, v):
    B, S, H, Dqk = q.shape
    n_blocks = S // Q_BLOCK
    scale = Dqk ** -0.5
    q_blocks = q.reshape(B, n_blocks, Q_BLOCK, H, Dqk).transpose(1, 0, 2, 3, 4)
    key_pos = jnp.arange(S)

    def one_block(args):
        q_blk, blk = args
        s = jnp.einsum("bqhd,bkhd->bhqk", q_blk, k, preferred_element_type=jnp.float32) * scale
        q_pos = blk * Q_BLOCK + jnp.arange(Q_BLOCK)
        s = jnp.where(key_pos[None, :] <= q_pos[:, None], s, -1e30)
        p = jax.nn.softmax(s, axis=-1)
        return jnp.einsum("bhqk,bkhd->bqhd", p.astype(v.dtype), v)

    o = lax.map(one_block, (q_blocks, jnp.arange(n_blocks)))
    return o.transpose(1, 0, 2, 3, 4).reshape(B, S, H * v.shape[-1])


def mla_mixer(h, positions, w_in, q_norm, w_uq, kv_norm, w_ukv, w_out):
    B, S, _ = h.shape
    c1 = Q_LORA_RANK
    c2 = c1 + KV_LORA_RANK
    c3 = c2 + QK_ROPE_DIM
    cq, ckv, k_rope, z = jnp.split(h @ w_in, [c1, c2, c3], axis=-1)
    q = (rms_norm(cq, q_norm) @ w_uq).reshape(B, S, MLA_HEADS, QK_NOPE_DIM + QK_ROPE_DIM)
    kv = (rms_norm(ckv, kv_norm) @ w_ukv).reshape(B, S, MLA_HEADS, QK_NOPE_DIM + V_HEAD_DIM)
    q_nope, q_rope = q[..., :QK_NOPE_DIM], q[..., QK_NOPE_DIM:]
    k_nope, v = kv[..., :QK_NOPE_DIM], kv[..., QK_NOPE_DIM:]
    q = jnp.concatenate([q_nope, apply_rope(q_rope, positions)], axis=-1)
    k_rope = apply_rope(k_rope, positions)
    k = jnp.concatenate(
        [k_nope, jnp.broadcast_to(k_rope[:, :, None, :], (B, S, MLA_HEADS, QK_ROPE_DIM))], axis=-1)
    o = causal_block_attention(q, k, v)
    return (o * jax.nn.silu(z)) @ w_out


def short_conv_mixer(h, w_in, w_conv, w_out):
    b_gate, c_gate, u, z = jnp.split(h @ w_in, 4, axis=-1)
    y = b_gate * causal_depthwise_conv(c_gate * u, w_conv)
    return (y * jax.nn.silu(z)) @ w_out


def gmlp_mixer(h, w_in, ln_g, ln_b, w_s, b_s, w_out):
    u, v, z = jnp.split(h @ w_in, 3, axis=-1)
    u = jax.nn.gelu(u, approximate=False)
    v = layer_norm(jax.nn.gelu(v, approximate=False), ln_g, ln_b)
    B, S, E = v.shape
    vc = v.reshape(B, S // GM_CHUNK, GM_CHUNK, GM_GROUPS, E // GM_GROUPS)
    causal = jnp.tril(jnp.ones((GM_CHUNK, GM_CHUNK), dtype=w_s.dtype))
    w_mix = (w_s * causal).astype(v.dtype)
    s = jnp.einsum("gts,bcsgd->bctgd", w_mix, vc) + b_s.T.astype(v.dtype)[None, None, :, :, None]
    y = u * s.reshape(B, S, E)
    return (y * jax.nn.silu(z)) @ w_out


def conformer_conv_mixer(h, w_in, w_dw, b_dw, ln_g, ln_b, w_out):
    a, g, z = jnp.split(h @ w_in, 3, axis=-1)
    y = a * jax.nn.sigmoid(g)
    y = causal_depthwise_conv(y, w_dw) + b_dw
    y = jax.nn.silu(layer_norm(y, ln_g, ln_b))
    return (y * jax.nn.silu(z)) @ w_out


def _fwd_setup_inputs(seed: int = 0) -> dict:
    key = jax.random.key(seed)
    ks = jax.random.split(key, 32)

    def dense(k, shape, fan_in):
        return jax.random.normal(k, shape, jnp.float32) * (fan_in ** -0.5)

    def gain(k, shape):
        return 1.0 + 0.05 * jax.random.normal(k, shape, jnp.float32)

    def small(k, shape):
        return 0.02 * jax.random.normal(k, shape, jnp.float32)

    x = jax.random.normal(ks[0], (BATCH, SEQ, D_MODEL), jnp.float32)
    positions = (jax.random.randint(ks[1], (BATCH, 1), 0, 1024)
                 + jnp.arange(SEQ)[None, :]).astype(jnp.int32)
    norm_pre = gain(ks[2], (DEPTH, D_MODEL))
    norm_post = gain(ks[3], (DEPTH, D_MODEL))

    n = N_MLA_LAYERS
    mla_in = Q_LORA_RANK + KV_LORA_RANK + QK_ROPE_DIM + MLA_HEADS * V_HEAD_DIM
    w_in_mla = dense(ks[4], (n, D_MODEL, mla_in), D_MODEL)
    mla_q_norm = gain(ks[5], (n, Q_LORA_RANK))
    w_uq = dense(ks[6], (n, Q_LORA_RANK, MLA_HEADS * (QK_NOPE_DIM + QK_ROPE_DIM)), Q_LORA_RANK)
    mla_kv_norm = gain(ks[7], (n, KV_LORA_RANK))
    w_ukv = dense(ks[8], (n, KV_LORA_RANK, MLA_HEADS * (QK_NOPE_DIM + V_HEAD_DIM)), KV_LORA_RANK)
    w_out_mla = dense(ks[9], (n, MLA_HEADS * V_HEAD_DIM, D_MODEL), MLA_HEADS * V_HEAD_DIM)

    n = N_SC_LAYERS
    w_in_sc = dense(ks[10], (n, D_MODEL, 4 * SC_WIDTH), D_MODEL)
    sc_conv = dense(ks[11], (n, SC_CONV_WIDTH, SC_WIDTH), SC_CONV_WIDTH)
    w_out_sc = dense(ks[12], (n, SC_WIDTH, D_MODEL), SC_WIDTH)

    n = N_GM_LAYERS
    w_in_gm = dense(ks[13], (n, D_MODEL, 3 * GM_WIDTH), D_MODEL)
    gm_ln_g = gain(ks[14], (n, GM_WIDTH))
    gm_ln_b = small(ks[15], (n, GM_WIDTH))
    gm_w_s = dense(ks[16], (n, GM_GROUPS, GM_CHUNK, GM_CHUNK), GM_CHUNK)
    gm_b_s = gain(ks[17], (n, GM_GROUPS, GM_CHUNK))
    w_out_gm = dense(ks[18], (n, GM_WIDTH, D_MODEL), GM_WIDTH)

    n = N_CF_LAYERS
    w_in_cf = dense(ks[19], (n, D_MODEL, 3 * CF_WIDTH), D_MODEL)
    cf_dw = dense(ks[20], (n, CF_CONV_WIDTH, CF_WIDTH), CF_CONV_WIDTH)
    cf_dw_b = small(ks[21], (n, CF_WIDTH))
    cf_ln_g = gain(ks[22], (n, CF_WIDTH))
    cf_ln_b = small(ks[23], (n, CF_WIDTH))
    w_out_cf = dense(ks[24], (n, CF_WIDTH, D_MODEL), CF_WIDTH)

    return {
        "x": x, "positions": positions, "norm_pre": norm_pre, "norm_post": norm_post,
        "w_in_mla": w_in_mla, "mla_q_norm": mla_q_norm, "w_uq": w_uq,
        "mla_kv_norm": mla_kv_norm, "w_ukv": w_ukv, "w_out_mla": w_out_mla,
        "w_in_sc": w_in_sc, "sc_conv": sc_conv, "w_out_sc": w_out_sc,
        "w_in_gm": w_in_gm, "gm_ln_g": gm_ln_g, "gm_ln_b": gm_ln_b,
        "gm_w_s": gm_w_s, "gm_b_s": gm_b_s, "w_out_gm": w_out_gm,
        "w_in_cf": w_in_cf, "cf_dw": cf_dw, "cf_dw_b": cf_dw_b,
        "cf_ln_g": cf_ln_g, "cf_ln_b": cf_ln_b, "w_out_cf": w_out_cf,
    }


def _fwd_reference(x, positions, norm_pre, norm_post,
              w_in_mla, mla_q_norm, w_uq, mla_kv_norm, w_ukv, w_out_mla,
              w_in_sc, sc_conv, w_out_sc,
              w_in_gm, gm_ln_g, gm_ln_b, gm_w_s, gm_b_s, w_out_gm,
              w_in_cf, cf_dw, cf_dw_b, cf_ln_g, cf_ln_b, w_out_cf):
    for i in range(DEPTH):
        m, j = i % N_MIXERS, i // N_MIXERS
        h = rms_norm(x, norm_pre[i])
        if m == 0:
            y = mla_mixer(h, positions, w_in_mla[j], mla_q_norm[j], w_uq[j],
                          mla_kv_norm[j], w_ukv[j], w_out_mla[j])
        elif m == 1:
            y = short_conv_mixer(h, w_in_sc[j], sc_conv[j], w_out_sc[j])
        elif m == 2:
            y = gmlp_mixer(h, w_in_gm[j], gm_ln_g[j], gm_ln_b[j], gm_w_s[j], gm_b_s[j], w_out_gm[j])
        else:
            y = conformer_conv_mixer(h, w_in_cf[j], cf_dw[j], cf_dw_b[j], cf_ln_g[j], cf_ln_b[j],
                                     w_out_cf[j])
        x = x + rms_norm(y, norm_post[i])
    return x


import jax as _jax
import jax.numpy as _jnp

TWIN_FORMAT = 'train_step'
FWD_PARAMS = ['x', 'positions', 'norm_pre', 'norm_post', 'w_in_mla', 'mla_q_norm', 'w_uq', 'mla_kv_norm', 'w_ukv', 'w_out_mla', 'w_in_sc', 'sc_conv', 'w_out_sc', 'w_in_gm', 'gm_ln_g', 'gm_ln_b', 'gm_w_s', 'gm_b_s', 'w_out_gm', 'w_in_cf', 'cf_dw', 'cf_dw_b', 'cf_ln_g', 'cf_ln_b', 'w_out_cf']
TWIN_WEIGHTS = ['norm_pre', 'norm_post', 'w_in_mla', 'mla_q_norm', 'w_uq', 'mla_kv_norm', 'w_ukv', 'w_out_mla', 'w_in_sc', 'sc_conv', 'w_out_sc', 'w_in_gm', 'gm_ln_g', 'gm_ln_b', 'gm_w_s', 'gm_b_s', 'w_out_gm', 'w_in_cf', 'cf_dw', 'cf_dw_b', 'cf_ln_g', 'cf_ln_b', 'w_out_cf']
TWIN_DIFF_INPUT = 'x'
TWIN_INPUTS = ['x', 'positions', 'norm_pre', 'norm_post', 'w_in_mla', 'mla_q_norm', 'w_uq', 'mla_kv_norm', 'w_ukv', 'w_out_mla', 'w_in_sc', 'sc_conv', 'w_out_sc', 'w_in_gm', 'gm_ln_g', 'gm_ln_b', 'gm_w_s', 'gm_b_s', 'w_out_gm', 'w_in_cf', 'cf_dw', 'cf_dw_b', 'cf_ln_g', 'cf_ln_b', 'w_out_cf', 'loss_target', 'm_norm_pre', 'm_norm_post', 'm_w_in_mla', 'm_mla_q_norm', 'm_w_uq', 'm_mla_kv_norm', 'm_w_ukv', 'm_w_out_mla', 'm_w_in_sc', 'm_sc_conv', 'm_w_out_sc', 'm_w_in_gm', 'm_gm_ln_g', 'm_gm_ln_b', 'm_gm_w_s', 'm_gm_b_s', 'm_w_out_gm', 'm_w_in_cf', 'm_cf_dw', 'm_cf_dw_b', 'm_cf_ln_g', 'm_cf_ln_b', 'm_w_out_cf', 'v_norm_pre', 'v_norm_post', 'v_w_in_mla', 'v_mla_q_norm', 'v_w_uq', 'v_mla_kv_norm', 'v_w_ukv', 'v_w_out_mla', 'v_w_in_sc', 'v_sc_conv', 'v_w_out_sc', 'v_w_in_gm', 'v_gm_ln_g', 'v_gm_ln_b', 'v_gm_w_s', 'v_gm_b_s', 'v_w_out_gm', 'v_w_in_cf', 'v_cf_dw', 'v_cf_dw_b', 'v_cf_ln_g', 'v_cf_ln_b', 'v_w_out_cf']
TWIN_OUTPUTS = ['loss', 'grad_x', 'grad_norm_pre', 'grad_norm_post', 'grad_w_in_mla', 'grad_mla_q_norm', 'grad_w_uq', 'grad_mla_kv_norm', 'grad_w_ukv', 'grad_w_out_mla', 'grad_w_in_sc', 'grad_sc_conv', 'grad_w_out_sc', 'grad_w_in_gm', 'grad_gm_ln_g', 'grad_gm_ln_b', 'grad_gm_w_s', 'grad_gm_b_s', 'grad_w_out_gm', 'grad_w_in_cf', 'grad_cf_dw', 'grad_cf_dw_b', 'grad_cf_ln_g', 'grad_cf_ln_b', 'grad_w_out_cf', 'delta_norm_pre', 'delta_norm_post', 'delta_w_in_mla', 'delta_mla_q_norm', 'delta_w_uq', 'delta_mla_kv_norm', 'delta_w_ukv', 'delta_w_out_mla', 'delta_w_in_sc', 'delta_sc_conv', 'delta_w_out_sc', 'delta_w_in_gm', 'delta_gm_ln_g', 'delta_gm_ln_b', 'delta_gm_w_s', 'delta_gm_b_s', 'delta_w_out_gm', 'delta_w_in_cf', 'delta_cf_dw', 'delta_cf_dw_b', 'delta_cf_ln_g', 'delta_cf_ln_b', 'delta_w_out_cf', 'new_m_norm_pre', 'new_m_norm_post', 'new_m_w_in_mla', 'new_m_mla_q_norm', 'new_m_w_uq', 'new_m_mla_kv_norm', 'new_m_w_ukv', 'new_m_w_out_mla', 'new_m_w_in_sc', 'new_m_sc_conv', 'new_m_w_out_sc', 'new_m_w_in_gm', 'new_m_gm_ln_g', 'new_m_gm_ln_b', 'new_m_gm_w_s', 'new_m_gm_b_s', 'new_m_w_out_gm', 'new_m_w_in_cf', 'new_m_cf_dw', 'new_m_cf_dw_b', 'new_m_cf_ln_g', 'new_m_cf_ln_b', 'new_m_w_out_cf', 'new_v_norm_pre', 'new_v_norm_post', 'new_v_w_in_mla', 'new_v_mla_q_norm', 'new_v_w_uq', 'new_v_mla_kv_norm', 'new_v_w_ukv', 'new_v_w_out_mla', 'new_v_w_in_sc', 'new_v_sc_conv', 'new_v_w_out_sc', 'new_v_w_in_gm', 'new_v_gm_ln_g', 'new_v_gm_ln_b', 'new_v_gm_w_s', 'new_v_gm_b_s', 'new_v_w_out_gm', 'new_v_w_in_cf', 'new_v_cf_dw', 'new_v_cf_dw_b', 'new_v_cf_ln_g', 'new_v_cf_ln_b', 'new_v_w_out_cf']
TWIN_LEAF_KINDS = {'loss': 'loss', 'grad_x': 'grad_x', 'grad_norm_pre': 'grad_w', 'grad_norm_post': 'grad_w', 'grad_w_in_mla': 'grad_w', 'grad_mla_q_norm': 'grad_w', 'grad_w_uq': 'grad_w', 'grad_mla_kv_norm': 'grad_w', 'grad_w_ukv': 'grad_w', 'grad_w_out_mla': 'grad_w', 'grad_w_in_sc': 'grad_w', 'grad_sc_conv': 'grad_w', 'grad_w_out_sc': 'grad_w', 'grad_w_in_gm': 'grad_w', 'grad_gm_ln_g': 'grad_w', 'grad_gm_ln_b': 'grad_w', 'grad_gm_w_s': 'grad_w', 'grad_gm_b_s': 'grad_w', 'grad_w_out_gm': 'grad_w', 'grad_w_in_cf': 'grad_w', 'grad_cf_dw': 'grad_w', 'grad_cf_dw_b': 'grad_w', 'grad_cf_ln_g': 'grad_w', 'grad_cf_ln_b': 'grad_w', 'grad_w_out_cf': 'grad_w', 'delta_norm_pre': 'delta_w', 'delta_norm_post': 'delta_w', 'delta_w_in_mla': 'delta_w', 'delta_mla_q_norm': 'delta_w', 'delta_w_uq': 'delta_w', 'delta_mla_kv_norm': 'delta_w', 'delta_w_ukv': 'delta_w', 'delta_w_out_mla': 'delta_w', 'delta_w_in_sc': 'delta_w', 'delta_sc_conv': 'delta_w', 'delta_w_out_sc': 'delta_w', 'delta_w_in_gm': 'delta_w', 'delta_gm_ln_g': 'delta_w', 'delta_gm_ln_b': 'delta_w', 'delta_gm_w_s': 'delta_w', 'delta_gm_b_s': 'delta_w', 'delta_w_out_gm': 'delta_w', 'delta_w_in_cf': 'delta_w', 'delta_cf_dw': 'delta_w', 'delta_cf_dw_b': 'delta_w', 'delta_cf_ln_g': 'delta_w', 'delta_cf_ln_b': 'delta_w', 'delta_w_out_cf': 'delta_w', 'new_m_norm_pre': 'new_m', 'new_m_norm_post': 'new_m', 'new_m_w_in_mla': 'new_m', 'new_m_mla_q_norm': 'new_m', 'new_m_w_uq': 'new_m', 'new_m_mla_kv_norm': 'new_m', 'new_m_w_ukv': 'new_m', 'new_m_w_out_mla': 'new_m', 'new_m_w_in_sc': 'new_m', 'new_m_sc_conv': 'new_m', 'new_m_w_out_sc': 'new_m', 'new_m_w_in_gm': 'new_m', 'new_m_gm_ln_g': 'new_m', 'new_m_gm_ln_b': 'new_m', 'new_m_gm_w_s': 'new_m', 'new_m_gm_b_s': 'new_m', 'new_m_w_out_gm': 'new_m', 'new_m_w_in_cf': 'new_m', 'new_m_cf_dw': 'new_m', 'new_m_cf_dw_b': 'new_m', 'new_m_cf_ln_g': 'new_m', 'new_m_cf_ln_b': 'new_m', 'new_m_w_out_cf': 'new_m', 'new_v_norm_pre': 'new_v', 'new_v_norm_post': 'new_v', 'new_v_w_in_mla': 'new_v', 'new_v_mla_q_norm': 'new_v', 'new_v_w_uq': 'new_v', 'new_v_mla_kv_norm': 'new_v', 'new_v_w_ukv': 'new_v', 'new_v_w_out_mla': 'new_v', 'new_v_w_in_sc': 'new_v', 'new_v_sc_conv': 'new_v', 'new_v_w_out_sc': 'new_v', 'new_v_w_in_gm': 'new_v', 'new_v_gm_ln_g': 'new_v', 'new_v_gm_ln_b': 'new_v', 'new_v_gm_w_s': 'new_v', 'new_v_gm_b_s': 'new_v', 'new_v_w_out_gm': 'new_v', 'new_v_w_in_cf': 'new_v', 'new_v_cf_dw': 'new_v', 'new_v_cf_dw_b': 'new_v', 'new_v_cf_ln_g': 'new_v', 'new_v_cf_ln_b': 'new_v', 'new_v_w_out_cf': 'new_v'}


def _forward(args):
    return _fwd_reference(*[args[k] for k in FWD_PARAMS])


def _output_shape():
    out = _jax.eval_shape(lambda: _forward(_fwd_setup_inputs(0)))
    return out.shape, out.dtype

N_MICROBATCH = 1
ADAM_LR = 0.001
ADAM_B1 = 0.9
ADAM_B2 = 0.999
ADAM_EPS = 1e-08
ADAM_WD = 0.01
ADAM_STEP = 10
PER_EXAMPLE_BATCH_AXIS = {'x': 0, 'positions': 0, 'loss_target': 0}
SHARED_INPUTS = []
_WEIGHT_DTYPES = {'norm_pre': _jnp.float32, 'norm_post': _jnp.float32, 'w_in_mla': _jnp.float32, 'mla_q_norm': _jnp.float32, 'w_uq': _jnp.float32, 'mla_kv_norm': _jnp.float32, 'w_ukv': _jnp.float32, 'w_out_mla': _jnp.float32, 'w_in_sc': _jnp.float32, 'sc_conv': _jnp.float32, 'w_out_sc': _jnp.float32, 'w_in_gm': _jnp.float32, 'gm_ln_g': _jnp.float32, 'gm_ln_b': _jnp.float32, 'gm_w_s': _jnp.float32, 'gm_b_s': _jnp.float32, 'w_out_gm': _jnp.float32, 'w_in_cf': _jnp.float32, 'cf_dw': _jnp.float32, 'cf_dw_b': _jnp.float32, 'cf_ln_g': _jnp.float32, 'cf_ln_b': _jnp.float32, 'w_out_cf': _jnp.float32}
MOMENT_SCALE = {'norm_pre': 3.598904e-01, 'norm_post': 7.962041e+00, 'w_in_mla': 4.844737e-01, 'mla_q_norm': 6.257422e-01, 'w_uq': 2.514989e-01, 'mla_kv_norm': 1.258401e+00, 'w_ukv': 2.893543e-01, 'w_out_mla': 3.040862e-01, 'w_in_sc': 1.718669e-01, 'sc_conv': 1.708158e-01, 'w_out_sc': 1.721962e-01, 'w_in_gm': 1.187544e-01, 'gm_ln_g': 6.828203e-02, 'gm_ln_b': 6.995378e-02, 'gm_w_s': 1.382698e-01, 'gm_b_s': 2.126993e-01, 'w_out_gm': 1.600213e-01, 'w_in_cf': 9.123503e-02, 'cf_dw': 1.062071e-01, 'cf_dw_b': 4.616913e-01, 'cf_ln_g': 2.013044e-01, 'cf_ln_b': 2.835096e-01, 'w_out_cf': 1.387866e-01}


def _to_microbatches(a, axis):
    t = _jnp.moveaxis(a, axis, 0)
    t = t.reshape((N_MICROBATCH, t.shape[0] // N_MICROBATCH) + t.shape[1:])
    return _jnp.moveaxis(t, 1, axis + 1)


def setup_inputs(seed: int = 0) -> dict:
    inp = _fwd_setup_inputs(seed)
    key = _jax.random.fold_in(_jax.random.key(seed), 7919)
    shape, _ = _output_shape()
    out = dict(inp)
    out["loss_target"] = _jax.random.normal(_jax.random.fold_in(key, 0), shape, _jnp.float32)
    for i, name in enumerate(TWIN_WEIGHTS):
        w = inp[name].astype(_jnp.float32)
        if MOMENT_SCALE is None:
            s = _jnp.sqrt(_jnp.mean(_jnp.square(w)) + 1e-30)
        else:
            s = MOMENT_SCALE[name]
        km, kv = _jax.random.split(_jax.random.fold_in(key, i + 1))
        out[name] = w
        out["m_" + name] = s * _jax.random.normal(km, w.shape, _jnp.float32)
        out["v_" + name] = (s * s) * _jax.random.uniform(kv, w.shape, _jnp.float32, 0.5, 1.5)
    if N_MICROBATCH > 1:
        for name, axis in PER_EXAMPLE_BATCH_AXIS.items():
            out[name] = _to_microbatches(out[name], axis)
    return {'x': out['x'], 'positions': out['positions'], 'norm_pre': out['norm_pre'], 'norm_post': out['norm_post'], 'w_in_mla': out['w_in_mla'], 'mla_q_norm': out['mla_q_norm'], 'w_uq': out['w_uq'], 'mla_kv_norm': out['mla_kv_norm'], 'w_ukv': out['w_ukv'], 'w_out_mla': out['w_out_mla'], 'w_in_sc': out['w_in_sc'], 'sc_conv': out['sc_conv'], 'w_out_sc': out['w_out_sc'], 'w_in_gm': out['w_in_gm'], 'gm_ln_g': out['gm_ln_g'], 'gm_ln_b': out['gm_ln_b'], 'gm_w_s': out['gm_w_s'], 'gm_b_s': out['gm_b_s'], 'w_out_gm': out['w_out_gm'], 'w_in_cf': out['w_in_cf'], 'cf_dw': out['cf_dw'], 'cf_dw_b': out['cf_dw_b'], 'cf_ln_g': out['cf_ln_g'], 'cf_ln_b': out['cf_ln_b'], 'w_out_cf': out['w_out_cf'], 'loss_target': out['loss_target'], 'm_norm_pre': out['m_norm_pre'], 'm_norm_post': out['m_norm_post'], 'm_w_in_mla': out['m_w_in_mla'], 'm_mla_q_norm': out['m_mla_q_norm'], 'm_w_uq': out['m_w_uq'], 'm_mla_kv_norm': out['m_mla_kv_norm'], 'm_w_ukv': out['m_w_ukv'], 'm_w_out_mla': out['m_w_out_mla'], 'm_w_in_sc': out['m_w_in_sc'], 'm_sc_conv': out['m_sc_conv'], 'm_w_out_sc': out['m_w_out_sc'], 'm_w_in_gm': out['m_w_in_gm'], 'm_gm_ln_g': out['m_gm_ln_g'], 'm_gm_ln_b': out['m_gm_ln_b'], 'm_gm_w_s': out['m_gm_w_s'], 'm_gm_b_s': out['m_gm_b_s'], 'm_w_out_gm': out['m_w_out_gm'], 'm_w_in_cf': out['m_w_in_cf'], 'm_cf_dw': out['m_cf_dw'], 'm_cf_dw_b': out['m_cf_dw_b'], 'm_cf_ln_g': out['m_cf_ln_g'], 'm_cf_ln_b': out['m_cf_ln_b'], 'm_w_out_cf': out['m_w_out_cf'], 'v_norm_pre': out['v_norm_pre'], 'v_norm_post': out['v_norm_post'], 'v_w_in_mla': out['v_w_in_mla'], 'v_mla_q_norm': out['v_mla_q_norm'], 'v_w_uq': out['v_w_uq'], 'v_mla_kv_norm': out['v_mla_kv_norm'], 'v_w_ukv': out['v_w_ukv'], 'v_w_out_mla': out['v_w_out_mla'], 'v_w_in_sc': out['v_w_in_sc'], 'v_sc_conv': out['v_sc_conv'], 'v_w_out_sc': out['v_w_out_sc'], 'v_w_in_gm': out['v_w_in_gm'], 'v_gm_ln_g': out['v_gm_ln_g'], 'v_gm_ln_b': out['v_gm_ln_b'], 'v_gm_w_s': out['v_gm_w_s'], 'v_gm_b_s': out['v_gm_b_s'], 'v_w_out_gm': out['v_w_out_gm'], 'v_w_in_cf': out['v_w_in_cf'], 'v_cf_dw': out['v_cf_dw'], 'v_cf_dw_b': out['v_cf_dw_b'], 'v_cf_ln_g': out['v_cf_ln_g'], 'v_cf_ln_b': out['v_cf_ln_b'], 'v_w_out_cf': out['v_w_out_cf']}


def _loss(weights, diff, rest, loss_target):
    with _jax.named_scope("forward"):
        args = {**rest, TWIN_DIFF_INPUT: diff, **{k: w.astype(_WEIGHT_DTYPES[k]) for k, w in weights.items()}}
        y = _forward(args)
    with _jax.named_scope("loss_head"):
        err = _jnp.square(y.astype(_jnp.float32) - loss_target)
        return 0.5 * _jnp.sum(_jnp.mean(err, axis=-1)) if err.ndim else 0.5 * err


def _adamw(w, g, m, v):
    m = ADAM_B1 * m + (1.0 - ADAM_B1) * g
    v = ADAM_B2 * v + (1.0 - ADAM_B2) * _jnp.square(g)
    m_hat = m / (1.0 - ADAM_B1 ** ADAM_STEP)
    v_hat = v / (1.0 - ADAM_B2 ** ADAM_STEP)
    delta = -ADAM_LR * (m_hat / (_jnp.sqrt(v_hat) + ADAM_EPS) + ADAM_WD * w)
    return delta, m, v


def reference(x, positions, norm_pre, norm_post, w_in_mla, mla_q_norm, w_uq, mla_kv_norm, w_ukv, w_out_mla, w_in_sc, sc_conv, w_out_sc, w_in_gm, gm_ln_g, gm_ln_b, gm_w_s, gm_b_s, w_out_gm, w_in_cf, cf_dw, cf_dw_b, cf_ln_g, cf_ln_b, w_out_cf, loss_target, m_norm_pre, m_norm_post, m_w_in_mla, m_mla_q_norm, m_w_uq, m_mla_kv_norm, m_w_ukv, m_w_out_mla, m_w_in_sc, m_sc_conv, m_w_out_sc, m_w_in_gm, m_gm_ln_g, m_gm_ln_b, m_gm_w_s, m_gm_b_s, m_w_out_gm, m_w_in_cf, m_cf_dw, m_cf_dw_b, m_cf_ln_g, m_cf_ln_b, m_w_out_cf, v_norm_pre, v_norm_post, v_w_in_mla, v_mla_q_norm, v_w_uq, v_mla_kv_norm, v_w_ukv, v_w_out_mla, v_w_in_sc, v_sc_conv, v_w_out_sc, v_w_in_gm, v_gm_ln_g, v_gm_ln_b, v_gm_w_s, v_gm_b_s, v_w_out_gm, v_w_in_cf, v_cf_dw, v_cf_dw_b, v_cf_ln_g, v_cf_ln_b, v_w_out_cf):
    given = dict(x=x, positions=positions, norm_pre=norm_pre, norm_post=norm_post, w_in_mla=w_in_mla, mla_q_norm=mla_q_norm, w_uq=w_uq, mla_kv_norm=mla_kv_norm, w_ukv=w_ukv, w_out_mla=w_out_mla, w_in_sc=w_in_sc, sc_conv=sc_conv, w_out_sc=w_out_sc, w_in_gm=w_in_gm, gm_ln_g=gm_ln_g, gm_ln_b=gm_ln_b, gm_w_s=gm_w_s, gm_b_s=gm_b_s, w_out_gm=w_out_gm, w_in_cf=w_in_cf, cf_dw=cf_dw, cf_dw_b=cf_dw_b, cf_ln_g=cf_ln_g, cf_ln_b=cf_ln_b, w_out_cf=w_out_cf, loss_target=loss_target, m_norm_pre=m_norm_pre, m_norm_post=m_norm_post, m_w_in_mla=m_w_in_mla, m_mla_q_norm=m_mla_q_norm, m_w_uq=m_w_uq, m_mla_kv_norm=m_mla_kv_norm, m_w_ukv=m_w_ukv, m_w_out_mla=m_w_out_mla, m_w_in_sc=m_w_in_sc, m_sc_conv=m_sc_conv, m_w_out_sc=m_w_out_sc, m_w_in_gm=m_w_in_gm, m_gm_ln_g=m_gm_ln_g, m_gm_ln_b=m_gm_ln_b, m_gm_w_s=m_gm_w_s, m_gm_b_s=m_gm_b_s, m_w_out_gm=m_w_out_gm, m_w_in_cf=m_w_in_cf, m_cf_dw=m_cf_dw, m_cf_dw_b=m_cf_dw_b, m_cf_ln_g=m_cf_ln_g, m_cf_ln_b=m_cf_ln_b, m_w_out_cf=m_w_out_cf, v_norm_pre=v_norm_pre, v_norm_post=v_norm_post, v_w_in_mla=v_w_in_mla, v_mla_q_norm=v_mla_q_norm, v_w_uq=v_w_uq, v_mla_kv_norm=v_mla_kv_norm, v_w_ukv=v_w_ukv, v_w_out_mla=v_w_out_mla, v_w_in_sc=v_w_in_sc, v_sc_conv=v_sc_conv, v_w_out_sc=v_w_out_sc, v_w_in_gm=v_w_in_gm, v_gm_ln_g=v_gm_ln_g, v_gm_ln_b=v_gm_ln_b, v_gm_w_s=v_gm_w_s, v_gm_b_s=v_gm_b_s, v_w_out_gm=v_w_out_gm, v_w_in_cf=v_w_in_cf, v_cf_dw=v_cf_dw, v_cf_dw_b=v_cf_dw_b, v_cf_ln_g=v_cf_ln_g, v_cf_ln_b=v_cf_ln_b, v_w_out_cf=v_w_out_cf)
    weights = {n: given[n] for n in TWIN_WEIGHTS}
    shared = {n: given[n] for n in SHARED_INPUTS}
    per_example = {n: given[n] for n in ['x', 'positions']}
    grad_fn = _jax.value_and_grad(_loss, argnums=(0, 1))

    def one_microbatch(ex, loss_target):
        ex = dict(ex)
        diff = ex.pop(TWIN_DIFF_INPUT)
        return grad_fn(weights, diff, {**shared, **ex}, loss_target)

    if N_MICROBATCH == 1:
        loss, (grad_w, grad_x) = one_microbatch(per_example, given["loss_target"])
    else:
        def body(carry, xs):
            loss_sum, grad_sum = carry
            l_k, (gw_k, gx_k) = one_microbatch(xs[0], xs[1])
            with _jax.named_scope("update"):
                return (loss_sum + l_k, _jax.tree.map(_jnp.add, grad_sum, gw_k)), gx_k

        init = (_jnp.zeros((), _jnp.float32), _jax.tree.map(_jnp.zeros_like, weights))
        (loss, grad_w), grad_x = _jax.lax.scan(body, init, (per_example, given["loss_target"]))
    with _jax.named_scope("update"):
        delta_w, new_m, new_v = {}, {}, {}
        for n in TWIN_WEIGHTS:
            delta_w[n], new_m[n], new_v[n] = _adamw(weights[n], grad_w[n], given["m_" + n], given["v_" + n])
    return (loss, grad_x, *[grad_w[n] for n in TWIN_WEIGHTS], *[delta_w[n] for n in TWIN_WEIGHTS],
            *[new_m[n] for n in TWIN_WEIGHTS], *[new_v[n] for n in TWIN_WEIGHTS])
```

```python
import functools
import math

import jax
import jax.numpy as jnp
from jax import lax
from jax.experimental import pallas as pl
from jax.experimental.pallas import tpu as pltpu

F32, BF16 = jnp.float32, jnp.bfloat16
S = jax.ShapeDtypeStruct
MESH_ID = pl.DeviceIdType.MESH

V7X_VMEM_BYTES = 64 * 1024 * 1024
VMEM_LIMIT = V7X_VMEM_BYTES - 8 * 1024 * 1024
LANES = 128
N_CHIPS = 4
N_DEV = 8

NORM_EPS = 1e-6
LN_EPS = 1e-5
ROPE_THETA = 10000.0
ROPE_DIM = 64
NOPE_DIM = 128
V_DIM = 128
HEAD_PAD = 256
GM_CHUNK = 128
GM_GROUPS = 8
NEG = -1e30

ADAM_LR, ADAM_B1, ADAM_B2, ADAM_EPS, ADAM_WD, ADAM_STEP = 0.001, 0.9, 0.999, 1e-08, 0.01, 10

FWD_PARAMS = ['x', 'positions', 'norm_pre', 'norm_post', 'w_in_mla', 'mla_q_norm', 'w_uq', 'mla_kv_norm', 'w_ukv',
              'w_out_mla', 'w_in_sc', 'sc_conv', 'w_out_sc', 'w_in_gm', 'gm_ln_g', 'gm_ln_b', 'gm_w_s', 'gm_b_s',
              'w_out_gm', 'w_in_cf', 'cf_dw', 'cf_dw_b', 'cf_ln_g', 'cf_ln_b', 'w_out_cf']
WEIGHTS = FWD_PARAMS[2:]
BIG = ['w_in_mla', 'w_uq', 'w_ukv', 'w_out_mla', 'w_in_sc', 'w_out_sc', 'w_in_gm', 'w_out_gm', 'w_in_cf', 'w_out_cf']
SMALL = [n for n in WEIGHTS if n not in BIG]
SMALL_SHARDED = ['sc_conv', 'gm_ln_g', 'gm_ln_b', 'cf_dw', 'cf_dw_b', 'cf_ln_g', 'cf_ln_b']


def _cparams(sem=None, **kw):
    return pltpu.CompilerParams(dimension_semantics=sem, vmem_limit_bytes=VMEM_LIMIT, **kw)


def _pick(dim, pref):
    if dim <= pref:
        return dim
    t = (pref // LANES) * LANES
    while t >= LANES and dim % t:
        t -= LANES
    if t >= min(pref, 512):
        return t
    return dim if (dim <= 2048 or t < LANES) else t


def _silu(x):
    return x * jax.nn.sigmoid(x)


def _dsilu(x):
    s = jax.nn.sigmoid(x)
    return s * (1.0 + x * (1.0 - s))


def _gelu(x):
    return 0.5 * x * (1.0 + lax.erf(x * (2.0 ** -0.5)))


def _dgelu(x):
    cdf = 0.5 * (1.0 + lax.erf(x * (2.0 ** -0.5)))
    return cdf + x * jnp.exp(-0.5 * x * x) * ((2.0 * math.pi) ** -0.5)


MM_ONE_DOT = 4096


def _contract_tile(dim, divisible_by, pref_when_split):
    return dim if dim <= MM_ONE_DOT and divisible_by % dim == 0 else _pick(divisible_by, pref_when_split)


def _mm_accumulate(step, nsteps, prod, o_ref, acc, init=None):
    if nsteps == 1:
        r = prod()
        if init is not None:
            r = r + init()
        o_ref[...] = r.astype(o_ref.dtype)
        return

    @pl.when(step == 0)
    def _():
        acc[...] = jnp.zeros_like(acc) if init is None else init()

    acc[...] += prod()

    @pl.when(step == nsteps - 1)
    def _():
        o_ref[...] = acc[...].astype(o_ref.dtype)


def _mm_nn(a, w, np_out, name, out_dtype=F32):
    M, K = a.shape
    J, _, n = w.shape
    N = J * n
    W = N // np_out
    tm, tn = _pick(M, 1024), _pick(math.gcd(W, n), 1024)
    tk = _contract_tile(K, K, 2048)
    nk = K // tk

    def body(*refs):
        a_ref, w_ref, o_ref = refs[:3]
        _mm_accumulate(pl.program_id(2), nk, lambda: jnp.dot(a_ref[...], w_ref[...], preferred_element_type=F32),
                       o_ref, refs[-1])

    return pl.pallas_call(
        body, name=name, grid=(M // tm, N // tn, nk),
        in_specs=[pl.BlockSpec((tm, tk), lambda i, j, k: (i, k)),
                  pl.BlockSpec((None, tk, tn), lambda i, j, k: (j // (n // tn), k, j % (n // tn)))],
        out_specs=pl.BlockSpec((None, tm, tn), lambda i, j, k: (j // (W // tn), i, j % (W // tn))),
        out_shape=S((np_out, M, W), out_dtype),
        scratch_shapes=[pltpu.VMEM((tm, tn), F32)] if nk > 1 else [],
        compiler_params=_cparams(("parallel", "parallel", "arbitrary")),
    )(a, w)


def _mm_nt(a3, w, name, add=None, out_dtype=F32):
    NP, M, W = a3.shape
    J, K, n = w.shape
    N = NP * W
    tm, to = _pick(M, 1024), _pick(K, 1024)
    tc = _contract_tile(N, math.gcd(W, n), 2048)
    nc = N // tc
    has_add = add is not None

    def body(*refs):
        a_ref, w_ref = refs[0], refs[1]
        o_ref = refs[3] if has_add else refs[2]
        _mm_accumulate(
            pl.program_id(2), nc,
            lambda: lax.dot_general(a_ref[...], w_ref[...], (((1,), (1,)), ((), ())), preferred_element_type=F32),
            o_ref, refs[-1], init=(lambda: refs[2][...].astype(F32)) if has_add else None)

    in_specs = [pl.BlockSpec((None, tm, tc), lambda i, j, c: (c // (W // tc), i, c % (W // tc))),
                pl.BlockSpec((None, to, tc), lambda i, j, c: (c // (n // tc), j, c % (n // tc)))]
    ops = [a3, w]
    if has_add:
        in_specs.append(pl.BlockSpec((tm, to), lambda i, j, c: (i, j)))
        ops.append(add)
    return pl.pallas_call(
        body, name=name, grid=(M // tm, K // to, nc),
        in_specs=in_specs,
        out_specs=pl.BlockSpec((tm, to), lambda i, j, c: (i, j)),
        out_shape=S((M, K), out_dtype),
        scratch_shapes=[pltpu.VMEM((tm, to), F32)] if nc > 1 else [],
        compiler_params=_cparams(("parallel", "parallel", "arbitrary")),
    )(*ops)


def _mm_tn(a, d3, j_out, name, out_dtype=F32):
    M, K = a.shape
    NP, _, W = d3.shape
    N = NP * W
    n = N // j_out
    to, tn = _pick(K, 1024), _pick(math.gcd(W, n), 1024)
    tmc = _contract_tile(M, M, 2048)
    nm = M // tmc

    def body(*refs):
        a_ref, d_ref, o_ref = refs[:3]
        _mm_accumulate(
            pl.program_id(2), nm,
            lambda: lax.dot_general(a_ref[...], d_ref[...], (((0,), (0,)), ((), ())), preferred_element_type=F32),
            o_ref, refs[-1])

    return pl.pallas_call(
        body, name=name, grid=(K // to, N // tn, nm),
        in_specs=[pl.BlockSpec((tmc, to), lambda i, j, m: (m, i)),
                  pl.BlockSpec((None, tmc, tn), lambda i, j, m: (j // (W // tn), m, j % (W // tn)))],
        out_specs=pl.BlockSpec((None, to, tn), lambda i, j, m: (j // (n // tn), i, j % (n // tn))),
        out_shape=S((j_out, K, n), out_dtype),
        scratch_shapes=[pltpu.VMEM((to, tn), F32)] if nm > 1 else [],
        compiler_params=_cparams(("parallel", "parallel", "arbitrary")),
    )(a, d3)


def _rms_fwd(x3, piece, col_blk, width, g, name, out_dtype, res=None):
    T = x3.shape[1]
    tr = _pick(T, 256)
    has_res = res is not None

    def body(*refs):
        x_ref, g_ref = refs[0], refs[1]
        o_ref = refs[-1]
        x = x_ref[...].astype(F32)
        y = x * lax.rsqrt(jnp.mean(x * x, axis=-1, keepdims=True) + NORM_EPS) * g_ref[...]
        if has_res:
            y = refs[2][...] + y
        o_ref[...] = y.astype(o_ref.dtype)

    in_specs = [pl.BlockSpec((None, tr, width), lambda i: (piece, i, col_blk)),
                pl.BlockSpec((1, width), lambda i: (0, 0))]
    ops = [x3, g.reshape(1, width)]
    if has_res:
        in_specs.append(pl.BlockSpec((tr, width), lambda i: (i, 0)))
        ops.append(res)
    return pl.pallas_call(
        body, name=name, grid=(T // tr,), in_specs=in_specs,
        out_specs=pl.BlockSpec((tr, width), lambda i: (i, 0)),
        out_shape=S((T, width), out_dtype),
        compiler_params=_cparams(("parallel",)),
    )(*ops)


def _rms_bwd(u3, piece, col_blk, width, g, dy, name, out_dtype, res=None):
    T = u3.shape[1]
    tr = _pick(T, 256)
    has_res = res is not None

    def body(*refs):
        u_ref, g_ref, dy_ref = refs[0], refs[1], refs[2]
        du_ref, dg_ref = refs[-2], refs[-1]
        i = pl.program_id(0)
        u = u_ref[...].astype(F32)
        dy_ = dy_ref[...].astype(F32)
        r = lax.rsqrt(jnp.mean(u * u, axis=-1, keepdims=True) + NORM_EPS)
        nrm = u * r
        gdy = g_ref[...] * dy_
        du = r * (gdy - nrm * jnp.mean(gdy * nrm, axis=-1, keepdims=True))
        if has_res:
            du = du + refs[3][...]
        du_ref[...] = du.astype(du_ref.dtype)

        @pl.when(i == 0)
        def _():
            dg_ref[...] = jnp.zeros_like(dg_ref)

        dg_ref[...] += jnp.sum(dy_ * nrm, axis=0, keepdims=True)

    in_specs = [pl.BlockSpec((None, tr, width), lambda i: (piece, i, col_blk)),
                pl.BlockSpec((1, width), lambda i: (0, 0)),
                pl.BlockSpec((tr, width), lambda i: (i, 0))]
    ops = [u3, g.reshape(1, width), dy]
    if has_res:
        in_specs.append(pl.BlockSpec((tr, width), lambda i: (i, 0)))
        ops.append(res)
    return pl.pallas_call(
        body, name=name, grid=(T // tr,), in_specs=in_specs,
        out_specs=[pl.BlockSpec((tr, width), lambda i: (i, 0)), pl.BlockSpec((1, width), lambda i: (0, 0))],
        out_shape=[S((T, width), out_dtype), S((1, width), F32)],
        compiler_params=_cparams(("arbitrary",)),
    )(*ops)


def _loss_head(xl, target, name):
    T, D = xl.shape
    tr = _pick(T, 256)

    def body(x_ref, t_ref, dx_ref, l_ref):
        i = pl.program_id(0)
        err = x_ref[...] - t_ref[...]
        dx_ref[...] = err * (1.0 / D)

        @pl.when(i == 0)
        def _():
            l_ref[...] = jnp.zeros_like(l_ref)

        l_ref[...] += jnp.sum(err * err)

    dx, l = pl.pallas_call(
        body, name=name, grid=(T // tr,),
        in_specs=[pl.BlockSpec((tr, D), lambda i: (i, 0)), pl.BlockSpec((tr, D), lambda i: (i, 0))],
        out_specs=[pl.BlockSpec((tr, D), lambda i: (i, 0)), pl.BlockSpec((8, LANES), lambda i: (0, 0))],
        out_shape=[S((T, D), F32), S((8, LANES), F32)],
        compiler_params=_cparams(("arbitrary",)),
    )(xl, target)
    return dx, l[0, 0] * (0.5 / D)


def _gate_fwd(o, z3, name):
    T, W = o.shape
    tr = _pick(T, 256)

    def body(o_ref, z_ref, g_ref):
        g_ref[...] = (o_ref[...] * _silu(z_ref[...])).astype(g_ref.dtype)

    return pl.pallas_call(
        body, name=name, grid=(T // tr,),
        in_specs=[pl.BlockSpec((tr, W), lambda i: (i, 0)), pl.BlockSpec((None, tr, W), lambda i: (0, i, 0))],
        out_specs=pl.BlockSpec((tr, W), lambda i: (i, 0)),
        out_shape=S((T, W), BF16), compiler_params=_cparams(("parallel",)),
    )(o, z3)


def _gate_bwd(dg, o, z3, name):
    T, W = o.shape
    tr = _pick(T, 256)

    def body(dg_ref, o_ref, z_ref, do_ref, dz_ref):
        dg_, z = dg_ref[...], z_ref[...]
        do_ref[...] = (dg_ * _silu(z)).astype(do_ref.dtype)
        dz_ref[...] = (dg_ * o_ref[...] * _dsilu(z)).astype(dz_ref.dtype)

    return pl.pallas_call(
        body, name=name, grid=(T // tr,),
        in_specs=[pl.BlockSpec((tr, W), lambda i: (i, 0)), pl.BlockSpec((tr, W), lambda i: (i, 0)),
                  pl.BlockSpec((None, tr, W), lambda i: (0, i, 0))],
        out_specs=[pl.BlockSpec((tr, W), lambda i: (i, 0)), pl.BlockSpec((None, tr, W), lambda i: (0, i, 0))],
        out_shape=[S((T, W), BF16), S((1, T, W), BF16)], compiler_params=_cparams(("parallel",)),
    )(dg, o, z3)


def _rope_tables(pos_col, invf, name):
    T = pos_col.shape[0]
    tr = _pick(T, 512)
    half = ROPE_DIM // 2

    def body(p_ref, f_ref, c_ref, sa_ref, sb_ref):
        ang = p_ref[...].astype(F32) * f_ref[...]
        lane = lax.broadcasted_iota(jnp.int32, ang.shape, 1)
        cs, sn = jnp.cos(ang), jnp.sin(ang)
        c_ref[...] = jnp.where(lane < ROPE_DIM, cs, 0.0)
        sa_ref[...] = jnp.where(lane < half, -sn, 0.0)
        sb_ref[...] = jnp.where((lane >= half) & (lane < ROPE_DIM), sn, 0.0)

    spec = pl.BlockSpec((tr, LANES), lambda i: (i, 0))
    return pl.pallas_call(
        body, name=name, grid=(T // tr,),
        in_specs=[pl.BlockSpec((tr, 1), lambda i: (i, 0)), pl.BlockSpec((1, LANES), lambda i: (0, 0))],
        out_specs=[spec, spec, spec], out_shape=[S((T, LANES), F32)] * 3,
        compiler_params=_cparams(("parallel",)),
    )(pos_col, invf)


def _rope(t, c, sa, sb):
    half = ROPE_DIM // 2
    return t * c + pltpu.roll(t, LANES - half, 1) * sa + pltpu.roll(t, half, 1) * sb


def _rope_t(d, c, sa, sb):
    half = ROPE_DIM // 2
    return d * c + pltpu.roll(d * sa, half, 1) + pltpu.roll(d * sb, LANES - half, 1)


def _qkv_layout(q3, kv3, pa3, kr_blk, tabs, H, name):
    T = q3.shape[1]
    tr = _pick(T, 512)

    def body(q_ref, kv_ref, kr_ref, c_ref, sa_ref, sb_ref, qf_ref, kf_ref, v_ref):
        c, sa, sb = c_ref[...], sa_ref[...], sb_ref[...]
        qf_ref[:, :NOPE_DIM] = q_ref[:, :NOPE_DIM].astype(BF16)
        qf_ref[:, NOPE_DIM:] = _rope(q_ref[:, NOPE_DIM:], c, sa, sb).astype(BF16)
        kf_ref[:, :NOPE_DIM] = kv_ref[:, :NOPE_DIM].astype(BF16)
        kf_ref[:, NOPE_DIM:] = _rope(kr_ref[...], c, sa, sb).astype(BF16)
        v_ref[...] = kv_ref[:, NOPE_DIM:].astype(BF16)

    tab = pl.BlockSpec((tr, LANES), lambda i, h: (i, 0))
    hp = pl.BlockSpec((None, tr, HEAD_PAD), lambda i, h: (0, i, h))
    return pl.pallas_call(
        body, name=name, grid=(T // tr, H),
        in_specs=[hp, hp, pl.BlockSpec((None, tr, LANES), lambda i, h: (0, i, kr_blk)), tab, tab, tab],
        out_specs=[pl.BlockSpec((tr, HEAD_PAD), lambda i, h: (i, h)), pl.BlockSpec((tr, HEAD_PAD), lambda i, h: (i, h)),
                   pl.BlockSpec((tr, V_DIM), lambda i, h: (i, h))],
        out_shape=[S((T, H * HEAD_PAD), BF16), S((T, H * HEAD_PAD), BF16), S((T, H * V_DIM), BF16)],
        compiler_params=_cparams(("parallel", "arbitrary")),
    )(q3, kv3, pa3, *tabs)


def _qkv_layout_bwd(dqf, dkf, dv, tabs, H, name):
    T = dqf.shape[0]
    tr = _pick(T, 512)

    def body(dqf_ref, dkf_ref, dv_ref, c_ref, sa_ref, sb_ref, dq_ref, dkv_ref, dkr_ref, acc):
        h = pl.program_id(1)
        c, sa, sb = c_ref[...], sa_ref[...], sb_ref[...]
        dq_ref[:, :NOPE_DIM] = dqf_ref[:, :NOPE_DIM].astype(BF16)
        dq_ref[:, NOPE_DIM:] = _rope_t(dqf_ref[:, NOPE_DIM:], c, sa, sb).astype(BF16)
        dkv_ref[:, :NOPE_DIM] = dkf_ref[:, :NOPE_DIM].astype(BF16)
        dkv_ref[:, NOPE_DIM:] = dv_ref[...].astype(BF16)

        @pl.when(h == 0)
        def _():
            acc[...] = jnp.zeros_like(acc)

        acc[...] += dkf_ref[:, NOPE_DIM:]

        @pl.when(h == H - 1)
        def _():
            dkr_ref[...] = _rope_t(acc[...], c, sa, sb).astype(BF16)

    tab = pl.BlockSpec((tr, LANES), lambda i, h: (i, 0))
    hp_in = pl.BlockSpec((tr, HEAD_PAD), lambda i, h: (i, h))
    hp_out = pl.BlockSpec((None, tr, HEAD_PAD), lambda i, h: (0, i, h))
    return pl.pallas_call(
        body, name=name, grid=(T // tr, H),
        in_specs=[hp_in, hp_in, pl.BlockSpec((tr, V_DIM), lambda i, h: (i, h)), tab, tab, tab],
        out_specs=[hp_out, hp_out, pl.BlockSpec((tr, LANES), lambda i, h: (i, 0))],
        out_shape=[S((1, T, H * HEAD_PAD), BF16), S((1, T, H * HEAD_PAD), BF16), S((T, LANES), BF16)],
        scratch_shapes=[pltpu.VMEM((tr, LANES), F32)],
        compiler_params=_cparams(("parallel", "arbitrary")),
    )(dqf, dkf, dv, *tabs)


def _scores(q, k, qi, ki, tq, tk, scale):
    s = lax.dot_general(q, k, (((1,), (1,)), ((), ())), preferred_element_type=F32) * scale
    qpos = qi * tq + lax.broadcasted_iota(jnp.int32, s.shape, 0)
    kpos = ki * tk + lax.broadcasted_iota(jnp.int32, s.shape, 1)
    return jnp.where(kpos <= qpos, s, NEG)


def _attn_fwd(qf, kf, v, H, scale, name):
    T = qf.shape[0]
    tq = tk = _pick(T, 512)
    nk = T // tk

    def body(q_ref, k_ref, v_ref, o_ref, lse_ref, m_s, l_s, acc):
        qi, ki = pl.program_id(1), pl.program_id(2)

        @pl.when(ki == 0)
        def _():
            m_s[...] = jnp.full_like(m_s, NEG)
            l_s[...] = jnp.zeros_like(l_s)
            acc[...] = jnp.zeros_like(acc)

        @pl.when(ki <= qi)
        def _():
            s = _scores(q_ref[...], k_ref[...], qi, ki, tq, tk, scale)
            m_prev = m_s[...]
            m_new = jnp.maximum(m_prev, jnp.max(s, axis=-1, keepdims=True))
            p = jnp.exp(s - m_new)
            alpha = jnp.exp(m_prev - m_new)
            l_s[...] = alpha * l_s[...] + jnp.sum(p, axis=-1, keepdims=True)
            acc[...] = alpha * acc[...] + jnp.dot(p.astype(BF16), v_ref[...], preferred_element_type=F32)
            m_s[...] = m_new

        @pl.when(ki == nk - 1)
        def _():
            o_ref[...] = acc[...] / l_s[...]
            lse_ref[...] = jnp.broadcast_to(m_s[...] + jnp.log(l_s[...]), lse_ref.shape)

    kv_idx = lambda h, qi, ki: (jnp.minimum(ki, qi), h)
    return pl.pallas_call(
        body, name=name, grid=(H, T // tq, nk),
        in_specs=[pl.BlockSpec((tq, HEAD_PAD), lambda h, qi, ki: (qi, h)),
                  pl.BlockSpec((tk, HEAD_PAD), kv_idx), pl.BlockSpec((tk, V_DIM), kv_idx)],
        out_specs=[pl.BlockSpec((tq, V_DIM), lambda h, qi, ki: (qi, h)), pl.BlockSpec((tq, LANES), lambda h, qi, ki: (qi, h))],
        out_shape=[S((T, H * V_DIM), F32), S((T, H * LANES), F32)],
        scratch_shapes=[pltpu.VMEM((tq, 1), F32), pltpu.VMEM((tq, 1), F32), pltpu.VMEM((tq, V_DIM), F32)],
        compiler_params=_cparams(("parallel", "parallel", "arbitrary")),
    )(qf, kf, v)


def _p_ds(q_ref, k_ref, v_ref, do_ref, o_ref, lse_ref, qi, ki, tq, tk, scale):
    s = _scores(q_ref[...], k_ref[...], qi, ki, tq, tk, scale)
    p = jnp.exp(s - lse_ref[:, :1])
    do = do_ref[...]
    dp = lax.dot_general(do, v_ref[...], (((1,), (1,)), ((), ())), preferred_element_type=F32)
    delta = jnp.sum(do.astype(F32) * o_ref[...], axis=-1, keepdims=True)
    ds = p * (dp - delta) * scale
    return p.astype(BF16), ds.astype(BF16)


def _attn_bwd_kv(qf, kf, v, do, o, lse, H, scale, name):
    T = qf.shape[0]
    tq = tk = _pick(T, 512)
    nq = T // tq

    def body(q_ref, k_ref, v_ref, do_ref, o_ref, lse_ref, dk_ref, dv_ref, dk_acc, dv_acc):
        ki, qi = pl.program_id(1), pl.program_id(2)

        @pl.when(qi == 0)
        def _():
            dk_acc[...] = jnp.zeros_like(dk_acc)
            dv_acc[...] = jnp.zeros_like(dv_acc)

        @pl.when(qi >= ki)
        def _():
            p, ds = _p_ds(q_ref, k_ref, v_ref, do_ref, o_ref, lse_ref, qi, ki, tq, tk, scale)
            dv_acc[...] += lax.dot_general(p, do_ref[...], (((0,), (0,)), ((), ())), preferred_element_type=F32)
            dk_acc[...] += lax.dot_general(ds, q_ref[...], (((0,), (0,)), ((), ())), preferred_element_type=F32)

        @pl.when(qi == nq - 1)
        def _():
            dk_ref[...] = dk_acc[...]
            dv_ref[...] = dv_acc[...]

    q_idx = lambda h, ki, qi: (jnp.maximum(qi, ki), h)
    k_idx = lambda h, ki, qi: (ki, h)
    return pl.pallas_call(
        body, name=name, grid=(H, T // tk, nq),
        in_specs=[pl.BlockSpec((tq, HEAD_PAD), q_idx), pl.BlockSpec((tk, HEAD_PAD), k_idx), pl.BlockSpec((tk, V_DIM), k_idx),
                  pl.BlockSpec((tq, V_DIM), q_idx), pl.BlockSpec((tq, V_DIM), q_idx), pl.BlockSpec((tq, LANES), q_idx)],
        out_specs=[pl.BlockSpec((tk, HEAD_PAD), k_idx), pl.BlockSpec((tk, V_DIM), k_idx)],
        out_shape=[S((T, H * HEAD_PAD), F32), S((T, H * V_DIM), F32)],
        scratch_shapes=[pltpu.VMEM((tk, HEAD_PAD), F32), pltpu.VMEM((tk, V_DIM), F32)],
        compiler_params=_cparams(("parallel", "parallel", "arbitrary")),
    )(qf, kf, v, do, o, lse)


def _attn_bwd_q(qf, kf, v, do, o, lse, H, scale, name):
    T = qf.shape[0]
    tq = tk = _pick(T, 512)
    nk = T // tk

    def body(q_ref, k_ref, v_ref, do_ref, o_ref, lse_ref, dq_ref, dq_acc):
        qi, ki = pl.program_id(1), pl.program_id(2)

        @pl.when(ki == 0)
        def _():
            dq_acc[...] = jnp.zeros_like(dq_acc)

        @pl.when(ki <= qi)
        def _():
            _, ds = _p_ds(q_ref, k_ref, v_ref, do_ref, o_ref, lse_ref, qi, ki, tq, tk, scale)
            dq_acc[...] += jnp.dot(ds, k_ref[...], preferred_element_type=F32)

        @pl.when(ki == nk - 1)
        def _():
            dq_ref[...] = dq_acc[...]

    q_idx = lambda h, qi, ki: (qi, h)
    k_idx = lambda h, qi, ki: (jnp.minimum(ki, qi), h)
    return pl.pallas_call(
        body, name=name, grid=(H, T // tq, nk),
        in_specs=[pl.BlockSpec((tq, HEAD_PAD), q_idx), pl.BlockSpec((tk, HEAD_PAD), k_idx), pl.BlockSpec((tk, V_DIM), k_idx),
                  pl.BlockSpec((tq, V_DIM), q_idx), pl.BlockSpec((tq, V_DIM), q_idx), pl.BlockSpec((tq, LANES), q_idx)],
        out_specs=pl.BlockSpec((tq, HEAD_PAD), q_idx),
        out_shape=S((T, H * HEAD_PAD), F32),
        scratch_shapes=[pltpu.VMEM((tq, HEAD_PAD), F32)],
        compiler_params=_cparams(("parallel", "parallel", "arbitrary")),
    )(qf, kf, v, do, o, lse)


CONV_ROWS = 256
CONV_COLS = 128


def _conv_chunks(T):
    rc = min(CONV_ROWS, T)
    return [(r, rc) for r in range(0, T, rc)]


def _causal_conv(pad_ref, lead, w_ref, width, r0, rc):
    acc = None
    for k in range(width):
        term = w_ref[k:k + 1, :] * pad_ref[pl.ds(lead + r0 - (width - 1) + k, rc), :]
        acc = term if acc is None else acc + term
    return acc


def _anticausal_conv(pad_ref, w_ref, width, r0, rc):
    acc = None
    for k in range(width):
        term = w_ref[k:k + 1, :] * pad_ref[pl.ds(r0 + (width - 1) - k, rc), :]
        acc = term if acc is None else acc + term
    return acc


def _conv_wgrad(dpad_ref, xpad_ref, lead, width, T, dw_ref):
    for k in range(width):
        tot = None
        for r0, rc in _conv_chunks(T):
            part = jnp.sum(dpad_ref[pl.ds(r0, rc), :] * xpad_ref[pl.ds(lead + r0 - (width - 1) + k, rc), :],
                           axis=0, keepdims=True)
            tot = part if tot is None else tot + part
        dw_ref[k:k + 1, :] = tot


def _sc_fwd(p3, wconv, name):
    _, T, W = p3.shape
    width = wconv.shape[0]
    cw = min(CONV_COLS, W)
    lead = 8

    def body(p_ref, w_ref, g_ref, pad):
        pad[0:lead, :] = jnp.zeros((lead, cw), F32)
        for r0, rc in _conv_chunks(T):
            pad[pl.ds(lead + r0, rc), :] = p_ref[1, pl.ds(r0, rc), :] * p_ref[2, pl.ds(r0, rc), :]
        for r0, rc in _conv_chunks(T):
            rows = pl.ds(r0, rc)
            y = p_ref[0, rows, :] * _causal_conv(pad, lead, w_ref, width, r0, rc)
            g_ref[rows, :] = (y * _silu(p_ref[3, rows, :])).astype(g_ref.dtype)

    return pl.pallas_call(
        body, name=name, grid=(W // cw,),
        in_specs=[pl.BlockSpec((4, T, cw), lambda j: (0, 0, j)), pl.BlockSpec((width, cw), lambda j: (0, j))],
        out_specs=pl.BlockSpec((T, cw), lambda j: (0, j)),
        out_shape=S((T, W), BF16),
        scratch_shapes=[pltpu.VMEM((T + lead, cw), F32)],
        compiler_params=_cparams(("parallel",)),
    )(p3, wconv)


def _sc_bwd(p3, wconv, dg, name):
    _, T, W = p3.shape
    width = wconv.shape[0]
    cw = min(CONV_COLS, W)
    lead = 8

    def body(p_ref, w_ref, dg_ref, dp_ref, dw_ref, cupad, dvpad):
        cupad[0:lead, :] = jnp.zeros((lead, cw), F32)
        dvpad[pl.ds(T, lead), :] = jnp.zeros((lead, cw), F32)
        for r0, rc in _conv_chunks(T):
            cupad[pl.ds(lead + r0, rc), :] = p_ref[1, pl.ds(r0, rc), :] * p_ref[2, pl.ds(r0, rc), :]
        for r0, rc in _conv_chunks(T):
            rows = pl.ds(r0, rc)
            b, z, dg_ = p_ref[0, rows, :], p_ref[3, rows, :], dg_ref[rows, :]
            v = _causal_conv(cupad, lead, w_ref, width, r0, rc)
            dy = dg_ * _silu(z)
            dp_ref[3, rows, :] = (dg_ * b * v * _dsilu(z)).astype(dp_ref.dtype)
            dp_ref[0, rows, :] = (dy * v).astype(dp_ref.dtype)
            dvpad[rows, :] = dy * b
        for r0, rc in _conv_chunks(T):
            rows = pl.ds(r0, rc)
            dcu = _anticausal_conv(dvpad, w_ref, width, r0, rc)
            dp_ref[1, rows, :] = (dcu * p_ref[2, rows, :]).astype(dp_ref.dtype)
            dp_ref[2, rows, :] = (dcu * p_ref[1, rows, :]).astype(dp_ref.dtype)
        _conv_wgrad(dvpad, cupad, lead, width, T, dw_ref)

    return pl.pallas_call(
        body, name=name, grid=(W // cw,),
        in_specs=[pl.BlockSpec((4, T, cw), lambda j: (0, 0, j)), pl.BlockSpec((width, cw), lambda j: (0, j)),
                  pl.BlockSpec((T, cw), lambda j: (0, j))],
        out_specs=[pl.BlockSpec((4, T, cw), lambda j: (0, 0, j)), pl.BlockSpec((width, cw), lambda j: (0, j))],
        out_shape=[S((4, T, W), BF16), S((width, W), F32)],
        scratch_shapes=[pltpu.VMEM((T + lead, cw), F32), pltpu.VMEM((T + lead, cw), F32)],
        compiler_params=_cparams(("parallel",)),
    )(p3, wconv, dg)


def _gm_common(p_ref, lng_ref, lnb_ref):
    ug = _gelu(p_ref[0])
    vg = _gelu(p_ref[1])
    mu = jnp.mean(vg, axis=-1, keepdims=True)
    xc = vg - mu
    rstd = lax.rsqrt(jnp.mean(xc * xc, axis=-1, keepdims=True) + LN_EPS)
    xhat = xc * rstd
    vn = xhat * lng_ref[...] + lnb_ref[...]
    return ug, xhat, rstd, vn


def _gm_mix_weights(ws_ref, g):
    row = lax.broadcasted_iota(jnp.int32, (GM_CHUNK, GM_CHUNK), 0)
    col = lax.broadcasted_iota(jnp.int32, (GM_CHUNK, GM_CHUNK), 1)
    return jnp.where(col <= row, ws_ref[g], 0.0).astype(BF16)


def _gm_fwd(p3, lng, lnb, ws, bs_t, name):
    _, T, W = p3.shape
    gw = W // GM_GROUPS

    def body(p_ref, lng_ref, lnb_ref, ws_ref, bs_ref, g_ref):
        ug, _, _, vn = _gm_common(p_ref, lng_ref, lnb_ref)
        sz = _silu(p_ref[2])
        vnb = vn.astype(BF16)
        for g in range(GM_GROUPS):
            cols = slice(g * gw, (g + 1) * gw)
            s = jnp.dot(_gm_mix_weights(ws_ref, g), vnb[:, cols], preferred_element_type=F32) + bs_ref[:, g:g + 1]
            g_ref[:, cols] = (ug[:, cols] * s * sz[:, cols]).astype(g_ref.dtype)

    return pl.pallas_call(
        body, name=name, grid=(T // GM_CHUNK,),
        in_specs=[pl.BlockSpec((3, GM_CHUNK, W), lambda i: (0, i, 0)), pl.BlockSpec((1, W), lambda i: (0, 0)),
                  pl.BlockSpec((1, W), lambda i: (0, 0)),
                  pl.BlockSpec((GM_GROUPS, GM_CHUNK, GM_CHUNK), lambda i: (0, 0, 0)),
                  pl.BlockSpec((GM_CHUNK, GM_GROUPS), lambda i: (0, 0))],
        out_specs=pl.BlockSpec((GM_CHUNK, W), lambda i: (i, 0)),
        out_shape=S((T, W), BF16), compiler_params=_cparams(("parallel",)),
    )(p3, lng.reshape(1, W), lnb.reshape(1, W), ws, bs_t)


def _gm_bwd(p3, lng, lnb, ws, bs_t, dg, name):
    _, T, W = p3.shape
    gw = W // GM_GROUPS

    def body(p_ref, lng_ref, lnb_ref, ws_ref, bs_ref, dg_ref, dp_ref, dlng_ref, dlnb_ref, dws_ref, dbs_ref, dvn_s):
        i = pl.program_id(0)

        @pl.when(i == 0)
        def _():
            dlng_ref[...] = jnp.zeros_like(dlng_ref)
            dlnb_ref[...] = jnp.zeros_like(dlnb_ref)
            dws_ref[...] = jnp.zeros_like(dws_ref)
            dbs_ref[...] = jnp.zeros_like(dbs_ref)

        ug, xhat, rstd, vn = _gm_common(p_ref, lng_ref, lnb_ref)
        z = p_ref[2]
        dg_ = dg_ref[...]
        dy = dg_ * _silu(z)
        vnb = vn.astype(BF16)
        row = lax.broadcasted_iota(jnp.int32, (GM_CHUNK, GM_CHUNK), 0)
        col = lax.broadcasted_iota(jnp.int32, (GM_CHUNK, GM_CHUNK), 1)
        dbs = jnp.zeros((GM_CHUNK, LANES), F32)
        for g in range(GM_GROUPS):
            cols = slice(g * gw, (g + 1) * gw)
            wm = _gm_mix_weights(ws_ref, g)
            s = jnp.dot(wm, vnb[:, cols], preferred_element_type=F32) + bs_ref[:, g:g + 1]
            dp_ref[2, :, cols] = (dg_[:, cols] * ug[:, cols] * s * _dsilu(z[:, cols])).astype(dp_ref.dtype)
            dp_ref[0, :, cols] = (dy[:, cols] * s * _dgelu(p_ref[0, :, cols])).astype(dp_ref.dtype)
            ds = dy[:, cols] * ug[:, cols]
            dsb = ds.astype(BF16)
            dwm = lax.dot_general(dsb, vnb[:, cols], (((1,), (1,)), ((), ())), preferred_element_type=F32)
            dws_ref[g] += jnp.where(col <= row, dwm, 0.0)
            dbs = dbs + jnp.where(col == g, jnp.sum(ds, axis=-1, keepdims=True), 0.0)
            dvn_s[:, cols] = lax.dot_general(wm, dsb, (((0,), (0,)), ((), ())), preferred_element_type=F32)
        dbs_ref[...] += dbs
        dvn = dvn_s[...]
        dlng_ref[...] += jnp.sum(dvn * xhat, axis=0, keepdims=True)
        dlnb_ref[...] += jnp.sum(dvn, axis=0, keepdims=True)
        dxh = dvn * lng_ref[...]
        dvg = rstd * (dxh - jnp.mean(dxh, axis=-1, keepdims=True) - xhat * jnp.mean(dxh * xhat, axis=-1, keepdims=True))
        dp_ref[1] = (dvg * _dgelu(p_ref[1])).astype(dp_ref.dtype)

    row1 = pl.BlockSpec((1, W), lambda i: (0, 0))
    return pl.pallas_call(
        body, name=name, grid=(T // GM_CHUNK,),
        in_specs=[pl.BlockSpec((3, GM_CHUNK, W), lambda i: (0, i, 0)), row1, row1,
                  pl.BlockSpec((GM_GROUPS, GM_CHUNK, GM_CHUNK), lambda i: (0, 0, 0)),
                  pl.BlockSpec((GM_CHUNK, GM_GROUPS), lambda i: (0, 0)),
                  pl.BlockSpec((GM_CHUNK, W), lambda i: (i, 0))],
        out_specs=[pl.BlockSpec((3, GM_CHUNK, W), lambda i: (0, i, 0)), row1, row1,
                   pl.BlockSpec((GM_GROUPS, GM_CHUNK, GM_CHUNK), lambda i: (0, 0, 0)),
                   pl.BlockSpec((GM_CHUNK, LANES), lambda i: (0, 0))],
        out_shape=[S((3, T, W), BF16), S((1, W), F32), S((1, W), F32),
                   S((GM_GROUPS, GM_CHUNK, GM_CHUNK), F32), S((GM_CHUNK, LANES), F32)],
        scratch_shapes=[pltpu.VMEM((GM_CHUNK, W), F32)],
        compiler_params=_cparams(("arbitrary",)),
    )(p3, lng.reshape(1, W), lnb.reshape(1, W), ws, bs_t, dg)


def _cf_conv_fwd(p3, wdw, bdw, name):
    _, T, W = p3.shape
    width = wdw.shape[0]
    cw = min(CONV_COLS, W)
    lead = 32

    def body(p_ref, w_ref, b_ref, y_ref, pad):
        pad[0:lead, :] = jnp.zeros((lead, cw), F32)
        for r0, rc in _conv_chunks(T):
            rows = pl.ds(r0, rc)
            pad[pl.ds(lead + r0, rc), :] = p_ref[0, rows, :] * jax.nn.sigmoid(p_ref[1, rows, :])
        for r0, rc in _conv_chunks(T):
            y_ref[pl.ds(r0, rc), :] = _causal_conv(pad, lead, w_ref, width, r0, rc) + b_ref[...]

    return pl.pallas_call(
        body, name=name, grid=(W // cw,),
        in_specs=[pl.BlockSpec((2, T, cw), lambda j: (0, 0, j)), pl.BlockSpec((width, cw), lambda j: (0, j)),
                  pl.BlockSpec((1, cw), lambda j: (0, j))],
        out_specs=pl.BlockSpec((T, cw), lambda j: (0, j)),
        out_shape=S((T, W), F32),
        scratch_shapes=[pltpu.VMEM((T + lead, cw), F32)],
        compiler_params=_cparams(("parallel",)),
    )(p3, wdw, bdw.reshape(1, W))


def _cf_ln(y1_ref, lng_ref, lnb_ref):
    y1 = y1_ref[...]
    mu = jnp.mean(y1, axis=-1, keepdims=True)
    xc = y1 - mu
    rstd = lax.rsqrt(jnp.mean(xc * xc, axis=-1, keepdims=True) + LN_EPS)
    xhat = xc * rstd
    return xhat, rstd, xhat * lng_ref[...] + lnb_ref[...]


def _cf_gate_fwd(y1, p3, lng, lnb, name):
    T, W = y1.shape
    tr = _pick(T, 256)

    def body(y1_ref, z_ref, lng_ref, lnb_ref, g_ref):
        _, _, y2 = _cf_ln(y1_ref, lng_ref, lnb_ref)
        g_ref[...] = (_silu(y2) * _silu(z_ref[...])).astype(g_ref.dtype)

    row1 = pl.BlockSpec((1, W), lambda i: (0, 0))
    return pl.pallas_call(
        body, name=name, grid=(T // tr,),
        in_specs=[pl.BlockSpec((tr, W), lambda i: (i, 0)), pl.BlockSpec((None, tr, W), lambda i: (2, i, 0)), row1, row1],
        out_specs=pl.BlockSpec((tr, W), lambda i: (i, 0)),
        out_shape=S((T, W), BF16), compiler_params=_cparams(("parallel",)),
    )(y1, p3, lng.reshape(1, W), lnb.reshape(1, W))


def _cf_gate_bwd(y1, p3, lng, lnb, dg, name):
    T, W = y1.shape
    tr = _pick(T, 128)

    def body(y1_ref, z_ref, lng_ref, lnb_ref, dg_ref, dz_ref, dy1_ref, dlng_ref, dlnb_ref):
        i = pl.program_id(0)

        @pl.when(i == 0)
        def _():
            dlng_ref[...] = jnp.zeros_like(dlng_ref)
            dlnb_ref[...] = jnp.zeros_like(dlnb_ref)

        xhat, rstd, y2 = _cf_ln(y1_ref, lng_ref, lnb_ref)
        z, dg_ = z_ref[...], dg_ref[...]
        dz_ref[...] = (dg_ * _silu(y2) * _dsilu(z)).astype(dz_ref.dtype)
        dy2 = dg_ * _silu(z) * _dsilu(y2)
        dlng_ref[...] += jnp.sum(dy2 * xhat, axis=0, keepdims=True)
        dlnb_ref[...] += jnp.sum(dy2, axis=0, keepdims=True)
        dxh = dy2 * lng_ref[...]
        dy1_ref[...] = rstd * (dxh - jnp.mean(dxh, axis=-1, keepdims=True)
                               - xhat * jnp.mean(dxh * xhat, axis=-1, keepdims=True))

    row1 = pl.BlockSpec((1, W), lambda i: (0, 0))
    blk = pl.BlockSpec((tr, W), lambda i: (i, 0))
    return pl.pallas_call(
        body, name=name, grid=(T // tr,),
        in_specs=[blk, pl.BlockSpec((None, tr, W), lambda i: (2, i, 0)), row1, row1, blk],
        out_specs=[blk, blk, row1, row1],
        out_shape=[S((T, W), BF16), S((T, W), F32), S((1, W), F32), S((1, W), F32)],
        compiler_params=_cparams(("arbitrary",)),
    )(y1, p3, lng.reshape(1, W), lnb.reshape(1, W), dg)


def _cf_conv_bwd(p3, wdw, dy1, dz, name):
    _, T, W = p3.shape
    width = wdw.shape[0]
    cw = min(CONV_COLS, W)
    lead = 32

    def body(p_ref, w_ref, dy1_ref, dz_ref, dp_ref, dw_ref, db_ref, y0pad, dpad):
        y0pad[0:lead, :] = jnp.zeros((lead, cw), F32)
        dpad[pl.ds(T, lead), :] = jnp.zeros((lead, cw), F32)
        bsum = None
        for r0, rc in _conv_chunks(T):
            rows = pl.ds(r0, rc)
            y0pad[pl.ds(lead + r0, rc), :] = p_ref[0, rows, :] * jax.nn.sigmoid(p_ref[1, rows, :])
            d = dy1_ref[rows, :]
            dpad[rows, :] = d
            part = jnp.sum(d, axis=0, keepdims=True)
            bsum = part if bsum is None else bsum + part
        db_ref[...] = bsum
        for r0, rc in _conv_chunks(T):
            rows = pl.ds(r0, rc)
            dy0 = _anticausal_conv(dpad, w_ref, width, r0, rc)
            a = p_ref[0, rows, :]
            sg = jax.nn.sigmoid(p_ref[1, rows, :])
            dp_ref[0, rows, :] = (dy0 * sg).astype(dp_ref.dtype)
            dp_ref[1, rows, :] = (dy0 * a * sg * (1.0 - sg)).astype(dp_ref.dtype)
            dp_ref[2, rows, :] = dz_ref[rows, :]
        _conv_wgrad(dpad, y0pad, lead, width, T, dw_ref)

    return pl.pallas_call(
        body, name=name, grid=(W // cw,),
        in_specs=[pl.BlockSpec((2, T, cw), lambda j: (0, 0, j)), pl.BlockSpec((width, cw), lambda j: (0, j)),
                  pl.BlockSpec((T, cw), lambda j: (0, j)), pl.BlockSpec((T, cw), lambda j: (0, j))],
        out_specs=[pl.BlockSpec((3, T, cw), lambda j: (0, 0, j)), pl.BlockSpec((width, cw), lambda j: (0, j)),
                   pl.BlockSpec((1, cw), lambda j: (0, j))],
        out_shape=[S((3, T, W), BF16), S((width, W), F32), S((1, W), F32)],
        scratch_shapes=[pltpu.VMEM((T + lead, cw), F32), pltpu.VMEM((T + lead, cw), F32)],
        compiler_params=_cparams(("parallel",)),
    )(p3, wdw, dy1, dz)


def _rows_call(body, ins, out_dtypes, name, row_pref=256):
    R, C = ins[0].shape
    tr = _pick(R, row_pref) if R % 8 == 0 else R
    while tr > 8 and tr * C * 4 * (len(ins) + len(out_dtypes)) * 2 > VMEM_LIMIT // 2 and tr % 16 == 0:
        tr //= 2
    blk = pl.BlockSpec((tr, C), lambda i: (i, 0))
    return pl.pallas_call(
        body, name=name, grid=(R // tr,), in_specs=[blk] * len(ins), out_specs=[blk] * len(out_dtypes),
        out_shape=[S((R, C), dt) for dt in out_dtypes], compiler_params=_cparams(("parallel",)),
    )(*ins)


def _pair_sum(g_half, r, name):
    def body(a_ref, b_ref, o_ref):
        o_ref[...] = (a_ref[...] + b_ref[...]).astype(BF16)
    return _rows_call(body, [g_half, r], [BF16], name)[0]


def _chip_sum(rc, name):
    J, R, C = rc.shape
    tr = _pick(R, 256)

    def body(r_ref, o_ref):
        acc = r_ref[0].astype(F32)
        for j in range(1, J):
            acc = acc + r_ref[j].astype(F32)
        o_ref[...] = acc

    return pl.pallas_call(
        body, name=name, grid=(R // tr,),
        in_specs=[pl.BlockSpec((J, tr, C), lambda i: (0, i, 0))], out_specs=pl.BlockSpec((tr, C), lambda i: (i, 0)),
        out_shape=S((R, C), F32), compiler_params=_cparams(("parallel",)),
    )(rc)


def _slot_sum(slots, name):
    J, R, C = slots.shape
    tr = _pick(R, 512)

    def body(r_ref, o_ref):
        acc = r_ref[0]
        for j in range(1, J):
            acc = acc + r_ref[j]
        o_ref[...] = acc

    return pl.pallas_call(
        body, name=name, grid=(R // tr,),
        in_specs=[pl.BlockSpec((J, tr, C), lambda i: (0, i, 0))], out_specs=pl.BlockSpec((tr, C), lambda i: (i, 0)),
        out_shape=S((R, C), F32), compiler_params=_cparams(("parallel",)),
    )(slots)


def _adamw(w, g, m, v, name):
    def body(w_ref, g_ref, m_ref, v_ref, d_ref, nm_ref, nv_ref):
        g_ = g_ref[...]
        nm = ADAM_B1 * m_ref[...] + (1.0 - ADAM_B1) * g_
        nv = ADAM_B2 * v_ref[...] + (1.0 - ADAM_B2) * (g_ * g_)
        m_hat = nm / (1.0 - ADAM_B1 ** ADAM_STEP)
        v_hat = nv / (1.0 - ADAM_B2 ** ADAM_STEP)
        d_ref[...] = -ADAM_LR * (m_hat / (jnp.sqrt(v_hat) + ADAM_EPS) + ADAM_WD * w_ref[...])
        nm_ref[...] = nm
        nv_ref[...] = nv
    return _rows_call(body, [w, g, m, v], [F32, F32, F32], name)


ANY = pl.BlockSpec(memory_space=pl.ANY)


def _place():
    x, y, c = lax.axis_index("x"), lax.axis_index("y"), lax.axis_index("c")
    return x, y, c


def _other_chips(x, y):
    return [(1 - x, y), (x, 1 - y), (1 - x, 1 - y)]


def _all_gather_weights(shards, name):
    nw = len(shards)

    def body(*refs):
        ins, outs = refs[:nw], refs[nw:2 * nw]
        send_sems, recv_sems, loc_sems = refs[2 * nw:]
        x, y, c = _place()
        me_chip = 2 * x + y
        sibling = (x, y, 1 - c)
        chips = _other_chips(x, y)

        def half(k_rows, which):
            return pl.ds(which * (k_rows // 2), k_rows // 2)

        local = []
        for w in range(nw):
            cp = pltpu.make_async_copy(ins[w], outs[w].at[me_chip], loc_sems.at[w])
            cp.start()
            local.append(cp)
        sends = []
        for w in range(nw):
            kr = ins[w].shape[0]
            for j, (cx, cy) in enumerate(chips):
                cp = pltpu.make_async_remote_copy(
                    src_ref=ins[w].at[half(kr, c)], dst_ref=outs[w].at[me_chip, half(kr, c)],
                    send_sem=send_sems.at[w * 6 + j], recv_sem=recv_sems.at[w * 6 + j],
                    device_id=(cx, cy, c), device_id_type=MESH_ID)
                cp.start()
                sends.append(cp)
        for w in range(nw):
            kr = ins[w].shape[0]
            for j, (cx, cy) in enumerate(chips):
                blk = outs[w].at[2 * cx + cy, half(kr, c)]
                pltpu.make_async_remote_copy(
                    src_ref=blk, dst_ref=blk, send_sem=send_sems.at[w * 6 + j], recv_sem=recv_sems.at[w * 6 + j],
                    device_id=(cx, cy, c), device_id_type=MESH_ID).wait_recv()
                cp = pltpu.make_async_remote_copy(
                    src_ref=blk, dst_ref=blk, send_sem=send_sems.at[w * 6 + 3 + j], recv_sem=recv_sems.at[w * 6 + 3 + j],
                    device_id=sibling, device_id_type=MESH_ID)
                cp.start()
                sends.append(cp)
        for w in range(nw):
            kr = ins[w].shape[0]
            for j, (cx, cy) in enumerate(chips):
                blk = outs[w].at[2 * cx + cy, half(kr, 1 - c)]
                pltpu.make_async_remote_copy(
                    src_ref=blk, dst_ref=blk, send_sem=send_sems.at[w * 6 + 3 + j], recv_sem=recv_sems.at[w * 6 + 3 + j],
                    device_id=sibling, device_id_type=MESH_ID).wait_recv()
        for cp in sends:
            cp.wait_send()
        for cp in local:
            cp.wait()

    return pl.pallas_call(
        body, name=name, in_specs=[ANY] * nw, out_specs=[ANY] * nw,
        out_shape=[S((N_CHIPS,) + s.shape, s.dtype) for s in shards],
        scratch_shapes=[pltpu.SemaphoreType.DMA((6 * nw,)), pltpu.SemaphoreType.DMA((6 * nw,)),
                        pltpu.SemaphoreType.DMA((nw,))],
        compiler_params=pltpu.CompilerParams(has_side_effects=True),
    )(*shards)


def _core_exchange_halves(grads, name):
    nw = len(grads)

    def body(*refs):
        ins, outs = refs[:nw], refs[nw:2 * nw]
        send_sems, recv_sems = refs[2 * nw:]
        x, y, c = _place()
        sibling = (x, y, 1 - c)
        cps = []
        for w in range(nw):
            kh = ins[w].shape[1] // 2
            cp = pltpu.make_async_remote_copy(
                src_ref=ins[w].at[:, pl.ds((1 - c) * kh, kh), :], dst_ref=outs[w],
                send_sem=send_sems.at[w], recv_sem=recv_sems.at[w], device_id=sibling, device_id_type=MESH_ID)
            cp.start()
            cps.append(cp)
        for cp in cps:
            cp.wait()

    return pl.pallas_call(
        body, name=name, in_specs=[ANY] * nw, out_specs=[ANY] * nw,
        out_shape=[S((g.shape[0], g.shape[1] // 2, g.shape[2]), g.dtype) for g in grads],
        scratch_shapes=[pltpu.SemaphoreType.DMA((nw,)), pltpu.SemaphoreType.DMA((nw,))],
        compiler_params=pltpu.CompilerParams(has_side_effects=True),
    )(*grads)


def _chip_scatter(parts, name):
    nw = len(parts)

    def body(*refs):
        ins, outs = refs[:nw], refs[nw:2 * nw]
        send_sems, recv_sems, loc_sems = refs[2 * nw:]
        x, y, c = _place()
        me_chip = 2 * x + y
        chips = _other_chips(x, y)
        loc, rem = [], []
        for w in range(nw):
            cp = pltpu.make_async_copy(ins[w].at[me_chip], outs[w].at[me_chip], loc_sems.at[w])
            cp.start()
            loc.append(cp)
            for j, (cx, cy) in enumerate(chips):
                cp = pltpu.make_async_remote_copy(
                    src_ref=ins[w].at[2 * cx + cy], dst_ref=outs[w].at[me_chip],
                    send_sem=send_sems.at[w * 3 + j], recv_sem=recv_sems.at[w * 3 + j],
                    device_id=(cx, cy, c), device_id_type=MESH_ID)
                cp.start()
                rem.append(cp)
        for w in range(nw):
            for j, (cx, cy) in enumerate(chips):
                slot = outs[w].at[2 * cx + cy]
                pltpu.make_async_remote_copy(
                    src_ref=slot, dst_ref=slot, send_sem=send_sems.at[w * 3 + j], recv_sem=recv_sems.at[w * 3 + j],
                    device_id=(cx, cy, c), device_id_type=MESH_ID).wait_recv()
        for cp in rem:
            cp.wait_send()
        for cp in loc:
            cp.wait()

    return pl.pallas_call(
        body, name=name, in_specs=[ANY] * nw, out_specs=[ANY] * nw,
        out_shape=[S(p.shape, p.dtype) for p in parts],
        scratch_shapes=[pltpu.SemaphoreType.DMA((3 * nw,)), pltpu.SemaphoreType.DMA((3 * nw,)),
                        pltpu.SemaphoreType.DMA((nw,))],
        compiler_params=pltpu.CompilerParams(has_side_effects=True),
    )(*parts)


def _core_join_halves(halves, name):
    nw = len(halves)

    def body(*refs):
        ins, outs = refs[:nw], refs[nw:2 * nw]
        send_sems, recv_sems, loc_sems = refs[2 * nw:]
        x, y, c = _place()
        sibling = (x, y, 1 - c)
        loc, rem = [], []
        for w in range(nw):
            r = ins[w].shape[0]
            mine = outs[w].at[pl.ds(c * r, r)]
            cp = pltpu.make_async_copy(ins[w], mine, loc_sems.at[w])
            cp.start()
            loc.append(cp)
            cp = pltpu.make_async_remote_copy(
                src_ref=ins[w], dst_ref=mine, send_sem=send_sems.at[w], recv_sem=recv_sems.at[w],
                device_id=sibling, device_id_type=MESH_ID)
            cp.start()
            rem.append(cp)
        for w in range(nw):
            r = ins[w].shape[0]
            theirs = outs[w].at[pl.ds((1 - c) * r, r)]
            pltpu.make_async_remote_copy(
                src_ref=theirs, dst_ref=theirs, send_sem=send_sems.at[w], recv_sem=recv_sems.at[w],
                device_id=sibling, device_id_type=MESH_ID).wait_recv()
        for cp in rem:
            cp.wait_send()
        for cp in loc:
            cp.wait()

    return pl.pallas_call(
        body, name=name, in_specs=[ANY] * nw, out_specs=[ANY] * nw,
        out_shape=[S((2 * h.shape[0],) + h.shape[1:], h.dtype) for h in halves],
        scratch_shapes=[pltpu.SemaphoreType.DMA((nw,)), pltpu.SemaphoreType.DMA((nw,)), pltpu.SemaphoreType.DMA((nw,))],
        compiler_params=pltpu.CompilerParams(has_side_effects=True),
    )(*halves)


def _broadcast_all(buf, name):
    def body(in_ref, out_ref, send_sems, recv_sems, loc_sem):
        x, y, c = _place()
        me = 4 * x + 2 * y + c
        loc = pltpu.make_async_copy(in_ref, out_ref.at[me], loc_sem)
        loc.start()
        cps = []
        for k in range(1, N_DEV):
            fx, fy, fc = (k >> 2) & 1, (k >> 1) & 1, k & 1
            px, py, pc = x ^ fx, y ^ fy, c ^ fc
            cp = pltpu.make_async_remote_copy(
                src_ref=in_ref, dst_ref=out_ref.at[me], send_sem=send_sems.at[k - 1], recv_sem=recv_sems.at[k - 1],
                device_id=(px, py, pc), device_id_type=MESH_ID)
            cp.start()
            cps.append(cp)
        for k in range(1, N_DEV):
            fx, fy, fc = (k >> 2) & 1, (k >> 1) & 1, k & 1
            px, py, pc = x ^ fx, y ^ fy, c ^ fc
            slot = out_ref.at[4 * px + 2 * py + pc]
            pltpu.make_async_remote_copy(
                src_ref=slot, dst_ref=slot, send_sem=send_sems.at[k - 1], recv_sem=recv_sems.at[k - 1],
                device_id=(px, py, pc), device_id_type=MESH_ID).wait_recv()
        for cp in cps:
            cp.wait_send()
        loc.wait()

    return pl.pallas_call(
        body, name=name, in_specs=[ANY], out_specs=ANY,
        out_shape=S((N_DEV,) + buf.shape, buf.dtype),
        scratch_shapes=[pltpu.SemaphoreType.DMA((N_DEV - 1,)), pltpu.SemaphoreType.DMA((N_DEV - 1,)),
                        pltpu.SemaphoreType.DMA],
        compiler_params=pltpu.CompilerParams(has_side_effects=True),
    )(buf)


PACK_ALIGN = 8 * LANES


def _pack(arrs):
    flat = []
    for a in arrs:
        f = a.reshape(-1).astype(F32)
        pad = (-f.shape[0]) % PACK_ALIGN
        flat.append(jnp.pad(f, (0, pad)) if pad else f)
    return jnp.concatenate(flat).reshape(-1, LANES)


def _unpack(buf, shapes):
    out, off = [], 0
    flat = buf.reshape(-1)
    for shp in shapes:
        n = math.prod(shp)
        out.append(flat[off:off + n].reshape(shp))
        off += n + ((-n) % PACK_ALIGN)
    return out


def kernel(x, positions, norm_pre, norm_post, w_in_mla, mla_q_norm, w_uq, mla_kv_norm, w_ukv, w_out_mla, w_in_sc, sc_conv, w_out_sc, w_in_gm, gm_ln_g, gm_ln_b, gm_w_s, gm_b_s, w_out_gm, w_in_cf, cf_dw, cf_dw_b, cf_ln_g, cf_ln_b, w_out_cf, loss_target, m_norm_pre, m_norm_post, m_w_in_mla, m_mla_q_norm, m_w_uq, m_mla_kv_norm, m_w_ukv, m_w_out_mla, m_w_in_sc, m_sc_conv, m_w_out_sc, m_w_in_gm, m_gm_ln_g, m_gm_ln_b, m_gm_w_s, m_gm_b_s, m_w_out_gm, m_w_in_cf, m_cf_dw, m_cf_dw_b, m_cf_ln_g, m_cf_ln_b, m_w_out_cf, v_norm_pre, v_norm_post, v_w_in_mla, v_mla_q_norm, v_w_uq, v_mla_kv_norm, v_w_ukv, v_w_out_mla, v_w_in_sc, v_sc_conv, v_w_out_sc, v_w_in_gm, v_gm_ln_g, v_gm_ln_b, v_gm_w_s, v_gm_b_s, v_w_out_gm, v_w_in_cf, v_cf_dw, v_cf_dw_b, v_cf_ln_g, v_cf_ln_b, v_w_out_cf):
    loc = dict(locals())
    wts = {n: loc[n] for n in WEIGHTS}
    mom_m = {n: loc["m_" + n] for n in WEIGHTS}
    mom_v = {n: loc["v_" + n] for n in WEIGHTS}

    T, D = x.shape[1], x.shape[2]
    xin = x.reshape(T, D)
    target = loss_target.reshape(T, D)
    q_rank, kv_rank = mla_q_norm.shape[1], mla_kv_norm.shape[1]
    H = (w_uq.shape[2] * N_CHIPS) // (NOPE_DIM + ROPE_DIM)
    hv = H * V_DIM
    c_kr = q_rank + kv_rank
    wa_cols = c_kr + ROPE_DIM
    wa_pad = wa_cols + (LANES - ROPE_DIM)
    chip = 2 * lax.axis_index("x") + lax.axis_index("y")

    gathered = _all_gather_weights([wts[n][0].astype(BF16) for n in BIG], "gather_weights")
    gw = dict(zip(BIG, gathered))
    small_sh_shapes = [wts[n][0].shape for n in SMALL_SHARDED]
    slots = _broadcast_all(_pack([wts[n][0] for n in SMALL_SHARDED]), "gather_small")
    per_chip = [_unpack(slots[2 * k], small_sh_shapes) for k in range(N_CHIPS)]
    sp = {n: jnp.concatenate([per_chip[k][i] for k in range(N_CHIPS)], axis=-1) for i, n in enumerate(SMALL_SHARDED)}

    def cols_major(w4):
        return jnp.transpose(w4, (1, 0, 2)).reshape(w4.shape[1], -1)

    w_in_full = cols_major(gw['w_in_mla'])
    w_a = jnp.pad(w_in_full[:, :wa_cols], ((0, 0), (0, wa_pad - wa_cols)))[None]
    w_z = w_in_full[:, wa_cols:][None]
    wq = cols_major(gw['w_uq']).reshape(q_rank, H, NOPE_DIM + ROPE_DIM)
    wq = jnp.pad(wq, ((0, 0), (0, 0), (0, HEAD_PAD - NOPE_DIM - ROPE_DIM))).reshape(1, q_rank, H * HEAD_PAD)
    wkv = cols_major(gw['w_ukv'])[None]
    w_out = {n: gw[n].reshape(1, -1, D) for n in ['w_out_mla', 'w_out_sc', 'w_out_gm', 'w_out_cf']}

    half = ROPE_DIM // 2
    inv_freq = ROPE_THETA ** (-jnp.arange(half, dtype=F32) / half)
    invf = jnp.concatenate([inv_freq, inv_freq, jnp.zeros((LANES - ROPE_DIM,), F32)]).reshape(1, LANES)
    tabs = _rope_tables(positions.reshape(T, 1), invf, "rope_tables")
    scale = float(NOPE_DIM + ROPE_DIM) ** -0.5

    xs = [xin]
    saved = []
    for i in range(4):
        xi = xs[-1]
        h = _rms_fwd(xi[None], 0, 0, D, norm_pre[i], f"pre_norm_{i}", BF16)
        if i == 0:
            pa = _mm_nn(h, w_a, 1, "mla_in_a")
            pz = _mm_nn(h, w_z, 1, "mla_in_z")
            qn = _rms_fwd(pa, 0, 0, q_rank, mla_q_norm[0], "mla_q_norm", BF16)
            kvn = _rms_fwd(pa, 0, q_rank // kv_rank, kv_rank, mla_kv_norm[0], "mla_kv_norm", BF16)
            q3 = _mm_nn(qn, wq, 1, "mla_q_up")
            kv3 = _mm_nn(kvn, wkv, 1, "mla_kv_up")
            qf, kf, vv = _qkv_layout(q3, kv3, pa, c_kr // LANES, tabs, H, "mla_qkv_layout")
            o, lse = _attn_fwd(qf, kf, vv, H, scale, "mla_attn_fwd")
            g = _gate_fwd(o, pz, "mla_gate_fwd")
            saved.append(dict(h=h, pa=pa, pz=pz, qn=qn, kvn=kvn, qf=qf, kf=kf, vv=vv, o=o, lse=lse, g=g))
            wo = w_out['w_out_mla']
        elif i == 1:
            p3 = _mm_nn(h, gw['w_in_sc'], 4, "sc_in")
            g = _sc_fwd(p3, sp['sc_conv'], "sc_mix_fwd")
            saved.append(dict(h=h, p3=p3, g=g))
            wo = w_out['w_out_sc']
        elif i == 2:
            p3 = _mm_nn(h, gw['w_in_gm'], 3, "gm_in")
            bs_t = jnp.transpose(gm_b_s[0])
            g = _gm_fwd(p3, sp['gm_ln_g'], sp['gm_ln_b'], gm_w_s[0], bs_t, "gm_mix_fwd")
            saved.append(dict(h=h, p3=p3, g=g, bs_t=bs_t))
            wo = w_out['w_out_gm']
        else:
            p3 = _mm_nn(h, gw['w_in_cf'], 3, "cf_in")
            y1 = _cf_conv_fwd(p3, sp['cf_dw'], sp['cf_dw_b'], "cf_conv_fwd")
            g = _cf_gate_fwd(y1, p3, sp['cf_ln_g'], sp['cf_ln_b'], "cf_gate_fwd")
            saved.append(dict(h=h, p3=p3, y1=y1, g=g))
            wo = w_out['w_out_cf']
        yo = _mm_nn(g, wo, 1, f"out_proj_{i}")
        saved[-1]['yo'] = yo
        xs.append(_rms_fwd(yo, 0, 0, D, norm_post[i], f"post_norm_{i}", F32, res=xi))

    dx, loss_local = _loss_head(xs[4], target, "loss_head")
    loss = lax.psum(loss_local, ("x", "y", "c"))

    big_grads = {}
    sgrad = {}
    d_npre, d_npost = [None] * 4, [None] * 4
    for i in (3, 2, 1, 0):
        sv = saved[i]
        h = sv['h']
        dyo, d_npost[i] = _rms_bwd(sv['yo'], 0, 0, D, norm_post[i], dx, f"post_norm_bwd_{i}", BF16)
        dyo3 = dyo[None]
        wo_name = ['w_out_mla', 'w_out_sc', 'w_out_gm', 'w_out_cf'][i]
        dg = _mm_nt(dyo3, w_out[wo_name], f"out_proj_dx_{i}")
        big_grads[wo_name] = _mm_tn(sv['g'], dyo3, 1, f"out_proj_dw_{i}").reshape(N_CHIPS, -1, D)
        if i == 3:
            dz, dy1, sgrad['cf_ln_g'], sgrad['cf_ln_b'] = _cf_gate_bwd(sv['y1'], sv['p3'], sp['cf_ln_g'], sp['cf_ln_b'], dg, "cf_gate_bwd")
            dp3, sgrad['cf_dw'], sgrad['cf_dw_b'] = _cf_conv_bwd(sv['p3'], sp['cf_dw'], dy1, dz, "cf_conv_bwd")
            big_grads['w_in_cf'] = _mm_tn(h, dp3, N_CHIPS, "cf_in_dw")
            dh = _mm_nt(dp3, gw['w_in_cf'], "cf_in_dx")
        elif i == 2:
            dp3, sgrad['gm_ln_g'], sgrad['gm_ln_b'], sgrad['gm_w_s'], dbs_t = _gm_bwd(
                sv['p3'], sp['gm_ln_g'], sp['gm_ln_b'], gm_w_s[0], sv['bs_t'], dg, "gm_mix_bwd")
            sgrad['gm_b_s'] = jnp.transpose(dbs_t[:, :GM_GROUPS])
            big_grads['w_in_gm'] = _mm_tn(h, dp3, N_CHIPS, "gm_in_dw")
            dh = _mm_nt(dp3, gw['w_in_gm'], "gm_in_dx")
        elif i == 1:
            dp3, sgrad['sc_conv'] = _sc_bwd(sv['p3'], sp['sc_conv'], dg, "sc_mix_bwd")
            big_grads['w_in_sc'] = _mm_tn(h, dp3, N_CHIPS, "sc_in_dw")
            dh = _mm_nt(dp3, gw['w_in_sc'], "sc_in_dx")
        else:
            do, dpz = _gate_bwd(dg, sv['o'], sv['pz'], "mla_gate_bwd")
            dkf, dv = _attn_bwd_kv(sv['qf'], sv['kf'], sv['vv'], do, sv['o'], sv['lse'], H, scale, "mla_attn_bwd_kv")
            dqf = _attn_bwd_q(sv['qf'], sv['kf'], sv['vv'], do, sv['o'], sv['lse'], H, scale, "mla_attn_bwd_q")
            dq3, dkv3, dkr = _qkv_layout_bwd(dqf, dkf, dv, tabs, H, "mla_qkv_layout_bwd")
            dqn = _mm_nt(dq3, wq, "mla_q_up_dx")
            dwq = _mm_tn(sv['qn'], dq3, 1, "mla_q_up_dw")
            dkvn = _mm_nt(dkv3, wkv, "mla_kv_up_dx")
            dwkv = _mm_tn(sv['kvn'], dkv3, 1, "mla_kv_up_dw")
            dcq, dqg = _rms_bwd(sv['pa'], 0, 0, q_rank, mla_q_norm[0], dqn, "mla_q_norm_bwd", BF16)
            dckv, dkvg = _rms_bwd(sv['pa'], 0, q_rank // kv_rank, kv_rank, mla_kv_norm[0], dkvn, "mla_kv_norm_bwd", BF16)
            sgrad['mla_q_norm'], sgrad['mla_kv_norm'] = dqg, dkvg
            dpa = jnp.concatenate([dcq, dckv, dkr], axis=1)[None]
            dwa = _mm_tn(h, dpa, 1, "mla_in_a_dw")
            dwz = _mm_tn(h, dpz, 1, "mla_in_z_dw")
            dh_a = _mm_nt(dpa, w_a, "mla_in_a_dx")
            dh = _mm_nt(dpz, w_z, "mla_in_z_dx", add=dh_a)
            dw_in = jnp.concatenate([dwa[0][:, :wa_cols], dwz[0]], axis=1)
            big_grads['w_in_mla'] = jnp.transpose(dw_in.reshape(D, N_CHIPS, -1), (1, 0, 2))
            dwq_ = dwq[0].reshape(q_rank, H, HEAD_PAD)[:, :, :NOPE_DIM + ROPE_DIM].reshape(q_rank, N_CHIPS, -1)
            big_grads['w_uq'] = jnp.transpose(dwq_, (1, 0, 2))
            big_grads['w_ukv'] = jnp.transpose(dwkv[0].reshape(kv_rank, N_CHIPS, -1), (1, 0, 2))
        dx, d_npre[i] = _rms_bwd(xs[i][None], 0, 0, D, norm_pre[i], dh, f"pre_norm_bwd_{i}", F32, res=dx)
    grad_x = dx.reshape(1, T, D)
    sgrad['norm_pre'] = jnp.concatenate(d_npre, axis=0)
    sgrad['norm_post'] = jnp.concatenate(d_npost, axis=0)

    c = lax.axis_index("c")
    full = [big_grads[n] for n in BIG]
    recv = _core_exchange_halves(full, "grads_core_exchange")
    pair = []
    for n, gfull, r in zip(BIG, full, recv):
        J, K, nn = gfull.shape
        kh = K // 2
        mine = lax.dynamic_slice_in_dim(gfull, c * kh, kh, axis=1)
        pair.append(_pair_sum(mine.reshape(J * kh, nn), r.reshape(J * kh, nn), f"pair_sum_{n}").reshape(J, kh, nn))
    got = _chip_scatter(pair, "grads_chip_scatter")
    halves = [_chip_sum(r, f"chip_sum_{n}") for n, r in zip(BIG, got)]
    joined = _core_join_halves(halves, "grads_core_join")
    grads = dict(zip(BIG, [j[None] for j in joined]))

    small_full_shapes = [sgrad[n].reshape(wts[n].shape[:-1] + (-1,)).shape for n in SMALL]
    gslots = _broadcast_all(_pack([sgrad[n] for n in SMALL]), "grads_small_exchange")
    gsum = _unpack(_slot_sum(gslots, "grads_small_sum"), small_full_shapes)
    for n, gs in zip(SMALL, gsum):
        if n in SMALL_SHARDED:
            per = wts[n].shape[-1]
            gs = lax.dynamic_slice_in_dim(gs, chip * per, per, axis=gs.ndim - 1)
        grads[n] = gs.reshape(wts[n].shape)

    delta, new_m, new_v = {}, {}, {}
    for n in BIG:
        shp = wts[n].shape
        two_d = (shp[1], shp[2])
        d_, m_, v_ = _adamw(wts[n].reshape(two_d), grads[n].reshape(two_d), mom_m[n].reshape(two_d),
                            mom_v[n].reshape(two_d), f"adamw_{n}")
        delta[n], new_m[n], new_v[n] = d_.reshape(shp), m_.reshape(shp), v_.reshape(shp)
    shapes = [wts[n].shape for n in SMALL]
    d_, m_, v_ = _adamw(_pack([wts[n] for n in SMALL]), _pack([grads[n] for n in SMALL]),
                        _pack([mom_m[n] for n in SMALL]), _pack([mom_v[n] for n in SMALL]), "adamw_small")
    for n, a, b, cc in zip(SMALL, _unpack(d_, shapes), _unpack(m_, shapes), _unpack(v_, shapes)):
        delta[n], new_m[n], new_v[n] = a, b, cc

    return (loss, grad_x, *[grads[n] for n in WEIGHTS], *[delta[n] for n in WEIGHTS],
            *[new_m[n] for n in WEIGHTS], *[new_v[n] for n in WEIGHTS])
```

```python
import functools
import math

import jax
import jax.numpy as jnp
from jax import lax
from jax.experimental import pallas as pl
from jax.experimental.pallas import tpu as pltpu

F32, BF16 = jnp.float32, jnp.bfloat16
S = jax.ShapeDtypeStruct
MESH_ID = pl.DeviceIdType.MESH

V7X_VMEM_BYTES = 64 * 1024 * 1024
VMEM_LIMIT = V7X_VMEM_BYTES - 8 * 1024 * 1024
LANES = 128
N_CHIPS = 4
N_DEV = 8

NORM_EPS = 1e-6
LN_EPS = 1e-5
ROPE_THETA = 10000.0
ROPE_DIM = 64
NOPE_DIM = 128
V_DIM = 128
HEAD_PAD = 256
GM_CHUNK = 128
GM_GROUPS = 8
NEG = -1e30

ADAM_LR, ADAM_B1, ADAM_B2, ADAM_EPS, ADAM_WD, ADAM_STEP = 0.001, 0.9, 0.999, 1e-08, 0.01, 10

FWD_PARAMS = ['x', 'positions', 'norm_pre', 'norm_post', 'w_in_mla', 'mla_q_norm', 'w_uq', 'mla_kv_norm', 'w_ukv',
              'w_out_mla', 'w_in_sc', 'sc_conv', 'w_out_sc', 'w_in_gm', 'gm_ln_g', 'gm_ln_b', 'gm_w_s', 'gm_b_s',
              'w_out_gm', 'w_in_cf', 'cf_dw', 'cf_dw_b', 'cf_ln_g', 'cf_ln_b', 'w_out_cf']
WEIGHTS = FWD_PARAMS[2:]
BIG = ['w_in_mla', 'w_uq', 'w_ukv', 'w_out_mla', 'w_in_sc', 'w_out_sc', 'w_in_gm', 'w_out_gm', 'w_in_cf', 'w_out_cf']
SMALL = [n for n in WEIGHTS if n not in BIG]
SMALL_SHARDED = ['sc_conv', 'gm_ln_g', 'gm_ln_b', 'cf_dw', 'cf_dw_b', 'cf_ln_g', 'cf_ln_b']


def _cparams(sem=None, **kw):
    return pltpu.CompilerParams(dimension_semantics=sem, vmem_limit_bytes=VMEM_LIMIT, **kw)


def _pick(dim, pref):
    if dim <= pref:
        return dim
    t = (pref // LANES) * LANES
    while t >= LANES and dim % t:
        t -= LANES
    if t >= min(pref, 512):
        return t
    return dim if (dim <= 2048 or t < LANES) else t


def _silu(x):
    return x * jax.nn.sigmoid(x)


def _dsilu(x):
    s = jax.nn.sigmoid(x)
    return s * (1.0 + x * (1.0 - s))


def _gelu(x):
    return 0.5 * x * (1.0 + lax.erf(x * (2.0 ** -0.5)))


def _dgelu(x):
    cdf = 0.5 * (1.0 + lax.erf(x * (2.0 ** -0.5)))
    return cdf + x * jnp.exp(-0.5 * x * x) * ((2.0 * math.pi) ** -0.5)


MM_ONE_DOT = 4096


def _contract_tile(dim, divisible_by, pref_when_split):
    return dim if dim <= MM_ONE_DOT and divisible_by % dim == 0 else _pick(divisible_by, pref_when_split)


def _mm_accumulate(step, nsteps, prod, o_ref, acc, init=None):
    if nsteps == 1:
        r = prod()
        if init is not None:
            r = r + init()
        o_ref[...] = r.astype(o_ref.dtype)
        return

    @pl.when(step == 0)
    def _():
        acc[...] = jnp.zeros_like(acc) if init is None else init()

    acc[...] += prod()

    @pl.when(step == nsteps - 1)
    def _():
        o_ref[...] = acc[...].astype(o_ref.dtype)


def _mm_nn(a, w, np_out, name, out_dtype=F32):
    M, K = a.shape
    J, _, n = w.shape
    N = J * n
    W = N // np_out
    tm, tn = _pick(M, 1024), _pick(math.gcd(W, n), 1024)
    tk = _contract_tile(K, K, 2048)
    nk = K // tk

    def body(*refs):
        a_ref, w_ref, o_ref = refs[:3]
        _mm_accumulate(pl.program_id(2), nk, lambda: jnp.dot(a_ref[...], w_ref[...], preferred_element_type=F32),
                       o_ref, refs[-1])

    return pl.pallas_call(
        body, name=name, grid=(M // tm, N // tn, nk),
        in_specs=[pl.BlockSpec((tm, tk), lambda i, j, k: (i, k)),
                  pl.BlockSpec((None, tk, tn), lambda i, j, k: (j // (n // tn), k, j % (n // tn)))],
        out_specs=pl.BlockSpec((None, tm, tn), lambda i, j, k: (j // (W // tn), i, j % (W // tn))),
        out_shape=S((np_out, M, W), out_dtype),
        scratch_shapes=[pltpu.VMEM((tm, tn), F32)] if nk > 1 else [],
        compiler_params=_cparams(("parallel", "parallel", "arbitrary")),
    )(a, w)


def _mm_nt(a3, w, name, add=None, out_dtype=F32):
    NP, M, W = a3.shape
    J, K, n = w.shape
    N = NP * W
    tm, to = _pick(M, 1024), _pick(K, 1024)
    tc = _contract_tile(N, math.gcd(W, n), 2048)
    nc = N // tc
    has_add = add is not None

    def body(*refs):
        a_ref, w_ref = refs[0], refs[1]
        o_ref = refs[3] if has_add else refs[2]
        _mm_accumulate(
            pl.program_id(2), nc,
            lambda: lax.dot_general(a_ref[...], w_ref[...], (((1,), (1,)), ((), ())), preferred_element_type=F32),
            o_ref, refs[-1], init=(lambda: refs[2][...].astype(F32)) if has_add else None)

    in_specs = [pl.BlockSpec((None, tm, tc), lambda i, j, c: (c // (W // tc), i, c % (W // tc))),
                pl.BlockSpec((None, to, tc), lambda i, j, c: (c // (n // tc), j, c % (n // tc)))]
    ops = [a3, w]
    if has_add:
        in_specs.append(pl.BlockSpec((tm, to), lambda i, j, c: (i, j)))
        ops.append(add)
    return pl.pallas_call(
        body, name=name, grid=(M // tm, K // to, nc),
        in_specs=in_specs,
        out_specs=pl.BlockSpec((tm, to), lambda i, j, c: (i, j)),
        out_shape=S((M, K), out_dtype),
        scratch_shapes=[pltpu.VMEM((tm, to), F32)] if nc > 1 else [],
        compiler_params=_cparams(("parallel", "parallel", "arbitrary")),
    )(*ops)


def _mm_tn(a, d3, j_out, name, out_dtype=F32):
    M, K = a.shape
    NP, _, W = d3.shape
    N = NP * W
    n = N // j_out
    to, tn = _pick(K, 1024), _pick(math.gcd(W, n), 1024)
    tmc = _contract_tile(M, M, 2048)
    nm = M // tmc

    def body(*refs):
        a_ref, d_ref, o_ref = refs[:3]
        _mm_accumulate(
            pl.program_id(2), nm,
            lambda: lax.dot_general(a_ref[...], d_ref[...], (((0,), (0,)), ((), ())), preferred_element_type=F32),
            o_ref, refs[-1])

    return pl.pallas_call(
        body, name=name, grid=(K // to, N // tn, nm),
        in_specs=[pl.BlockSpec((tmc, to), lambda i, j, m: (m, i)),
                  pl.BlockSpec((None, tmc, tn), lambda i, j, m: (j // (W // tn), m, j % (W // tn)))],
        out_specs=pl.BlockSpec((None, to, tn), lambda i, j, m: (j // (n // tn), i, j % (n // tn))),
        out_shape=S((j_out, K, n), out_dtype),
        scratch_shapes=[pltpu.VMEM((to, tn), F32)] if nm > 1 else [],
        compiler_params=_cparams(("parallel", "parallel", "arbitrary")),
    )(a, d3)


def _rms_fwd(x3, piece, col_blk, width, g, name, out_dtype, res=None):
    T = x3.shape[1]
    tr = _pick(T, 256)
    has_res = res is not None

    def body(*refs):
        x_ref, g_ref = refs[0], refs[1]
        o_ref = refs[-1]
        x = x_ref[...].astype(F32)
        y = x * lax.rsqrt(jnp.mean(x * x, axis=-1, keepdims=True) + NORM_EPS) * g_ref[...]
        if has_res:
            y = refs[2][...] + y
        o_ref[...] = y.astype(o_ref.dtype)

    in_specs = [pl.BlockSpec((None, tr, width), lambda i: (piece, i, col_blk)),
                pl.BlockSpec((1, width), lambda i: (0, 0))]
    ops = [x3, g.reshape(1, width)]
    if has_res:
        in_specs.append(pl.BlockSpec((tr, width), lambda i: (i, 0)))
        ops.append(res)
    return pl.pallas_call(
        body, name=name, grid=(T // tr,), in_specs=in_specs,
        out_specs=pl.BlockSpec((tr, width), lambda i: (i, 0)),
        out_shape=S((T, width), out_dtype),
        compiler_params=_cparams(("parallel",)),
    )(*ops)


def _rms_bwd(u3, piece, col_blk, width, g, dy, name, out_dtype, res=None):
    T = u3.shape[1]
    tr = _pick(T, 256)
    has_res = res is not None

    def body(*refs):
        u_ref, g_ref, dy_ref = refs[0], refs[1], refs[2]
        du_ref, dg_ref = refs[-2], refs[-1]
        i = pl.program_id(0)
        u = u_ref[...].astype(F32)
        dy_ = dy_ref[...].astype(F32)
        r = lax.rsqrt(jnp.mean(u * u, axis=-1, keepdims=True) + NORM_EPS)
        nrm = u * r
        gdy = g_ref[...] * dy_
        du = r * (gdy - nrm * jnp.mean(gdy * nrm, axis=-1, keepdims=True))
        if has_res:
            du = du + refs[3][...]
        du_ref[...] = du.astype(du_ref.dtype)

        @pl.when(i == 0)
        def _():
            dg_ref[...] = jnp.zeros_like(dg_ref)

        dg_ref[...] += jnp.sum(dy_ * nrm, axis=0, keepdims=True)

    in_specs = [pl.BlockSpec((None, tr, width), lambda i: (piece, i, col_blk)),
                pl.BlockSpec((1, width), lambda i: (0, 0)),
                pl.BlockSpec((tr, width), lambda i: (i, 0))]
    ops = [u3, g.reshape(1, width), dy]
    if has_res:
        in_specs.append(pl.BlockSpec((tr, width), lambda i: (i, 0)))
        ops.append(res)
    return pl.pallas_call(
        body, name=name, grid=(T // tr,), in_specs=in_specs,
        out_specs=[pl.BlockSpec((tr, width), lambda i: (i, 0)), pl.BlockSpec((1, width), lambda i: (0, 0))],
        out_shape=[S((T, width), out_dtype), S((1, width), F32)],
        compiler_params=_cparams(("arbitrary",)),
    )(*ops)


def _loss_head(xl, target, name):
    T, D = xl.shape
    tr = _pick(T, 256)

    def body(x_ref, t_ref, dx_ref, l_ref):
        i = pl.program_id(0)
        err = x_ref[...] - t_ref[...]
        dx_ref[...] = err * (1.0 / D)

        @pl.when(i == 0)
        def _():
            l_ref[...] = jnp.zeros_like(l_ref)

        l_ref[...] += jnp.sum(err * err)

    dx, l = pl.pallas_call(
        body, name=name, grid=(T // tr,),
        in_specs=[pl.BlockSpec((tr, D), lambda i: (i, 0)), pl.BlockSpec((tr, D), lambda i: (i, 0))],
        out_specs=[pl.BlockSpec((tr, D), lambda i: (i, 0)), pl.BlockSpec((8, LANES), lambda i: (0, 0))],
        out_shape=[S((T, D), F32), S((8, LANES), F32)],
        compiler_params=_cparams(("arbitrary",)),
    )(xl, target)
    return dx, l[0, 0] * (0.5 / D)


def _gate_fwd(o, z3, name):
    T, W = o.shape
    tr = _pick(T, 256)

    def body(o_ref, z_ref, g_ref):
        g_ref[...] = (o_ref[...] * _silu(z_ref[...])).astype(g_ref.dtype)

    return pl.pallas_call(
        body, name=name, grid=(T // tr,),
        in_specs=[pl.BlockSpec((tr, W), lambda i: (i, 0)), pl.BlockSpec((None, tr, W), lambda i: (0, i, 0))],
        out_specs=pl.BlockSpec((tr, W), lambda i: (i, 0)),
        out_shape=S((T, W), BF16), compiler_params=_cparams(("parallel",)),
    )(o, z3)


def _gate_bwd(dg, o, z3, name):
    T, W = o.shape
    tr = _pick(T, 256)

    def body(dg_ref, o_ref, z_ref, do_ref, dz_ref):
        dg_, z = dg_ref[...], z_ref[...]
        do_ref[...] = (dg_ * _silu(z)).astype(do_ref.dtype)
        dz_ref[...] = (dg_ * o_ref[...] * _dsilu(z)).astype(dz_ref.dtype)

    return pl.pallas_call(
        body, name=name, grid=(T // tr,),
        in_specs=[pl.BlockSpec((tr, W), lambda i: (i, 0)), pl.BlockSpec((tr, W), lambda i: (i, 0)),
                  pl.BlockSpec((None, tr, W), lambda i: (0, i, 0))],
        out_specs=[pl.BlockSpec((tr, W), lambda i: (i, 0)), pl.BlockSpec((None, tr, W), lambda i: (0, i, 0))],
        out_shape=[S((T, W), BF16), S((1, T, W), BF16)], compiler_params=_cparams(("parallel",)),
    )(dg, o, z3)


def _rope_tables(pos_col, invf, name):
    T = pos_col.shape[0]
    tr = _pick(T, 512)
    half = ROPE_DIM // 2

    def body(p_ref, f_ref, c_ref, sa_ref, sb_ref):
        ang = p_ref[...].astype(F32) * f_ref[...]
        lane = lax.broadcasted_iota(jnp.int32, ang.shape, 1)
        cs, sn = jnp.cos(ang), jnp.sin(ang)
        c_ref[...] = jnp.where(lane < ROPE_DIM, cs, 0.0)
        sa_ref[...] = jnp.where(lane < half, -sn, 0.0)
        sb_ref[...] = jnp.where((lane >= half) & (lane < ROPE_DIM), sn, 0.0)

    spec = pl.BlockSpec((tr, LANES), lambda i: (i, 0))
    return pl.pallas_call(
        body, name=name, grid=(T // tr,),
        in_specs=[pl.BlockSpec((tr, 1), lambda i: (i, 0)), pl.BlockSpec((1, LANES), lambda i: (0, 0))],
        out_specs=[spec, spec, spec], out_shape=[S((T, LANES), F32)] * 3,
        compiler_params=_cparams(("parallel",)),
    )(pos_col, invf)


def _rope(t, c, sa, sb):
    half = ROPE_DIM // 2
    return t * c + pltpu.roll(t, LANES - half, 1) * sa + pltpu.roll(t, half, 1) * sb


def _rope_t(d, c, sa, sb):
    half = ROPE_DIM // 2
    return d * c + pltpu.roll(d * sa, half, 1) + pltpu.roll(d * sb, LANES - half, 1)


def _qkv_layout(q3, kv3, pa3, kr_blk, tabs, H, name):
    T = q3.shape[1]
    tr = _pick(T, 512)

    def body(q_ref, kv_ref, kr_ref, c_ref, sa_ref, sb_ref, qf_ref, kf_ref, v_ref):
        c, sa, sb = c_ref[...], sa_ref[...], sb_ref[...]
        qf_ref[:, :NOPE_DIM] = q_ref[:, :NOPE_DIM].astype(BF16)
        qf_ref[:, NOPE_DIM:] = _rope(q_ref[:, NOPE_DIM:], c, sa, sb).astype(BF16)
        kf_ref[:, :NOPE_DIM] = kv_ref[:, :NOPE_DIM].astype(BF16)
        kf_ref[:, NOPE_DIM:] = _rope(kr_ref[...], c, sa, sb).astype(BF16)
        v_ref[...] = kv_ref[:, NOPE_DIM:].astype(BF16)

    tab = pl.BlockSpec((tr, LANES), lambda i, h: (i, 0))
    hp = pl.BlockSpec((None, tr, HEAD_PAD), lambda i, h: (0, i, h))
    return pl.pallas_call(
        body, name=name, grid=(T // tr, H),
        in_specs=[hp, hp, pl.BlockSpec((None, tr, LANES), lambda i, h: (0, i, kr_blk)), tab, tab, tab],
        out_specs=[pl.BlockSpec((tr, HEAD_PAD), lambda i, h: (i, h)), pl.BlockSpec((tr, HEAD_PAD), lambda i, h: (i, h)),
                   pl.BlockSpec((tr, V_DIM), lambda i, h: (i, h))],
        out_shape=[S((T, H * HEAD_PAD), BF16), S((T, H * HEAD_PAD), BF16), S((T, H * V_DIM), BF16)],
        compiler_params=_cparams(("parallel", "arbitrary")),
    )(q3, kv3, pa3, *tabs)


def _qkv_layout_bwd(dqf, dkf, dv, tabs, H, name):
    T = dqf.shape[0]
    tr = _pick(T, 512)

    def body(dqf_ref, dkf_ref, dv_ref, c_ref, sa_ref, sb_ref, dq_ref, dkv_ref, dkr_ref, acc):
        h = pl.program_id(1)
        c, sa, sb = c_ref[...], sa_ref[...], sb_ref[...]
        dq_ref[:, :NOPE_DIM] = dqf_ref[:, :NOPE_DIM].astype(BF16)
        dq_ref[:, NOPE_DIM:] = _rope_t(dqf_ref[:, NOPE_DIM:].astype(F32), c, sa, sb).astype(BF16)
        dkv_ref[:, :NOPE_DIM] = dkf_ref[:, :NOPE_DIM].astype(BF16)
        dkv_ref[:, NOPE_DIM:] = dv_ref[...].astype(BF16)

        @pl.when(h == 0)
        def _():
            acc[...] = jnp.zeros_like(acc)

        acc[...] += dkf_ref[:, NOPE_DIM:].astype(F32)

        @pl.when(h == H - 1)
        def _():
            dkr_ref[...] = _rope_t(acc[...], c, sa, sb).astype(BF16)

    tab = pl.BlockSpec((tr, LANES), lambda i, h: (i, 0))
    hp_in = pl.BlockSpec((tr, HEAD_PAD), lambda i, h: (i, h))
    hp_out = pl.BlockSpec((None, tr, HEAD_PAD), lambda i, h: (0, i, h))
    return pl.pallas_call(
        body, name=name, grid=(T // tr, H),
        in_specs=[hp_in, hp_in, pl.BlockSpec((tr, V_DIM), lambda i, h: (i, h)), tab, tab, tab],
        out_specs=[hp_out, hp_out, pl.BlockSpec((tr, LANES), lambda i, h: (i, 0))],
        out_shape=[S((1, T, H * HEAD_PAD), BF16), S((1, T, H * HEAD_PAD), BF16), S((T, LANES), BF16)],
        scratch_shapes=[pltpu.VMEM((tr, LANES), F32)],
        compiler_params=_cparams(("parallel", "arbitrary")),
    )(dqf, dkf, dv, *tabs)


ATTN_BLOCK = 512


def _nt(a, b):
    return lax.dot_general(a, b, (((1,), (1,)), ((), ())), preferred_element_type=F32)


def _tn(a, b):
    return lax.dot_general(a, b, (((0,), (0,)), ((), ())), preferred_element_type=F32)


def _causal_blocks(qi, tb, block):
    if qi > 0:
        def step(ki, carry):
            block(pl.multiple_of(ki * tb, tb), False)
            return carry
        lax.fori_loop(0, qi, step, 0)
    block(qi * tb, True)


def _attn_fwd(qf, kf, v, H, scale, name):
    T = qf.shape[0]
    tb = _pick(T, ATTN_BLOCK)
    nb = T // tb

    def body(q_ref, k_ref, v_ref, o_ref, lse_ref, m_s, l_s, acc):
        row = lax.broadcasted_iota(jnp.int32, (tb, tb), 0)
        col = lax.broadcasted_iota(jnp.int32, (tb, tb), 1)
        for qi in range(nb):
            rows = pl.ds(qi * tb, tb)
            m_s[...] = jnp.full_like(m_s, NEG)
            l_s[...] = jnp.zeros_like(l_s)
            acc[...] = jnp.zeros_like(acc)

            def block(k0, masked, rows=rows):
                s = _nt(q_ref[rows, :], k_ref[pl.ds(k0, tb), :])
                if masked:
                    s = jnp.where(col <= row, s, NEG)
                m_prev = m_s[...]
                m_new = jnp.maximum(m_prev, jnp.max(s, axis=-1, keepdims=True))
                p = jnp.exp((s - m_new) * scale)
                alpha = jnp.exp((m_prev - m_new) * scale)
                l_s[...] = alpha * l_s[...] + jnp.sum(p, axis=-1, keepdims=True)
                acc[...] = alpha * acc[...] + jnp.dot(p.astype(BF16), v_ref[pl.ds(k0, tb), :], preferred_element_type=F32)
                m_s[...] = m_new

            _causal_blocks(qi, tb, block)
            o_ref[rows, :] = acc[...] / l_s[...]
            lse_ref[rows, :] = jnp.broadcast_to(m_s[...] * scale + jnp.log(l_s[...]), (tb, LANES))

    return pl.pallas_call(
        body, name=name, grid=(H,),
        in_specs=[pl.BlockSpec((T, HEAD_PAD), lambda h: (0, h)), pl.BlockSpec((T, HEAD_PAD), lambda h: (0, h)),
                  pl.BlockSpec((T, V_DIM), lambda h: (0, h))],
        out_specs=[pl.BlockSpec((T, V_DIM), lambda h: (0, h)), pl.BlockSpec((T, LANES), lambda h: (0, h))],
        out_shape=[S((T, H * V_DIM), F32), S((T, H * LANES), F32)],
        scratch_shapes=[pltpu.VMEM((tb, 1), F32), pltpu.VMEM((tb, 1), F32), pltpu.VMEM((tb, V_DIM), F32)],
        compiler_params=_cparams(("parallel",)),
    )(qf, kf, v)


def _attn_bwd(qf, kf, v, do, o, lse, H, scale, name):
    T = qf.shape[0]
    tb = _pick(T, ATTN_BLOCK)
    nb = T // tb

    def body(q_ref, k_ref, v_ref, do_ref, o_ref, lse_ref, dq_ref, dk_ref, dv_ref, dq_acc, dk_acc, dv_acc):
        row = lax.broadcasted_iota(jnp.int32, (tb, tb), 0)
        col = lax.broadcasted_iota(jnp.int32, (tb, tb), 1)
        dk_acc[...] = jnp.zeros_like(dk_acc)
        dv_acc[...] = jnp.zeros_like(dv_acc)
        for qi in range(nb):
            rows = pl.ds(qi * tb, tb)
            dq_acc[...] = jnp.zeros_like(dq_acc)
            delta = jnp.sum(do_ref[rows, :].astype(F32) * o_ref[rows, :], axis=-1, keepdims=True)
            lse_q = lse_ref[rows, 0:1]

            def block(k0, masked, rows=rows, delta=delta, lse_q=lse_q):
                keys = pl.ds(k0, tb)
                q, k, do_ = q_ref[rows, :], k_ref[keys, :], do_ref[rows, :]
                s = _nt(q, k)
                if masked:
                    s = jnp.where(col <= row, s, NEG)
                p = jnp.exp(s * scale - lse_q)
                dp = _nt(do_, v_ref[keys, :])
                ds = (p * (dp - delta) * scale).astype(BF16)
                dv_acc[keys, :] += _tn(p.astype(BF16), do_)
                dk_acc[keys, :] += _tn(ds, q)
                dq_acc[...] += jnp.dot(ds, k, preferred_element_type=F32)

            _causal_blocks(qi, tb, block)
            dq_ref[rows, :] = dq_acc[...].astype(dq_ref.dtype)
        dk_ref[...] = dk_acc[...].astype(dk_ref.dtype)
        dv_ref[...] = dv_acc[...].astype(dv_ref.dtype)

    hp = pl.BlockSpec((T, HEAD_PAD), lambda h: (0, h))
    hv = pl.BlockSpec((T, V_DIM), lambda h: (0, h))
    return pl.pallas_call(
        body, name=name, grid=(H,),
        in_specs=[hp, hp, hv, hv, hv, pl.BlockSpec((T, LANES), lambda h: (0, h))],
        out_specs=[hp, hp, hv],
        out_shape=[S((T, H * HEAD_PAD), BF16), S((T, H * HEAD_PAD), BF16), S((T, H * V_DIM), BF16)],
        scratch_shapes=[pltpu.VMEM((tb, HEAD_PAD), F32), pltpu.VMEM((T, HEAD_PAD), F32), pltpu.VMEM((T, V_DIM), F32)],
        compiler_params=_cparams(("parallel",)),
    )(qf, kf, v, do, o, lse)


CONV_ROWS = 256
CONV_COLS = 128


def _conv_chunks(T):
    rc = min(CONV_ROWS, T)
    return [(r, rc) for r in range(0, T, rc)]


def _causal_conv(pad_ref, lead, w_ref, width, r0, rc):
    acc = None
    for k in range(width):
        term = w_ref[k:k + 1, :] * pad_ref[pl.ds(lead + r0 - (width - 1) + k, rc), :]
        acc = term if acc is None else acc + term
    return acc


def _anticausal_conv(pad_ref, w_ref, width, r0, rc):
    acc = None
    for k in range(width):
        term = w_ref[k:k + 1, :] * pad_ref[pl.ds(r0 + (width - 1) - k, rc), :]
        acc = term if acc is None else acc + term
    return acc


def _conv_wgrad(dpad_ref, xpad_ref, lead, width, T, dw_ref):
    for k in range(width):
        tot = None
        for r0, rc in _conv_chunks(T):
            part = jnp.sum(dpad_ref[pl.ds(r0, rc), :] * xpad_ref[pl.ds(lead + r0 - (width - 1) + k, rc), :],
                           axis=0, keepdims=True)
            tot = part if tot is None else tot + part
        dw_ref[k:k + 1, :] = tot


def _sc_fwd(p3, wconv, name):
    _, T, W = p3.shape
    width = wconv.shape[0]
    cw = min(CONV_COLS, W)
    lead = 8

    def body(p_ref, w_ref, g_ref, pad):
        pad[0:lead, :] = jnp.zeros((lead, cw), F32)
        for r0, rc in _conv_chunks(T):
            pad[pl.ds(lead + r0, rc), :] = p_ref[1, pl.ds(r0, rc), :] * p_ref[2, pl.ds(r0, rc), :]
        for r0, rc in _conv_chunks(T):
            rows = pl.ds(r0, rc)
            y = p_ref[0, rows, :] * _causal_conv(pad, lead, w_ref, width, r0, rc)
            g_ref[rows, :] = (y * _silu(p_ref[3, rows, :])).astype(g_ref.dtype)

    return pl.pallas_call(
        body, name=name, grid=(W // cw,),
        in_specs=[pl.BlockSpec((4, T, cw), lambda j: (0, 0, j)), pl.BlockSpec((width, cw), lambda j: (0, j))],
        out_specs=pl.BlockSpec((T, cw), lambda j: (0, j)),
        out_shape=S((T, W), BF16),
        scratch_shapes=[pltpu.VMEM((T + lead, cw), F32)],
        compiler_params=_cparams(("parallel",)),
    )(p3, wconv)


def _sc_bwd(p3, wconv, dg, name):
    _, T, W = p3.shape
    width = wconv.shape[0]
    cw = min(CONV_COLS, W)
    lead = 8

    def body(p_ref, w_ref, dg_ref, dp_ref, dw_ref, cupad, dvpad):
        cupad[0:lead, :] = jnp.zeros((lead, cw), F32)
        dvpad[pl.ds(T, lead), :] = jnp.zeros((lead, cw), F32)
        for r0, rc in _conv_chunks(T):
            cupad[pl.ds(lead + r0, rc), :] = p_ref[1, pl.ds(r0, rc), :] * p_ref[2, pl.ds(r0, rc), :]
        for r0, rc in _conv_chunks(T):
            rows = pl.ds(r0, rc)
            b, z, dg_ = p_ref[0, rows, :], p_ref[3, rows, :], dg_ref[rows, :]
            v = _causal_conv(cupad, lead, w_ref, width, r0, rc)
            dy = dg_ * _silu(z)
            dp_ref[3, rows, :] = (dg_ * b * v * _dsilu(z)).astype(dp_ref.dtype)
            dp_ref[0, rows, :] = (dy * v).astype(dp_ref.dtype)
            dvpad[rows, :] = dy * b
        for r0, rc in _conv_chunks(T):
            rows = pl.ds(r0, rc)
            dcu = _anticausal_conv(dvpad, w_ref, width, r0, rc)
            dp_ref[1, rows, :] = (dcu * p_ref[2, rows, :]).astype(dp_ref.dtype)
            dp_ref[2, rows, :] = (dcu * p_ref[1, rows, :]).astype(dp_ref.dtype)
        _conv_wgrad(dvpad, cupad, lead, width, T, dw_ref)

    return pl.pallas_call(
        body, name=name, grid=(W // cw,),
        in_specs=[pl.BlockSpec((4, T, cw), lambda j: (0, 0, j)), pl.BlockSpec((width, cw), lambda j: (0, j)),
                  pl.BlockSpec((T, cw), lambda j: (0, j))],
        out_specs=[pl.BlockSpec((4, T, cw), lambda j: (0, 0, j)), pl.BlockSpec((width, cw), lambda j: (0, j))],
        out_shape=[S((4, T, W), BF16), S((width, W), F32)],
        scratch_shapes=[pltpu.VMEM((T + lead, cw), F32), pltpu.VMEM((T + lead, cw), F32)],
        compiler_params=_cparams(("parallel",)),
    )(p3, wconv, dg)


def _gm_common(p_ref, lng_ref, lnb_ref):
    ug = _gelu(p_ref[0])
    vg = _gelu(p_ref[1])
    mu = jnp.mean(vg, axis=-1, keepdims=True)
    xc = vg - mu
    rstd = lax.rsqrt(jnp.mean(xc * xc, axis=-1, keepdims=True) + LN_EPS)
    xhat = xc * rstd
    vn = xhat * lng_ref[...] + lnb_ref[...]
    return ug, xhat, rstd, vn


def _gm_mix_weights(ws_ref, g):
    row = lax.broadcasted_iota(jnp.int32, (GM_CHUNK, GM_CHUNK), 0)
    col = lax.broadcasted_iota(jnp.int32, (GM_CHUNK, GM_CHUNK), 1)
    return jnp.where(col <= row, ws_ref[g], 0.0).astype(BF16)


def _gm_fwd(p3, lng, lnb, ws, bs_t, name):
    _, T, W = p3.shape
    gw = W // GM_GROUPS

    def body(p_ref, lng_ref, lnb_ref, ws_ref, bs_ref, g_ref):
        ug, _, _, vn = _gm_common(p_ref, lng_ref, lnb_ref)
        sz = _silu(p_ref[2])
        vnb = vn.astype(BF16)
        for g in range(GM_GROUPS):
            cols = slice(g * gw, (g + 1) * gw)
            s = jnp.dot(_gm_mix_weights(ws_ref, g), vnb[:, cols], preferred_element_type=F32) + bs_ref[:, g:g + 1]
            g_ref[:, cols] = (ug[:, cols] * s * sz[:, cols]).astype(g_ref.dtype)

    return pl.pallas_call(
        body, name=name, grid=(T // GM_CHUNK,),
        in_specs=[pl.BlockSpec((3, GM_CHUNK, W), lambda i: (0, i, 0)), pl.BlockSpec((1, W), lambda i: (0, 0)),
                  pl.BlockSpec((1, W), lambda i: (0, 0)),
                  pl.BlockSpec((GM_GROUPS, GM_CHUNK, GM_CHUNK), lambda i: (0, 0, 0)),
                  pl.BlockSpec((GM_CHUNK, GM_GROUPS), lambda i: (0, 0))],
        out_specs=pl.BlockSpec((GM_CHUNK, W), lambda i: (i, 0)),
        out_shape=S((T, W), BF16), compiler_params=_cparams(("parallel",)),
    )(p3, lng.reshape(1, W), lnb.reshape(1, W), ws, bs_t)


def _gm_bwd(p3, lng, lnb, ws, bs_t, dg, name):
    _, T, W = p3.shape
    gw = W // GM_GROUPS

    def body(p_ref, lng_ref, lnb_ref, ws_ref, bs_ref, dg_ref, dp_ref, dlng_ref, dlnb_ref, dws_ref, dbs_ref, dvn_s):
        i = pl.program_id(0)

        @pl.when(i == 0)
        def _():
            dlng_ref[...] = jnp.zeros_like(dlng_ref)
            dlnb_ref[...] = jnp.zeros_like(dlnb_ref)
            dws_ref[...] = jnp.zeros_like(dws_ref)
            dbs_ref[...] = jnp.zeros_like(dbs_ref)

        ug, xhat, rstd, vn = _gm_common(p_ref, lng_ref, lnb_ref)
        z = p_ref[2]
        dg_ = dg_ref[...]
        dy = dg_ * _silu(z)
        vnb = vn.astype(BF16)
        row = lax.broadcasted_iota(jnp.int32, (GM_CHUNK, GM_CHUNK), 0)
        col = lax.broadcasted_iota(jnp.int32, (GM_CHUNK, GM_CHUNK), 1)
        dbs = jnp.zeros((GM_CHUNK, LANES), F32)
        for g in range(GM_GROUPS):
            cols = slice(g * gw, (g + 1) * gw)
            wm = _gm_mix_weights(ws_ref, g)
            s = jnp.dot(wm, vnb[:, cols], preferred_element_type=F32) + bs_ref[:, g:g + 1]
            dp_ref[2, :, cols] = (dg_[:, cols] * ug[:, cols] * s * _dsilu(z[:, cols])).astype(dp_ref.dtype)
            dp_ref[0, :, cols] = (dy[:, cols] * s * _dgelu(p_ref[0, :, cols])).astype(dp_ref.dtype)
            ds = dy[:, cols] * ug[:, cols]
            dsb = ds.astype(BF16)
            dwm = lax.dot_general(dsb, vnb[:, cols], (((1,), (1,)), ((), ())), preferred_element_type=F32)
            dws_ref[g] += jnp.where(col <= row, dwm, 0.0)
            dbs = dbs + jnp.where(col == g, jnp.sum(ds, axis=-1, keepdims=True), 0.0)
            dvn_s[:, cols] = lax.dot_general(wm, dsb, (((0,), (0,)), ((), ())), preferred_element_type=F32)
        dbs_ref[...] += dbs
        dvn = dvn_s[...]
        dlng_ref[...] += jnp.sum(dvn * xhat, axis=0, keepdims=True)
        dlnb_ref[...] += jnp.sum(dvn, axis=0, keepdims=True)
        dxh = dvn * lng_ref[...]
        dvg = rstd * (dxh - jnp.mean(dxh, axis=-1, keepdims=True) - xhat * jnp.mean(dxh * xhat, axis=-1, keepdims=True))
        dp_ref[1] = (dvg * _dgelu(p_ref[1])).astype(dp_ref.dtype)

    row1 = pl.BlockSpec((1, W), lambda i: (0, 0))
    return pl.pallas_call(
        body, name=name, grid=(T // GM_CHUNK,),
        in_specs=[pl.BlockSpec((3, GM_CHUNK, W), lambda i: (0, i, 0)), row1, row1,
                  pl.BlockSpec((GM_GROUPS, GM_CHUNK, GM_CHUNK), lambda i: (0, 0, 0)),
                  pl.BlockSpec((GM_CHUNK, GM_GROUPS), lambda i: (0, 0)),
                  pl.BlockSpec((GM_CHUNK, W), lambda i: (i, 0))],
        out_specs=[pl.BlockSpec((3, GM_CHUNK, W), lambda i: (0, i, 0)), row1, row1,
                   pl.BlockSpec((GM_GROUPS, GM_CHUNK, GM_CHUNK), lambda i: (0, 0, 0)),
                   pl.BlockSpec((GM_CHUNK, LANES), lambda i: (0, 0))],
        out_shape=[S((3, T, W), BF16), S((1, W), F32), S((1, W), F32),
                   S((GM_GROUPS, GM_CHUNK, GM_CHUNK), F32), S((GM_CHUNK, LANES), F32)],
        scratch_shapes=[pltpu.VMEM((GM_CHUNK, W), F32)],
        compiler_params=_cparams(("arbitrary",)),
    )(p3, lng.reshape(1, W), lnb.reshape(1, W), ws, bs_t, dg)


def _cf_conv_fwd(p3, wdw, bdw, name):
    _, T, W = p3.shape
    width = wdw.shape[0]
    cw = min(CONV_COLS, W)
    lead = 32

    def body(p_ref, w_ref, b_ref, y_ref, pad):
        pad[0:lead, :] = jnp.zeros((lead, cw), F32)
        for r0, rc in _conv_chunks(T):
            rows = pl.ds(r0, rc)
            pad[pl.ds(lead + r0, rc), :] = p_ref[0, rows, :] * jax.nn.sigmoid(p_ref[1, rows, :])
        for r0, rc in _conv_chunks(T):
            y_ref[pl.ds(r0, rc), :] = _causal_conv(pad, lead, w_ref, width, r0, rc) + b_ref[...]

    return pl.pallas_call(
        body, name=name, grid=(W // cw,),
        in_specs=[pl.BlockSpec((2, T, cw), lambda j: (0, 0, j)), pl.BlockSpec((width, cw), lambda j: (0, j)),
                  pl.BlockSpec((1, cw), lambda j: (0, j))],
        out_specs=pl.BlockSpec((T, cw), lambda j: (0, j)),
        out_shape=S((T, W), F32),
        scratch_shapes=[pltpu.VMEM((T + lead, cw), F32)],
        compiler_params=_cparams(("parallel",)),
    )(p3, wdw, bdw.reshape(1, W))


def _cf_ln(y1_ref, lng_ref, lnb_ref):
    y1 = y1_ref[...]
    mu = jnp.mean(y1, axis=-1, keepdims=True)
    xc = y1 - mu
    rstd = lax.rsqrt(jnp.mean(xc * xc, axis=-1, keepdims=True) + LN_EPS)
    xhat = xc * rstd
    return xhat, rstd, xhat * lng_ref[...] + lnb_ref[...]


def _cf_gate_fwd(y1, p3, lng, lnb, name):
    T, W = y1.shape
    tr = _pick(T, 256)

    def body(y1_ref, z_ref, lng_ref, lnb_ref, g_ref):
        _, _, y2 = _cf_ln(y1_ref, lng_ref, lnb_ref)
        g_ref[...] = (_silu(y2) * _silu(z_ref[...])).astype(g_ref.dtype)

    row1 = pl.BlockSpec((1, W), lambda i: (0, 0))
    return pl.pallas_call(
        body, name=name, grid=(T // tr,),
        in_specs=[pl.BlockSpec((tr, W), lambda i: (i, 0)), pl.BlockSpec((None, tr, W), lambda i: (2, i, 0)), row1, row1],
        out_specs=pl.BlockSpec((tr, W), lambda i: (i, 0)),
        out_shape=S((T, W), BF16), compiler_params=_cparams(("parallel",)),
    )(y1, p3, lng.reshape(1, W), lnb.reshape(1, W))


def _cf_gate_bwd(y1, p3, lng, lnb, dg, name):
    T, W = y1.shape
    tr = _pick(T, 128)

    def body(y1_ref, z_ref, lng_ref, lnb_ref, dg_ref, dz_ref, dy1_ref, dlng_ref, dlnb_ref):
        i = pl.program_id(0)

        @pl.when(i == 0)
        def _():
            dlng_ref[...] = jnp.zeros_like(dlng_ref)
            dlnb_ref[...] = jnp.zeros_like(dlnb_ref)

        xhat, rstd, y2 = _cf_ln(y1_ref, lng_ref, lnb_ref)
        z, dg_ = z_ref[...], dg_ref[...]
        dz_ref[...] = (dg_ * _silu(y2) * _dsilu(z)).astype(dz_ref.dtype)
        dy2 = dg_ * _silu(z) * _dsilu(y2)
        dlng_ref[...] += jnp.sum(dy2 * xhat, axis=0, keepdims=True)
        dlnb_ref[...] += jnp.sum(dy2, axis=0, keepdims=True)
        dxh = dy2 * lng_ref[...]
        dy1_ref[...] = rstd * (dxh - jnp.mean(dxh, axis=-1, keepdims=True)
                               - xhat * jnp.mean(dxh * xhat, axis=-1, keepdims=True))

    row1 = pl.BlockSpec((1, W), lambda i: (0, 0))
    blk = pl.BlockSpec((tr, W), lambda i: (i, 0))
    return pl.pallas_call(
        body, name=name, grid=(T // tr,),
        in_specs=[blk, pl.BlockSpec((None, tr, W), lambda i: (2, i, 0)), row1, row1, blk],
        out_specs=[blk, blk, row1, row1],
        out_shape=[S((T, W), BF16), S((T, W), F32), S((1, W), F32), S((1, W), F32)],
        compiler_params=_cparams(("arbitrary",)),
    )(y1, p3, lng.reshape(1, W), lnb.reshape(1, W), dg)


def _cf_conv_bwd(p3, wdw, dy1, dz, name):
    _, T, W = p3.shape
    width = wdw.shape[0]
    cw = min(CONV_COLS, W)
    lead = 32

    def body(p_ref, w_ref, dy1_ref, dz_ref, dp_ref, dw_ref, db_ref, y0pad, dpad):
        y0pad[0:lead, :] = jnp.zeros((lead, cw), F32)
        dpad[pl.ds(T, lead), :] = jnp.zeros((lead, cw), F32)
        bsum = None
        for r0, rc in _conv_chunks(T):
            rows = pl.ds(r0, rc)
            y0pad[pl.ds(lead + r0, rc), :] = p_ref[0, rows, :] * jax.nn.sigmoid(p_ref[1, rows, :])
            d = dy1_ref[rows, :]
            dpad[rows, :] = d
            part = jnp.sum(d, axis=0, keepdims=True)
            bsum = part if bsum is None else bsum + part
        db_ref[...] = bsum
        for r0, rc in _conv_chunks(T):
            rows = pl.ds(r0, rc)
            dy0 = _anticausal_conv(dpad, w_ref, width, r0, rc)
            a = p_ref[0, rows, :]
            sg = jax.nn.sigmoid(p_ref[1, rows, :])
            dp_ref[0, rows, :] = (dy0 * sg).astype(dp_ref.dtype)
            dp_ref[1, rows, :] = (dy0 * a * sg * (1.0 - sg)).astype(dp_ref.dtype)
            dp_ref[2, rows, :] = dz_ref[rows, :]
        _conv_wgrad(dpad, y0pad, lead, width, T, dw_ref)

    return pl.pallas_call(
        body, name=name, grid=(W // cw,),
        in_specs=[pl.BlockSpec((2, T, cw), lambda j: (0, 0, j)), pl.BlockSpec((width, cw), lambda j: (0, j)),
                  pl.BlockSpec((T, cw), lambda j: (0, j)), pl.BlockSpec((T, cw), lambda j: (0, j))],
        out_specs=[pl.BlockSpec((3, T, cw), lambda j: (0, 0, j)), pl.BlockSpec((width, cw), lambda j: (0, j)),
                   pl.BlockSpec((1, cw), lambda j: (0, j))],
        out_shape=[S((3, T, W), BF16), S((width, W), F32), S((1, W), F32)],
        scratch_shapes=[pltpu.VMEM((T + lead, cw), F32), pltpu.VMEM((T + lead, cw), F32)],
        compiler_params=_cparams(("parallel",)),
    )(p3, wdw, dy1, dz)


def _rows_call(body, ins, out_dtypes, name, row_pref=256):
    R, C = ins[0].shape
    tr = _pick(R, row_pref) if R % 8 == 0 else R
    while tr > 8 and tr * C * 4 * (len(ins) + len(out_dtypes)) * 2 > VMEM_LIMIT // 2 and tr % 16 == 0:
        tr //= 2
    blk = pl.BlockSpec((tr, C), lambda i: (i, 0))
    return pl.pallas_call(
        body, name=name, grid=(R // tr,), in_specs=[blk] * len(ins), out_specs=[blk] * len(out_dtypes),
        out_shape=[S((R, C), dt) for dt in out_dtypes], compiler_params=_cparams(("parallel",)),
    )(*ins)


def _pair_sum(g_half, r, name):
    def body(a_ref, b_ref, o_ref):
        o_ref[...] = (a_ref[...] + b_ref[...]).astype(BF16)
    return _rows_call(body, [g_half, r], [BF16], name)[0]


def _chip_sum(rc, name):
    J, R, C = rc.shape
    tr = _pick(R, 256)

    def body(r_ref, o_ref):
        acc = r_ref[0].astype(F32)
        for j in range(1, J):
            acc = acc + r_ref[j].astype(F32)
        o_ref[...] = acc

    return pl.pallas_call(
        body, name=name, grid=(R // tr,),
        in_specs=[pl.BlockSpec((J, tr, C), lambda i: (0, i, 0))], out_specs=pl.BlockSpec((tr, C), lambda i: (i, 0)),
        out_shape=S((R, C), F32), compiler_params=_cparams(("parallel",)),
    )(rc)


def _slot_sum(slots, name):
    J, R, C = slots.shape
    tr = _pick(R, 512)

    def body(r_ref, o_ref):
        acc = r_ref[0]
        for j in range(1, J):
            acc = acc + r_ref[j]
        o_ref[...] = acc

    return pl.pallas_call(
        body, name=name, grid=(R // tr,),
        in_specs=[pl.BlockSpec((J, tr, C), lambda i: (0, i, 0))], out_specs=pl.BlockSpec((tr, C), lambda i: (i, 0)),
        out_shape=S((R, C), F32), compiler_params=_cparams(("parallel",)),
    )(slots)


def _adamw(w, g, m, v, name):
    def body(w_ref, g_ref, m_ref, v_ref, d_ref, nm_ref, nv_ref):
        g_ = g_ref[...]
        nm = ADAM_B1 * m_ref[...] + (1.0 - ADAM_B1) * g_
        nv = ADAM_B2 * v_ref[...] + (1.0 - ADAM_B2) * (g_ * g_)
        m_hat = nm / (1.0 - ADAM_B1 ** ADAM_STEP)
        v_hat = nv / (1.0 - ADAM_B2 ** ADAM_STEP)
        d_ref[...] = -ADAM_LR * (m_hat / (jnp.sqrt(v_hat) + ADAM_EPS) + ADAM_WD * w_ref[...])
        nm_ref[...] = nm
        nv_ref[...] = nv
    return _rows_call(body, [w, g, m, v], [F32, F32, F32], name)


ANY = pl.BlockSpec(memory_space=pl.ANY)


def _place():
    x, y, c = lax.axis_index("x"), lax.axis_index("y"), lax.axis_index("c")
    return x, y, c


def _other_chips(x, y):
    return [(1 - x, y), (x, 1 - y), (1 - x, 1 - y)]


def _all_gather_weights(shards, name):
    nw = len(shards)

    def body(*refs):
        ins, outs = refs[:nw], refs[nw:2 * nw]
        send_sems, recv_sems = refs[2 * nw:]
        x, y, c = _place()
        me_chip = 2 * x + y
        sibling = (x, y, 1 - c)
        chips = _other_chips(x, y)

        def half(k_rows, which):
            return pl.ds(which * (k_rows // 2), k_rows // 2)

        sends = []
        for w in range(nw):
            kr = ins[w].shape[0]
            for j, (cx, cy) in enumerate(chips):
                cp = pltpu.make_async_remote_copy(
                    src_ref=ins[w].at[half(kr, c)], dst_ref=outs[w].at[me_chip, half(kr, c)],
                    send_sem=send_sems.at[w * 6 + j], recv_sem=recv_sems.at[w * 6 + j],
                    device_id=(cx, cy, c), device_id_type=MESH_ID)
                cp.start()
                sends.append(cp)
        for w in range(nw):
            kr = ins[w].shape[0]
            for j, (cx, cy) in enumerate(chips):
                blk = outs[w].at[2 * cx + cy, half(kr, c)]
                pltpu.make_async_remote_copy(
                    src_ref=blk, dst_ref=blk, send_sem=send_sems.at[w * 6 + j], recv_sem=recv_sems.at[w * 6 + j],
                    device_id=(cx, cy, c), device_id_type=MESH_ID).wait_recv()
                cp = pltpu.make_async_remote_copy(
                    src_ref=blk, dst_ref=blk, send_sem=send_sems.at[w * 6 + 3 + j], recv_sem=recv_sems.at[w * 6 + 3 + j],
                    device_id=sibling, device_id_type=MESH_ID)
                cp.start()
                sends.append(cp)
        for w in range(nw):
            kr = ins[w].shape[0]
            for j, (cx, cy) in enumerate(chips):
                blk = outs[w].at[2 * cx + cy, half(kr, 1 - c)]
                pltpu.make_async_remote_copy(
                    src_ref=blk, dst_ref=blk, send_sem=send_sems.at[w * 6 + 3 + j], recv_sem=recv_sems.at[w * 6 + 3 + j],
                    device_id=sibling, device_id_type=MESH_ID).wait_recv()
        for cp in sends:
            cp.wait_send()

    return pl.pallas_call(
        body, name=name, in_specs=[ANY] * nw, out_specs=[ANY] * nw,
        out_shape=[S((N_CHIPS,) + s.shape, s.dtype) for s in shards],
        scratch_shapes=[pltpu.SemaphoreType.DMA((6 * nw,)), pltpu.SemaphoreType.DMA((6 * nw,))],
        compiler_params=pltpu.CompilerParams(has_side_effects=True),
    )(*shards)


def _core_exchange_halves(grads, name):
    nw = len(grads)

    def body(*refs):
        ins, outs = refs[:nw], refs[nw:2 * nw]
        send_sems, recv_sems = refs[2 * nw:]
        x, y, c = _place()
        sibling = (x, y, 1 - c)
        cps = []
        for w in range(nw):
            kh = ins[w].shape[1] // 2
            cp = pltpu.make_async_remote_copy(
                src_ref=ins[w].at[:, pl.ds((1 - c) * kh, kh), :], dst_ref=outs[w],
                send_sem=send_sems.at[w], recv_sem=recv_sems.at[w], device_id=sibling, device_id_type=MESH_ID)
            cp.start()
            cps.append(cp)
        for cp in cps:
            cp.wait()

    return pl.pallas_call(
        body, name=name, in_specs=[ANY] * nw, out_specs=[ANY] * nw,
        out_shape=[S((g.shape[0], g.shape[1] // 2, g.shape[2]), g.dtype) for g in grads],
        scratch_shapes=[pltpu.SemaphoreType.DMA((nw,)), pltpu.SemaphoreType.DMA((nw,))],
        compiler_params=pltpu.CompilerParams(has_side_effects=True),
    )(*grads)


def _chip_scatter(parts, name):
    nw = len(parts)

    def body(*refs):
        ins, outs = refs[:nw], refs[nw:2 * nw]
        send_sems, recv_sems = refs[2 * nw:]
        x, y, c = _place()
        me_chip = 2 * x + y
        chips = _other_chips(x, y)
        rem = []
        for w in range(nw):
            for j, (cx, cy) in enumerate(chips):
                cp = pltpu.make_async_remote_copy(
                    src_ref=ins[w].at[2 * cx + cy], dst_ref=outs[w].at[me_chip],
                    send_sem=send_sems.at[w * 3 + j], recv_sem=recv_sems.at[w * 3 + j],
                    device_id=(cx, cy, c), device_id_type=MESH_ID)
                cp.start()
                rem.append(cp)
        for w in range(nw):
            for j, (cx, cy) in enumerate(chips):
                slot = outs[w].at[2 * cx + cy]
                pltpu.make_async_remote_copy(
                    src_ref=slot, dst_ref=slot, send_sem=send_sems.at[w * 3 + j], recv_sem=recv_sems.at[w * 3 + j],
                    device_id=(cx, cy, c), device_id_type=MESH_ID).wait_recv()
        for cp in rem:
            cp.wait_send()

    return pl.pallas_call(
        body, name=name, in_specs=[ANY] * nw, out_specs=[ANY] * nw,
        out_shape=[S(p.shape, p.dtype) for p in parts],
        scratch_shapes=[pltpu.SemaphoreType.DMA((3 * nw,)), pltpu.SemaphoreType.DMA((3 * nw,))],
        compiler_params=pltpu.CompilerParams(has_side_effects=True),
    )(*parts)


def _core_join_halves(halves, name):
    nw = len(halves)

    def body(*refs):
        ins, outs = refs[:nw], refs[nw:2 * nw]
        send_sems, recv_sems = refs[2 * nw:]
        x, y, c = _place()
        sibling = (x, y, 1 - c)
        rem = []
        for w in range(nw):
            r = ins[w].shape[0]
            mine = outs[w].at[pl.ds(c * r, r)]
            cp = pltpu.make_async_remote_copy(
                src_ref=ins[w], dst_ref=mine, send_sem=send_sems.at[w], recv_sem=recv_sems.at[w],
                device_id=sibling, device_id_type=MESH_ID)
            cp.start()
            rem.append(cp)
        for w in range(nw):
            r = ins[w].shape[0]
            theirs = outs[w].at[pl.ds((1 - c) * r, r)]
            pltpu.make_async_remote_copy(
                src_ref=theirs, dst_ref=theirs, send_sem=send_sems.at[w], recv_sem=recv_sems.at[w],
                device_id=sibling, device_id_type=MESH_ID).wait_recv()
        for cp in rem:
            cp.wait_send()

    return pl.pallas_call(
        body, name=name, in_specs=[ANY] * nw, out_specs=[ANY] * nw,
        out_shape=[S((2 * h.shape[0],) + h.shape[1:], h.dtype) for h in halves],
        scratch_shapes=[pltpu.SemaphoreType.DMA((nw,)), pltpu.SemaphoreType.DMA((nw,))],
        compiler_params=pltpu.CompilerParams(has_side_effects=True),
    )(*halves)


def _broadcast_all(buf, name):
    def body(in_ref, out_ref, send_sems, recv_sems, loc_sem):
        x, y, c = _place()
        me = 4 * x + 2 * y + c
        loc = pltpu.make_async_copy(in_ref, out_ref.at[me], loc_sem)
        loc.start()
        cps = []
        for k in range(1, N_DEV):
            fx, fy, fc = (k >> 2) & 1, (k >> 1) & 1, k & 1
            px, py, pc = x ^ fx, y ^ fy, c ^ fc
            cp = pltpu.make_async_remote_copy(
                src_ref=in_ref, dst_ref=out_ref.at[me], send_sem=send_sems.at[k - 1], recv_sem=recv_sems.at[k - 1],
                device_id=(px, py, pc), device_id_type=MESH_ID)
            cp.start()
            cps.append(cp)
        for k in range(1, N_DEV):
            fx, fy, fc = (k >> 2) & 1, (k >> 1) & 1, k & 1
            px, py, pc = x ^ fx, y ^ fy, c ^ fc
            slot = out_ref.at[4 * px + 2 * py + pc]
            pltpu.make_async_remote_copy(
                src_ref=slot, dst_ref=slot, send_sem=send_sems.at[k - 1], recv_sem=recv_sems.at[k - 1],
                device_id=(px, py, pc), device_id_type=MESH_ID).wait_recv()
        for cp in cps:
            cp.wait_send()
        loc.wait()

    return pl.pallas_call(
        body, name=name, in_specs=[ANY], out_specs=ANY,
        out_shape=S((N_DEV,) + buf.shape, buf.dtype),
        scratch_shapes=[pltpu.SemaphoreType.DMA((N_DEV - 1,)), pltpu.SemaphoreType.DMA((N_DEV - 1,)),
                        pltpu.SemaphoreType.DMA],
        compiler_params=pltpu.CompilerParams(has_side_effects=True),
    )(buf)


PACK_ALIGN = 8 * LANES


def _pack(arrs):
    flat = []
    for a in arrs:
        f = a.reshape(-1).astype(F32)
        pad = (-f.shape[0]) % PACK_ALIGN
        flat.append(jnp.pad(f, (0, pad)) if pad else f)
    return jnp.concatenate(flat).reshape(-1, LANES)


def _unpack(buf, shapes):
    out, off = [], 0
    flat = buf.reshape(-1)
    for shp in shapes:
        n = math.prod(shp)
        out.append(flat[off:off + n].reshape(shp))
        off += n + ((-n) % PACK_ALIGN)
    return out


def kernel(x, positions, norm_pre, norm_post, w_in_mla, mla_q_norm, w_uq, mla_kv_norm, w_ukv, w_out_mla, w_in_sc, sc_conv, w_out_sc, w_in_gm, gm_ln_g, gm_ln_b, gm_w_s, gm_b_s, w_out_gm, w_in_cf, cf_dw, cf_dw_b, cf_ln_g, cf_ln_b, w_out_cf, loss_target, m_norm_pre, m_norm_post, m_w_in_mla, m_mla_q_norm, m_w_uq, m_mla_kv_norm, m_w_ukv, m_w_out_mla, m_w_in_sc, m_sc_conv, m_w_out_sc, m_w_in_gm, m_gm_ln_g, m_gm_ln_b, m_gm_w_s, m_gm_b_s, m_w_out_gm, m_w_in_cf, m_cf_dw, m_cf_dw_b, m_cf_ln_g, m_cf_ln_b, m_w_out_cf, v_norm_pre, v_norm_post, v_w_in_mla, v_mla_q_norm, v_w_uq, v_mla_kv_norm, v_w_ukv, v_w_out_mla, v_w_in_sc, v_sc_conv, v_w_out_sc, v_w_in_gm, v_gm_ln_g, v_gm_ln_b, v_gm_w_s, v_gm_b_s, v_w_out_gm, v_w_in_cf, v_cf_dw, v_cf_dw_b, v_cf_ln_g, v_cf_ln_b, v_w_out_cf):
    loc = dict(locals())
    wts = {n: loc[n] for n in WEIGHTS}
    mom_m = {n: loc["m_" + n] for n in WEIGHTS}
    mom_v = {n: loc["v_" + n] for n in WEIGHTS}

    T, D = x.shape[1], x.shape[2]
    xin = x.reshape(T, D)
    target = loss_target.reshape(T, D)
    q_rank, kv_rank = mla_q_norm.shape[1], mla_kv_norm.shape[1]
    H = (w_uq.shape[2] * N_CHIPS) // (NOPE_DIM + ROPE_DIM)
    hv = H * V_DIM
    c_kr = q_rank + kv_rank
    wa_cols = c_kr + ROPE_DIM
    wa_pad = wa_cols + (LANES - ROPE_DIM)
    chip = 2 * lax.axis_index("x") + lax.axis_index("y")

    own_bf16 = [wts[n][0].astype(BF16) for n in BIG]
    gathered = _all_gather_weights(own_bf16, "gather_weights")
    gw = {n: lax.dynamic_update_slice(g, own[None], (chip, 0, 0)) for n, g, own in zip(BIG, gathered, own_bf16)}
    small_sh_shapes = [wts[n][0].shape for n in SMALL_SHARDED]
    slots = _broadcast_all(_pack([wts[n][0] for n in SMALL_SHARDED]), "gather_small")
    per_chip = [_unpack(slots[2 * k], small_sh_shapes) for k in range(N_CHIPS)]
    sp = {n: jnp.concatenate([per_chip[k][i] for k in range(N_CHIPS)], axis=-1) for i, n in enumerate(SMALL_SHARDED)}

    def cols_major(w4):
        return jnp.transpose(w4, (1, 0, 2)).reshape(w4.shape[1], -1)

    w_in_full = cols_major(gw['w_in_mla'])
    w_a = jnp.pad(w_in_full[:, :wa_cols], ((0, 0), (0, wa_pad - wa_cols)))[None]
    w_z = w_in_full[:, wa_cols:][None]
    wq = cols_major(gw['w_uq']).reshape(q_rank, H, NOPE_DIM + ROPE_DIM)
    wq = jnp.pad(wq, ((0, 0), (0, 0), (0, HEAD_PAD - NOPE_DIM - ROPE_DIM))).reshape(1, q_rank, H * HEAD_PAD)
    wkv = cols_major(gw['w_ukv'])[None]
    w_out = {n: gw[n].reshape(1, -1, D) for n in ['w_out_mla', 'w_out_sc', 'w_out_gm', 'w_out_cf']}

    half = ROPE_DIM // 2
    inv_freq = ROPE_THETA ** (-jnp.arange(half, dtype=F32) / half)
    invf = jnp.concatenate([inv_freq, inv_freq, jnp.zeros((LANES - ROPE_DIM,), F32)]).reshape(1, LANES)
    tabs = _rope_tables(positions.reshape(T, 1), invf, "rope_tables")
    scale = float(NOPE_DIM + ROPE_DIM) ** -0.5

    xs = [xin]
    saved = []
    for i in range(4):
        xi = xs[-1]
        h = _rms_fwd(xi[None], 0, 0, D, norm_pre[i], f"pre_norm_{i}", BF16)
        if i == 0:
            pa = _mm_nn(h, w_a, 1, "mla_in_a")
            pz = _mm_nn(h, w_z, 1, "mla_in_z")
            qn = _rms_fwd(pa, 0, 0, q_rank, mla_q_norm[0], "mla_q_norm", BF16)
            kvn = _rms_fwd(pa, 0, q_rank // kv_rank, kv_rank, mla_kv_norm[0], "mla_kv_norm", BF16)
            q3 = _mm_nn(qn, wq, 1, "mla_q_up")
            kv3 = _mm_nn(kvn, wkv, 1, "mla_kv_up")
            qf, kf, vv = _qkv_layout(q3, kv3, pa, c_kr // LANES, tabs, H, "mla_qkv_layout")
            o, lse = _attn_fwd(qf, kf, vv, H, scale, "mla_attn_fwd")
            g = _gate_fwd(o, pz, "mla_gate_fwd")
            saved.append(dict(h=h, pa=pa, pz=pz, qn=qn, kvn=kvn, qf=qf, kf=kf, vv=vv, o=o, lse=lse, g=g))
            wo = w_out['w_out_mla']
        elif i == 1:
            p3 = _mm_nn(h, gw['w_in_sc'], 4, "sc_in")
            g = _sc_fwd(p3, sp['sc_conv'], "sc_mix_fwd")
            saved.append(dict(h=h, p3=p3, g=g))
            wo = w_out['w_out_sc']
        elif i == 2:
            p3 = _mm_nn(h, gw['w_in_gm'], 3, "gm_in")
            bs_t = jnp.transpose(gm_b_s[0])
            g = _gm_fwd(p3, sp['gm_ln_g'], sp['gm_ln_b'], gm_w_s[0], bs_t, "gm_mix_fwd")
            saved.append(dict(h=h, p3=p3, g=g, bs_t=bs_t))
            wo = w_out['w_out_gm']
        else:
            p3 = _mm_nn(h, gw['w_in_cf'], 3, "cf_in")
            y1 = _cf_conv_fwd(p3, sp['cf_dw'], sp['cf_dw_b'], "cf_conv_fwd")
            g = _cf_gate_fwd(y1, p3, sp['cf_ln_g'], sp['cf_ln_b'], "cf_gate_fwd")
            saved.append(dict(h=h, p3=p3, y1=y1, g=g))
            wo = w_out['w_out_cf']
        yo = _mm_nn(g, wo, 1, f"out_proj_{i}")
        saved[-1]['yo'] = yo
        xs.append(_rms_fwd(yo, 0, 0, D, norm_post[i], f"post_norm_{i}", F32, res=xi))

    dx, loss_local = _loss_head(xs[4], target, "loss_head")
    loss = lax.psum(loss_local, ("x", "y", "c"))

    big_grads = {}
    sgrad = {}
    d_npre, d_npost = [None] * 4, [None] * 4
    for i in (3, 2, 1, 0):
        sv = saved[i]
        h = sv['h']
        dyo, d_npost[i] = _rms_bwd(sv['yo'], 0, 0, D, norm_post[i], dx, f"post_norm_bwd_{i}", BF16)
        dyo3 = dyo[None]
        wo_name = ['w_out_mla', 'w_out_sc', 'w_out_gm', 'w_out_cf'][i]
        dg = _mm_nt(dyo3, w_out[wo_name], f"out_proj_dx_{i}")
        big_grads[wo_name] = _mm_tn(sv['g'], dyo3, 1, f"out_proj_dw_{i}").reshape(N_CHIPS, -1, D)
        if i == 3:
            dz, dy1, sgrad['cf_ln_g'], sgrad['cf_ln_b'] = _cf_gate_bwd(sv['y1'], sv['p3'], sp['cf_ln_g'], sp['cf_ln_b'], dg, "cf_gate_bwd")
            dp3, sgrad['cf_dw'], sgrad['cf_dw_b'] = _cf_conv_bwd(sv['p3'], sp['cf_dw'], dy1, dz, "cf_conv_bwd")
            big_grads['w_in_cf'] = _mm_tn(h, dp3, N_CHIPS, "cf_in_dw")
            dh = _mm_nt(dp3, gw['w_in_cf'], "cf_in_dx")
        elif i == 2:
            dp3, sgrad['gm_ln_g'], sgrad['gm_ln_b'], sgrad['gm_w_s'], dbs_t = _gm_bwd(
                sv['p3'], sp['gm_ln_g'], sp['gm_ln_b'], gm_w_s[0], sv['bs_t'], dg, "gm_mix_bwd")
            sgrad['gm_b_s'] = jnp.transpose(dbs_t[:, :GM_GROUPS])
            big_grads['w_in_gm'] = _mm_tn(h, dp3, N_CHIPS, "gm_in_dw")
            dh = _mm_nt(dp3, gw['w_in_gm'], "gm_in_dx")
        elif i == 1:
            dp3, sgrad['sc_conv'] = _sc_bwd(sv['p3'], sp['sc_conv'], dg, "sc_mix_bwd")
            big_grads['w_in_sc'] = _mm_tn(h, dp3, N_CHIPS, "sc_in_dw")
            dh = _mm_nt(dp3, gw['w_in_sc'], "sc_in_dx")
        else:
            do, dpz = _gate_bwd(dg, sv['o'], sv['pz'], "mla_gate_bwd")
            dqf, dkf, dv = _attn_bwd(sv['qf'], sv['kf'], sv['vv'], do, sv['o'], sv['lse'], H, scale, "mla_attn_bwd")
            dq3, dkv3, dkr = _qkv_layout_bwd(dqf, dkf, dv, tabs, H, "mla_qkv_layout_bwd")
            dqn = _mm_nt(dq3, wq, "mla_q_up_dx")
            dwq = _mm_tn(sv['qn'], dq3, 1, "mla_q_up_dw")
            dkvn = _mm_nt(dkv3, wkv, "mla_kv_up_dx")
            dwkv = _mm_tn(sv['kvn'], dkv3, 1, "mla_kv_up_dw")
            dcq, dqg = _rms_bwd(sv['pa'], 0, 0, q_rank, mla_q_norm[0], dqn, "mla_q_norm_bwd", BF16)
            dckv, dkvg = _rms_bwd(sv['pa'], 0, q_rank // kv_rank, kv_rank, mla_kv_norm[0], dkvn, "mla_kv_norm_bwd", BF16)
            sgrad['mla_q_norm'], sgrad['mla_kv_norm'] = dqg, dkvg
            dpa = jnp.concatenate([dcq, dckv, dkr], axis=1)[None]
            dwa = _mm_tn(h, dpa, 1, "mla_in_a_dw")
            dwz = _mm_tn(h, dpz, 1, "mla_in_z_dw")
            dh_a = _mm_nt(dpa, w_a, "mla_in_a_dx")
            dh = _mm_nt(dpz, w_z, "mla_in_z_dx", add=dh_a)
            dw_in = jnp.concatenate([dwa[0][:, :wa_cols], dwz[0]], axis=1)
            big_grads['w_in_mla'] = jnp.transpose(dw_in.reshape(D, N_CHIPS, -1), (1, 0, 2))
            dwq_ = dwq[0].reshape(q_rank, H, HEAD_PAD)[:, :, :NOPE_DIM + ROPE_DIM].reshape(q_rank, N_CHIPS, -1)
            big_grads['w_uq'] = jnp.transpose(dwq_, (1, 0, 2))
            big_grads['w_ukv'] = jnp.transpose(dwkv[0].reshape(kv_rank, N_CHIPS, -1), (1, 0, 2))
        dx, d_npre[i] = _rms_bwd(xs[i][None], 0, 0, D, norm_pre[i], dh, f"pre_norm_bwd_{i}", F32, res=dx)
    grad_x = dx.reshape(1, T, D)
    sgrad['norm_pre'] = jnp.concatenate(d_npre, axis=0)
    sgrad['norm_post'] = jnp.concatenate(d_npost, axis=0)

    c = lax.axis_index("c")
    full = [big_grads[n] for n in BIG]
    recv = _core_exchange_halves(full, "grads_core_exchange")
    pair = []
    for n, gfull, r in zip(BIG, full, recv):
        J, K, nn = gfull.shape
        kh = K // 2
        mine = lax.dynamic_slice_in_dim(gfull, c * kh, kh, axis=1)
        pair.append(_pair_sum(mine.reshape(J * kh, nn), r.reshape(J * kh, nn), f"pair_sum_{n}").reshape(J, kh, nn))
    got = _chip_scatter(pair, "grads_chip_scatter")
    got = [lax.dynamic_update_slice(r, lax.dynamic_slice_in_dim(p, chip, 1, axis=0), (chip, 0, 0)) for r, p in zip(got, pair)]
    halves = [_chip_sum(r, f"chip_sum_{n}") for n, r in zip(BIG, got)]
    joined = _core_join_halves(halves, "grads_core_join")
    joined = [lax.dynamic_update_slice(j, hf, (c * hf.shape[0], 0)) for j, hf in zip(joined, halves)]
    grads = dict(zip(BIG, [j[None] for j in joined]))

    small_full_shapes = [sgrad[n].reshape(wts[n].shape[:-1] + (-1,)).shape for n in SMALL]
    gslots = _broadcast_all(_pack([sgrad[n] for n in SMALL]), "grads_small_exchange")
    gsum = _unpack(_slot_sum(gslots, "grads_small_sum"), small_full_shapes)
    for n, gs in zip(SMALL, gsum):
        if n in SMALL_SHARDED:
            per = wts[n].shape[-1]
            gs = lax.dynamic_slice_in_dim(gs, chip * per, per, axis=gs.ndim - 1)
        grads[n] = gs.reshape(wts[n].shape)

    delta, new_m, new_v = {}, {}, {}
    for n in BIG:
        shp = wts[n].shape
        two_d = (shp[1], shp[2])
        d_, m_, v_ = _adamw(wts[n].reshape(two_d), grads[n].reshape(two_d), mom_m[n].reshape(two_d),
                            mom_v[n].reshape(two_d), f"adamw_{n}")
        delta[n], new_m[n], new_v[n] = d_.reshape(shp), m_.reshape(shp), v_.reshape(shp)
    shapes = [wts[n].shape for n in SMALL]
    d_, m_, v_ = _adamw(_pack([wts[n] for n in SMALL]), _pack([grads[n] for n in SMALL]),
                        _pack([mom_m[n] for n in SMALL]), _pack([mom_v[n] for n in SMALL]), "adamw_small")
    for n, a, b, cc in zip(SMALL, _unpack(d_, shapes), _unpack(m_, shapes), _unpack(v_, shapes)):
        delta[n], new_m[n], new_v[n] = a, b, cc

    return (loss, grad_x, *[grads[n] for n in WEIGHTS], *[delta[n] for n in WEIGHTS],
            *[new_m[n] for n in WEIGHTS], *[new_v[n] for n in WEIGHTS])
```

```python
import functools
import math

import jax
import jax.numpy as jnp
from jax import lax
from jax.experimental import pallas as pl
from jax.experimental.pallas import tpu as pltpu

F32, BF16 = jnp.float32, jnp.bfloat16
S = jax.ShapeDtypeStruct
MESH_ID = pl.DeviceIdType.MESH

V7X_VMEM_BYTES = 64 * 1024 * 1024
VMEM_LIMIT = V7X_VMEM_BYTES - 8 * 1024 * 1024
LANES = 128
N_CHIPS = 4
N_DEV = 8

NORM_EPS = 1e-6
LN_EPS = 1e-5
ROPE_THETA = 10000.0
ROPE_DIM = 64
NOPE_DIM = 128
V_DIM = 128
HEAD_PAD = 256
GM_CHUNK = 128
GM_GROUPS = 8
NEG = -1e30

ADAM_LR, ADAM_B1, ADAM_B2, ADAM_EPS, ADAM_WD, ADAM_STEP = 0.001, 0.9, 0.999, 1e-08, 0.01, 10

FWD_PARAMS = ['x', 'positions', 'norm_pre', 'norm_post', 'w_in_mla', 'mla_q_norm', 'w_uq', 'mla_kv_norm', 'w_ukv',
              'w_out_mla', 'w_in_sc', 'sc_conv', 'w_out_sc', 'w_in_gm', 'gm_ln_g', 'gm_ln_b', 'gm_w_s', 'gm_b_s',
              'w_out_gm', 'w_in_cf', 'cf_dw', 'cf_dw_b', 'cf_ln_g', 'cf_ln_b', 'w_out_cf']
WEIGHTS = FWD_PARAMS[2:]
BIG = ['w_in_mla', 'w_uq', 'w_ukv', 'w_out_mla', 'w_in_sc', 'w_out_sc', 'w_in_gm', 'w_out_gm', 'w_in_cf', 'w_out_cf']
LAYER_BIG = [['w_in_mla', 'w_uq', 'w_ukv', 'w_out_mla'], ['w_in_sc', 'w_out_sc'], ['w_in_gm', 'w_out_gm'],
             ['w_in_cf', 'w_out_cf']]
WO_NAMES = ['w_out_mla', 'w_out_sc', 'w_out_gm', 'w_out_cf']
SMALL = [n for n in WEIGHTS if n not in BIG]
SMALL_SHARDED = ['sc_conv', 'gm_ln_g', 'gm_ln_b', 'cf_dw', 'cf_dw_b', 'cf_ln_g', 'cf_ln_b']


def _cparams(sem=None, **kw):
    return pltpu.CompilerParams(dimension_semantics=sem, vmem_limit_bytes=VMEM_LIMIT, **kw)


def _pick(dim, pref):
    if dim <= pref:
        return dim
    t = (pref // LANES) * LANES
    while t >= LANES and dim % t:
        t -= LANES
    if t >= min(pref, 512):
        return t
    return dim if (dim <= 2048 or t < LANES) else t


def _silu(x):
    return x * jax.nn.sigmoid(x)


def _dsilu(x):
    s = jax.nn.sigmoid(x)
    return s * (1.0 + x * (1.0 - s))


def _gelu(x):
    return 0.5 * x * (1.0 + lax.erf(x * (2.0 ** -0.5)))


def _dgelu(x):
    cdf = 0.5 * (1.0 + lax.erf(x * (2.0 ** -0.5)))
    return cdf + x * jnp.exp(-0.5 * x * x) * ((2.0 * math.pi) ** -0.5)


MM_ONE_DOT = 4096


def _contract_tile(dim, divisible_by, pref_when_split):
    return dim if dim <= MM_ONE_DOT and divisible_by % dim == 0 else _pick(divisible_by, pref_when_split)


def _mm_accumulate(step, nsteps, prod, o_ref, acc, init=None):
    if nsteps == 1:
        r = prod()
        if init is not None:
            r = r + init()
        o_ref[...] = r.astype(o_ref.dtype)
        return

    @pl.when(step == 0)
    def _():
        acc[...] = jnp.zeros_like(acc) if init is None else init()

    acc[...] += prod()

    @pl.when(step == nsteps - 1)
    def _():
        o_ref[...] = acc[...].astype(o_ref.dtype)


def _mm_nn(a, w, np_out, name, out_dtype=F32):
    M, K = a.shape
    J, _, n = w.shape
    N = J * n
    W = N // np_out
    tm, tn = _pick(M, 1024), _pick(math.gcd(W, n), 1024)
    tk = _contract_tile(K, K, 2048)
    nk = K // tk

    def body(*refs):
        a_ref, w_ref, o_ref = refs[:3]
        _mm_accumulate(pl.program_id(2), nk, lambda: jnp.dot(a_ref[...], w_ref[...], preferred_element_type=F32),
                       o_ref, refs[-1])

    return pl.pallas_call(
        body, name=name, grid=(M // tm, N // tn, nk),
        in_specs=[pl.BlockSpec((tm, tk), lambda i, j, k: (i, k)),
                  pl.BlockSpec((None, tk, tn), lambda i, j, k: (j // (n // tn), k, j % (n // tn)))],
        out_specs=pl.BlockSpec((None, tm, tn), lambda i, j, k: (j // (W // tn), i, j % (W // tn))),
        out_shape=S((np_out, M, W), out_dtype),
        scratch_shapes=[pltpu.VMEM((tm, tn), F32)] if nk > 1 else [],
        compiler_params=_cparams(("parallel", "parallel", "arbitrary")),
    )(a, w)


def _mm_nt(a3, w, name, add=None, out_dtype=F32):
    NP, M, W = a3.shape
    J, K, n = w.shape
    N = NP * W
    tm, to = _pick(M, 1024), _pick(K, 1024)
    tc = _contract_tile(N, math.gcd(W, n), 2048)
    nc = N // tc
    has_add = add is not None

    def body(*refs):
        a_ref, w_ref = refs[0], refs[1]
        o_ref = refs[3] if has_add else refs[2]
        _mm_accumulate(
            pl.program_id(2), nc,
            lambda: lax.dot_general(a_ref[...], w_ref[...], (((1,), (1,)), ((), ())), preferred_element_type=F32),
            o_ref, refs[-1], init=(lambda: refs[2][...].astype(F32)) if has_add else None)

    in_specs = [pl.BlockSpec((None, tm, tc), lambda i, j, c: (c // (W // tc), i, c % (W // tc))),
                pl.BlockSpec((None, to, tc), lambda i, j, c: (c // (n // tc), j, c % (n // tc)))]
    ops = [a3, w]
    if has_add:
        in_specs.append(pl.BlockSpec((tm, to), lambda i, j, c: (i, j)))
        ops.append(add)
    return pl.pallas_call(
        body, name=name, grid=(M // tm, K // to, nc),
        in_specs=in_specs,
        out_specs=pl.BlockSpec((tm, to), lambda i, j, c: (i, j)),
        out_shape=S((M, K), out_dtype),
        scratch_shapes=[pltpu.VMEM((tm, to), F32)] if nc > 1 else [],
        compiler_params=_cparams(("parallel", "parallel", "arbitrary")),
    )(*ops)


def _mm_tn(a, d3, j_out, name, out_dtype=F32):
    M, K = a.shape
    NP, _, W = d3.shape
    N = NP * W
    n = N // j_out
    to, tn = _pick(K, 1024), _pick(math.gcd(W, n), 1024)
    tmc = _contract_tile(M, M, 2048)
    nm = M // tmc

    def body(*refs):
        a_ref, d_ref, o_ref = refs[:3]
        _mm_accumulate(
            pl.program_id(2), nm,
            lambda: lax.dot_general(a_ref[...], d_ref[...], (((0,), (0,)), ((), ())), preferred_element_type=F32),
            o_ref, refs[-1])

    return pl.pallas_call(
        body, name=name, grid=(K // to, N // tn, nm),
        in_specs=[pl.BlockSpec((tmc, to), lambda i, j, m: (m, i)),
                  pl.BlockSpec((None, tmc, tn), lambda i, j, m: (j // (W // tn), m, j % (W // tn)))],
        out_specs=pl.BlockSpec((None, to, tn), lambda i, j, m: (j // (n // tn), i, j % (n // tn))),
        out_shape=S((j_out, K, n), out_dtype),
        scratch_shapes=[pltpu.VMEM((to, tn), F32)] if nm > 1 else [],
        compiler_params=_cparams(("parallel", "parallel", "arbitrary")),
    )(a, d3)


def _rms_fwd(x3, piece, col_blk, width, g, name, out_dtype, res=None):
    T = x3.shape[1]
    tr = _pick(T, 256)
    has_res = res is not None

    def body(*refs):
        x_ref, g_ref = refs[0], refs[1]
        o_ref = refs[-1]
        x = x_ref[...].astype(F32)
        y = x * lax.rsqrt(jnp.mean(x * x, axis=-1, keepdims=True) + NORM_EPS) * g_ref[...]
        if has_res:
            y = refs[2][...] + y
        o_ref[...] = y.astype(o_ref.dtype)

    in_specs = [pl.BlockSpec((None, tr, width), lambda i: (piece, i, col_blk)),
                pl.BlockSpec((1, width), lambda i: (0, 0))]
    ops = [x3, g.reshape(1, width)]
    if has_res:
        in_specs.append(pl.BlockSpec((tr, width), lambda i: (i, 0)))
        ops.append(res)
    return pl.pallas_call(
        body, name=name, grid=(T // tr,), in_specs=in_specs,
        out_specs=pl.BlockSpec((tr, width), lambda i: (i, 0)),
        out_shape=S((T, width), out_dtype),
        compiler_params=_cparams(("parallel",)),
    )(*ops)


def _rms_bwd(u3, piece, col_blk, width, g, dy, name, out_dtype, res=None):
    T = u3.shape[1]
    tr = _pick(T, 256)
    has_res = res is not None

    def body(*refs):
        u_ref, g_ref, dy_ref = refs[0], refs[1], refs[2]
        du_ref, dg_ref = refs[-2], refs[-1]
        i = pl.program_id(0)
        u = u_ref[...].astype(F32)
        dy_ = dy_ref[...].astype(F32)
        r = lax.rsqrt(jnp.mean(u * u, axis=-1, keepdims=True) + NORM_EPS)
        nrm = u * r
        gdy = g_ref[...] * dy_
        du = r * (gdy - nrm * jnp.mean(gdy * nrm, axis=-1, keepdims=True))
        if has_res:
            du = du + refs[3][...]
        du_ref[...] = du.astype(du_ref.dtype)

        @pl.when(i == 0)
        def _():
            dg_ref[...] = jnp.zeros_like(dg_ref)

        dg_ref[...] += jnp.sum(dy_ * nrm, axis=0, keepdims=True)

    in_specs = [pl.BlockSpec((None, tr, width), lambda i: (piece, i, col_blk)),
                pl.BlockSpec((1, width), lambda i: (0, 0)),
                pl.BlockSpec((tr, width), lambda i: (i, 0))]
    ops = [u3, g.reshape(1, width), dy]
    if has_res:
        in_specs.append(pl.BlockSpec((tr, width), lambda i: (i, 0)))
        ops.append(res)
    return pl.pallas_call(
        body, name=name, grid=(T // tr,), in_specs=in_specs,
        out_specs=[pl.BlockSpec((tr, width), lambda i: (i, 0)), pl.BlockSpec((1, width), lambda i: (0, 0))],
        out_shape=[S((T, width), out_dtype), S((1, width), F32)],
        compiler_params=_cparams(("arbitrary",)),
    )(*ops)


def _loss_head(xl, target, name):
    T, D = xl.shape
    tr = _pick(T, 256)

    def body(x_ref, t_ref, dx_ref, l_ref):
        i = pl.program_id(0)
        err = x_ref[...] - t_ref[...]
        dx_ref[...] = err * (1.0 / D)

        @pl.when(i == 0)
        def _():
            l_ref[...] = jnp.zeros_like(l_ref)

        l_ref[...] += jnp.sum(err * err)

    dx, l = pl.pallas_call(
        body, name=name, grid=(T // tr,),
        in_specs=[pl.BlockSpec((tr, D), lambda i: (i, 0)), pl.BlockSpec((tr, D), lambda i: (i, 0))],
        out_specs=[pl.BlockSpec((tr, D), lambda i: (i, 0)), pl.BlockSpec((8, LANES), lambda i: (0, 0))],
        out_shape=[S((T, D), F32), S((8, LANES), F32)],
        compiler_params=_cparams(("arbitrary",)),
    )(xl, target)
    return dx, l[0, 0] * (0.5 / D)


def _gate_fwd(o, z3, name):
    T, W = o.shape
    tr = _pick(T, 256)

    def body(o_ref, z_ref, g_ref):
        g_ref[...] = (o_ref[...] * _silu(z_ref[...])).astype(g_ref.dtype)

    return pl.pallas_call(
        body, name=name, grid=(T // tr,),
        in_specs=[pl.BlockSpec((tr, W), lambda i: (i, 0)), pl.BlockSpec((None, tr, W), lambda i: (0, i, 0))],
        out_specs=pl.BlockSpec((tr, W), lambda i: (i, 0)),
        out_shape=S((T, W), BF16), compiler_params=_cparams(("parallel",)),
    )(o, z3)


def _gate_bwd(dg, o, z3, name):
    T, W = o.shape
    tr = _pick(T, 256)

    def body(dg_ref, o_ref, z_ref, do_ref, dz_ref):
        dg_, z = dg_ref[...], z_ref[...]
        do_ref[...] = (dg_ * _silu(z)).astype(do_ref.dtype)
        dz_ref[...] = (dg_ * o_ref[...] * _dsilu(z)).astype(dz_ref.dtype)

    return pl.pallas_call(
        body, name=name, grid=(T // tr,),
        in_specs=[pl.BlockSpec((tr, W), lambda i: (i, 0)), pl.BlockSpec((tr, W), lambda i: (i, 0)),
                  pl.BlockSpec((None, tr, W), lambda i: (0, i, 0))],
        out_specs=[pl.BlockSpec((tr, W), lambda i: (i, 0)), pl.BlockSpec((None, tr, W), lambda i: (0, i, 0))],
        out_shape=[S((T, W), BF16), S((1, T, W), BF16)], compiler_params=_cparams(("parallel",)),
    )(dg, o, z3)


def _rope_tables(pos_col, invf, name):
    T = pos_col.shape[0]
    tr = _pick(T, 512)
    half = ROPE_DIM // 2

    def body(p_ref, f_ref, c_ref, sa_ref, sb_ref):
        ang = p_ref[...].astype(F32) * f_ref[...]
        lane = lax.broadcasted_iota(jnp.int32, ang.shape, 1)
        cs, sn = jnp.cos(ang), jnp.sin(ang)
        c_ref[...] = jnp.where(lane < ROPE_DIM, cs, 0.0)
        sa_ref[...] = jnp.where(lane < half, -sn, 0.0)
        sb_ref[...] = jnp.where((lane >= half) & (lane < ROPE_DIM), sn, 0.0)

    spec = pl.BlockSpec((tr, LANES), lambda i: (i, 0))
    return pl.pallas_call(
        body, name=name, grid=(T // tr,),
        in_specs=[pl.BlockSpec((tr, 1), lambda i: (i, 0)), pl.BlockSpec((1, LANES), lambda i: (0, 0))],
        out_specs=[spec, spec, spec], out_shape=[S((T, LANES), F32)] * 3,
        compiler_params=_cparams(("parallel",)),
    )(pos_col, invf)


def _rope(t, c, sa, sb):
    half = ROPE_DIM // 2
    return t * c + pltpu.roll(t, LANES - half, 1) * sa + pltpu.roll(t, half, 1) * sb


def _rope_t(d, c, sa, sb):
    half = ROPE_DIM // 2
    return d * c + pltpu.roll(d * sa, half, 1) + pltpu.roll(d * sb, LANES - half, 1)


def _qkv_layout(q3, kv3, pa3, kr_blk, tabs, H, name):
    T = q3.shape[1]
    tr = _pick(T, 512)

    def body(q_ref, kv_ref, kr_ref, c_ref, sa_ref, sb_ref, qf_ref, kf_ref, v_ref):
        c, sa, sb = c_ref[...], sa_ref[...], sb_ref[...]
        qf_ref[:, :NOPE_DIM] = q_ref[:, :NOPE_DIM].astype(BF16)
        qf_ref[:, NOPE_DIM:] = _rope(q_ref[:, NOPE_DIM:], c, sa, sb).astype(BF16)
        kf_ref[:, :NOPE_DIM] = kv_ref[:, :NOPE_DIM].astype(BF16)
        kf_ref[:, NOPE_DIM:] = _rope(kr_ref[...], c, sa, sb).astype(BF16)
        v_ref[...] = kv_ref[:, NOPE_DIM:].astype(BF16)

    tab = pl.BlockSpec((tr, LANES), lambda i, h: (i, 0))
    hp = pl.BlockSpec((None, tr, HEAD_PAD), lambda i, h: (0, i, h))
    return pl.pallas_call(
        body, name=name, grid=(T // tr, H),
        in_specs=[hp, hp, pl.BlockSpec((None, tr, LANES), lambda i, h: (0, i, kr_blk)), tab, tab, tab],
        out_specs=[pl.BlockSpec((tr, HEAD_PAD), lambda i, h: (i, h)), pl.BlockSpec((tr, HEAD_PAD), lambda i, h: (i, h)),
                   pl.BlockSpec((tr, V_DIM), lambda i, h: (i, h))],
        out_shape=[S((T, H * HEAD_PAD), BF16), S((T, H * HEAD_PAD), BF16), S((T, H * V_DIM), BF16)],
        compiler_params=_cparams(("parallel", "arbitrary")),
    )(q3, kv3, pa3, *tabs)


def _qkv_layout_bwd(dqf, dkf, dv, tabs, H, name):
    T = dqf.shape[0]
    tr = _pick(T, 512)

    def body(dqf_ref, dkf_ref, dv_ref, c_ref, sa_ref, sb_ref, dq_ref, dkv_ref, dkr_ref, acc):
        h = pl.program_id(1)
        c, sa, sb = c_ref[...], sa_ref[...], sb_ref[...]
        dq_ref[:, :NOPE_DIM] = dqf_ref[:, :NOPE_DIM].astype(BF16)
        dq_ref[:, NOPE_DIM:] = _rope_t(dqf_ref[:, NOPE_DIM:].astype(F32), c, sa, sb).astype(BF16)
        dkv_ref[:, :NOPE_DIM] = dkf_ref[:, :NOPE_DIM].astype(BF16)
        dkv_ref[:, NOPE_DIM:] = dv_ref[...].astype(BF16)

        @pl.when(h == 0)
        def _():
            acc[...] = jnp.zeros_like(acc)

        acc[...] += dkf_ref[:, NOPE_DIM:].astype(F32)

        @pl.when(h == H - 1)
        def _():
            dkr_ref[...] = _rope_t(acc[...], c, sa, sb).astype(BF16)

    tab = pl.BlockSpec((tr, LANES), lambda i, h: (i, 0))
    hp_in = pl.BlockSpec((tr, HEAD_PAD), lambda i, h: (i, h))
    hp_out = pl.BlockSpec((None, tr, HEAD_PAD), lambda i, h: (0, i, h))
    return pl.pallas_call(
        body, name=name, grid=(T // tr, H),
        in_specs=[hp_in, hp_in, pl.BlockSpec((tr, V_DIM), lambda i, h: (i, h)), tab, tab, tab],
        out_specs=[hp_out, hp_out, pl.BlockSpec((tr, LANES), lambda i, h: (i, 0))],
        out_shape=[S((1, T, H * HEAD_PAD), BF16), S((1, T, H * HEAD_PAD), BF16), S((T, LANES), BF16)],
        scratch_shapes=[pltpu.VMEM((tr, LANES), F32)],
        compiler_params=_cparams(("parallel", "arbitrary")),
    )(dqf, dkf, dv, *tabs)


ATTN_BLOCK = 512


def _nt(a, b):
    return lax.dot_general(a, b, (((1,), (1,)), ((), ())), preferred_element_type=F32)


def _tn(a, b):
    return lax.dot_general(a, b, (((0,), (0,)), ((), ())), preferred_element_type=F32)


def _causal_blocks(qi, tb, block):
    if qi > 0:
        def step(ki, carry):
            block(pl.multiple_of(ki * tb, tb), False)
            return carry
        lax.fori_loop(0, qi, step, 0)
    block(qi * tb, True)


def _attn_fwd(qf, kf, v, H, scale, name):
    T = qf.shape[0]
    tb = _pick(T, ATTN_BLOCK)
    nb = T // tb

    def body(q_ref, k_ref, v_ref, o_ref, lse_ref):
        row = lax.broadcasted_iota(jnp.int32, (tb, tb), 0)
        col = lax.broadcasted_iota(jnp.int32, (tb, tb), 1)
        for qi in range(nb):
            rows, before = pl.ds(qi * tb, tb), qi * tb
            q = q_ref[rows, :]
            s_own = jnp.where(col <= row, _nt(q, k_ref[rows, :]), NEG)
            m = jnp.max(s_own, axis=-1, keepdims=True)
            if qi > 0:
                s_pre = _nt(q, k_ref[0:before, :])
                m = jnp.maximum(m, jnp.max(s_pre, axis=-1, keepdims=True))
            p_own = jnp.exp((s_own - m) * scale)
            l = jnp.sum(p_own, axis=-1, keepdims=True)
            acc = jnp.dot(p_own.astype(BF16), v_ref[rows, :], preferred_element_type=F32)
            if qi > 0:
                p_pre = jnp.exp((s_pre - m) * scale)
                l = l + jnp.sum(p_pre, axis=-1, keepdims=True)
                acc = acc + jnp.dot(p_pre.astype(BF16), v_ref[0:before, :], preferred_element_type=F32)
            o_ref[rows, :] = acc / l
            lse_ref[rows, :] = jnp.broadcast_to(m * scale + jnp.log(l), (tb, LANES))

    return pl.pallas_call(
        body, name=name, grid=(H,),
        in_specs=[pl.BlockSpec((T, HEAD_PAD), lambda h: (0, h)), pl.BlockSpec((T, HEAD_PAD), lambda h: (0, h)),
                  pl.BlockSpec((T, V_DIM), lambda h: (0, h))],
        out_specs=[pl.BlockSpec((T, V_DIM), lambda h: (0, h)), pl.BlockSpec((T, LANES), lambda h: (0, h))],
        out_shape=[S((T, H * V_DIM), F32), S((T, H * LANES), F32)],
        compiler_params=_cparams(("parallel",)),
    )(qf, kf, v)


def _attn_bwd(qf, kf, v, do, o, lse, H, scale, name):
    T = qf.shape[0]
    tb = _pick(T, ATTN_BLOCK)
    nb = T // tb

    def body(q_ref, k_ref, v_ref, do_ref, o_ref, lse_ref, dq_ref, dk_ref, dv_ref, dq_acc, dk_acc, dv_acc):
        row = lax.broadcasted_iota(jnp.int32, (tb, tb), 0)
        col = lax.broadcasted_iota(jnp.int32, (tb, tb), 1)
        dk_acc[...] = jnp.zeros_like(dk_acc)
        dv_acc[...] = jnp.zeros_like(dv_acc)
        for qi in range(nb):
            rows = pl.ds(qi * tb, tb)
            dq_acc[...] = jnp.zeros_like(dq_acc)
            delta = jnp.sum(do_ref[rows, :].astype(F32) * o_ref[rows, :], axis=-1, keepdims=True)
            lse_q = lse_ref[rows, 0:1]

            def block(k0, masked, rows=rows, delta=delta, lse_q=lse_q):
                keys = pl.ds(k0, tb)
                q, k, do_ = q_ref[rows, :], k_ref[keys, :], do_ref[rows, :]
                s = _nt(q, k)
                if masked:
                    s = jnp.where(col <= row, s, NEG)
                p = jnp.exp(s * scale - lse_q)
                dp = _nt(do_, v_ref[keys, :])
                ds = (p * (dp - delta) * scale).astype(BF16)
                dv_acc[keys, :] += _tn(p.astype(BF16), do_)
                dk_acc[keys, :] += _tn(ds, q)
                dq_acc[...] += jnp.dot(ds, k, preferred_element_type=F32)

            _causal_blocks(qi, tb, block)
            dq_ref[rows, :] = dq_acc[...].astype(dq_ref.dtype)
        dk_ref[...] = dk_acc[...].astype(dk_ref.dtype)
        dv_ref[...] = dv_acc[...].astype(dv_ref.dtype)

    hp = pl.BlockSpec((T, HEAD_PAD), lambda h: (0, h))
    hv = pl.BlockSpec((T, V_DIM), lambda h: (0, h))
    return pl.pallas_call(
        body, name=name, grid=(H,),
        in_specs=[hp, hp, hv, hv, hv, pl.BlockSpec((T, LANES), lambda h: (0, h))],
        out_specs=[hp, hp, hv],
        out_shape=[S((T, H * HEAD_PAD), BF16), S((T, H * HEAD_PAD), BF16), S((T, H * V_DIM), BF16)],
        scratch_shapes=[pltpu.VMEM((tb, HEAD_PAD), F32), pltpu.VMEM((T, HEAD_PAD), F32), pltpu.VMEM((T, V_DIM), F32)],
        compiler_params=_cparams(("parallel",)),
    )(qf, kf, v, do, o, lse)


CONV_ROWS = 256
CONV_COLS = 128


def _conv_chunks(T):
    rc = min(CONV_ROWS, T)
    return [(r, rc) for r in range(0, T, rc)]


def _causal_conv(pad_ref, lead, w_ref, width, r0, rc):
    acc = None
    for k in range(width):
        term = w_ref[k:k + 1, :] * pad_ref[pl.ds(lead + r0 - (width - 1) + k, rc), :]
        acc = term if acc is None else acc + term
    return acc


def _anticausal_conv(pad_ref, w_ref, width, r0, rc):
    acc = None
    for k in range(width):
        term = w_ref[k:k + 1, :] * pad_ref[pl.ds(r0 + (width - 1) - k, rc), :]
        acc = term if acc is None else acc + term
    return acc


def _conv_wgrad(dpad_ref, xpad_ref, lead, width, T, dw_ref):
    for k in range(width):
        tot = None
        for r0, rc in _conv_chunks(T):
            part = jnp.sum(dpad_ref[pl.ds(r0, rc), :] * xpad_ref[pl.ds(lead + r0 - (width - 1) + k, rc), :],
                           axis=0, keepdims=True)
            tot = part if tot is None else tot + part
        dw_ref[k:k + 1, :] = tot


def _sc_fwd(p3, wconv, name):
    _, T, W = p3.shape
    width = wconv.shape[0]
    cw = min(CONV_COLS, W)
    lead = 8

    def body(p_ref, w_ref, g_ref, pad):
        pad[0:lead, :] = jnp.zeros((lead, cw), F32)
        for r0, rc in _conv_chunks(T):
            pad[pl.ds(lead + r0, rc), :] = p_ref[1, pl.ds(r0, rc), :] * p_ref[2, pl.ds(r0, rc), :]
        for r0, rc in _conv_chunks(T):
            rows = pl.ds(r0, rc)
            y = p_ref[0, rows, :] * _causal_conv(pad, lead, w_ref, width, r0, rc)
            g_ref[rows, :] = (y * _silu(p_ref[3, rows, :])).astype(g_ref.dtype)

    return pl.pallas_call(
        body, name=name, grid=(W // cw,),
        in_specs=[pl.BlockSpec((4, T, cw), lambda j: (0, 0, j)), pl.BlockSpec((width, cw), lambda j: (0, j))],
        out_specs=pl.BlockSpec((T, cw), lambda j: (0, j)),
        out_shape=S((T, W), BF16),
        scratch_shapes=[pltpu.VMEM((T + lead, cw), F32)],
        compiler_params=_cparams(("parallel",)),
    )(p3, wconv)


def _sc_bwd(p3, wconv, dg, name):
    _, T, W = p3.shape
    width = wconv.shape[0]
    cw = min(CONV_COLS, W)
    lead = 8

    def body(p_ref, w_ref, dg_ref, dp_ref, dw_ref, cupad, dvpad):
        cupad[0:lead, :] = jnp.zeros((lead, cw), F32)
        dvpad[pl.ds(T, lead), :] = jnp.zeros((lead, cw), F32)
        for r0, rc in _conv_chunks(T):
            cupad[pl.ds(lead + r0, rc), :] = p_ref[1, pl.ds(r0, rc), :] * p_ref[2, pl.ds(r0, rc), :]
        for r0, rc in _conv_chunks(T):
            rows = pl.ds(r0, rc)
            b, z, dg_ = p_ref[0, rows, :], p_ref[3, rows, :], dg_ref[rows, :]
            v = _causal_conv(cupad, lead, w_ref, width, r0, rc)
            dy = dg_ * _silu(z)
            dp_ref[3, rows, :] = (dg_ * b * v * _dsilu(z)).astype(dp_ref.dtype)
            dp_ref[0, rows, :] = (dy * v).astype(dp_ref.dtype)
            dvpad[rows, :] = dy * b
        for r0, rc in _conv_chunks(T):
            rows = pl.ds(r0, rc)
            dcu = _anticausal_conv(dvpad, w_ref, width, r0, rc)
            dp_ref[1, rows, :] = (dcu * p_ref[2, rows, :]).astype(dp_ref.dtype)
            dp_ref[2, rows, :] = (dcu * p_ref[1, rows, :]).astype(dp_ref.dtype)
        _conv_wgrad(dvpad, cupad, lead, width, T, dw_ref)

    return pl.pallas_call(
        body, name=name, grid=(W // cw,),
        in_specs=[pl.BlockSpec((4, T, cw), lambda j: (0, 0, j)), pl.BlockSpec((width, cw), lambda j: (0, j)),
                  pl.BlockSpec((T, cw), lambda j: (0, j))],
        out_specs=[pl.BlockSpec((4, T, cw), lambda j: (0, 0, j)), pl.BlockSpec((width, cw), lambda j: (0, j))],
        out_shape=[S((4, T, W), BF16), S((width, W), F32)],
        scratch_shapes=[pltpu.VMEM((T + lead, cw), F32), pltpu.VMEM((T + lead, cw), F32)],
        compiler_params=_cparams(("parallel",)),
    )(p3, wconv, dg)


def _gm_common(p_ref, lng_ref, lnb_ref):
    ug = _gelu(p_ref[0])
    vg = _gelu(p_ref[1])
    mu = jnp.mean(vg, axis=-1, keepdims=True)
    xc = vg - mu
    rstd = lax.rsqrt(jnp.mean(xc * xc, axis=-1, keepdims=True) + LN_EPS)
    xhat = xc * rstd
    vn = xhat * lng_ref[...] + lnb_ref[...]
    return ug, xhat, rstd, vn


def _gm_mix_weights(ws_ref, g):
    row = lax.broadcasted_iota(jnp.int32, (GM_CHUNK, GM_CHUNK), 0)
    col = lax.broadcasted_iota(jnp.int32, (GM_CHUNK, GM_CHUNK), 1)
    return jnp.where(col <= row, ws_ref[g], 0.0).astype(BF16)


def _gm_fwd(p3, lng, lnb, ws, bs_t, name):
    _, T, W = p3.shape
    gw = W // GM_GROUPS

    def body(p_ref, lng_ref, lnb_ref, ws_ref, bs_ref, g_ref):
        ug, _, _, vn = _gm_common(p_ref, lng_ref, lnb_ref)
        sz = _silu(p_ref[2])
        vnb = vn.astype(BF16)
        for g in range(GM_GROUPS):
            cols = slice(g * gw, (g + 1) * gw)
            s = jnp.dot(_gm_mix_weights(ws_ref, g), vnb[:, cols], preferred_element_type=F32) + bs_ref[:, g:g + 1]
            g_ref[:, cols] = (ug[:, cols] * s * sz[:, cols]).astype(g_ref.dtype)

    return pl.pallas_call(
        body, name=name, grid=(T // GM_CHUNK,),
        in_specs=[pl.BlockSpec((3, GM_CHUNK, W), lambda i: (0, i, 0)), pl.BlockSpec((1, W), lambda i: (0, 0)),
                  pl.BlockSpec((1, W), lambda i: (0, 0)),
                  pl.BlockSpec((GM_GROUPS, GM_CHUNK, GM_CHUNK), lambda i: (0, 0, 0)),
                  pl.BlockSpec((GM_CHUNK, GM_GROUPS), lambda i: (0, 0))],
        out_specs=pl.BlockSpec((GM_CHUNK, W), lambda i: (i, 0)),
        out_shape=S((T, W), BF16), compiler_params=_cparams(("parallel",)),
    )(p3, lng.reshape(1, W), lnb.reshape(1, W), ws, bs_t)


def _gm_bwd(p3, lng, lnb, ws, bs_t, dg, name):
    _, T, W = p3.shape
    gw = W // GM_GROUPS

    def body(p_ref, lng_ref, lnb_ref, ws_ref, bs_ref, dg_ref, dp_ref, dlng_ref, dlnb_ref, dws_ref, dbs_ref, dvn_s):
        i = pl.program_id(0)

        @pl.when(i == 0)
        def _():
            dlng_ref[...] = jnp.zeros_like(dlng_ref)
            dlnb_ref[...] = jnp.zeros_like(dlnb_ref)
            dws_ref[...] = jnp.zeros_like(dws_ref)
            dbs_ref[...] = jnp.zeros_like(dbs_ref)

        ug, xhat, rstd, vn = _gm_common(p_ref, lng_ref, lnb_ref)
        z = p_ref[2]
        dg_ = dg_ref[...]
        dy = dg_ * _silu(z)
        vnb = vn.astype(BF16)
        row = lax.broadcasted_iota(jnp.int32, (GM_CHUNK, GM_CHUNK), 0)
        col = lax.broadcasted_iota(jnp.int32, (GM_CHUNK, GM_CHUNK), 1)
        dbs = jnp.zeros((GM_CHUNK, LANES), F32)
        for g in range(GM_GROUPS):
            cols = slice(g * gw, (g + 1) * gw)
            wm = _gm_mix_weights(ws_ref, g)
            s = jnp.dot(wm, vnb[:, cols], preferred_element_type=F32) + bs_ref[:, g:g + 1]
            dp_ref[2, :, cols] = (dg_[:, cols] * ug[:, cols] * s * _dsilu(z[:, cols])).astype(dp_ref.dtype)
            dp_ref[0, :, cols] = (dy[:, cols] * s * _dgelu(p_ref[0, :, cols])).astype(dp_ref.dtype)
            ds = dy[:, cols] * ug[:, cols]
            dsb = ds.astype(BF16)
            dwm = lax.dot_general(dsb, vnb[:, cols], (((1,), (1,)), ((), ())), preferred_element_type=F32)
            dws_ref[g] += jnp.where(col <= row, dwm, 0.0)
            dbs = dbs + jnp.where(col == g, jnp.sum(ds, axis=-1, keepdims=True), 0.0)
            dvn_s[:, cols] = lax.dot_general(wm, dsb, (((0,), (0,)), ((), ())), preferred_element_type=F32)
        dbs_ref[...] += dbs
        dvn = dvn_s[...]
        dlng_ref[...] += jnp.sum(dvn * xhat, axis=0, keepdims=True)
        dlnb_ref[...] += jnp.sum(dvn, axis=0, keepdims=True)
        dxh = dvn * lng_ref[...]
        dvg = rstd * (dxh - jnp.mean(dxh, axis=-1, keepdims=True) - xhat * jnp.mean(dxh * xhat, axis=-1, keepdims=True))
        dp_ref[1] = (dvg * _dgelu(p_ref[1])).astype(dp_ref.dtype)

    row1 = pl.BlockSpec((1, W), lambda i: (0, 0))
    return pl.pallas_call(
        body, name=name, grid=(T // GM_CHUNK,),
        in_specs=[pl.BlockSpec((3, GM_CHUNK, W), lambda i: (0, i, 0)), row1, row1,
                  pl.BlockSpec((GM_GROUPS, GM_CHUNK, GM_CHUNK), lambda i: (0, 0, 0)),
                  pl.BlockSpec((GM_CHUNK, GM_GROUPS), lambda i: (0, 0)),
                  pl.BlockSpec((GM_CHUNK, W), lambda i: (i, 0))],
        out_specs=[pl.BlockSpec((3, GM_CHUNK, W), lambda i: (0, i, 0)), row1, row1,
                   pl.BlockSpec((GM_GROUPS, GM_CHUNK, GM_CHUNK), lambda i: (0, 0, 0)),
                   pl.BlockSpec((GM_CHUNK, LANES), lambda i: (0, 0))],
        out_shape=[S((3, T, W), BF16), S((1, W), F32), S((1, W), F32),
                   S((GM_GROUPS, GM_CHUNK, GM_CHUNK), F32), S((GM_CHUNK, LANES), F32)],
        scratch_shapes=[pltpu.VMEM((GM_CHUNK, W), F32)],
        compiler_params=_cparams(("arbitrary",)),
    )(p3, lng.reshape(1, W), lnb.reshape(1, W), ws, bs_t, dg)


def _cf_conv_fwd(p3, wdw, bdw, name):
    _, T, W = p3.shape
    width = wdw.shape[0]
    cw = min(CONV_COLS, W)
    lead = 32

    def body(p_ref, w_ref, b_ref, y_ref, pad):
        pad[0:lead, :] = jnp.zeros((lead, cw), F32)
        for r0, rc in _conv_chunks(T):
            rows = pl.ds(r0, rc)
            pad[pl.ds(lead + r0, rc), :] = p_ref[0, rows, :] * jax.nn.sigmoid(p_ref[1, rows, :])
        for r0, rc in _conv_chunks(T):
            y_ref[pl.ds(r0, rc), :] = _causal_conv(pad, lead, w_ref, width, r0, rc) + b_ref[...]

    return pl.pallas_call(
        body, name=name, grid=(W // cw,),
        in_specs=[pl.BlockSpec((2, T, cw), lambda j: (0, 0, j)), pl.BlockSpec((width, cw), lambda j: (0, j)),
                  pl.BlockSpec((1, cw), lambda j: (0, j))],
        out_specs=pl.BlockSpec((T, cw), lambda j: (0, j)),
        out_shape=S((T, W), F32),
        scratch_shapes=[pltpu.VMEM((T + lead, cw), F32)],
        compiler_params=_cparams(("parallel",)),
    )(p3, wdw, bdw.reshape(1, W))


def _cf_ln(y1_ref, lng_ref, lnb_ref):
    y1 = y1_ref[...]
    mu = jnp.mean(y1, axis=-1, keepdims=True)
    xc = y1 - mu
    rstd = lax.rsqrt(jnp.mean(xc * xc, axis=-1, keepdims=True) + LN_EPS)
    xhat = xc * rstd
    return xhat, rstd, xhat * lng_ref[...] + lnb_ref[...]


def _cf_gate_fwd(y1, p3, lng, lnb, name):
    T, W = y1.shape
    tr = _pick(T, 256)

    def body(y1_ref, z_ref, lng_ref, lnb_ref, g_ref):
        _, _, y2 = _cf_ln(y1_ref, lng_ref, lnb_ref)
        g_ref[...] = (_silu(y2) * _silu(z_ref[...])).astype(g_ref.dtype)

    row1 = pl.BlockSpec((1, W), lambda i: (0, 0))
    return pl.pallas_call(
        body, name=name, grid=(T // tr,),
        in_specs=[pl.BlockSpec((tr, W), lambda i: (i, 0)), pl.BlockSpec((None, tr, W), lambda i: (2, i, 0)), row1, row1],
        out_specs=pl.BlockSpec((tr, W), lambda i: (i, 0)),
        out_shape=S((T, W), BF16), compiler_params=_cparams(("parallel",)),
    )(y1, p3, lng.reshape(1, W), lnb.reshape(1, W))


def _cf_gate_bwd(y1, p3, lng, lnb, dg, name):
    T, W = y1.shape
    tr = _pick(T, 128)

    def body(y1_ref, z_ref, lng_ref, lnb_ref, dg_ref, dz_ref, dy1_ref, dlng_ref, dlnb_ref):
        i = pl.program_id(0)

        @pl.when(i == 0)
        def _():
            dlng_ref[...] = jnp.zeros_like(dlng_ref)
            dlnb_ref[...] = jnp.zeros_like(dlnb_ref)

        xhat, rstd, y2 = _cf_ln(y1_ref, lng_ref, lnb_ref)
        z, dg_ = z_ref[...], dg_ref[...]
        dz_ref[...] = (dg_ * _silu(y2) * _dsilu(z)).astype(dz_ref.dtype)
        dy2 = dg_ * _silu(z) * _dsilu(y2)
        dlng_ref[...] += jnp.sum(dy2 * xhat, axis=0, keepdims=True)
        dlnb_ref[...] += jnp.sum(dy2, axis=0, keepdims=True)
        dxh = dy2 * lng_ref[...]
        dy1_ref[...] = rstd * (dxh - jnp.mean(dxh, axis=-1, keepdims=True)
                               - xhat * jnp.mean(dxh * xhat, axis=-1, keepdims=True))

    row1 = pl.BlockSpec((1, W), lambda i: (0, 0))
    blk = pl.BlockSpec((tr, W), lambda i: (i, 0))
    return pl.pallas_call(
        body, name=name, grid=(T // tr,),
        in_specs=[blk, pl.BlockSpec((None, tr, W), lambda i: (2, i, 0)), row1, row1, blk],
        out_specs=[blk, blk, row1, row1],
        out_shape=[S((T, W), BF16), S((T, W), F32), S((1, W), F32), S((1, W), F32)],
        compiler_params=_cparams(("arbitrary",)),
    )(y1, p3, lng.reshape(1, W), lnb.reshape(1, W), dg)


def _cf_conv_bwd(p3, wdw, dy1, dz, name):
    _, T, W = p3.shape
    width = wdw.shape[0]
    cw = min(CONV_COLS, W)
    lead = 32

    def body(p_ref, w_ref, dy1_ref, dz_ref, dp_ref, dw_ref, db_ref, y0pad, dpad):
        y0pad[0:lead, :] = jnp.zeros((lead, cw), F32)
        dpad[pl.ds(T, lead), :] = jnp.zeros((lead, cw), F32)
        bsum = None
        for r0, rc in _conv_chunks(T):
            rows = pl.ds(r0, rc)
            y0pad[pl.ds(lead + r0, rc), :] = p_ref[0, rows, :] * jax.nn.sigmoid(p_ref[1, rows, :])
            d = dy1_ref[rows, :]
            dpad[rows, :] = d
            part = jnp.sum(d, axis=0, keepdims=True)
            bsum = part if bsum is None else bsum + part
        db_ref[...] = bsum
        for r0, rc in _conv_chunks(T):
            rows = pl.ds(r0, rc)
            dy0 = _anticausal_conv(dpad, w_ref, width, r0, rc)
            a = p_ref[0, rows, :]
            sg = jax.nn.sigmoid(p_ref[1, rows, :])
            dp_ref[0, rows, :] = (dy0 * sg).astype(dp_ref.dtype)
            dp_ref[1, rows, :] = (dy0 * a * sg * (1.0 - sg)).astype(dp_ref.dtype)
            dp_ref[2, rows, :] = dz_ref[rows, :]
        _conv_wgrad(dpad, y0pad, lead, width, T, dw_ref)

    return pl.pallas_call(
        body, name=name, grid=(W // cw,),
        in_specs=[pl.BlockSpec((2, T, cw), lambda j: (0, 0, j)), pl.BlockSpec((width, cw), lambda j: (0, j)),
                  pl.BlockSpec((T, cw), lambda j: (0, j)), pl.BlockSpec((T, cw), lambda j: (0, j))],
        out_specs=[pl.BlockSpec((3, T, cw), lambda j: (0, 0, j)), pl.BlockSpec((width, cw), lambda j: (0, j)),
                   pl.BlockSpec((1, cw), lambda j: (0, j))],
        out_shape=[S((3, T, W), BF16), S((width, W), F32), S((1, W), F32)],
        scratch_shapes=[pltpu.VMEM((T + lead, cw), F32), pltpu.VMEM((T + lead, cw), F32)],
        compiler_params=_cparams(("parallel",)),
    )(p3, wdw, dy1, dz)


def _rows_call(body, ins, out_dtypes, name, row_pref=256):
    R, C = ins[0].shape
    tr = _pick(R, row_pref) if R % 8 == 0 else R
    while tr > 8 and tr * C * 4 * (len(ins) + len(out_dtypes)) * 2 > VMEM_LIMIT // 2 and tr % 16 == 0:
        tr //= 2
    blk = pl.BlockSpec((tr, C), lambda i: (i, 0))
    return pl.pallas_call(
        body, name=name, grid=(R // tr,), in_specs=[blk] * len(ins), out_specs=[blk] * len(out_dtypes),
        out_shape=[S((R, C), dt) for dt in out_dtypes], compiler_params=_cparams(("parallel",)),
    )(*ins)


def _pair_sum(g_half, r, name):
    def body(a_ref, b_ref, o_ref):
        o_ref[...] = (a_ref[...] + b_ref[...]).astype(BF16)
    return _rows_call(body, [g_half, r], [BF16], name)[0]


def _chip_sum(rc, name):
    J, R, C = rc.shape
    tr = _pick(R, 256)

    def body(r_ref, o_ref):
        acc = r_ref[0].astype(F32)
        for j in range(1, J):
            acc = acc + r_ref[j].astype(F32)
        o_ref[...] = acc

    return pl.pallas_call(
        body, name=name, grid=(R // tr,),
        in_specs=[pl.BlockSpec((J, tr, C), lambda i: (0, i, 0))], out_specs=pl.BlockSpec((tr, C), lambda i: (i, 0)),
        out_shape=S((R, C), F32), compiler_params=_cparams(("parallel",)),
    )(rc)


def _slot_sum(slots, name):
    J, R, C = slots.shape
    tr = _pick(R, 512)

    def body(r_ref, o_ref):
        acc = r_ref[0]
        for j in range(1, J):
            acc = acc + r_ref[j]
        o_ref[...] = acc

    return pl.pallas_call(
        body, name=name, grid=(R // tr,),
        in_specs=[pl.BlockSpec((J, tr, C), lambda i: (0, i, 0))], out_specs=pl.BlockSpec((tr, C), lambda i: (i, 0)),
        out_shape=S((R, C), F32), compiler_params=_cparams(("parallel",)),
    )(slots)


def _adamw(w, g, m, v, name):
    def body(w_ref, g_ref, m_ref, v_ref, d_ref, nm_ref, nv_ref):
        g_ = g_ref[...]
        nm = ADAM_B1 * m_ref[...] + (1.0 - ADAM_B1) * g_
        nv = ADAM_B2 * v_ref[...] + (1.0 - ADAM_B2) * (g_ * g_)
        m_hat = nm / (1.0 - ADAM_B1 ** ADAM_STEP)
        v_hat = nv / (1.0 - ADAM_B2 ** ADAM_STEP)
        d_ref[...] = -ADAM_LR * (m_hat / (jnp.sqrt(v_hat) + ADAM_EPS) + ADAM_WD * w_ref[...])
        nm_ref[...] = nm
        nv_ref[...] = nv
    return _rows_call(body, [w, g, m, v], [F32, F32, F32], name)


ANY = pl.BlockSpec(memory_space=pl.ANY)


def _place():
    x, y, c = lax.axis_index("x"), lax.axis_index("y"), lax.axis_index("c")
    return x, y, c


def _other_chips(x, y):
    return [(1 - x, y), (x, 1 - y), (1 - x, 1 - y)]


HBM = pl.BlockSpec(memory_space=pltpu.HBM)
SEM = pl.BlockSpec(memory_space=pltpu.SEMAPHORE)
DATAFLOW = pltpu.SideEffectType.DATAFLOW_SIDE_EFFECTING


def _in_hbm(a):
    return pltpu.with_memory_space_constraint(a, pltpu.HBM)


def _half_rows(k_rows, which):
    return pl.ds(which * (k_rows // 2), k_rows // 2)


def _chip_copies(srcs, lands, send_sems, recv_sems, gather):
    x, y, c = _place()
    me_chip = 2 * x + y
    out = []
    for w, (src, land) in enumerate(zip(srcs, lands)):
        for j, (cx, cy) in enumerate(_other_chips(x, y)):
            them = 2 * cx + cy
            sems = dict(send_sem=send_sems.at[3 * w + j], recv_sem=recv_sems.at[3 * w + j],
                        device_id=(cx, cy, c), device_id_type=MESH_ID)
            if gather:
                rows = _half_rows(src.shape[0], c)
                go = pltpu.make_async_remote_copy(src_ref=src.at[rows], dst_ref=land.at[me_chip, rows], **sems)
                arrive = pltpu.make_async_remote_copy(src_ref=src.at[rows], dst_ref=land.at[them, rows], **sems)
            else:
                go = pltpu.make_async_remote_copy(src_ref=src.at[them], dst_ref=land.at[me_chip], **sems)
                arrive = pltpu.make_async_remote_copy(src_ref=src.at[them], dst_ref=land.at[them], **sems)
            out.append((go, arrive))
    return out


def _chip_copies_start(srcs, lands, gather, name):
    nw = len(srcs)

    def body(*refs):
        ins, lnd = refs[:nw], refs[nw:2 * nw]
        send_sems, recv_sems, token = refs[2 * nw], refs[2 * nw + 1], refs[-1]
        for go, _ in _chip_copies(ins, lnd, send_sems, recv_sems, gather):
            go.start()
        token[...] = jnp.zeros_like(token)

    thru = [pltpu.HBM(a.shape, a.dtype) for a in list(srcs) + list(lands)]
    res = pl.pallas_call(
        body, name=name, in_specs=[HBM] * (2 * nw),
        out_specs=[SEM, SEM] + [HBM] * (2 * nw) + [pl.BlockSpec(memory_space=pltpu.VMEM)],
        out_shape=[pltpu.SemaphoreType.DMA((3 * nw,)), pltpu.SemaphoreType.DMA((3 * nw,))] + thru + [S((8, LANES), F32)],
        input_output_aliases={i: 2 + i for i in range(2 * nw)},
        compiler_params=pltpu.CompilerParams(has_side_effects=DATAFLOW),
    )(*[_in_hbm(a) for a in list(srcs) + list(lands)])
    return res[0], res[1], list(res[2:2 + nw]), list(res[2 + nw:2 + 2 * nw]), res[-1]


def _chip_copies_wait(send_sems, recv_sems, srcs, lands, after, gather, name):
    nw = len(srcs)

    def body(*refs):
        ins, lnd = refs[:nw], refs[nw:2 * nw]
        send, recv = refs[2 * nw], refs[2 * nw + 1]
        for _, arrive in _chip_copies(ins, lnd, send, recv, gather):
            arrive.wait_send()
            arrive.wait_recv()

    res = pl.pallas_call(
        body, name=name, in_specs=[HBM] * (2 * nw) + [SEM, SEM, ANY], out_specs=[HBM] * (2 * nw),
        out_shape=[pltpu.HBM(a.shape, a.dtype) for a in list(srcs) + list(lands)],
        input_output_aliases={i: i for i in range(2 * nw)},
        compiler_params=pltpu.CompilerParams(has_side_effects=DATAFLOW),
    )(*srcs, *lands, send_sems, recv_sems, after)
    return list(res[nw:])


def _gather_forward(lands, name):
    nw = len(lands)

    def body(*refs):
        ins, outs = refs[:nw], refs[nw:2 * nw]
        send_sems, recv_sems = refs[2 * nw:]
        x, y, c = _place()
        sibling = (x, y, 1 - c)
        cps = []
        for w in range(nw):
            kr = ins[w].shape[1]
            for j, (cx, cy) in enumerate(_other_chips(x, y)):
                them = 2 * cx + cy
                sems = dict(send_sem=send_sems.at[3 * w + j], recv_sem=recv_sems.at[3 * w + j],
                            device_id=sibling, device_id_type=MESH_ID)
                mine, theirs = _half_rows(kr, c), _half_rows(kr, 1 - c)
                go = pltpu.make_async_remote_copy(src_ref=ins[w].at[them, mine], dst_ref=outs[w].at[them, mine], **sems)
                go.start()
                cps.append((go, pltpu.make_async_remote_copy(
                    src_ref=ins[w].at[them, theirs], dst_ref=outs[w].at[them, theirs], **sems)))
        for go, arrive in cps:
            arrive.wait_recv()
            go.wait_send()

    return pl.pallas_call(
        body, name=name, in_specs=[ANY] * nw, out_specs=[ANY] * nw,
        out_shape=[S(a.shape, a.dtype) for a in lands],
        scratch_shapes=[pltpu.SemaphoreType.DMA((3 * nw,)), pltpu.SemaphoreType.DMA((3 * nw,))],
        input_output_aliases={i: i for i in range(nw)},
        compiler_params=pltpu.CompilerParams(has_side_effects=True),
    )(*lands)


def _core_exchange_halves(grads, name):
    nw = len(grads)

    def body(*refs):
        ins, outs = refs[:nw], refs[nw:2 * nw]
        send_sems, recv_sems = refs[2 * nw:]
        x, y, c = _place()
        sibling = (x, y, 1 - c)
        cps = []
        for w in range(nw):
            kh = ins[w].shape[1] // 2
            cp = pltpu.make_async_remote_copy(
                src_ref=ins[w].at[:, pl.ds((1 - c) * kh, kh), :], dst_ref=outs[w],
                send_sem=send_sems.at[w], recv_sem=recv_sems.at[w], device_id=sibling, device_id_type=MESH_ID)
            cp.start()
            cps.append(cp)
        for cp in cps:
            cp.wait()

    return pl.pallas_call(
        body, name=name, in_specs=[ANY] * nw, out_specs=[ANY] * nw,
        out_shape=[S((g.shape[0], g.shape[1] // 2, g.shape[2]), g.dtype) for g in grads],
        scratch_shapes=[pltpu.SemaphoreType.DMA((nw,)), pltpu.SemaphoreType.DMA((nw,))],
        compiler_params=pltpu.CompilerParams(has_side_effects=True),
    )(*grads)


def _core_join_halves(halves, name):
    nw = len(halves)

    def body(*refs):
        ins, outs = refs[:nw], refs[nw:2 * nw]
        send_sems, recv_sems = refs[2 * nw:]
        x, y, c = _place()
        sibling = (x, y, 1 - c)
        rem = []
        for w in range(nw):
            r = ins[w].shape[0]
            mine = outs[w].at[pl.ds(c * r, r)]
            cp = pltpu.make_async_remote_copy(
                src_ref=ins[w], dst_ref=mine, send_sem=send_sems.at[w], recv_sem=recv_sems.at[w],
                device_id=sibling, device_id_type=MESH_ID)
            cp.start()
            rem.append(cp)
        for w in range(nw):
            r = ins[w].shape[0]
            theirs = outs[w].at[pl.ds((1 - c) * r, r)]
            pltpu.make_async_remote_copy(
                src_ref=theirs, dst_ref=theirs, send_sem=send_sems.at[w], recv_sem=recv_sems.at[w],
                device_id=sibling, device_id_type=MESH_ID).wait_recv()
        for cp in rem:
            cp.wait_send()

    return pl.pallas_call(
        body, name=name, in_specs=[ANY] * nw, out_specs=[ANY] * nw,
        out_shape=[S((2 * h.shape[0],) + h.shape[1:], h.dtype) for h in halves],
        scratch_shapes=[pltpu.SemaphoreType.DMA((nw,)), pltpu.SemaphoreType.DMA((nw,))],
        compiler_params=pltpu.CompilerParams(has_side_effects=True),
    )(*halves)


def _broadcast_all(buf, name):
    def body(in_ref, out_ref, send_sems, recv_sems, loc_sem):
        x, y, c = _place()
        me = 4 * x + 2 * y + c
        loc = pltpu.make_async_copy(in_ref, out_ref.at[me], loc_sem)
        loc.start()
        cps = []
        for k in range(1, N_DEV):
            fx, fy, fc = (k >> 2) & 1, (k >> 1) & 1, k & 1
            px, py, pc = x ^ fx, y ^ fy, c ^ fc
            cp = pltpu.make_async_remote_copy(
                src_ref=in_ref, dst_ref=out_ref.at[me], send_sem=send_sems.at[k - 1], recv_sem=recv_sems.at[k - 1],
                device_id=(px, py, pc), device_id_type=MESH_ID)
            cp.start()
            cps.append(cp)
        for k in range(1, N_DEV):
            fx, fy, fc = (k >> 2) & 1, (k >> 1) & 1, k & 1
            px, py, pc = x ^ fx, y ^ fy, c ^ fc
            slot = out_ref.at[4 * px + 2 * py + pc]
            pltpu.make_async_remote_copy(
                src_ref=slot, dst_ref=slot, send_sem=send_sems.at[k - 1], recv_sem=recv_sems.at[k - 1],
                device_id=(px, py, pc), device_id_type=MESH_ID).wait_recv()
        for cp in cps:
            cp.wait_send()
        loc.wait()

    return pl.pallas_call(
        body, name=name, in_specs=[ANY], out_specs=ANY,
        out_shape=S((N_DEV,) + buf.shape, buf.dtype),
        scratch_shapes=[pltpu.SemaphoreType.DMA((N_DEV - 1,)), pltpu.SemaphoreType.DMA((N_DEV - 1,)),
                        pltpu.SemaphoreType.DMA],
        compiler_params=pltpu.CompilerParams(has_side_effects=True),
    )(buf)


PACK_ALIGN = 8 * LANES


def _pack(arrs):
    flat = []
    for a in arrs:
        f = a.reshape(-1).astype(F32)
        pad = (-f.shape[0]) % PACK_ALIGN
        flat.append(jnp.pad(f, (0, pad)) if pad else f)
    return jnp.concatenate(flat).reshape(-1, LANES)


def _unpack(buf, shapes):
    out, off = [], 0
    flat = buf.reshape(-1)
    for shp in shapes:
        n = math.prod(shp)
        out.append(flat[off:off + n].reshape(shp))
        off += n + ((-n) % PACK_ALIGN)
    return out


def kernel(x, positions, norm_pre, norm_post, w_in_mla, mla_q_norm, w_uq, mla_kv_norm, w_ukv, w_out_mla, w_in_sc, sc_conv, w_out_sc, w_in_gm, gm_ln_g, gm_ln_b, gm_w_s, gm_b_s, w_out_gm, w_in_cf, cf_dw, cf_dw_b, cf_ln_g, cf_ln_b, w_out_cf, loss_target, m_norm_pre, m_norm_post, m_w_in_mla, m_mla_q_norm, m_w_uq, m_mla_kv_norm, m_w_ukv, m_w_out_mla, m_w_in_sc, m_sc_conv, m_w_out_sc, m_w_in_gm, m_gm_ln_g, m_gm_ln_b, m_gm_w_s, m_gm_b_s, m_w_out_gm, m_w_in_cf, m_cf_dw, m_cf_dw_b, m_cf_ln_g, m_cf_ln_b, m_w_out_cf, v_norm_pre, v_norm_post, v_w_in_mla, v_mla_q_norm, v_w_uq, v_mla_kv_norm, v_w_ukv, v_w_out_mla, v_w_in_sc, v_sc_conv, v_w_out_sc, v_w_in_gm, v_gm_ln_g, v_gm_ln_b, v_gm_w_s, v_gm_b_s, v_w_out_gm, v_w_in_cf, v_cf_dw, v_cf_dw_b, v_cf_ln_g, v_cf_ln_b, v_w_out_cf):
    loc = dict(locals())
    wts = {n: loc[n] for n in WEIGHTS}
    mom_m = {n: loc["m_" + n] for n in WEIGHTS}
    mom_v = {n: loc["v_" + n] for n in WEIGHTS}

    T, D = x.shape[1], x.shape[2]
    xin = x.reshape(T, D)
    target = loss_target.reshape(T, D)
    q_rank, kv_rank = mla_q_norm.shape[1], mla_kv_norm.shape[1]
    H = (w_uq.shape[2] * N_CHIPS) // (NOPE_DIM + ROPE_DIM)
    hv = H * V_DIM
    c_kr = q_rank + kv_rank
    wa_cols = c_kr + ROPE_DIM
    wa_pad = wa_cols + (LANES - ROPE_DIM)
    chip = 2 * lax.axis_index("x") + lax.axis_index("y")

    c = lax.axis_index("c")
    gather_started = []
    for li, names in enumerate(LAYER_BIG):
        own = [wts[n][0].astype(BF16) for n in names]
        lands = [lax.dynamic_update_slice(lax.empty((N_CHIPS,) + s.shape, BF16), s[None], (chip, 0, 0)) for s in own]
        gather_started.append(_chip_copies_start(own, lands, True, f"gather_start_{li}"))
    started_token = sum(st[4][0, 0] for st in gather_started)
    gw = {}

    def gathered_weights(li, after):
        send_sems, recv_sems, own, lands, _ = gather_started[li]
        landed = _chip_copies_wait(send_sems, recv_sems, own, lands, after, True, f"gather_wait_{li}")
        gw.update(zip(LAYER_BIG[li], _gather_forward(landed, f"gather_forward_{li}")))

    small_sh_shapes = [wts[n][0].shape for n in SMALL_SHARDED]
    slots = _broadcast_all(_pack([wts[n][0] for n in SMALL_SHARDED]), "gather_small")
    per_chip = [_unpack(slots[2 * k], small_sh_shapes) for k in range(N_CHIPS)]
    sp = {n: jnp.concatenate([per_chip[k][i] for k in range(N_CHIPS)], axis=-1) for i, n in enumerate(SMALL_SHARDED)}

    def cols_major(w4):
        return jnp.transpose(w4, (1, 0, 2)).reshape(w4.shape[1], -1)

    half = ROPE_DIM // 2
    inv_freq = ROPE_THETA ** (-jnp.arange(half, dtype=F32) / half)
    invf = jnp.concatenate([inv_freq, inv_freq, jnp.zeros((LANES - ROPE_DIM,), F32)]).reshape(1, LANES)
    tabs = _rope_tables(positions.reshape(T, 1), invf, "rope_tables")
    scale = float(NOPE_DIM + ROPE_DIM) ** -0.5

    xs = [xin]
    saved = []
    w_out = {}
    for i in range(4):
        xi = xs[-1]
        gathered_weights(i, xi)
        w_out[WO_NAMES[i]] = gw[WO_NAMES[i]].reshape(1, -1, D)
        h = _rms_fwd(xi[None], 0, 0, D, norm_pre[i] + started_token if i == 0 else norm_pre[i], f"pre_norm_{i}", BF16)
        if i == 0:
            w_in_full = cols_major(gw['w_in_mla'])
            w_a = jnp.pad(w_in_full[:, :wa_cols], ((0, 0), (0, wa_pad - wa_cols)))[None]
            w_z = w_in_full[:, wa_cols:][None]
            wq = cols_major(gw['w_uq']).reshape(q_rank, H, NOPE_DIM + ROPE_DIM)
            wq = jnp.pad(wq, ((0, 0), (0, 0), (0, HEAD_PAD - NOPE_DIM - ROPE_DIM))).reshape(1, q_rank, H * HEAD_PAD)
            wkv = cols_major(gw['w_ukv'])[None]
            pa = _mm_nn(h, w_a, 1, "mla_in_a")
            pz = _mm_nn(h, w_z, 1, "mla_in_z")
            qn = _rms_fwd(pa, 0, 0, q_rank, mla_q_norm[0], "mla_q_norm", BF16)
            kvn = _rms_fwd(pa, 0, q_rank // kv_rank, kv_rank, mla_kv_norm[0], "mla_kv_norm", BF16)
            q3 = _mm_nn(qn, wq, 1, "mla_q_up")
            kv3 = _mm_nn(kvn, wkv, 1, "mla_kv_up")
            qf, kf, vv = _qkv_layout(q3, kv3, pa, c_kr // LANES, tabs, H, "mla_qkv_layout")
            o, lse = _attn_fwd(qf, kf, vv, H, scale, "mla_attn_fwd")
            g = _gate_fwd(o, pz, "mla_gate_fwd")
            saved.append(dict(h=h, pa=pa, pz=pz, qn=qn, kvn=kvn, qf=qf, kf=kf, vv=vv, o=o, lse=lse, g=g))
            wo = w_out['w_out_mla']
        elif i == 1:
            p3 = _mm_nn(h, gw['w_in_sc'], 4, "sc_in")
            g = _sc_fwd(p3, sp['sc_conv'], "sc_mix_fwd")
            saved.append(dict(h=h, p3=p3, g=g))
            wo = w_out['w_out_sc']
        elif i == 2:
            p3 = _mm_nn(h, gw['w_in_gm'], 3, "gm_in")
            bs_t = jnp.transpose(gm_b_s[0])
            g = _gm_fwd(p3, sp['gm_ln_g'], sp['gm_ln_b'], gm_w_s[0], bs_t, "gm_mix_fwd")
            saved.append(dict(h=h, p3=p3, g=g, bs_t=bs_t))
            wo = w_out['w_out_gm']
        else:
            p3 = _mm_nn(h, gw['w_in_cf'], 3, "cf_in")
            y1 = _cf_conv_fwd(p3, sp['cf_dw'], sp['cf_dw_b'], "cf_conv_fwd")
            g = _cf_gate_fwd(y1, p3, sp['cf_ln_g'], sp['cf_ln_b'], "cf_gate_fwd")
            saved.append(dict(h=h, p3=p3, y1=y1, g=g))
            wo = w_out['w_out_cf']
        yo = _mm_nn(g, wo, 1, f"out_proj_{i}")
        saved[-1]['yo'] = yo
        xs.append(_rms_fwd(yo, 0, 0, D, norm_post[i], f"post_norm_{i}", F32, res=xi))

    dx, loss_local = _loss_head(xs[4], target, "loss_head")
    loss = lax.psum(loss_local, ("x", "y", "c"))

    big_grads = {}
    sgrad = {}
    d_npre, d_npost = [None] * 4, [None] * 4
    scatter_started, scattered = {}, {}
    scatter_token = 0.0

    def start_scatter(li):
        full = [big_grads[n] for n in LAYER_BIG[li]]
        recv = _core_exchange_halves(full, f"grads_core_exchange_{li}")
        pair = []
        for n, gfull, r in zip(LAYER_BIG[li], full, recv):
            J, K, nn = gfull.shape
            kh = K // 2
            mine = lax.dynamic_slice_in_dim(gfull, c * kh, kh, axis=1)
            pair.append(_pair_sum(mine.reshape(J * kh, nn), r.reshape(J * kh, nn), f"pair_sum_{n}").reshape(J, kh, nn))
        lands = [lax.dynamic_update_slice(lax.empty(p.shape, p.dtype), lax.dynamic_slice_in_dim(p, chip, 1, axis=0),
                                          (chip, 0, 0)) for p in pair]
        scatter_started[li] = _chip_copies_start(pair, lands, False, f"scatter_start_{li}")
        return scatter_started[li][4][0, 0]

    def finish_scatter(li, after):
        send_sems, recv_sems, pair, lands, _ = scatter_started[li]
        scattered[li] = _chip_copies_wait(send_sems, recv_sems, pair, lands, after, False, f"scatter_wait_{li}")

    for i in (3, 2, 1, 0):
        sv = saved[i]
        h = sv['h']
        dyo, d_npost[i] = _rms_bwd(sv['yo'], 0, 0, D, norm_post[i] + scatter_token, dx, f"post_norm_bwd_{i}", BF16)
        dyo3 = dyo[None]
        wo_name = WO_NAMES[i]
        dg = _mm_nt(dyo3, w_out[wo_name], f"out_proj_dx_{i}")
        big_grads[wo_name] = _mm_tn(sv['g'], dyo3, 1, f"out_proj_dw_{i}").reshape(N_CHIPS, -1, D)
        if i == 3:
            dz, dy1, sgrad['cf_ln_g'], sgrad['cf_ln_b'] = _cf_gate_bwd(sv['y1'], sv['p3'], sp['cf_ln_g'], sp['cf_ln_b'], dg, "cf_gate_bwd")
            dp3, sgrad['cf_dw'], sgrad['cf_dw_b'] = _cf_conv_bwd(sv['p3'], sp['cf_dw'], dy1, dz, "cf_conv_bwd")
            big_grads['w_in_cf'] = _mm_tn(h, dp3, N_CHIPS, "cf_in_dw")
            dh = _mm_nt(dp3, gw['w_in_cf'], "cf_in_dx")
        elif i == 2:
            dp3, sgrad['gm_ln_g'], sgrad['gm_ln_b'], sgrad['gm_w_s'], dbs_t = _gm_bwd(
                sv['p3'], sp['gm_ln_g'], sp['gm_ln_b'], gm_w_s[0], sv['bs_t'], dg, "gm_mix_bwd")
            sgrad['gm_b_s'] = jnp.transpose(dbs_t[:, :GM_GROUPS])
            big_grads['w_in_gm'] = _mm_tn(h, dp3, N_CHIPS, "gm_in_dw")
            dh = _mm_nt(dp3, gw['w_in_gm'], "gm_in_dx")
        elif i == 1:
            dp3, sgrad['sc_conv'] = _sc_bwd(sv['p3'], sp['sc_conv'], dg, "sc_mix_bwd")
            big_grads['w_in_sc'] = _mm_tn(h, dp3, N_CHIPS, "sc_in_dw")
            dh = _mm_nt(dp3, gw['w_in_sc'], "sc_in_dx")
        else:
            do, dpz = _gate_bwd(dg, sv['o'], sv['pz'], "mla_gate_bwd")
            dqf, dkf, dv = _attn_bwd(sv['qf'], sv['kf'], sv['vv'], do, sv['o'], sv['lse'], H, scale, "mla_attn_bwd")
            dq3, dkv3, dkr = _qkv_layout_bwd(dqf, dkf, dv, tabs, H, "mla_qkv_layout_bwd")
            dqn = _mm_nt(dq3, wq, "mla_q_up_dx")
            dwq = _mm_tn(sv['qn'], dq3, 1, "mla_q_up_dw")
            dkvn = _mm_nt(dkv3, wkv, "mla_kv_up_dx")
            dwkv = _mm_tn(sv['kvn'], dkv3, 1, "mla_kv_up_dw")
            dcq, dqg = _rms_bwd(sv['pa'], 0, 0, q_rank, mla_q_norm[0], dqn, "mla_q_norm_bwd", BF16)
            dckv, dkvg = _rms_bwd(sv['pa'], 0, q_rank // kv_rank, kv_rank, mla_kv_norm[0], dkvn, "mla_kv_norm_bwd", BF16)
            sgrad['mla_q_norm'], sgrad['mla_kv_norm'] = dqg, dkvg
            dpa = jnp.concatenate([dcq, dckv, dkr], axis=1)[None]
            dwa = _mm_tn(h, dpa, 1, "mla_in_a_dw")
            dwz = _mm_tn(h, dpz, 1, "mla_in_z_dw")
            dh_a = _mm_nt(dpa, w_a, "mla_in_a_dx")
            dh = _mm_nt(dpz, w_z, "mla_in_z_dx", add=dh_a)
            dw_in = jnp.concatenate([dwa[0][:, :wa_cols], dwz[0]], axis=1)
            big_grads['w_in_mla'] = jnp.transpose(dw_in.reshape(D, N_CHIPS, -1), (1, 0, 2))
            dwq_ = dwq[0].reshape(q_rank, H, HEAD_PAD)[:, :, :NOPE_DIM + ROPE_DIM].reshape(q_rank, N_CHIPS, -1)
            big_grads['w_uq'] = jnp.transpose(dwq_, (1, 0, 2))
            big_grads['w_ukv'] = jnp.transpose(dwkv[0].reshape(kv_rank, N_CHIPS, -1), (1, 0, 2))
        dx, d_npre[i] = _rms_bwd(xs[i][None], 0, 0, D, norm_pre[i], dh, f"pre_norm_bwd_{i}", F32, res=dx)
        if i < 3:
            finish_scatter(i + 1, dx)
        scatter_token = start_scatter(i)
    finish_scatter(0, dx)
    grad_x = dx.reshape(1, T, D)
    sgrad['norm_pre'] = jnp.concatenate(d_npre, axis=0)
    sgrad['norm_post'] = jnp.concatenate(d_npost, axis=0)

    halves = [_chip_sum(r, f"chip_sum_{n}") for li in range(4) for n, r in zip(LAYER_BIG[li], scattered[li])]
    order = [n for li in range(4) for n in LAYER_BIG[li]]
    assert order == BIG
    joined = _core_join_halves(halves, "grads_core_join")
    joined = [lax.dynamic_update_slice(j, hf, (c * hf.shape[0], 0)) for j, hf in zip(joined, halves)]
    grads = dict(zip(BIG, [j[None] for j in joined]))

    small_full_shapes = [sgrad[n].reshape(wts[n].shape[:-1] + (-1,)).shape for n in SMALL]
    gslots = _broadcast_all(_pack([sgrad[n] for n in SMALL]), "grads_small_exchange")
    gsum = _unpack(_slot_sum(gslots, "grads_small_sum"), small_full_shapes)
    for n, gs in zip(SMALL, gsum):
        if n in SMALL_SHARDED:
            per = wts[n].shape[-1]
            gs = lax.dynamic_slice_in_dim(gs, chip * per, per, axis=gs.ndim - 1)
        grads[n] = gs.reshape(wts[n].shape)

    delta, new_m, new_v = {}, {}, {}
    for n in BIG:
        shp = wts[n].shape
        two_d = (shp[1], shp[2])
        d_, m_, v_ = _adamw(wts[n].reshape(two_d), grads[n].reshape(two_d), mom_m[n].reshape(two_d),
                            mom_v[n].reshape(two_d), f"adamw_{n}")
        delta[n], new_m[n], new_v[n] = d_.reshape(shp), m_.reshape(shp), v_.reshape(shp)
    shapes = [wts[n].shape for n in SMALL]
    d_, m_, v_ = _adamw(_pack([wts[n] for n in SMALL]), _pack([grads[n] for n in SMALL]),
                        _pack([mom_m[n] for n in SMALL]), _pack([mom_v[n] for n in SMALL]), "adamw_small")
    for n, a, b, cc in zip(SMALL, _unpack(d_, shapes), _unpack(m_, shapes), _unpack(v_, shapes)):
        delta[n], new_m[n], new_v[n] = a, b, cc

    return (loss, grad_x, *[grads[n] for n in WEIGHTS], *[delta[n] for n in WEIGHTS],
            *[new_m[n] for n in WEIGHTS], *[new_v[n] for n in WEIGHTS])
```

```python
import functools
import math

import jax
import jax.numpy as jnp
from jax import lax
from jax.experimental import pallas as pl
from jax.experimental.pallas import tpu as pltpu

F32, BF16 = jnp.float32, jnp.bfloat16
S = jax.ShapeDtypeStruct
MESH_ID = pl.DeviceIdType.MESH

V7X_VMEM_BYTES = 64 * 1024 * 1024
VMEM_LIMIT = V7X_VMEM_BYTES - 8 * 1024 * 1024
LANES = 128
N_CHIPS = 4
N_DEV = 8

NORM_EPS = 1e-6
LN_EPS = 1e-5
ROPE_THETA = 10000.0
ROPE_DIM = 64
NOPE_DIM = 128
V_DIM = 128
HEAD_PAD = 256
GM_CHUNK = 128
GM_GROUPS = 8
NEG = -1e30

ADAM_LR, ADAM_B1, ADAM_B2, ADAM_EPS, ADAM_WD, ADAM_STEP = 0.001, 0.9, 0.999, 1e-08, 0.01, 10

FWD_PARAMS = ['x', 'positions', 'norm_pre', 'norm_post', 'w_in_mla', 'mla_q_norm', 'w_uq', 'mla_kv_norm', 'w_ukv',
              'w_out_mla', 'w_in_sc', 'sc_conv', 'w_out_sc', 'w_in_gm', 'gm_ln_g', 'gm_ln_b', 'gm_w_s', 'gm_b_s',
              'w_out_gm', 'w_in_cf', 'cf_dw', 'cf_dw_b', 'cf_ln_g', 'cf_ln_b', 'w_out_cf']
WEIGHTS = FWD_PARAMS[2:]
BIG = ['w_in_mla', 'w_uq', 'w_ukv', 'w_out_mla', 'w_in_sc', 'w_out_sc', 'w_in_gm', 'w_out_gm', 'w_in_cf', 'w_out_cf']
GROUPS = [['w_in_mla'], ['w_uq', 'w_ukv', 'w_out_mla'], ['w_in_sc', 'w_out_sc'], ['w_in_gm', 'w_out_gm'],
          ['w_in_cf', 'w_out_cf']]
LAYER_GROUPS = [[0, 1], [2], [3], [4]]
WO_NAMES = ['w_out_mla', 'w_out_sc', 'w_out_gm', 'w_out_cf']
SMALL = [n for n in WEIGHTS if n not in BIG]
SMALL_SHARDED = ['sc_conv', 'gm_ln_g', 'gm_ln_b', 'cf_dw', 'cf_dw_b', 'cf_ln_g', 'cf_ln_b']


def _cparams(sem=None, **kw):
    return pltpu.CompilerParams(dimension_semantics=sem, vmem_limit_bytes=VMEM_LIMIT, **kw)


def _pick(dim, pref):
    if dim <= pref:
        return dim
    t = (pref // LANES) * LANES
    while t >= LANES and dim % t:
        t -= LANES
    if t >= min(pref, 512):
        return t
    return dim if (dim <= 2048 or t < LANES) else t


def _silu(x):
    return x * jax.nn.sigmoid(x)


def _dsilu(x):
    s = jax.nn.sigmoid(x)
    return s * (1.0 + x * (1.0 - s))


def _gelu(x):
    return 0.5 * x * (1.0 + lax.erf(x * (2.0 ** -0.5)))


def _dgelu(x):
    cdf = 0.5 * (1.0 + lax.erf(x * (2.0 ** -0.5)))
    return cdf + x * jnp.exp(-0.5 * x * x) * ((2.0 * math.pi) ** -0.5)


MM_ONE_DOT = 4096


def _contract_tile(dim, divisible_by, pref_when_split):
    return dim if dim <= MM_ONE_DOT and divisible_by % dim == 0 else _pick(divisible_by, pref_when_split)


def _mm_accumulate(step, nsteps, prod, o_ref, acc, init=None):
    if nsteps == 1:
        r = prod()
        if init is not None:
            r = r + init()
        o_ref[...] = r.astype(o_ref.dtype)
        return

    @pl.when(step == 0)
    def _():
        acc[...] = jnp.zeros_like(acc) if init is None else init()

    acc[...] += prod()

    @pl.when(step == nsteps - 1)
    def _():
        o_ref[...] = acc[...].astype(o_ref.dtype)


def _mm_nn(a, w, np_out, name, out_dtype=F32):
    M, K = a.shape
    J, _, n = w.shape
    N = J * n
    W = N // np_out
    tm, tn = _pick(M, 1024), _pick(math.gcd(W, n), 1024)
    tk = _contract_tile(K, K, 2048)
    nk = K // tk

    def body(*refs):
        a_ref, w_ref, o_ref = refs[:3]
        _mm_accumulate(pl.program_id(2), nk, lambda: jnp.dot(a_ref[...], w_ref[...], preferred_element_type=F32),
                       o_ref, refs[-1])

    return pl.pallas_call(
        body, name=name, grid=(M // tm, N // tn, nk),
        in_specs=[pl.BlockSpec((tm, tk), lambda i, j, k: (i, k)),
                  pl.BlockSpec((None, tk, tn), lambda i, j, k: (j // (n // tn), k, j % (n // tn)))],
        out_specs=pl.BlockSpec((None, tm, tn), lambda i, j, k: (j // (W // tn), i, j % (W // tn))),
        out_shape=S((np_out, M, W), out_dtype),
        scratch_shapes=[pltpu.VMEM((tm, tn), F32)] if nk > 1 else [],
        compiler_params=_cparams(("parallel", "parallel", "arbitrary")),
    )(a, w)


def _mm_nt(a3, w, name, add=None, out_dtype=F32):
    NP, M, W = a3.shape
    J, K, n = w.shape
    N = NP * W
    tm, to = _pick(M, 1024), _pick(K, 1024)
    tc = _contract_tile(N, math.gcd(W, n), 2048)
    nc = N // tc
    has_add = add is not None

    def body(*refs):
        a_ref, w_ref = refs[0], refs[1]
        o_ref = refs[3] if has_add else refs[2]
        _mm_accumulate(
            pl.program_id(2), nc,
            lambda: lax.dot_general(a_ref[...], w_ref[...], (((1,), (1,)), ((), ())), preferred_element_type=F32),
            o_ref, refs[-1], init=(lambda: refs[2][...].astype(F32)) if has_add else None)

    in_specs = [pl.BlockSpec((None, tm, tc), lambda i, j, c: (c // (W // tc), i, c % (W // tc))),
                pl.BlockSpec((None, to, tc), lambda i, j, c: (c // (n // tc), j, c % (n // tc)))]
    ops = [a3, w]
    if has_add:
        in_specs.append(pl.BlockSpec((tm, to), lambda i, j, c: (i, j)))
        ops.append(add)
    return pl.pallas_call(
        body, name=name, grid=(M // tm, K // to, nc),
        in_specs=in_specs,
        out_specs=pl.BlockSpec((tm, to), lambda i, j, c: (i, j)),
        out_shape=S((M, K), out_dtype),
        scratch_shapes=[pltpu.VMEM((tm, to), F32)] if nc > 1 else [],
        compiler_params=_cparams(("parallel", "parallel", "arbitrary")),
    )(*ops)


def _mm_tn(a, d3, j_out, name, out_dtype=F32):
    M, K = a.shape
    NP, _, W = d3.shape
    N = NP * W
    n = N // j_out
    to, tn = _pick(K, 1024), _pick(math.gcd(W, n), 1024)
    tmc = _contract_tile(M, M, 2048)
    nm = M // tmc

    def body(*refs):
        a_ref, d_ref, o_ref = refs[:3]
        _mm_accumulate(
            pl.program_id(2), nm,
            lambda: lax.dot_general(a_ref[...], d_ref[...], (((0,), (0,)), ((), ())), preferred_element_type=F32),
            o_ref, refs[-1])

    return pl.pallas_call(
        body, name=name, grid=(K // to, N // tn, nm),
        in_specs=[pl.BlockSpec((tmc, to), lambda i, j, m: (m, i)),
                  pl.BlockSpec((None, tmc, tn), lambda i, j, m: (j // (W // tn), m, j % (W // tn)))],
        out_specs=pl.BlockSpec((None, to, tn), lambda i, j, m: (j // (n // tn), i, j % (n // tn))),
        out_shape=S((j_out, K, n), out_dtype),
        scratch_shapes=[pltpu.VMEM((to, tn), F32)] if nm > 1 else [],
        compiler_params=_cparams(("parallel", "parallel", "arbitrary")),
    )(a, d3)


def _rms_fwd(x3, piece, col_blk, width, g, name, out_dtype, res=None):
    T = x3.shape[1]
    tr = _pick(T, 256)
    has_res = res is not None

    def body(*refs):
        x_ref, g_ref = refs[0], refs[1]
        o_ref = refs[-1]
        x = x_ref[...].astype(F32)
        y = x * lax.rsqrt(jnp.mean(x * x, axis=-1, keepdims=True) + NORM_EPS) * g_ref[...]
        if has_res:
            y = refs[2][...] + y
        o_ref[...] = y.astype(o_ref.dtype)

    in_specs = [pl.BlockSpec((None, tr, width), lambda i: (piece, i, col_blk)),
                pl.BlockSpec((1, width), lambda i: (0, 0))]
    ops = [x3, g.reshape(1, width)]
    if has_res:
        in_specs.append(pl.BlockSpec((tr, width), lambda i: (i, 0)))
        ops.append(res)
    return pl.pallas_call(
        body, name=name, grid=(T // tr,), in_specs=in_specs,
        out_specs=pl.BlockSpec((tr, width), lambda i: (i, 0)),
        out_shape=S((T, width), out_dtype),
        compiler_params=_cparams(("parallel",)),
    )(*ops)


def _rms_bwd(u3, piece, col_blk, width, g, dy, name, out_dtype, res=None):
    T = u3.shape[1]
    tr = _pick(T, 256)
    has_res = res is not None

    def body(*refs):
        u_ref, g_ref, dy_ref = refs[0], refs[1], refs[2]
        du_ref, dg_ref = refs[-2], refs[-1]
        i = pl.program_id(0)
        u = u_ref[...].astype(F32)
        dy_ = dy_ref[...].astype(F32)
        r = lax.rsqrt(jnp.mean(u * u, axis=-1, keepdims=True) + NORM_EPS)
        nrm = u * r
        gdy = g_ref[...] * dy_
        du = r * (gdy - nrm * jnp.mean(gdy * nrm, axis=-1, keepdims=True))
        if has_res:
            du = du + refs[3][...]
        du_ref[...] = du.astype(du_ref.dtype)

        @pl.when(i == 0)
        def _():
            dg_ref[...] = jnp.zeros_like(dg_ref)

        dg_ref[...] += jnp.sum(dy_ * nrm, axis=0, keepdims=True)

    in_specs = [pl.BlockSpec((None, tr, width), lambda i: (piece, i, col_blk)),
                pl.BlockSpec((1, width), lambda i: (0, 0)),
                pl.BlockSpec((tr, width), lambda i: (i, 0))]
    ops = [u3, g.reshape(1, width), dy]
    if has_res:
        in_specs.append(pl.BlockSpec((tr, width), lambda i: (i, 0)))
        ops.append(res)
    return pl.pallas_call(
        body, name=name, grid=(T // tr,), in_specs=in_specs,
        out_specs=[pl.BlockSpec((tr, width), lambda i: (i, 0)), pl.BlockSpec((1, width), lambda i: (0, 0))],
        out_shape=[S((T, width), out_dtype), S((1, width), F32)],
        compiler_params=_cparams(("arbitrary",)),
    )(*ops)


def _loss_head(xl, target, name):
    T, D = xl.shape
    tr = _pick(T, 256)

    def body(x_ref, t_ref, dx_ref, l_ref):
        i = pl.program_id(0)
        err = x_ref[...] - t_ref[...]
        dx_ref[...] = err * (1.0 / D)

        @pl.when(i == 0)
        def _():
            l_ref[...] = jnp.zeros_like(l_ref)

        l_ref[...] += jnp.sum(err * err)

    dx, l = pl.pallas_call(
        body, name=name, grid=(T // tr,),
        in_specs=[pl.BlockSpec((tr, D), lambda i: (i, 0)), pl.BlockSpec((tr, D), lambda i: (i, 0))],
        out_specs=[pl.BlockSpec((tr, D), lambda i: (i, 0)), pl.BlockSpec((8, LANES), lambda i: (0, 0))],
        out_shape=[S((T, D), F32), S((8, LANES), F32)],
        compiler_params=_cparams(("arbitrary",)),
    )(xl, target)
    return dx, l[0, 0] * (0.5 / D)


def _gate_fwd(o, z3, name):
    T, W = o.shape
    tr = _pick(T, 256)

    def body(o_ref, z_ref, g_ref):
        g_ref[...] = (o_ref[...] * _silu(z_ref[...])).astype(g_ref.dtype)

    return pl.pallas_call(
        body, name=name, grid=(T // tr,),
        in_specs=[pl.BlockSpec((tr, W), lambda i: (i, 0)), pl.BlockSpec((None, tr, W), lambda i: (0, i, 0))],
        out_specs=pl.BlockSpec((tr, W), lambda i: (i, 0)),
        out_shape=S((T, W), BF16), compiler_params=_cparams(("parallel",)),
    )(o, z3)


def _gate_bwd(dg, o, z3, name):
    T, W = o.shape
    tr = _pick(T, 256)

    def body(dg_ref, o_ref, z_ref, do_ref, dz_ref):
        dg_, z = dg_ref[...], z_ref[...]
        do_ref[...] = (dg_ * _silu(z)).astype(do_ref.dtype)
        dz_ref[...] = (dg_ * o_ref[...] * _dsilu(z)).astype(dz_ref.dtype)

    return pl.pallas_call(
        body, name=name, grid=(T // tr,),
        in_specs=[pl.BlockSpec((tr, W), lambda i: (i, 0)), pl.BlockSpec((tr, W), lambda i: (i, 0)),
                  pl.BlockSpec((None, tr, W), lambda i: (0, i, 0))],
        out_specs=[pl.BlockSpec((tr, W), lambda i: (i, 0)), pl.BlockSpec((None, tr, W), lambda i: (0, i, 0))],
        out_shape=[S((T, W), BF16), S((1, T, W), BF16)], compiler_params=_cparams(("parallel",)),
    )(dg, o, z3)


def _rope_tables(pos_col, invf, name):
    T = pos_col.shape[0]
    tr = _pick(T, 512)
    half = ROPE_DIM // 2

    def body(p_ref, f_ref, c_ref, sa_ref, sb_ref):
        ang = p_ref[...].astype(F32) * f_ref[...]
        lane = lax.broadcasted_iota(jnp.int32, ang.shape, 1)
        cs, sn = jnp.cos(ang), jnp.sin(ang)
        c_ref[...] = jnp.where(lane < ROPE_DIM, cs, 0.0)
        sa_ref[...] = jnp.where(lane < half, -sn, 0.0)
        sb_ref[...] = jnp.where((lane >= half) & (lane < ROPE_DIM), sn, 0.0)

    spec = pl.BlockSpec((tr, LANES), lambda i: (i, 0))
    return pl.pallas_call(
        body, name=name, grid=(T // tr,),
        in_specs=[pl.BlockSpec((tr, 1), lambda i: (i, 0)), pl.BlockSpec((1, LANES), lambda i: (0, 0))],
        out_specs=[spec, spec, spec], out_shape=[S((T, LANES), F32)] * 3,
        compiler_params=_cparams(("parallel",)),
    )(pos_col, invf)


def _rope(t, c, sa, sb):
    half = ROPE_DIM // 2
    return t * c + pltpu.roll(t, LANES - half, 1) * sa + pltpu.roll(t, half, 1) * sb


def _rope_t(d, c, sa, sb):
    half = ROPE_DIM // 2
    return d * c + pltpu.roll(d * sa, half, 1) + pltpu.roll(d * sb, LANES - half, 1)


def _qkv_layout(q3, kv3, pa3, kr_blk, tabs, H, name):
    T = q3.shape[1]
    tr = _pick(T, 128)

    def body(q_ref, kv_ref, kr_ref, c_ref, sa_ref, sb_ref, qf_ref, kf_ref, v_ref):
        c, sa, sb = c_ref[...], sa_ref[...], sb_ref[...]
        kr = _rope(kr_ref[...], c, sa, sb).astype(BF16)
        for h in range(H):
            nope = slice(h * HEAD_PAD, h * HEAD_PAD + NOPE_DIM)
            rest = slice(h * HEAD_PAD + NOPE_DIM, (h + 1) * HEAD_PAD)
            qf_ref[:, nope] = q_ref[:, nope].astype(BF16)
            qf_ref[:, rest] = _rope(q_ref[:, rest], c, sa, sb).astype(BF16)
            kf_ref[:, nope] = kv_ref[:, nope].astype(BF16)
            kf_ref[:, rest] = kr
            v_ref[:, h * V_DIM:(h + 1) * V_DIM] = kv_ref[:, rest].astype(BF16)

    tab = pl.BlockSpec((tr, LANES), lambda i: (i, 0))
    wide = pl.BlockSpec((None, tr, H * HEAD_PAD), lambda i: (0, i, 0))
    return pl.pallas_call(
        body, name=name, grid=(T // tr,),
        in_specs=[wide, wide, pl.BlockSpec((None, tr, LANES), lambda i: (0, i, kr_blk)), tab, tab, tab],
        out_specs=[pl.BlockSpec((tr, H * HEAD_PAD), lambda i: (i, 0)), pl.BlockSpec((tr, H * HEAD_PAD), lambda i: (i, 0)),
                   pl.BlockSpec((tr, H * V_DIM), lambda i: (i, 0))],
        out_shape=[S((T, H * HEAD_PAD), BF16), S((T, H * HEAD_PAD), BF16), S((T, H * V_DIM), BF16)],
        compiler_params=_cparams(("parallel",)),
    )(q3, kv3, pa3, *tabs)


def _qkv_layout_bwd(dqf, dkf, dv, tabs, H, name):
    T = dqf.shape[0]
    tr = _pick(T, 128)

    def body(dqf_ref, dkf_ref, dv_ref, c_ref, sa_ref, sb_ref, dq_ref, dkv_ref, dkr_ref):
        c, sa, sb = c_ref[...], sa_ref[...], sb_ref[...]
        dkr = jnp.zeros((tr, LANES), F32)
        for h in range(H):
            nope = slice(h * HEAD_PAD, h * HEAD_PAD + NOPE_DIM)
            rest = slice(h * HEAD_PAD + NOPE_DIM, (h + 1) * HEAD_PAD)
            dq_ref[:, nope] = dqf_ref[:, nope]
            dq_ref[:, rest] = _rope_t(dqf_ref[:, rest].astype(F32), c, sa, sb).astype(BF16)
            dkv_ref[:, nope] = dkf_ref[:, nope]
            dkv_ref[:, rest] = dv_ref[:, h * V_DIM:(h + 1) * V_DIM]
            dkr = dkr + dkf_ref[:, rest].astype(F32)
        dkr_ref[...] = _rope_t(dkr, c, sa, sb).astype(BF16)

    tab = pl.BlockSpec((tr, LANES), lambda i: (i, 0))
    wide_in = pl.BlockSpec((tr, H * HEAD_PAD), lambda i: (i, 0))
    wide_out = pl.BlockSpec((None, tr, H * HEAD_PAD), lambda i: (0, i, 0))
    return pl.pallas_call(
        body, name=name, grid=(T // tr,),
        in_specs=[wide_in, wide_in, pl.BlockSpec((tr, H * V_DIM), lambda i: (i, 0)), tab, tab, tab],
        out_specs=[wide_out, wide_out, pl.BlockSpec((tr, LANES), lambda i: (i, 0))],
        out_shape=[S((1, T, H * HEAD_PAD), BF16), S((1, T, H * HEAD_PAD), BF16), S((T, LANES), BF16)],
        compiler_params=_cparams(("parallel",)),
    )(dqf, dkf, dv, *tabs)


ATTN_BLOCK = 512


def _nt(a, b):
    return lax.dot_general(a, b, (((1,), (1,)), ((), ())), preferred_element_type=F32)


def _tn(a, b):
    return lax.dot_general(a, b, (((0,), (0,)), ((), ())), preferred_element_type=F32)


def _causal_blocks(qi, tb, block):
    if qi > 0:
        def step(ki, carry):
            block(pl.multiple_of(ki * tb, tb), False)
            return carry
        lax.fori_loop(0, qi, step, 0)
    block(qi * tb, True)


def _attn_fwd(qf, kf, v, H, scale, name):
    T = qf.shape[0]
    tb = _pick(T, ATTN_BLOCK)
    nb = T // tb

    def body(q_ref, k_ref, v_ref, o_ref, lse_ref):
        row = lax.broadcasted_iota(jnp.int32, (tb, tb), 0)
        col = lax.broadcasted_iota(jnp.int32, (tb, tb), 1)
        for qi in range(nb):
            rows, before = pl.ds(qi * tb, tb), qi * tb
            q = q_ref[rows, :]
            s_own = jnp.where(col <= row, _nt(q, k_ref[rows, :]), NEG)
            m = jnp.max(s_own, axis=-1, keepdims=True)
            if qi > 0:
                s_pre = _nt(q, k_ref[0:before, :])
                m = jnp.maximum(m, jnp.max(s_pre, axis=-1, keepdims=True))
            p_own = jnp.exp((s_own - m) * scale)
            l = jnp.sum(p_own, axis=-1, keepdims=True)
            acc = jnp.dot(p_own.astype(BF16), v_ref[rows, :], preferred_element_type=F32)
            if qi > 0:
                p_pre = jnp.exp((s_pre - m) * scale)
                l = l + jnp.sum(p_pre, axis=-1, keepdims=True)
                acc = acc + jnp.dot(p_pre.astype(BF16), v_ref[0:before, :], preferred_element_type=F32)
            o_ref[rows, :] = acc / l
            lse_ref[rows, :] = jnp.broadcast_to(m * scale + jnp.log(l), (tb, LANES))

    return pl.pallas_call(
        body, name=name, grid=(H,),
        in_specs=[pl.BlockSpec((T, HEAD_PAD), lambda h: (0, h)), pl.BlockSpec((T, HEAD_PAD), lambda h: (0, h)),
                  pl.BlockSpec((T, V_DIM), lambda h: (0, h))],
        out_specs=[pl.BlockSpec((T, V_DIM), lambda h: (0, h)), pl.BlockSpec((T, LANES), lambda h: (0, h))],
        out_shape=[S((T, H * V_DIM), F32), S((T, H * LANES), F32)],
        compiler_params=_cparams(("parallel",)),
    )(qf, kf, v)


def _attn_bwd(qf, kf, v, do, o, lse, H, scale, name):
    T = qf.shape[0]
    tb = _pick(T, ATTN_BLOCK)
    nb = T // tb

    def body(q_ref, k_ref, v_ref, do_ref, o_ref, lse_ref, dq_ref, dk_ref, dv_ref, dq_acc, dk_acc, dv_acc):
        row = lax.broadcasted_iota(jnp.int32, (tb, tb), 0)
        col = lax.broadcasted_iota(jnp.int32, (tb, tb), 1)
        dk_acc[...] = jnp.zeros_like(dk_acc)
        dv_acc[...] = jnp.zeros_like(dv_acc)
        for qi in range(nb):
            rows = pl.ds(qi * tb, tb)
            dq_acc[...] = jnp.zeros_like(dq_acc)
            delta = jnp.sum(do_ref[rows, :].astype(F32) * o_ref[rows, :], axis=-1, keepdims=True)
            lse_q = lse_ref[rows, 0:1]

            def block(k0, masked, rows=rows, delta=delta, lse_q=lse_q):
                keys = pl.ds(k0, tb)
                q, k, do_ = q_ref[rows, :], k_ref[keys, :], do_ref[rows, :]
                s = _nt(q, k)
                if masked:
                    s = jnp.where(col <= row, s, NEG)
                p = jnp.exp(s * scale - lse_q)
                dp = _nt(do_, v_ref[keys, :])
                ds = (p * (dp - delta) * scale).astype(BF16)
                dv_acc[keys, :] += _tn(p.astype(BF16), do_)
                dk_acc[keys, :] += _tn(ds, q)
                dq_acc[...] += jnp.dot(ds, k, preferred_element_type=F32)

            _causal_blocks(qi, tb, block)
            dq_ref[rows, :] = dq_acc[...].astype(dq_ref.dtype)
        dk_ref[...] = dk_acc[...].astype(dk_ref.dtype)
        dv_ref[...] = dv_acc[...].astype(dv_ref.dtype)

    hp = pl.BlockSpec((T, HEAD_PAD), lambda h: (0, h))
    hv = pl.BlockSpec((T, V_DIM), lambda h: (0, h))
    return pl.pallas_call(
        body, name=name, grid=(H,),
        in_specs=[hp, hp, hv, hv, hv, pl.BlockSpec((T, LANES), lambda h: (0, h))],
        out_specs=[hp, hp, hv],
        out_shape=[S((T, H * HEAD_PAD), BF16), S((T, H * HEAD_PAD), BF16), S((T, H * V_DIM), BF16)],
        scratch_shapes=[pltpu.VMEM((tb, HEAD_PAD), F32), pltpu.VMEM((T, HEAD_PAD), F32), pltpu.VMEM((T, V_DIM), F32)],
        compiler_params=_cparams(("parallel",)),
    )(qf, kf, v, do, o, lse)


CONV_ROWS = 256
CONV_COLS = 128


def _conv_chunks(T):
    rc = min(CONV_ROWS, T)
    return [(r, rc) for r in range(0, T, rc)]


def _causal_conv(pad_ref, lead, w_ref, width, r0, rc):
    acc = None
    for k in range(width):
        term = w_ref[k:k + 1, :] * pad_ref[pl.ds(lead + r0 - (width - 1) + k, rc), :]
        acc = term if acc is None else acc + term
    return acc


def _anticausal_conv(pad_ref, w_ref, width, r0, rc):
    acc = None
    for k in range(width):
        term = w_ref[k:k + 1, :] * pad_ref[pl.ds(r0 + (width - 1) - k, rc), :]
        acc = term if acc is None else acc + term
    return acc


def _conv_wgrad(dpad_ref, xpad_ref, lead, width, T, dw_ref):
    for k in range(width):
        tot = None
        for r0, rc in _conv_chunks(T):
            part = jnp.sum(dpad_ref[pl.ds(r0, rc), :] * xpad_ref[pl.ds(lead + r0 - (width - 1) + k, rc), :],
                           axis=0, keepdims=True)
            tot = part if tot is None else tot + part
        dw_ref[k:k + 1, :] = tot


def _sc_fwd(p3, wconv, name):
    _, T, W = p3.shape
    width = wconv.shape[0]
    cw = min(CONV_COLS, W)
    lead = 8

    def body(p_ref, w_ref, g_ref, pad):
        pad[0:lead, :] = jnp.zeros((lead, cw), F32)
        for r0, rc in _conv_chunks(T):
            pad[pl.ds(lead + r0, rc), :] = p_ref[1, pl.ds(r0, rc), :] * p_ref[2, pl.ds(r0, rc), :]
        for r0, rc in _conv_chunks(T):
            rows = pl.ds(r0, rc)
            y = p_ref[0, rows, :] * _causal_conv(pad, lead, w_ref, width, r0, rc)
            g_ref[rows, :] = (y * _silu(p_ref[3, rows, :])).astype(g_ref.dtype)

    return pl.pallas_call(
        body, name=name, grid=(W // cw,),
        in_specs=[pl.BlockSpec((4, T, cw), lambda j: (0, 0, j)), pl.BlockSpec((width, cw), lambda j: (0, j))],
        out_specs=pl.BlockSpec((T, cw), lambda j: (0, j)),
        out_shape=S((T, W), BF16),
        scratch_shapes=[pltpu.VMEM((T + lead, cw), F32)],
        compiler_params=_cparams(("parallel",)),
    )(p3, wconv)


def _sc_bwd(p3, wconv, dg, name):
    _, T, W = p3.shape
    width = wconv.shape[0]
    cw = min(CONV_COLS, W)
    lead = 8

    def body(p_ref, w_ref, dg_ref, dp_ref, dw_ref, cupad, dvpad):
        cupad[0:lead, :] = jnp.zeros((lead, cw), F32)
        dvpad[pl.ds(T, lead), :] = jnp.zeros((lead, cw), F32)
        for r0, rc in _conv_chunks(T):
            cupad[pl.ds(lead + r0, rc), :] = p_ref[1, pl.ds(r0, rc), :] * p_ref[2, pl.ds(r0, rc), :]
        for r0, rc in _conv_chunks(T):
            rows = pl.ds(r0, rc)
            b, z, dg_ = p_ref[0, rows, :], p_ref[3, rows, :], dg_ref[rows, :]
            v = _causal_conv(cupad, lead, w_ref, width, r0, rc)
            dy = dg_ * _silu(z)
            dp_ref[3, rows, :] = (dg_ * b * v * _dsilu(z)).astype(dp_ref.dtype)
            dp_ref[0, rows, :] = (dy * v).astype(dp_ref.dtype)
            dvpad[rows, :] = dy * b
        for r0, rc in _conv_chunks(T):
            rows = pl.ds(r0, rc)
            dcu = _anticausal_conv(dvpad, w_ref, width, r0, rc)
            dp_ref[1, rows, :] = (dcu * p_ref[2, rows, :]).astype(dp_ref.dtype)
            dp_ref[2, rows, :] = (dcu * p_ref[1, rows, :]).astype(dp_ref.dtype)
        _conv_wgrad(dvpad, cupad, lead, width, T, dw_ref)

    return pl.pallas_call(
        body, name=name, grid=(W // cw,),
        in_specs=[pl.BlockSpec((4, T, cw), lambda j: (0, 0, j)), pl.BlockSpec((width, cw), lambda j: (0, j)),
                  pl.BlockSpec((T, cw), lambda j: (0, j))],
        out_specs=[pl.BlockSpec((4, T, cw), lambda j: (0, 0, j)), pl.BlockSpec((width, cw), lambda j: (0, j))],
        out_shape=[S((4, T, W), BF16), S((width, W), F32)],
        scratch_shapes=[pltpu.VMEM((T + lead, cw), F32), pltpu.VMEM((T + lead, cw), F32)],
        compiler_params=_cparams(("parallel",)),
    )(p3, wconv, dg)


def _gm_common(p_ref, lng_ref, lnb_ref):
    ug = _gelu(p_ref[0])
    vg = _gelu(p_ref[1])
    mu = jnp.mean(vg, axis=-1, keepdims=True)
    xc = vg - mu
    rstd = lax.rsqrt(jnp.mean(xc * xc, axis=-1, keepdims=True) + LN_EPS)
    xhat = xc * rstd
    vn = xhat * lng_ref[...] + lnb_ref[...]
    return ug, xhat, rstd, vn


def _gm_mix_weights(ws_ref, g):
    row = lax.broadcasted_iota(jnp.int32, (GM_CHUNK, GM_CHUNK), 0)
    col = lax.broadcasted_iota(jnp.int32, (GM_CHUNK, GM_CHUNK), 1)
    return jnp.where(col <= row, ws_ref[g], 0.0).astype(BF16)


def _gm_fwd(p3, lng, lnb, ws, bs_t, name):
    _, T, W = p3.shape
    gw = W // GM_GROUPS

    def body(p_ref, lng_ref, lnb_ref, ws_ref, bs_ref, g_ref):
        ug, _, _, vn = _gm_common(p_ref, lng_ref, lnb_ref)
        sz = _silu(p_ref[2])
        vnb = vn.astype(BF16)
        for g in range(GM_GROUPS):
            cols = slice(g * gw, (g + 1) * gw)
            s = jnp.dot(_gm_mix_weights(ws_ref, g), vnb[:, cols], preferred_element_type=F32) + bs_ref[:, g:g + 1]
            g_ref[:, cols] = (ug[:, cols] * s * sz[:, cols]).astype(g_ref.dtype)

    return pl.pallas_call(
        body, name=name, grid=(T // GM_CHUNK,),
        in_specs=[pl.BlockSpec((3, GM_CHUNK, W), lambda i: (0, i, 0)), pl.BlockSpec((1, W), lambda i: (0, 0)),
                  pl.BlockSpec((1, W), lambda i: (0, 0)),
                  pl.BlockSpec((GM_GROUPS, GM_CHUNK, GM_CHUNK), lambda i: (0, 0, 0)),
                  pl.BlockSpec((GM_CHUNK, GM_GROUPS), lambda i: (0, 0))],
        out_specs=pl.BlockSpec((GM_CHUNK, W), lambda i: (i, 0)),
        out_shape=S((T, W), BF16), compiler_params=_cparams(("parallel",)),
    )(p3, lng.reshape(1, W), lnb.reshape(1, W), ws, bs_t)


def _gm_bwd(p3, lng, lnb, ws, bs_t, dg, name):
    _, T, W = p3.shape
    gw = W // GM_GROUPS

    def body(p_ref, lng_ref, lnb_ref, ws_ref, bs_ref, dg_ref, dp_ref, dlng_ref, dlnb_ref, dws_ref, dbs_ref, dvn_s):
        i = pl.program_id(0)

        @pl.when(i == 0)
        def _():
            dlng_ref[...] = jnp.zeros_like(dlng_ref)
            dlnb_ref[...] = jnp.zeros_like(dlnb_ref)
            dws_ref[...] = jnp.zeros_like(dws_ref)
            dbs_ref[...] = jnp.zeros_like(dbs_ref)

        ug, xhat, rstd, vn = _gm_common(p_ref, lng_ref, lnb_ref)
        z = p_ref[2]
        dg_ = dg_ref[...]
        dy = dg_ * _silu(z)
        vnb = vn.astype(BF16)
        row = lax.broadcasted_iota(jnp.int32, (GM_CHUNK, GM_CHUNK), 0)
        col = lax.broadcasted_iota(jnp.int32, (GM_CHUNK, GM_CHUNK), 1)
        dbs = jnp.zeros((GM_CHUNK, LANES), F32)
        for g in range(GM_GROUPS):
            cols = slice(g * gw, (g + 1) * gw)
            wm = _gm_mix_weights(ws_ref, g)
            s = jnp.dot(wm, vnb[:, cols], preferred_element_type=F32) + bs_ref[:, g:g + 1]
            dp_ref[2, :, cols] = (dg_[:, cols] * ug[:, cols] * s * _dsilu(z[:, cols])).astype(dp_ref.dtype)
            dp_ref[0, :, cols] = (dy[:, cols] * s * _dgelu(p_ref[0, :, cols])).astype(dp_ref.dtype)
            ds = dy[:, cols] * ug[:, cols]
            dsb = ds.astype(BF16)
            dwm = lax.dot_general(dsb, vnb[:, cols], (((1,), (1,)), ((), ())), preferred_element_type=F32)
            dws_ref[g] += jnp.where(col <= row, dwm, 0.0)
            dbs = dbs + jnp.where(col == g, jnp.sum(ds, axis=-1, keepdims=True), 0.0)
            dvn_s[:, cols] = lax.dot_general(wm, dsb, (((0,), (0,)), ((), ())), preferred_element_type=F32)
        dbs_ref[...] += dbs
        dvn = dvn_s[...]
        dlng_ref[...] += jnp.sum(dvn * xhat, axis=0, keepdims=True)
        dlnb_ref[...] += jnp.sum(dvn, axis=0, keepdims=True)
        dxh = dvn * lng_ref[...]
        dvg = rstd * (dxh - jnp.mean(dxh, axis=-1, keepdims=True) - xhat * jnp.mean(dxh * xhat, axis=-1, keepdims=True))
        dp_ref[1] = (dvg * _dgelu(p_ref[1])).astype(dp_ref.dtype)

    row1 = pl.BlockSpec((1, W), lambda i: (0, 0))
    return pl.pallas_call(
        body, name=name, grid=(T // GM_CHUNK,),
        in_specs=[pl.BlockSpec((3, GM_CHUNK, W), lambda i: (0, i, 0)), row1, row1,
                  pl.BlockSpec((GM_GROUPS, GM_CHUNK, GM_CHUNK), lambda i: (0, 0, 0)),
                  pl.BlockSpec((GM_CHUNK, GM_GROUPS), lambda i: (0, 0)),
                  pl.BlockSpec((GM_CHUNK, W), lambda i: (i, 0))],
        out_specs=[pl.BlockSpec((3, GM_CHUNK, W), lambda i: (0, i, 0)), row1, row1,
                   pl.BlockSpec((GM_GROUPS, GM_CHUNK, GM_CHUNK), lambda i: (0, 0, 0)),
                   pl.BlockSpec((GM_CHUNK, LANES), lambda i: (0, 0))],
        out_shape=[S((3, T, W), BF16), S((1, W), F32), S((1, W), F32),
                   S((GM_GROUPS, GM_CHUNK, GM_CHUNK), F32), S((GM_CHUNK, LANES), F32)],
        scratch_shapes=[pltpu.VMEM((GM_CHUNK, W), F32)],
        compiler_params=_cparams(("arbitrary",)),
    )(p3, lng.reshape(1, W), lnb.reshape(1, W), ws, bs_t, dg)


def _cf_conv_fwd(p3, wdw, bdw, name):
    _, T, W = p3.shape
    width = wdw.shape[0]
    cw = min(CONV_COLS, W)
    lead = 32

    def body(p_ref, w_ref, b_ref, y_ref, pad):
        pad[0:lead, :] = jnp.zeros((lead, cw), F32)
        for r0, rc in _conv_chunks(T):
            rows = pl.ds(r0, rc)
            pad[pl.ds(lead + r0, rc), :] = p_ref[0, rows, :] * jax.nn.sigmoid(p_ref[1, rows, :])
        for r0, rc in _conv_chunks(T):
            y_ref[pl.ds(r0, rc), :] = _causal_conv(pad, lead, w_ref, width, r0, rc) + b_ref[...]

    return pl.pallas_call(
        body, name=name, grid=(W // cw,),
        in_specs=[pl.BlockSpec((2, T, cw), lambda j: (0, 0, j)), pl.BlockSpec((width, cw), lambda j: (0, j)),
                  pl.BlockSpec((1, cw), lambda j: (0, j))],
        out_specs=pl.BlockSpec((T, cw), lambda j: (0, j)),
        out_shape=S((T, W), F32),
        scratch_shapes=[pltpu.VMEM((T + lead, cw), F32)],
        compiler_params=_cparams(("parallel",)),
    )(p3, wdw, bdw.reshape(1, W))


def _cf_ln(y1_ref, lng_ref, lnb_ref):
    y1 = y1_ref[...]
    mu = jnp.mean(y1, axis=-1, keepdims=True)
    xc = y1 - mu
    rstd = lax.rsqrt(jnp.mean(xc * xc, axis=-1, keepdims=True) + LN_EPS)
    xhat = xc * rstd
    return xhat, rstd, xhat * lng_ref[...] + lnb_ref[...]


def _cf_gate_fwd(y1, p3, lng, lnb, name):
    T, W = y1.shape
    tr = _pick(T, 256)

    def body(y1_ref, z_ref, lng_ref, lnb_ref, g_ref):
        _, _, y2 = _cf_ln(y1_ref, lng_ref, lnb_ref)
        g_ref[...] = (_silu(y2) * _silu(z_ref[...])).astype(g_ref.dtype)

    row1 = pl.BlockSpec((1, W), lambda i: (0, 0))
    return pl.pallas_call(
        body, name=name, grid=(T // tr,),
        in_specs=[pl.BlockSpec((tr, W), lambda i: (i, 0)), pl.BlockSpec((None, tr, W), lambda i: (2, i, 0)), row1, row1],
        out_specs=pl.BlockSpec((tr, W), lambda i: (i, 0)),
        out_shape=S((T, W), BF16), compiler_params=_cparams(("parallel",)),
    )(y1, p3, lng.reshape(1, W), lnb.reshape(1, W))


def _cf_gate_bwd(y1, p3, lng, lnb, dg, name):
    T, W = y1.shape
    tr = _pick(T, 128)

    def body(y1_ref, z_ref, lng_ref, lnb_ref, dg_ref, dz_ref, dy1_ref, dlng_ref, dlnb_ref):
        i = pl.program_id(0)

        @pl.when(i == 0)
        def _():
            dlng_ref[...] = jnp.zeros_like(dlng_ref)
            dlnb_ref[...] = jnp.zeros_like(dlnb_ref)

        xhat, rstd, y2 = _cf_ln(y1_ref, lng_ref, lnb_ref)
        z, dg_ = z_ref[...], dg_ref[...]
        dz_ref[...] = (dg_ * _silu(y2) * _dsilu(z)).astype(dz_ref.dtype)
        dy2 = dg_ * _silu(z) * _dsilu(y2)
        dlng_ref[...] += jnp.sum(dy2 * xhat, axis=0, keepdims=True)
        dlnb_ref[...] += jnp.sum(dy2, axis=0, keepdims=True)
        dxh = dy2 * lng_ref[...]
        dy1_ref[...] = rstd * (dxh - jnp.mean(dxh, axis=-1, keepdims=True)
                               - xhat * jnp.mean(dxh * xhat, axis=-1, keepdims=True))

    row1 = pl.BlockSpec((1, W), lambda i: (0, 0))
    blk = pl.BlockSpec((tr, W), lambda i: (i, 0))
    return pl.pallas_call(
        body, name=name, grid=(T // tr,),
        in_specs=[blk, pl.BlockSpec((None, tr, W), lambda i: (2, i, 0)), row1, row1, blk],
        out_specs=[blk, blk, row1, row1],
        out_shape=[S((T, W), BF16), S((T, W), F32), S((1, W), F32), S((1, W), F32)],
        compiler_params=_cparams(("arbitrary",)),
    )(y1, p3, lng.reshape(1, W), lnb.reshape(1, W), dg)


def _cf_conv_bwd(p3, wdw, dy1, dz, name):
    _, T, W = p3.shape
    width = wdw.shape[0]
    cw = min(CONV_COLS, W)
    lead = 32

    def body(p_ref, w_ref, dy1_ref, dz_ref, dp_ref, dw_ref, db_ref, y0pad, dpad):
        y0pad[0:lead, :] = jnp.zeros((lead, cw), F32)
        dpad[pl.ds(T, lead), :] = jnp.zeros((lead, cw), F32)
        bsum = None
        for r0, rc in _conv_chunks(T):
            rows = pl.ds(r0, rc)
            y0pad[pl.ds(lead + r0, rc), :] = p_ref[0, rows, :] * jax.nn.sigmoid(p_ref[1, rows, :])
            d = dy1_ref[rows, :]
            dpad[rows, :] = d
            part = jnp.sum(d, axis=0, keepdims=True)
            bsum = part if bsum is None else bsum + part
        db_ref[...] = bsum
        for r0, rc in _conv_chunks(T):
            rows = pl.ds(r0, rc)
            dy0 = _anticausal_conv(dpad, w_ref, width, r0, rc)
            a = p_ref[0, rows, :]
            sg = jax.nn.sigmoid(p_ref[1, rows, :])
            dp_ref[0, rows, :] = (dy0 * sg).astype(dp_ref.dtype)
            dp_ref[1, rows, :] = (dy0 * a * sg * (1.0 - sg)).astype(dp_ref.dtype)
            dp_ref[2, rows, :] = dz_ref[rows, :]
        _conv_wgrad(dpad, y0pad, lead, width, T, dw_ref)

    return pl.pallas_call(
        body, name=name, grid=(W // cw,),
        in_specs=[pl.BlockSpec((2, T, cw), lambda j: (0, 0, j)), pl.BlockSpec((width, cw), lambda j: (0, j)),
                  pl.BlockSpec((T, cw), lambda j: (0, j)), pl.BlockSpec((T, cw), lambda j: (0, j))],
        out_specs=[pl.BlockSpec((3, T, cw), lambda j: (0, 0, j)), pl.BlockSpec((width, cw), lambda j: (0, j)),
                   pl.BlockSpec((1, cw), lambda j: (0, j))],
        out_shape=[S((3, T, W), BF16), S((width, W), F32), S((1, W), F32)],
        scratch_shapes=[pltpu.VMEM((T + lead, cw), F32), pltpu.VMEM((T + lead, cw), F32)],
        compiler_params=_cparams(("parallel",)),
    )(p3, wdw, dy1, dz)


def _rows_call(body, ins, out_dtypes, name, row_pref=256):
    R, C = ins[0].shape
    tr = _pick(R, row_pref) if R % 8 == 0 else R
    while tr > 8 and tr * C * 4 * (len(ins) + len(out_dtypes)) * 2 > VMEM_LIMIT // 2 and tr % 16 == 0:
        tr //= 2
    blk = pl.BlockSpec((tr, C), lambda i: (i, 0))
    return pl.pallas_call(
        body, name=name, grid=(R // tr,), in_specs=[blk] * len(ins), out_specs=[blk] * len(out_dtypes),
        out_shape=[S((R, C), dt) for dt in out_dtypes], compiler_params=_cparams(("parallel",)),
    )(*ins)


def _pair_sum(g, r, core, name):
    J, K, n = g.shape
    kh = K // 2
    tr = _pick(kh, 256)
    nb = kh // tr

    def body(core_ref, g_ref, r_ref, o_ref):
        o_ref[...] = (g_ref[...] + r_ref[...]).astype(BF16)

    return pl.pallas_call(
        body, name=name,
        grid_spec=pltpu.PrefetchScalarGridSpec(
            num_scalar_prefetch=1, grid=(J, nb),
            in_specs=[pl.BlockSpec((None, tr, n), lambda j, i, core_ref: (j, core_ref[0] * nb + i, 0)),
                      pl.BlockSpec((None, tr, n), lambda j, i, core_ref: (j, i, 0))],
            out_specs=pl.BlockSpec((None, tr, n), lambda j, i, core_ref: (j, i, 0))),
        out_shape=S((J, kh, n), BF16),
        compiler_params=_cparams(("parallel", "parallel")),
    )(core, g, r)


def _chip_sum(rc, core, name):
    J, R, C = rc.shape
    tr = _pick(R, 256)
    nb = R // tr

    def body(core_ref, r_ref, o_ref):
        acc = r_ref[0].astype(F32)
        for j in range(1, J):
            acc = acc + r_ref[j].astype(F32)
        o_ref[...] = acc

    return pl.pallas_call(
        body, name=name,
        grid_spec=pltpu.PrefetchScalarGridSpec(
            num_scalar_prefetch=1, grid=(nb,),
            in_specs=[pl.BlockSpec((J, tr, C), lambda i, core_ref: (0, i, 0))],
            out_specs=pl.BlockSpec((tr, C), lambda i, core_ref: (core_ref[0] * nb + i, 0))),
        out_shape=S((2 * R, C), F32), compiler_params=_cparams(("parallel",)),
    )(core, rc)


def _slot_sum(slots, name):
    J, R, C = slots.shape
    tr = _pick(R, 512)

    def body(r_ref, o_ref):
        acc = r_ref[0]
        for j in range(1, J):
            acc = acc + r_ref[j]
        o_ref[...] = acc

    return pl.pallas_call(
        body, name=name, grid=(R // tr,),
        in_specs=[pl.BlockSpec((J, tr, C), lambda i: (0, i, 0))], out_specs=pl.BlockSpec((tr, C), lambda i: (i, 0)),
        out_shape=S((R, C), F32), compiler_params=_cparams(("parallel",)),
    )(slots)


def _adamw(w, g, m, v, name):
    def body(w_ref, g_ref, m_ref, v_ref, d_ref, nm_ref, nv_ref):
        g_ = g_ref[...]
        nm = ADAM_B1 * m_ref[...] + (1.0 - ADAM_B1) * g_
        nv = ADAM_B2 * v_ref[...] + (1.0 - ADAM_B2) * (g_ * g_)
        m_hat = nm / (1.0 - ADAM_B1 ** ADAM_STEP)
        v_hat = nv / (1.0 - ADAM_B2 ** ADAM_STEP)
        d_ref[...] = -ADAM_LR * (m_hat / (jnp.sqrt(v_hat) + ADAM_EPS) + ADAM_WD * w_ref[...])
        nm_ref[...] = nm
        nv_ref[...] = nv
    return _rows_call(body, [w, g, m, v], [F32, F32, F32], name)


ANY = pl.BlockSpec(memory_space=pl.ANY)


def _place():
    x, y, c = lax.axis_index("x"), lax.axis_index("y"), lax.axis_index("c")
    return x, y, c


def _other_chips(x, y):
    return [(1 - x, y), (x, 1 - y), (1 - x, 1 - y)]


HBM = pl.BlockSpec(memory_space=pltpu.HBM)
SEM = pl.BlockSpec(memory_space=pltpu.SEMAPHORE)
DATAFLOW = pltpu.SideEffectType.DATAFLOW_SIDE_EFFECTING


def _in_hbm(a):
    return pltpu.with_memory_space_constraint(a, pltpu.HBM)


def _half_rows(k_rows, which):
    return pl.ds(which * (k_rows // 2), k_rows // 2)


COPIES_PER_ARRAY = {"gather": 3, "scatter": 3, "halves": 1}


def _split_copies(srcs, lands, send_sems, recv_sems, mode):
    x, y, c = _place()
    me_chip = 2 * x + y
    out = []
    for w, (src, land) in enumerate(zip(srcs, lands)):
        if mode == "halves":
            rows = _half_rows(src.shape[1], 1 - c)
            cp = pltpu.make_async_remote_copy(src_ref=src.at[:, rows, :], dst_ref=land, send_sem=send_sems.at[w],
                                              recv_sem=recv_sems.at[w], device_id=(x, y, 1 - c), device_id_type=MESH_ID)
            out.append((cp, cp))
            continue
        for j, (cx, cy) in enumerate(_other_chips(x, y)):
            them = 2 * cx + cy
            sems = dict(send_sem=send_sems.at[3 * w + j], recv_sem=recv_sems.at[3 * w + j],
                        device_id=(cx, cy, c), device_id_type=MESH_ID)
            if mode == "gather":
                rows = _half_rows(src.shape[0], c)
                go = pltpu.make_async_remote_copy(src_ref=src.at[rows], dst_ref=land.at[me_chip, rows], **sems)
                arrive = pltpu.make_async_remote_copy(src_ref=src.at[rows], dst_ref=land.at[them, rows], **sems)
            else:
                go = pltpu.make_async_remote_copy(src_ref=src.at[them], dst_ref=land.at[me_chip], **sems)
                arrive = pltpu.make_async_remote_copy(src_ref=src.at[them], dst_ref=land.at[them], **sems)
            out.append((go, arrive))
    return out


def _split_copies_start(srcs, lands, mode, name):
    nw = len(srcs)
    n_sems = COPIES_PER_ARRAY[mode] * nw

    def body(*refs):
        ins, lnd = refs[:nw], refs[nw:2 * nw]
        send_sems, recv_sems, token = refs[2 * nw], refs[2 * nw + 1], refs[-1]
        for go, _ in _split_copies(ins, lnd, send_sems, recv_sems, mode):
            go.start()
        token[...] = jnp.zeros_like(token)

    thru = [pltpu.HBM(a.shape, a.dtype) for a in list(srcs) + list(lands)]
    res = pl.pallas_call(
        body, name=name, in_specs=[HBM] * (2 * nw),
        out_specs=[SEM, SEM] + [HBM] * (2 * nw) + [pl.BlockSpec(memory_space=pltpu.VMEM)],
        out_shape=[pltpu.SemaphoreType.DMA((n_sems,)), pltpu.SemaphoreType.DMA((n_sems,))] + thru + [S((8, LANES), F32)],
        input_output_aliases={i: 2 + i for i in range(2 * nw)},
        compiler_params=pltpu.CompilerParams(has_side_effects=DATAFLOW),
    )(*[_in_hbm(a) for a in list(srcs) + list(lands)])
    return res[0], res[1], list(res[2:2 + nw]), list(res[2 + nw:2 + 2 * nw]), res[-1]


def _split_copies_wait(send_sems, recv_sems, srcs, lands, after, mode, name):
    nw = len(srcs)

    def body(*refs):
        ins, lnd = refs[:nw], refs[nw:2 * nw]
        send, recv = refs[2 * nw], refs[2 * nw + 1]
        for _, arrive in _split_copies(ins, lnd, send, recv, mode):
            arrive.wait_send()
            arrive.wait_recv()

    res = pl.pallas_call(
        body, name=name, in_specs=[HBM] * (2 * nw) + [SEM, SEM, ANY], out_specs=[HBM] * (2 * nw),
        out_shape=[pltpu.HBM(a.shape, a.dtype) for a in list(srcs) + list(lands)],
        input_output_aliases={i: i for i in range(2 * nw)},
        compiler_params=pltpu.CompilerParams(has_side_effects=DATAFLOW),
    )(*srcs, *lands, send_sems, recv_sems, after)
    return list(res[:nw]), list(res[nw:])


def _gather_forward(lands, name):
    nw = len(lands)

    def body(*refs):
        ins, outs = refs[:nw], refs[nw:2 * nw]
        send_sems, recv_sems = refs[2 * nw:]
        x, y, c = _place()
        sibling = (x, y, 1 - c)
        cps = []
        for w in range(nw):
            kr = ins[w].shape[1]
            for j, (cx, cy) in enumerate(_other_chips(x, y)):
                them = 2 * cx + cy
                sems = dict(send_sem=send_sems.at[3 * w + j], recv_sem=recv_sems.at[3 * w + j],
                            device_id=sibling, device_id_type=MESH_ID)
                mine, theirs = _half_rows(kr, c), _half_rows(kr, 1 - c)
                go = pltpu.make_async_remote_copy(src_ref=ins[w].at[them, mine], dst_ref=outs[w].at[them, mine], **sems)
                go.start()
                cps.append((go, pltpu.make_async_remote_copy(
                    src_ref=ins[w].at[them, theirs], dst_ref=outs[w].at[them, theirs], **sems)))
        for go, arrive in cps:
            arrive.wait_recv()
            go.wait_send()

    return pl.pallas_call(
        body, name=name, in_specs=[ANY] * nw, out_specs=[ANY] * nw,
        out_shape=[S(a.shape, a.dtype) for a in lands],
        scratch_shapes=[pltpu.SemaphoreType.DMA((3 * nw,)), pltpu.SemaphoreType.DMA((3 * nw,))],
        input_output_aliases={i: i for i in range(nw)},
        compiler_params=pltpu.CompilerParams(has_side_effects=True),
    )(*lands)


def _core_join_halves(halves, name):
    nw = len(halves)

    def body(*refs):
        ins, outs = refs[:nw], refs[nw:2 * nw]
        send_sems, recv_sems = refs[2 * nw:]
        x, y, c = _place()
        sibling = (x, y, 1 - c)
        rem = []
        for w in range(nw):
            r = ins[w].shape[0] // 2
            mine = pl.ds(c * r, r)
            cp = pltpu.make_async_remote_copy(
                src_ref=ins[w].at[mine], dst_ref=outs[w].at[mine], send_sem=send_sems.at[w], recv_sem=recv_sems.at[w],
                device_id=sibling, device_id_type=MESH_ID)
            cp.start()
            rem.append(cp)
        for w in range(nw):
            r = ins[w].shape[0] // 2
            theirs = outs[w].at[pl.ds((1 - c) * r, r)]
            pltpu.make_async_remote_copy(
                src_ref=theirs, dst_ref=theirs, send_sem=send_sems.at[w], recv_sem=recv_sems.at[w],
                device_id=sibling, device_id_type=MESH_ID).wait_recv()
        for cp in rem:
            cp.wait_send()

    return pl.pallas_call(
        body, name=name, in_specs=[ANY] * nw, out_specs=[ANY] * nw,
        out_shape=[S(h.shape, h.dtype) for h in halves],
        scratch_shapes=[pltpu.SemaphoreType.DMA((nw,)), pltpu.SemaphoreType.DMA((nw,))],
        input_output_aliases={i: i for i in range(nw)},
        compiler_params=pltpu.CompilerParams(has_side_effects=True),
    )(*halves)


def _broadcast_all(buf, name):
    def body(in_ref, out_ref, send_sems, recv_sems, loc_sem):
        x, y, c = _place()
        me = 4 * x + 2 * y + c
        loc = pltpu.make_async_copy(in_ref, out_ref.at[me], loc_sem)
        loc.start()
        cps = []
        for k in range(1, N_DEV):
            fx, fy, fc = (k >> 2) & 1, (k >> 1) & 1, k & 1
            px, py, pc = x ^ fx, y ^ fy, c ^ fc
            cp = pltpu.make_async_remote_copy(
                src_ref=in_ref, dst_ref=out_ref.at[me], send_sem=send_sems.at[k - 1], recv_sem=recv_sems.at[k - 1],
                device_id=(px, py, pc), device_id_type=MESH_ID)
            cp.start()
            cps.append(cp)
        for k in range(1, N_DEV):
            fx, fy, fc = (k >> 2) & 1, (k >> 1) & 1, k & 1
            px, py, pc = x ^ fx, y ^ fy, c ^ fc
            slot = out_ref.at[4 * px + 2 * py + pc]
            pltpu.make_async_remote_copy(
                src_ref=slot, dst_ref=slot, send_sem=send_sems.at[k - 1], recv_sem=recv_sems.at[k - 1],
                device_id=(px, py, pc), device_id_type=MESH_ID).wait_recv()
        for cp in cps:
            cp.wait_send()
        loc.wait()

    return pl.pallas_call(
        body, name=name, in_specs=[ANY], out_specs=ANY,
        out_shape=S((N_DEV,) + buf.shape, buf.dtype),
        scratch_shapes=[pltpu.SemaphoreType.DMA((N_DEV - 1,)), pltpu.SemaphoreType.DMA((N_DEV - 1,)),
                        pltpu.SemaphoreType.DMA],
        compiler_params=pltpu.CompilerParams(has_side_effects=True),
    )(buf)


PACK_ALIGN = 8 * LANES


def _pack(arrs):
    flat = []
    for a in arrs:
        f = a.reshape(-1).astype(F32)
        pad = (-f.shape[0]) % PACK_ALIGN
        flat.append(jnp.pad(f, (0, pad)) if pad else f)
    return jnp.concatenate(flat).reshape(-1, LANES)


def _unpack(buf, shapes):
    out, off = [], 0
    flat = buf.reshape(-1)
    for shp in shapes:
        n = math.prod(shp)
        out.append(flat[off:off + n].reshape(shp))
        off += n + ((-n) % PACK_ALIGN)
    return out


def kernel(x, positions, norm_pre, norm_post, w_in_mla, mla_q_norm, w_uq, mla_kv_norm, w_ukv, w_out_mla, w_in_sc, sc_conv, w_out_sc, w_in_gm, gm_ln_g, gm_ln_b, gm_w_s, gm_b_s, w_out_gm, w_in_cf, cf_dw, cf_dw_b, cf_ln_g, cf_ln_b, w_out_cf, loss_target, m_norm_pre, m_norm_post, m_w_in_mla, m_mla_q_norm, m_w_uq, m_mla_kv_norm, m_w_ukv, m_w_out_mla, m_w_in_sc, m_sc_conv, m_w_out_sc, m_w_in_gm, m_gm_ln_g, m_gm_ln_b, m_gm_w_s, m_gm_b_s, m_w_out_gm, m_w_in_cf, m_cf_dw, m_cf_dw_b, m_cf_ln_g, m_cf_ln_b, m_w_out_cf, v_norm_pre, v_norm_post, v_w_in_mla, v_mla_q_norm, v_w_uq, v_mla_kv_norm, v_w_ukv, v_w_out_mla, v_w_in_sc, v_sc_conv, v_w_out_sc, v_w_in_gm, v_gm_ln_g, v_gm_ln_b, v_gm_w_s, v_gm_b_s, v_w_out_gm, v_w_in_cf, v_cf_dw, v_cf_dw_b, v_cf_ln_g, v_cf_ln_b, v_w_out_cf):
    loc = dict(locals())
    wts = {n: loc[n] for n in WEIGHTS}
    mom_m = {n: loc["m_" + n] for n in WEIGHTS}
    mom_v = {n: loc["v_" + n] for n in WEIGHTS}

    T, D = x.shape[1], x.shape[2]
    xin = x.reshape(T, D)
    target = loss_target.reshape(T, D)
    q_rank, kv_rank = mla_q_norm.shape[1], mla_kv_norm.shape[1]
    H = (w_uq.shape[2] * N_CHIPS) // (NOPE_DIM + ROPE_DIM)
    hv = H * V_DIM
    c_kr = q_rank + kv_rank
    wa_cols = c_kr + ROPE_DIM
    wa_pad = wa_cols + (LANES - ROPE_DIM)
    chip = 2 * lax.axis_index("x") + lax.axis_index("y")

    c = lax.axis_index("c")
    core = c.reshape(1).astype(jnp.int32)
    gather_started = []
    for gi, names in enumerate(GROUPS):
        own = [wts[n][0].astype(BF16) for n in names]
        lands = [lax.dynamic_update_slice(lax.empty((N_CHIPS,) + s.shape, BF16), s[None], (chip, 0, 0)) for s in own]
        gather_started.append(_split_copies_start(own, lands, "gather", f"gather_start_{gi}"))
    started_token = sum(st[4][0, 0] for st in gather_started)
    gw = {}

    def gathered_weights(gi, after):
        send_sems, recv_sems, own, lands, _ = gather_started[gi]
        _, landed = _split_copies_wait(send_sems, recv_sems, own, lands, after, "gather", f"gather_wait_{gi}")
        gw.update(zip(GROUPS[gi], _gather_forward(landed, f"gather_forward_{gi}")))

    small_sh_shapes = [wts[n][0].shape for n in SMALL_SHARDED]
    slots = _broadcast_all(_pack([wts[n][0] for n in SMALL_SHARDED]), "gather_small")
    per_chip = [_unpack(slots[2 * k], small_sh_shapes) for k in range(N_CHIPS)]
    sp = {n: jnp.concatenate([per_chip[k][i] for k in range(N_CHIPS)], axis=-1) for i, n in enumerate(SMALL_SHARDED)}

    def cols_major(w4):
        return jnp.transpose(w4, (1, 0, 2)).reshape(w4.shape[1], -1)

    half = ROPE_DIM // 2
    inv_freq = ROPE_THETA ** (-jnp.arange(half, dtype=F32) / half)
    invf = jnp.concatenate([inv_freq, inv_freq, jnp.zeros((LANES - ROPE_DIM,), F32)]).reshape(1, LANES)
    tabs = _rope_tables(positions.reshape(T, 1), invf, "rope_tables")
    scale = float(NOPE_DIM + ROPE_DIM) ** -0.5

    xs = [xin]
    saved = []
    w_out = {}
    for i in range(4):
        xi = xs[-1]
        gathered_weights(LAYER_GROUPS[i][0], xi)
        h = _rms_fwd(xi[None], 0, 0, D, norm_pre[i] + started_token if i == 0 else norm_pre[i], f"pre_norm_{i}", BF16)
        if i == 0:
            w_in_full = cols_major(gw['w_in_mla'])
            w_a = jnp.pad(w_in_full[:, :wa_cols], ((0, 0), (0, wa_pad - wa_cols)))[None]
            w_z = w_in_full[:, wa_cols:][None]
            pa = _mm_nn(h, w_a, 1, "mla_in_a")
            pz = _mm_nn(h, w_z, 1, "mla_in_z")
            gathered_weights(LAYER_GROUPS[i][1], pz)
            wq = cols_major(gw['w_uq']).reshape(q_rank, H, NOPE_DIM + ROPE_DIM)
            wq = jnp.pad(wq, ((0, 0), (0, 0), (0, HEAD_PAD - NOPE_DIM - ROPE_DIM))).reshape(1, q_rank, H * HEAD_PAD)
            wkv = cols_major(gw['w_ukv'])[None]
            qn = _rms_fwd(pa, 0, 0, q_rank, mla_q_norm[0], "mla_q_norm", BF16)
            kvn = _rms_fwd(pa, 0, q_rank // kv_rank, kv_rank, mla_kv_norm[0], "mla_kv_norm", BF16)
            q3 = _mm_nn(qn, wq, 1, "mla_q_up")
            kv3 = _mm_nn(kvn, wkv, 1, "mla_kv_up")
            qf, kf, vv = _qkv_layout(q3, kv3, pa, c_kr // LANES, tabs, H, "mla_qkv_layout")
            o, lse = _attn_fwd(qf, kf, vv, H, scale, "mla_attn_fwd")
            g = _gate_fwd(o, pz, "mla_gate_fwd")
            saved.append(dict(h=h, pa=pa, pz=pz, qn=qn, kvn=kvn, qf=qf, kf=kf, vv=vv, o=o, lse=lse, g=g))
        elif i == 1:
            p3 = _mm_nn(h, gw['w_in_sc'], 4, "sc_in")
            g = _sc_fwd(p3, sp['sc_conv'], "sc_mix_fwd")
            saved.append(dict(h=h, p3=p3, g=g))
        elif i == 2:
            p3 = _mm_nn(h, gw['w_in_gm'], 3, "gm_in")
            bs_t = jnp.transpose(gm_b_s[0])
            g = _gm_fwd(p3, sp['gm_ln_g'], sp['gm_ln_b'], gm_w_s[0], bs_t, "gm_mix_fwd")
            saved.append(dict(h=h, p3=p3, g=g, bs_t=bs_t))
        else:
            p3 = _mm_nn(h, gw['w_in_cf'], 3, "cf_in")
            y1 = _cf_conv_fwd(p3, sp['cf_dw'], sp['cf_dw_b'], "cf_conv_fwd")
            g = _cf_gate_fwd(y1, p3, sp['cf_ln_g'], sp['cf_ln_b'], "cf_gate_fwd")
            saved.append(dict(h=h, p3=p3, y1=y1, g=g))
        w_out[WO_NAMES[i]] = gw[WO_NAMES[i]].reshape(1, -1, D)
        yo = _mm_nn(g, w_out[WO_NAMES[i]], 1, f"out_proj_{i}")
        saved[-1]['yo'] = yo
        xs.append(_rms_fwd(yo, 0, 0, D, norm_post[i], f"post_norm_{i}", F32, res=xi))

    dx, loss_local = _loss_head(xs[4], target, "loss_head")
    loss = lax.psum(loss_local, ("x", "y", "c"))

    big_grads = {}
    sgrad = {}
    d_npre, d_npost = [None] * 4, [None] * 4
    exch_started, scatter_started, scattered = {}, {}, {}

    def start_exchange(gi):
        full = [big_grads[n] for n in GROUPS[gi]]
        lands = [lax.empty((g_.shape[0], g_.shape[1] // 2, g_.shape[2]), g_.dtype) for g_ in full]
        exch_started[gi] = _split_copies_start(full, lands, "halves", f"grads_exchange_start_{gi}")
        return exch_started[gi][4][0, 0]

    def start_scatter(gi, after):
        send_sems, recv_sems, full, lands, _ = exch_started[gi]
        full, recv = _split_copies_wait(send_sems, recv_sems, full, lands, after, "halves", f"grads_exchange_wait_{gi}")
        pair = [_pair_sum(g_, r, core, f"pair_sum_{n}") for n, g_, r in zip(GROUPS[gi], full, recv)]
        lands = [lax.dynamic_update_slice(lax.empty(p.shape, p.dtype), lax.dynamic_slice_in_dim(p, chip, 1, axis=0),
                                          (chip, 0, 0)) for p in pair]
        scatter_started[gi] = _split_copies_start(pair, lands, "scatter", f"scatter_start_{gi}")
        return scatter_started[gi][4][0, 0]

    def finish_scatter(gi, after):
        send_sems, recv_sems, pair, lands, _ = scatter_started[gi]
        scattered[gi] = _split_copies_wait(send_sems, recv_sems, pair, lands, after, "scatter", f"scatter_wait_{gi}")[1]

    token = 0.0
    for i in (3, 2, 1, 0):
        sv = saved[i]
        h = sv['h']
        dyo, d_npost[i] = _rms_bwd(sv['yo'], 0, 0, D, norm_post[i] + token, dx, f"post_norm_bwd_{i}", BF16)
        dyo3 = dyo[None]
        wo_name = WO_NAMES[i]
        dg = _mm_nt(dyo3, w_out[wo_name], f"out_proj_dx_{i}")
        big_grads[wo_name] = _mm_tn(sv['g'], dyo3, 1, f"out_proj_dw_{i}").reshape(N_CHIPS, -1, D)
        if i == 3:
            dz, dy1, sgrad['cf_ln_g'], sgrad['cf_ln_b'] = _cf_gate_bwd(sv['y1'], sv['p3'], sp['cf_ln_g'], sp['cf_ln_b'], dg, "cf_gate_bwd")
            dp3, sgrad['cf_dw'], sgrad['cf_dw_b'] = _cf_conv_bwd(sv['p3'], sp['cf_dw'], dy1, dz, "cf_conv_bwd")
            big_grads['w_in_cf'] = _mm_tn(h, dp3, N_CHIPS, "cf_in_dw")
            dh = _mm_nt(dp3, gw['w_in_cf'], "cf_in_dx")
        elif i == 2:
            dp3, sgrad['gm_ln_g'], sgrad['gm_ln_b'], sgrad['gm_w_s'], dbs_t = _gm_bwd(
                sv['p3'], sp['gm_ln_g'], sp['gm_ln_b'], gm_w_s[0], sv['bs_t'], dg, "gm_mix_bwd")
            sgrad['gm_b_s'] = jnp.transpose(dbs_t[:, :GM_GROUPS])
            big_grads['w_in_gm'] = _mm_tn(h, dp3, N_CHIPS, "gm_in_dw")
            dh = _mm_nt(dp3, gw['w_in_gm'], "gm_in_dx")
        elif i == 1:
            dp3, sgrad['sc_conv'] = _sc_bwd(sv['p3'], sp['sc_conv'], dg, "sc_mix_bwd")
            big_grads['w_in_sc'] = _mm_tn(h, dp3, N_CHIPS, "sc_in_dw")
            dh = _mm_nt(dp3, gw['w_in_sc'], "sc_in_dx")
        else:
            do, dpz = _gate_bwd(dg, sv['o'], sv['pz'], "mla_gate_bwd")
            dqf, dkf, dv = _attn_bwd(sv['qf'], sv['kf'], sv['vv'], do, sv['o'], sv['lse'], H, scale, "mla_attn_bwd")
            finish_scatter(3, dqf)
            token = start_scatter(2, dqf)
            dq3, dkv3, dkr = _qkv_layout_bwd(dqf, dkf, dv, (tabs[0] + token, tabs[1], tabs[2]), H, "mla_qkv_layout_bwd")
            dqn = _mm_nt(dq3, wq, "mla_q_up_dx")
            dwq = _mm_tn(sv['qn'], dq3, 1, "mla_q_up_dw")
            dkvn = _mm_nt(dkv3, wkv, "mla_kv_up_dx")
            dwkv = _mm_tn(sv['kvn'], dkv3, 1, "mla_kv_up_dw")
            dwq_ = dwq[0].reshape(q_rank, H, HEAD_PAD)[:, :, :NOPE_DIM + ROPE_DIM].reshape(q_rank, N_CHIPS, -1)
            big_grads['w_uq'] = jnp.transpose(dwq_, (1, 0, 2))
            big_grads['w_ukv'] = jnp.transpose(dwkv[0].reshape(kv_rank, N_CHIPS, -1), (1, 0, 2))
            token = start_exchange(1)
            dcq, dqg = _rms_bwd(sv['pa'], 0, 0, q_rank, mla_q_norm[0] + token, dqn, "mla_q_norm_bwd", BF16)
            dckv, dkvg = _rms_bwd(sv['pa'], 0, q_rank // kv_rank, kv_rank, mla_kv_norm[0], dkvn, "mla_kv_norm_bwd", BF16)
            sgrad['mla_q_norm'], sgrad['mla_kv_norm'] = dqg, dkvg
            dpa = jnp.concatenate([dcq, dckv, dkr], axis=1)[None]
            dwa = _mm_tn(h, dpa, 1, "mla_in_a_dw")
            dwz = _mm_tn(h, dpz, 1, "mla_in_z_dw")
            dh_a = _mm_nt(dpa, w_a, "mla_in_a_dx")
            dh = _mm_nt(dpz, w_z, "mla_in_z_dx", add=dh_a)
            dw_in = jnp.concatenate([dwa[0][:, :wa_cols], dwz[0]], axis=1)
            big_grads['w_in_mla'] = jnp.transpose(dw_in.reshape(D, N_CHIPS, -1), (1, 0, 2))
            token = start_scatter(1, dh) + start_exchange(0)
        dx, d_npre[i] = _rms_bwd(xs[i][None], 0, 0, D, norm_pre[i] + token if i == 0 else norm_pre[i], dh,
                                 f"pre_norm_bwd_{i}", F32, res=dx)
        if i == 3:
            token = start_exchange(4)
        elif i == 2:
            token = start_scatter(4, dx) + start_exchange(3)
        elif i == 1:
            finish_scatter(4, dx)
            token = start_scatter(3, dx) + start_exchange(2)
    finish_scatter(2, dx)
    start_scatter(0, dx)
    finish_scatter(1, dx)
    finish_scatter(0, dx)
    grad_x = dx.reshape(1, T, D)
    sgrad['norm_pre'] = jnp.concatenate(d_npre, axis=0)
    sgrad['norm_post'] = jnp.concatenate(d_npost, axis=0)

    assert [n for names in GROUPS for n in names] == BIG
    halves = [_chip_sum(r, core, f"chip_sum_{n}") for gi in range(len(GROUPS)) for n, r in zip(GROUPS[gi], scattered[gi])]
    joined = _core_join_halves(halves, "grads_core_join")
    grads = dict(zip(BIG, [j[None] for j in joined]))

    small_full_shapes = [sgrad[n].reshape(wts[n].shape[:-1] + (-1,)).shape for n in SMALL]
    gslots = _broadcast_all(_pack([sgrad[n] for n in SMALL]), "grads_small_exchange")
    gsum = _unpack(_slot_sum(gslots, "grads_small_sum"), small_full_shapes)
    for n, gs in zip(SMALL, gsum):
        if n in SMALL_SHARDED:
            per = wts[n].shape[-1]
            gs = lax.dynamic_slice_in_dim(gs, chip * per, per, axis=gs.ndim - 1)
        grads[n] = gs.reshape(wts[n].shape)

    delta, new_m, new_v = {}, {}, {}
    for n in BIG:
        shp = wts[n].shape
        two_d = (shp[1], shp[2])
        d_, m_, v_ = _adamw(wts[n].reshape(two_d), grads[n].reshape(two_d), mom_m[n].reshape(two_d),
                            mom_v[n].reshape(two_d), f"adamw_{n}")
        delta[n], new_m[n], new_v[n] = d_.reshape(shp), m_.reshape(shp), v_.reshape(shp)
    shapes = [wts[n].shape for n in SMALL]
    d_, m_, v_ = _adamw(_pack([wts[n] for n in SMALL]), _pack([grads[n] for n in SMALL]),
                        _pack([mom_m[n] for n in SMALL]), _pack([mom_v[n] for n in SMALL]), "adamw_small")
    for n, a, b, cc in zip(SMALL, _unpack(d_, shapes), _unpack(m_, shapes), _unpack(v_, shapes)):
        delta[n], new_m[n], new_v[n] = a, b, cc

    return (loss, grad_x, *[grads[n] for n in WEIGHTS], *[delta[n] for n in WEIGHTS],
            *[new_m[n] for n in WEIGHTS], *[new_v[n] for n in WEIGHTS])
```

```python
import functools
import math

import jax
import jax.numpy as jnp
from jax import lax
from jax.experimental import pallas as pl
from jax.experimental.pallas import tpu as pltpu

F32, BF16 = jnp.float32, jnp.bfloat16
S = jax.ShapeDtypeStruct
MESH_ID = pl.DeviceIdType.MESH

V7X_VMEM_BYTES = 64 * 1024 * 1024
VMEM_LIMIT = V7X_VMEM_BYTES - 8 * 1024 * 1024
LANES = 128
N_CHIPS = 4
N_DEV = 8

NORM_EPS = 1e-6
LN_EPS = 1e-5
ROPE_THETA = 10000.0
ROPE_DIM = 64
NOPE_DIM = 128
V_DIM = 128
HEAD_PAD = 256
GM_CHUNK = 128
GM_GROUPS = 8
NEG = -1e30

ADAM_LR, ADAM_B1, ADAM_B2, ADAM_EPS, ADAM_WD, ADAM_STEP = 0.001, 0.9, 0.999, 1e-08, 0.01, 10

FWD_PARAMS = ['x', 'positions', 'norm_pre', 'norm_post', 'w_in_mla', 'mla_q_norm', 'w_uq', 'mla_kv_norm', 'w_ukv',
              'w_out_mla', 'w_in_sc', 'sc_conv', 'w_out_sc', 'w_in_gm', 'gm_ln_g', 'gm_ln_b', 'gm_w_s', 'gm_b_s',
              'w_out_gm', 'w_in_cf', 'cf_dw', 'cf_dw_b', 'cf_ln_g', 'cf_ln_b', 'w_out_cf']
WEIGHTS = FWD_PARAMS[2:]
BIG = ['w_in_mla', 'w_uq', 'w_ukv', 'w_out_mla', 'w_in_sc', 'w_out_sc', 'w_in_gm', 'w_out_gm', 'w_in_cf', 'w_out_cf']
GROUPS = [['w_in_mla'], ['w_uq', 'w_ukv', 'w_out_mla'], ['w_in_sc', 'w_out_sc'], ['w_in_gm', 'w_out_gm'],
          ['w_in_cf', 'w_out_cf']]
LAYER_GROUPS = [[0, 1], [2], [3], [4]]
WO_NAMES = ['w_out_mla', 'w_out_sc', 'w_out_gm', 'w_out_cf']
SMALL = [n for n in WEIGHTS if n not in BIG]
SMALL_SHARDED = ['sc_conv', 'gm_ln_g', 'gm_ln_b', 'cf_dw', 'cf_dw_b', 'cf_ln_g', 'cf_ln_b']


def _cparams(sem=None, **kw):
    return pltpu.CompilerParams(dimension_semantics=sem, vmem_limit_bytes=VMEM_LIMIT, **kw)


def _pick(dim, pref):
    if dim <= pref:
        return dim
    t = (pref // LANES) * LANES
    while t >= LANES and dim % t:
        t -= LANES
    if t >= min(pref, 512):
        return t
    return dim if (dim <= 2048 or t < LANES) else t


def _silu(x):
    return x * jax.nn.sigmoid(x)


def _dsilu(x):
    s = jax.nn.sigmoid(x)
    return s * (1.0 + x * (1.0 - s))


def _gelu(x):
    return 0.5 * x * (1.0 + lax.erf(x * (2.0 ** -0.5)))


def _dgelu(x):
    cdf = 0.5 * (1.0 + lax.erf(x * (2.0 ** -0.5)))
    return cdf + x * jnp.exp(-0.5 * x * x) * ((2.0 * math.pi) ** -0.5)


MM_ONE_DOT = 4096


def _contract_tile(dim, divisible_by, pref_when_split):
    return dim if dim <= MM_ONE_DOT and divisible_by % dim == 0 else _pick(divisible_by, pref_when_split)


def _mm_accumulate(step, nsteps, prod, o_ref, acc, init=None):
    if nsteps == 1:
        r = prod()
        if init is not None:
            r = r + init()
        o_ref[...] = r.astype(o_ref.dtype)
        return

    @pl.when(step == 0)
    def _():
        acc[...] = jnp.zeros_like(acc) if init is None else init()

    acc[...] += prod()

    @pl.when(step == nsteps - 1)
    def _():
        o_ref[...] = acc[...].astype(o_ref.dtype)


def _mm_nn(a, w, np_out, name, out_dtype=F32):
    M, K = a.shape
    J, _, n = w.shape
    N = J * n
    W = N // np_out
    tm, tn = _pick(M, 1024), _pick(math.gcd(W, n), 1024)
    tk = _contract_tile(K, K, 2048)
    nk = K // tk

    def body(*refs):
        a_ref, w_ref, o_ref = refs[:3]
        _mm_accumulate(pl.program_id(2), nk, lambda: jnp.dot(a_ref[...], w_ref[...], preferred_element_type=F32),
                       o_ref, refs[-1])

    return pl.pallas_call(
        body, name=name, grid=(M // tm, N // tn, nk),
        in_specs=[pl.BlockSpec((tm, tk), lambda i, j, k: (i, k)),
                  pl.BlockSpec((None, tk, tn), lambda i, j, k: (j // (n // tn), k, j % (n // tn)))],
        out_specs=pl.BlockSpec((None, tm, tn), lambda i, j, k: (j // (W // tn), i, j % (W // tn))),
        out_shape=S((np_out, M, W), out_dtype),
        scratch_shapes=[pltpu.VMEM((tm, tn), F32)] if nk > 1 else [],
        compiler_params=_cparams(("parallel", "parallel", "arbitrary")),
    )(a, w)


def _mm_nt(a3, w, name, add=None, out_dtype=F32):
    NP, M, W = a3.shape
    J, K, n = w.shape
    N = NP * W
    tm, to = _pick(M, 1024), _pick(K, 1024)
    tc = _contract_tile(N, math.gcd(W, n), 2048)
    nc = N // tc
    has_add = add is not None

    def body(*refs):
        a_ref, w_ref = refs[0], refs[1]
        o_ref = refs[3] if has_add else refs[2]
        _mm_accumulate(
            pl.program_id(2), nc,
            lambda: lax.dot_general(a_ref[...], w_ref[...], (((1,), (1,)), ((), ())), preferred_element_type=F32),
            o_ref, refs[-1], init=(lambda: refs[2][...].astype(F32)) if has_add else None)

    in_specs = [pl.BlockSpec((None, tm, tc), lambda i, j, c: (c // (W // tc), i, c % (W // tc))),
                pl.BlockSpec((None, to, tc), lambda i, j, c: (c // (n // tc), j, c % (n // tc)))]
    ops = [a3, w]
    if has_add:
        in_specs.append(pl.BlockSpec((tm, to), lambda i, j, c: (i, j)))
        ops.append(add)
    return pl.pallas_call(
        body, name=name, grid=(M // tm, K // to, nc),
        in_specs=in_specs,
        out_specs=pl.BlockSpec((tm, to), lambda i, j, c: (i, j)),
        out_shape=S((M, K), out_dtype),
        scratch_shapes=[pltpu.VMEM((tm, to), F32)] if nc > 1 else [],
        compiler_params=_cparams(("parallel", "parallel", "arbitrary")),
    )(*ops)


def _mm_tn(a, d3, j_out, name, out_dtype=F32):
    M, K = a.shape
    NP, _, W = d3.shape
    N = NP * W
    n = N // j_out
    to, tn = _pick(K, 1024), _pick(math.gcd(W, n), 1024)
    tmc = _contract_tile(M, M, 2048)
    nm = M // tmc

    def body(*refs):
        a_ref, d_ref, o_ref = refs[:3]
        _mm_accumulate(
            pl.program_id(2), nm,
            lambda: lax.dot_general(a_ref[...], d_ref[...], (((0,), (0,)), ((), ())), preferred_element_type=F32),
            o_ref, refs[-1])

    return pl.pallas_call(
        body, name=name, grid=(K // to, N // tn, nm),
        in_specs=[pl.BlockSpec((tmc, to), lambda i, j, m: (m, i)),
                  pl.BlockSpec((None, tmc, tn), lambda i, j, m: (j // (W // tn), m, j % (W // tn)))],
        out_specs=pl.BlockSpec((None, to, tn), lambda i, j, m: (j // (n // tn), i, j % (n // tn))),
        out_shape=S((j_out, K, n), out_dtype),
        scratch_shapes=[pltpu.VMEM((to, tn), F32)] if nm > 1 else [],
        compiler_params=_cparams(("parallel", "parallel", "arbitrary")),
    )(a, d3)


def _rms_fwd(x3, piece, col_blk, width, g, name, out_dtype, res=None):
    T = x3.shape[1]
    tr = _pick(T, 256)
    has_res = res is not None

    def body(*refs):
        x_ref, g_ref = refs[0], refs[1]
        o_ref = refs[-1]
        x = x_ref[...].astype(F32)
        y = x * lax.rsqrt(jnp.mean(x * x, axis=-1, keepdims=True) + NORM_EPS) * g_ref[...]
        if has_res:
            y = refs[2][...] + y
        o_ref[...] = y.astype(o_ref.dtype)

    in_specs = [pl.BlockSpec((None, tr, width), lambda i: (piece, i, col_blk)),
                pl.BlockSpec((1, width), lambda i: (0, 0))]
    ops = [x3, g.reshape(1, width)]
    if has_res:
        in_specs.append(pl.BlockSpec((tr, width), lambda i: (i, 0)))
        ops.append(res)
    return pl.pallas_call(
        body, name=name, grid=(T // tr,), in_specs=in_specs,
        out_specs=pl.BlockSpec((tr, width), lambda i: (i, 0)),
        out_shape=S((T, width), out_dtype),
        compiler_params=_cparams(("parallel",)),
    )(*ops)


def _rms_bwd(u3, piece, col_blk, width, g, dy, name, out_dtype, res=None):
    T = u3.shape[1]
    tr = _pick(T, 256)
    has_res = res is not None

    def body(*refs):
        u_ref, g_ref, dy_ref = refs[0], refs[1], refs[2]
        du_ref, dg_ref = refs[-2], refs[-1]
        i = pl.program_id(0)
        u = u_ref[...].astype(F32)
        dy_ = dy_ref[...].astype(F32)
        r = lax.rsqrt(jnp.mean(u * u, axis=-1, keepdims=True) + NORM_EPS)
        nrm = u * r
        gdy = g_ref[...] * dy_
        du = r * (gdy - nrm * jnp.mean(gdy * nrm, axis=-1, keepdims=True))
        if has_res:
            du = du + refs[3][...]
        du_ref[...] = du.astype(du_ref.dtype)

        @pl.when(i == 0)
        def _():
            dg_ref[...] = jnp.zeros_like(dg_ref)

        dg_ref[...] += jnp.sum(dy_ * nrm, axis=0, keepdims=True)

    in_specs = [pl.BlockSpec((None, tr, width), lambda i: (piece, i, col_blk)),
                pl.BlockSpec((1, width), lambda i: (0, 0)),
                pl.BlockSpec((tr, width), lambda i: (i, 0))]
    ops = [u3, g.reshape(1, width), dy]
    if has_res:
        in_specs.append(pl.BlockSpec((tr, width), lambda i: (i, 0)))
        ops.append(res)
    return pl.pallas_call(
        body, name=name, grid=(T // tr,), in_specs=in_specs,
        out_specs=[pl.BlockSpec((tr, width), lambda i: (i, 0)), pl.BlockSpec((1, width), lambda i: (0, 0))],
        out_shape=[S((T, width), out_dtype), S((1, width), F32)],
        compiler_params=_cparams(("arbitrary",)),
    )(*ops)


def _loss_head(xl, target, name):
    T, D = xl.shape
    tr = _pick(T, 256)

    def body(x_ref, t_ref, dx_ref, l_ref):
        i = pl.program_id(0)
        err = x_ref[...] - t_ref[...]
        dx_ref[...] = err * (1.0 / D)

        @pl.when(i == 0)
        def _():
            l_ref[...] = jnp.zeros_like(l_ref)

        l_ref[...] += jnp.sum(err * err)

    dx, l = pl.pallas_call(
        body, name=name, grid=(T // tr,),
        in_specs=[pl.BlockSpec((tr, D), lambda i: (i, 0)), pl.BlockSpec((tr, D), lambda i: (i, 0))],
        out_specs=[pl.BlockSpec((tr, D), lambda i: (i, 0)), pl.BlockSpec((8, LANES), lambda i: (0, 0))],
        out_shape=[S((T, D), F32), S((8, LANES), F32)],
        compiler_params=_cparams(("arbitrary",)),
    )(xl, target)
    return dx, l[0, 0] * (0.5 / D)


def _gate_fwd(o, z3, name):
    T, W = o.shape
    tr = _pick(T, 256)

    def body(o_ref, z_ref, g_ref):
        g_ref[...] = (o_ref[...] * _silu(z_ref[...])).astype(g_ref.dtype)

    return pl.pallas_call(
        body, name=name, grid=(T // tr,),
        in_specs=[pl.BlockSpec((tr, W), lambda i: (i, 0)), pl.BlockSpec((None, tr, W), lambda i: (0, i, 0))],
        out_specs=pl.BlockSpec((tr, W), lambda i: (i, 0)),
        out_shape=S((T, W), BF16), compiler_params=_cparams(("parallel",)),
    )(o, z3)


def _gate_bwd(dg, o, z3, name):
    T, W = o.shape
    tr = _pick(T, 256)

    def body(dg_ref, o_ref, z_ref, do_ref, dz_ref):
        dg_, z = dg_ref[...], z_ref[...]
        do_ref[...] = (dg_ * _silu(z)).astype(do_ref.dtype)
        dz_ref[...] = (dg_ * o_ref[...] * _dsilu(z)).astype(dz_ref.dtype)

    return pl.pallas_call(
        body, name=name, grid=(T // tr,),
        in_specs=[pl.BlockSpec((tr, W), lambda i: (i, 0)), pl.BlockSpec((tr, W), lambda i: (i, 0)),
                  pl.BlockSpec((None, tr, W), lambda i: (0, i, 0))],
        out_specs=[pl.BlockSpec((tr, W), lambda i: (i, 0)), pl.BlockSpec((None, tr, W), lambda i: (0, i, 0))],
        out_shape=[S((T, W), BF16), S((1, T, W), BF16)], compiler_params=_cparams(("parallel",)),
    )(dg, o, z3)


def _rope_tables(pos_col, invf, name):
    T = pos_col.shape[0]
    tr = _pick(T, 512)
    half = ROPE_DIM // 2

    def body(p_ref, f_ref, c_ref, sa_ref, sb_ref):
        ang = p_ref[...].astype(F32) * f_ref[...]
        lane = lax.broadcasted_iota(jnp.int32, ang.shape, 1)
        cs, sn = jnp.cos(ang), jnp.sin(ang)
        c_ref[...] = jnp.where(lane < ROPE_DIM, cs, 0.0)
        sa_ref[...] = jnp.where(lane < half, -sn, 0.0)
        sb_ref[...] = jnp.where((lane >= half) & (lane < ROPE_DIM), sn, 0.0)

    spec = pl.BlockSpec((tr, LANES), lambda i: (i, 0))
    return pl.pallas_call(
        body, name=name, grid=(T // tr,),
        in_specs=[pl.BlockSpec((tr, 1), lambda i: (i, 0)), pl.BlockSpec((1, LANES), lambda i: (0, 0))],
        out_specs=[spec, spec, spec], out_shape=[S((T, LANES), F32)] * 3,
        compiler_params=_cparams(("parallel",)),
    )(pos_col, invf)


def _rope(t, c, sa, sb):
    half = ROPE_DIM // 2
    return t * c + pltpu.roll(t, LANES - half, 1) * sa + pltpu.roll(t, half, 1) * sb


def _rope_t(d, c, sa, sb):
    half = ROPE_DIM // 2
    return d * c + pltpu.roll(d * sa, half, 1) + pltpu.roll(d * sb, LANES - half, 1)


def _qkv_layout(q3, kv3, pa3, kr_blk, tabs, H, name):
    T = q3.shape[1]
    tr = _pick(T, 128)

    def body(q_ref, kv_ref, kr_ref, c_ref, sa_ref, sb_ref, qf_ref, kf_ref, v_ref):
        c, sa, sb = c_ref[...], sa_ref[...], sb_ref[...]
        kr = _rope(kr_ref[...], c, sa, sb).astype(BF16)
        for h in range(H):
            nope = slice(h * HEAD_PAD, h * HEAD_PAD + NOPE_DIM)
            rest = slice(h * HEAD_PAD + NOPE_DIM, (h + 1) * HEAD_PAD)
            qf_ref[:, nope] = q_ref[:, nope].astype(BF16)
            qf_ref[:, rest] = _rope(q_ref[:, rest], c, sa, sb).astype(BF16)
            kf_ref[:, nope] = kv_ref[:, nope].astype(BF16)
            kf_ref[:, rest] = kr
            v_ref[:, h * V_DIM:(h + 1) * V_DIM] = kv_ref[:, rest].astype(BF16)

    tab = pl.BlockSpec((tr, LANES), lambda i: (i, 0))
    wide = pl.BlockSpec((None, tr, H * HEAD_PAD), lambda i: (0, i, 0))
    return pl.pallas_call(
        body, name=name, grid=(T // tr,),
        in_specs=[wide, wide, pl.BlockSpec((None, tr, LANES), lambda i: (0, i, kr_blk)), tab, tab, tab],
        out_specs=[pl.BlockSpec((tr, H * HEAD_PAD), lambda i: (i, 0)), pl.BlockSpec((tr, H * HEAD_PAD), lambda i: (i, 0)),
                   pl.BlockSpec((tr, H * V_DIM), lambda i: (i, 0))],
        out_shape=[S((T, H * HEAD_PAD), BF16), S((T, H * HEAD_PAD), BF16), S((T, H * V_DIM), BF16)],
        compiler_params=_cparams(("parallel",)),
    )(q3, kv3, pa3, *tabs)


def _qkv_layout_bwd(dqf, dkf, dv, tabs, H, name):
    T = dqf.shape[0]
    tr = _pick(T, 128)

    def body(dqf_ref, dkf_ref, dv_ref, c_ref, sa_ref, sb_ref, dq_ref, dkv_ref, dkr_ref):
        c, sa, sb = c_ref[...], sa_ref[...], sb_ref[...]
        dkr = jnp.zeros((tr, LANES), F32)
        for h in range(H):
            nope = slice(h * HEAD_PAD, h * HEAD_PAD + NOPE_DIM)
            rest = slice(h * HEAD_PAD + NOPE_DIM, (h + 1) * HEAD_PAD)
            dq_ref[:, nope] = dqf_ref[:, nope]
            dq_ref[:, rest] = _rope_t(dqf_ref[:, rest].astype(F32), c, sa, sb).astype(BF16)
            dkv_ref[:, nope] = dkf_ref[:, nope]
            dkv_ref[:, rest] = dv_ref[:, h * V_DIM:(h + 1) * V_DIM]
            dkr = dkr + dkf_ref[:, rest].astype(F32)
        dkr_ref[...] = _rope_t(dkr, c, sa, sb).astype(BF16)

    tab = pl.BlockSpec((tr, LANES), lambda i: (i, 0))
    wide_in = pl.BlockSpec((tr, H * HEAD_PAD), lambda i: (i, 0))
    wide_out = pl.BlockSpec((None, tr, H * HEAD_PAD), lambda i: (0, i, 0))
    return pl.pallas_call(
        body, name=name, grid=(T // tr,),
        in_specs=[wide_in, wide_in, pl.BlockSpec((tr, H * V_DIM), lambda i: (i, 0)), tab, tab, tab],
        out_specs=[wide_out, wide_out, pl.BlockSpec((tr, LANES), lambda i: (i, 0))],
        out_shape=[S((1, T, H * HEAD_PAD), BF16), S((1, T, H * HEAD_PAD), BF16), S((T, LANES), BF16)],
        compiler_params=_cparams(("parallel",)),
    )(dqf, dkf, dv, *tabs)


ATTN_BLOCK = 512


def _nt(a, b):
    return lax.dot_general(a, b, (((1,), (1,)), ((), ())), preferred_element_type=F32)


def _tn(a, b):
    return lax.dot_general(a, b, (((0,), (0,)), ((), ())), preferred_element_type=F32)


def _causal_blocks(qi, tb, block):
    if qi > 0:
        def step(ki, carry):
            block(pl.multiple_of(ki * tb, tb), False)
            return carry
        lax.fori_loop(0, qi, step, 0)
    block(qi * tb, True)


def _attn_fwd(qf, kf, v, H, scale, name):
    T = qf.shape[0]
    tb = _pick(T, ATTN_BLOCK)
    nb = T // tb

    def body(q_ref, k_ref, v_ref, o_ref, lse_ref):
        row = lax.broadcasted_iota(jnp.int32, (tb, tb), 0)
        col = lax.broadcasted_iota(jnp.int32, (tb, tb), 1)
        for qi in range(nb):
            rows, before = pl.ds(qi * tb, tb), qi * tb
            q = q_ref[rows, :]
            s_own = jnp.where(col <= row, _nt(q, k_ref[rows, :]), NEG)
            m = jnp.max(s_own, axis=-1, keepdims=True)
            if qi > 0:
                s_pre = _nt(q, k_ref[0:before, :])
                m = jnp.maximum(m, jnp.max(s_pre, axis=-1, keepdims=True))
            p_own = jnp.exp((s_own - m) * scale)
            l = jnp.sum(p_own, axis=-1, keepdims=True)
            acc = jnp.dot(p_own.astype(BF16), v_ref[rows, :], preferred_element_type=F32)
            if qi > 0:
                p_pre = jnp.exp((s_pre - m) * scale)
                l = l + jnp.sum(p_pre, axis=-1, keepdims=True)
                acc = acc + jnp.dot(p_pre.astype(BF16), v_ref[0:before, :], preferred_element_type=F32)
            o_ref[rows, :] = acc / l
            lse_ref[rows, :] = jnp.broadcast_to(m * scale + jnp.log(l), (tb, LANES))

    return pl.pallas_call(
        body, name=name, grid=(H,),
        in_specs=[pl.BlockSpec((T, HEAD_PAD), lambda h: (0, h)), pl.BlockSpec((T, HEAD_PAD), lambda h: (0, h)),
                  pl.BlockSpec((T, V_DIM), lambda h: (0, h))],
        out_specs=[pl.BlockSpec((T, V_DIM), lambda h: (0, h)), pl.BlockSpec((T, LANES), lambda h: (0, h))],
        out_shape=[S((T, H * V_DIM), F32), S((T, H * LANES), F32)],
        compiler_params=_cparams(("parallel",)),
    )(qf, kf, v)


def _attn_bwd(qf, kf, v, do, o, lse, H, scale, name):
    T = qf.shape[0]
    tb = _pick(T, ATTN_BLOCK)
    nb = T // tb

    def body(q_ref, k_ref, v_ref, do_ref, o_ref, lse_ref, dq_ref, dk_ref, dv_ref, dq_acc, dk_acc, dv_acc):
        row = lax.broadcasted_iota(jnp.int32, (tb, tb), 0)
        col = lax.broadcasted_iota(jnp.int32, (tb, tb), 1)
        dk_acc[...] = jnp.zeros_like(dk_acc)
        dv_acc[...] = jnp.zeros_like(dv_acc)
        for qi in range(nb):
            rows = pl.ds(qi * tb, tb)
            dq_acc[...] = jnp.zeros_like(dq_acc)
            delta = jnp.sum(do_ref[rows, :].astype(F32) * o_ref[rows, :], axis=-1, keepdims=True)
            lse_q = lse_ref[rows, 0:1]

            def block(k0, masked, rows=rows, delta=delta, lse_q=lse_q):
                keys = pl.ds(k0, tb)
                q, k, do_ = q_ref[rows, :], k_ref[keys, :], do_ref[rows, :]
                s = _nt(q, k)
                if masked:
                    s = jnp.where(col <= row, s, NEG)
                p = jnp.exp(s * scale - lse_q)
                dp = _nt(do_, v_ref[keys, :])
                ds = (p * (dp - delta) * scale).astype(BF16)
                dv_acc[keys, :] += _tn(p.astype(BF16), do_)
                dk_acc[keys, :] += _tn(ds, q)
                dq_acc[...] += jnp.dot(ds, k, preferred_element_type=F32)

            _causal_blocks(qi, tb, block)
            dq_ref[rows, :] = dq_acc[...].astype(dq_ref.dtype)
        dk_ref[...] = dk_acc[...].astype(dk_ref.dtype)
        dv_ref[...] = dv_acc[...].astype(dv_ref.dtype)

    hp = pl.BlockSpec((T, HEAD_PAD), lambda h: (0, h))
    hv = pl.BlockSpec((T, V_DIM), lambda h: (0, h))
    return pl.pallas_call(
        body, name=name, grid=(H,),
        in_specs=[hp, hp, hv, hv, hv, pl.BlockSpec((T, LANES), lambda h: (0, h))],
        out_specs=[hp, hp, hv],
        out_shape=[S((T, H * HEAD_PAD), BF16), S((T, H * HEAD_PAD), BF16), S((T, H * V_DIM), BF16)],
        scratch_shapes=[pltpu.VMEM((tb, HEAD_PAD), F32), pltpu.VMEM((T, HEAD_PAD), F32), pltpu.VMEM((T, V_DIM), F32)],
        compiler_params=_cparams(("parallel",)),
    )(qf, kf, v, do, o, lse)


CONV_ROWS = 256
CONV_COLS = 128


def _conv_chunks(T):
    rc = min(CONV_ROWS, T)
    return [(r, rc) for r in range(0, T, rc)]


def _causal_conv(pad_ref, lead, w_ref, width, r0, rc):
    acc = None
    for k in range(width):
        term = w_ref[k:k + 1, :] * pad_ref[pl.ds(lead + r0 - (width - 1) + k, rc), :]
        acc = term if acc is None else acc + term
    return acc


def _anticausal_conv(pad_ref, w_ref, width, r0, rc):
    acc = None
    for k in range(width):
        term = w_ref[k:k + 1, :] * pad_ref[pl.ds(r0 + (width - 1) - k, rc), :]
        acc = term if acc is None else acc + term
    return acc


def _conv_wgrad(dpad_ref, xpad_ref, lead, width, T, dw_ref):
    for k in range(width):
        tot = None
        for r0, rc in _conv_chunks(T):
            part = jnp.sum(dpad_ref[pl.ds(r0, rc), :] * xpad_ref[pl.ds(lead + r0 - (width - 1) + k, rc), :],
                           axis=0, keepdims=True)
            tot = part if tot is None else tot + part
        dw_ref[k:k + 1, :] = tot


def _sc_fwd(p3, wconv, name):
    _, T, W = p3.shape
    width = wconv.shape[0]
    cw = min(CONV_COLS, W)
    lead = 8

    def body(p_ref, w_ref, g_ref, pad):
        pad[0:lead, :] = jnp.zeros((lead, cw), F32)
        for r0, rc in _conv_chunks(T):
            pad[pl.ds(lead + r0, rc), :] = p_ref[1, pl.ds(r0, rc), :] * p_ref[2, pl.ds(r0, rc), :]
        for r0, rc in _conv_chunks(T):
            rows = pl.ds(r0, rc)
            y = p_ref[0, rows, :] * _causal_conv(pad, lead, w_ref, width, r0, rc)
            g_ref[rows, :] = (y * _silu(p_ref[3, rows, :])).astype(g_ref.dtype)

    return pl.pallas_call(
        body, name=name, grid=(W // cw,),
        in_specs=[pl.BlockSpec((4, T, cw), lambda j: (0, 0, j)), pl.BlockSpec((width, cw), lambda j: (0, j))],
        out_specs=pl.BlockSpec((T, cw), lambda j: (0, j)),
        out_shape=S((T, W), BF16),
        scratch_shapes=[pltpu.VMEM((T + lead, cw), F32)],
        compiler_params=_cparams(("parallel",)),
    )(p3, wconv)


def _sc_bwd(p3, wconv, dg, name):
    _, T, W = p3.shape
    width = wconv.shape[0]
    cw = min(CONV_COLS, W)
    lead = 8

    def body(p_ref, w_ref, dg_ref, dp_ref, dw_ref, cupad, dvpad):
        cupad[0:lead, :] = jnp.zeros((lead, cw), F32)
        dvpad[pl.ds(T, lead), :] = jnp.zeros((lead, cw), F32)
        for r0, rc in _conv_chunks(T):
            cupad[pl.ds(lead + r0, rc), :] = p_ref[1, pl.ds(r0, rc), :] * p_ref[2, pl.ds(r0, rc), :]
        for r0, rc in _conv_chunks(T):
            rows = pl.ds(r0, rc)
            b, z, dg_ = p_ref[0, rows, :], p_ref[3, rows, :], dg_ref[rows, :]
            v = _causal_conv(cupad, lead, w_ref, width, r0, rc)
            dy = dg_ * _silu(z)
            dp_ref[3, rows, :] = (dg_ * b * v * _dsilu(z)).astype(dp_ref.dtype)
            dp_ref[0, rows, :] = (dy * v).astype(dp_ref.dtype)
            dvpad[rows, :] = dy * b
        for r0, rc in _conv_chunks(T):
            rows = pl.ds(r0, rc)
            dcu = _anticausal_conv(dvpad, w_ref, width, r0, rc)
            dp_ref[1, rows, :] = (dcu * p_ref[2, rows, :]).astype(dp_ref.dtype)
            dp_ref[2, rows, :] = (dcu * p_ref[1, rows, :]).astype(dp_ref.dtype)
        _conv_wgrad(dvpad, cupad, lead, width, T, dw_ref)

    return pl.pallas_call(
        body, name=name, grid=(W // cw,),
        in_specs=[pl.BlockSpec((4, T, cw), lambda j: (0, 0, j)), pl.BlockSpec((width, cw), lambda j: (0, j)),
                  pl.BlockSpec((T, cw), lambda j: (0, j))],
        out_specs=[pl.BlockSpec((4, T, cw), lambda j: (0, 0, j)), pl.BlockSpec((width, cw), lambda j: (0, j))],
        out_shape=[S((4, T, W), BF16), S((width, W), F32)],
        scratch_shapes=[pltpu.VMEM((T + lead, cw), F32), pltpu.VMEM((T + lead, cw), F32)],
        compiler_params=_cparams(("parallel",)),
    )(p3, wconv, dg)


def _gm_common(p_ref, lng_ref, lnb_ref):
    ug = _gelu(p_ref[0])
    vg = _gelu(p_ref[1])
    mu = jnp.mean(vg, axis=-1, keepdims=True)
    xc = vg - mu
    rstd = lax.rsqrt(jnp.mean(xc * xc, axis=-1, keepdims=True) + LN_EPS)
    xhat = xc * rstd
    vn = xhat * lng_ref[...] + lnb_ref[...]
    return ug, xhat, rstd, vn


def _gm_mix_weights(ws_ref, g):
    row = lax.broadcasted_iota(jnp.int32, (GM_CHUNK, GM_CHUNK), 0)
    col = lax.broadcasted_iota(jnp.int32, (GM_CHUNK, GM_CHUNK), 1)
    return jnp.where(col <= row, ws_ref[g], 0.0).astype(BF16)


def _gm_fwd(p3, lng, lnb, ws, bs_t, name):
    _, T, W = p3.shape
    gw = W // GM_GROUPS

    def body(p_ref, lng_ref, lnb_ref, ws_ref, bs_ref, g_ref):
        ug, _, _, vn = _gm_common(p_ref, lng_ref, lnb_ref)
        sz = _silu(p_ref[2])
        vnb = vn.astype(BF16)
        for g in range(GM_GROUPS):
            cols = slice(g * gw, (g + 1) * gw)
            s = jnp.dot(_gm_mix_weights(ws_ref, g), vnb[:, cols], preferred_element_type=F32) + bs_ref[:, g:g + 1]
            g_ref[:, cols] = (ug[:, cols] * s * sz[:, cols]).astype(g_ref.dtype)

    return pl.pallas_call(
        body, name=name, grid=(T // GM_CHUNK,),
        in_specs=[pl.BlockSpec((3, GM_CHUNK, W), lambda i: (0, i, 0)), pl.BlockSpec((1, W), lambda i: (0, 0)),
                  pl.BlockSpec((1, W), lambda i: (0, 0)),
                  pl.BlockSpec((GM_GROUPS, GM_CHUNK, GM_CHUNK), lambda i: (0, 0, 0)),
                  pl.BlockSpec((GM_CHUNK, GM_GROUPS), lambda i: (0, 0))],
        out_specs=pl.BlockSpec((GM_CHUNK, W), lambda i: (i, 0)),
        out_shape=S((T, W), BF16), compiler_params=_cparams(("parallel",)),
    )(p3, lng.reshape(1, W), lnb.reshape(1, W), ws, bs_t)


def _gm_bwd(p3, lng, lnb, ws, bs_t, dg, name):
    _, T, W = p3.shape
    gw = W // GM_GROUPS

    def body(p_ref, lng_ref, lnb_ref, ws_ref, bs_ref, dg_ref, dp_ref, dlng_ref, dlnb_ref, dws_ref, dbs_ref, dvn_s):
        i = pl.program_id(0)

        @pl.when(i == 0)
        def _():
            dlng_ref[...] = jnp.zeros_like(dlng_ref)
            dlnb_ref[...] = jnp.zeros_like(dlnb_ref)
            dws_ref[...] = jnp.zeros_like(dws_ref)
            dbs_ref[...] = jnp.zeros_like(dbs_ref)

        ug, xhat, rstd, vn = _gm_common(p_ref, lng_ref, lnb_ref)
        z = p_ref[2]
        dg_ = dg_ref[...]
        dy = dg_ * _silu(z)
        vnb = vn.astype(BF16)
        row = lax.broadcasted_iota(jnp.int32, (GM_CHUNK, GM_CHUNK), 0)
        col = lax.broadcasted_iota(jnp.int32, (GM_CHUNK, GM_CHUNK), 1)
        dbs = jnp.zeros((GM_CHUNK, LANES), F32)
        for g in range(GM_GROUPS):
            cols = slice(g * gw, (g + 1) * gw)
            wm = _gm_mix_weights(ws_ref, g)
            s = jnp.dot(wm, vnb[:, cols], preferred_element_type=F32) + bs_ref[:, g:g + 1]
            dp_ref[2, :, cols] = (dg_[:, cols] * ug[:, cols] * s * _dsilu(z[:, cols])).astype(dp_ref.dtype)
            dp_ref[0, :, cols] = (dy[:, cols] * s * _dgelu(p_ref[0, :, cols])).astype(dp_ref.dtype)
            ds = dy[:, cols] * ug[:, cols]
            dsb = ds.astype(BF16)
            dwm = lax.dot_general(dsb, vnb[:, cols], (((1,), (1,)), ((), ())), preferred_element_type=F32)
            dws_ref[g] += jnp.where(col <= row, dwm, 0.0)
            dbs = dbs + jnp.where(col == g, jnp.sum(ds, axis=-1, keepdims=True), 0.0)
            dvn_s[:, cols] = lax.dot_general(wm, dsb, (((0,), (0,)), ((), ())), preferred_element_type=F32)
        dbs_ref[...] += dbs
        dvn = dvn_s[...]
        dlng_ref[...] += jnp.sum(dvn * xhat, axis=0, keepdims=True)
        dlnb_ref[...] += jnp.sum(dvn, axis=0, keepdims=True)
        dxh = dvn * lng_ref[...]
        dvg = rstd * (dxh - jnp.mean(dxh, axis=-1, keepdims=True) - xhat * jnp.mean(dxh * xhat, axis=-1, keepdims=True))
        dp_ref[1] = (dvg * _dgelu(p_ref[1])).astype(dp_ref.dtype)

    row1 = pl.BlockSpec((1, W), lambda i: (0, 0))
    return pl.pallas_call(
        body, name=name, grid=(T // GM_CHUNK,),
        in_specs=[pl.BlockSpec((3, GM_CHUNK, W), lambda i: (0, i, 0)), row1, row1,
                  pl.BlockSpec((GM_GROUPS, GM_CHUNK, GM_CHUNK), lambda i: (0, 0, 0)),
                  pl.BlockSpec((GM_CHUNK, GM_GROUPS), lambda i: (0, 0)),
                  pl.BlockSpec((GM_CHUNK, W), lambda i: (i, 0))],
        out_specs=[pl.BlockSpec((3, GM_CHUNK, W), lambda i: (0, i, 0)), row1, row1,
                   pl.BlockSpec((GM_GROUPS, GM_CHUNK, GM_CHUNK), lambda i: (0, 0, 0)),
                   pl.BlockSpec((GM_CHUNK, LANES), lambda i: (0, 0))],
        out_shape=[S((3, T, W), BF16), S((1, W), F32), S((1, W), F32),
                   S((GM_GROUPS, GM_CHUNK, GM_CHUNK), F32), S((GM_CHUNK, LANES), F32)],
        scratch_shapes=[pltpu.VMEM((GM_CHUNK, W), F32)],
        compiler_params=_cparams(("arbitrary",)),
    )(p3, lng.reshape(1, W), lnb.reshape(1, W), ws, bs_t, dg)


def _cf_conv_fwd(p3, wdw, bdw, name):
    _, T, W = p3.shape
    width = wdw.shape[0]
    cw = min(CONV_COLS, W)
    lead = 32

    def body(p_ref, w_ref, b_ref, y_ref, pad):
        pad[0:lead, :] = jnp.zeros((lead, cw), F32)
        for r0, rc in _conv_chunks(T):
            rows = pl.ds(r0, rc)
            pad[pl.ds(lead + r0, rc), :] = p_ref[0, rows, :] * jax.nn.sigmoid(p_ref[1, rows, :])
        for r0, rc in _conv_chunks(T):
            y_ref[pl.ds(r0, rc), :] = _causal_conv(pad, lead, w_ref, width, r0, rc) + b_ref[...]

    return pl.pallas_call(
        body, name=name, grid=(W // cw,),
        in_specs=[pl.BlockSpec((2, T, cw), lambda j: (0, 0, j)), pl.BlockSpec((width, cw), lambda j: (0, j)),
                  pl.BlockSpec((1, cw), lambda j: (0, j))],
        out_specs=pl.BlockSpec((T, cw), lambda j: (0, j)),
        out_shape=S((T, W), F32),
        scratch_shapes=[pltpu.VMEM((T + lead, cw), F32)],
        compiler_params=_cparams(("parallel",)),
    )(p3, wdw, bdw.reshape(1, W))


def _cf_ln(y1_ref, lng_ref, lnb_ref):
    y1 = y1_ref[...]
    mu = jnp.mean(y1, axis=-1, keepdims=True)
    xc = y1 - mu
    rstd = lax.rsqrt(jnp.mean(xc * xc, axis=-1, keepdims=True) + LN_EPS)
    xhat = xc * rstd
    return xhat, rstd, xhat * lng_ref[...] + lnb_ref[...]


def _cf_gate_fwd(y1, p3, lng, lnb, name):
    T, W = y1.shape
    tr = _pick(T, 256)

    def body(y1_ref, z_ref, lng_ref, lnb_ref, g_ref):
        _, _, y2 = _cf_ln(y1_ref, lng_ref, lnb_ref)
        g_ref[...] = (_silu(y2) * _silu(z_ref[...])).astype(g_ref.dtype)

    row1 = pl.BlockSpec((1, W), lambda i: (0, 0))
    return pl.pallas_call(
        body, name=name, grid=(T // tr,),
        in_specs=[pl.BlockSpec((tr, W), lambda i: (i, 0)), pl.BlockSpec((None, tr, W), lambda i: (2, i, 0)), row1, row1],
        out_specs=pl.BlockSpec((tr, W), lambda i: (i, 0)),
        out_shape=S((T, W), BF16), compiler_params=_cparams(("parallel",)),
    )(y1, p3, lng.reshape(1, W), lnb.reshape(1, W))


def _cf_gate_bwd(y1, p3, lng, lnb, dg, name):
    T, W = y1.shape
    tr = _pick(T, 128)

    def body(y1_ref, z_ref, lng_ref, lnb_ref, dg_ref, dz_ref, dy1_ref, dlng_ref, dlnb_ref):
        i = pl.program_id(0)

        @pl.when(i == 0)
        def _():
            dlng_ref[...] = jnp.zeros_like(dlng_ref)
            dlnb_ref[...] = jnp.zeros_like(dlnb_ref)

        xhat, rstd, y2 = _cf_ln(y1_ref, lng_ref, lnb_ref)
        z, dg_ = z_ref[...], dg_ref[...]
        dz_ref[...] = (dg_ * _silu(y2) * _dsilu(z)).astype(dz_ref.dtype)
        dy2 = dg_ * _silu(z) * _dsilu(y2)
        dlng_ref[...] += jnp.sum(dy2 * xhat, axis=0, keepdims=True)
        dlnb_ref[...] += jnp.sum(dy2, axis=0, keepdims=True)
        dxh = dy2 * lng_ref[...]
        dy1_ref[...] = rstd * (dxh - jnp.mean(dxh, axis=-1, keepdims=True)
                               - xhat * jnp.mean(dxh * xhat, axis=-1, keepdims=True))

    row1 = pl.BlockSpec((1, W), lambda i: (0, 0))
    blk = pl.BlockSpec((tr, W), lambda i: (i, 0))
    return pl.pallas_call(
        body, name=name, grid=(T // tr,),
        in_specs=[blk, pl.BlockSpec((None, tr, W), lambda i: (2, i, 0)), row1, row1, blk],
        out_specs=[blk, blk, row1, row1],
        out_shape=[S((T, W), BF16), S((T, W), F32), S((1, W), F32), S((1, W), F32)],
        compiler_params=_cparams(("arbitrary",)),
    )(y1, p3, lng.reshape(1, W), lnb.reshape(1, W), dg)


def _cf_conv_bwd(p3, wdw, dy1, dz, name):
    _, T, W = p3.shape
    width = wdw.shape[0]
    cw = min(CONV_COLS, W)
    lead = 32

    def body(p_ref, w_ref, dy1_ref, dz_ref, dp_ref, dw_ref, db_ref, y0pad, dpad):
        y0pad[0:lead, :] = jnp.zeros((lead, cw), F32)
        dpad[pl.ds(T, lead), :] = jnp.zeros((lead, cw), F32)
        bsum = None
        for r0, rc in _conv_chunks(T):
            rows = pl.ds(r0, rc)
            y0pad[pl.ds(lead + r0, rc), :] = p_ref[0, rows, :] * jax.nn.sigmoid(p_ref[1, rows, :])
            d = dy1_ref[rows, :]
            dpad[rows, :] = d
            part = jnp.sum(d, axis=0, keepdims=True)
            bsum = part if bsum is None else bsum + part
        db_ref[...] = bsum
        for r0, rc in _conv_chunks(T):
            rows = pl.ds(r0, rc)
            dy0 = _anticausal_conv(dpad, w_ref, width, r0, rc)
            a = p_ref[0, rows, :]
            sg = jax.nn.sigmoid(p_ref[1, rows, :])
            dp_ref[0, rows, :] = (dy0 * sg).astype(dp_ref.dtype)
            dp_ref[1, rows, :] = (dy0 * a * sg * (1.0 - sg)).astype(dp_ref.dtype)
            dp_ref[2, rows, :] = dz_ref[rows, :]
        _conv_wgrad(dpad, y0pad, lead, width, T, dw_ref)

    return pl.pallas_call(
        body, name=name, grid=(W // cw,),
        in_specs=[pl.BlockSpec((2, T, cw), lambda j: (0, 0, j)), pl.BlockSpec((width, cw), lambda j: (0, j)),
                  pl.BlockSpec((T, cw), lambda j: (0, j)), pl.BlockSpec((T, cw), lambda j: (0, j))],
        out_specs=[pl.BlockSpec((3, T, cw), lambda j: (0, 0, j)), pl.BlockSpec((width, cw), lambda j: (0, j)),
                   pl.BlockSpec((1, cw), lambda j: (0, j))],
        out_shape=[S((3, T, W), BF16), S((width, W), F32), S((1, W), F32)],
        scratch_shapes=[pltpu.VMEM((T + lead, cw), F32), pltpu.VMEM((T + lead, cw), F32)],
        compiler_params=_cparams(("parallel",)),
    )(p3, wdw, dy1, dz)


def _rows_call(body, ins, out_dtypes, name, row_pref=256):
    R, C = ins[0].shape
    tr = _pick(R, row_pref) if R % 8 == 0 else R
    while tr > 8 and tr * C * 4 * (len(ins) + len(out_dtypes)) * 2 > VMEM_LIMIT // 2 and tr % 16 == 0:
        tr //= 2
    blk = pl.BlockSpec((tr, C), lambda i: (i, 0))
    return pl.pallas_call(
        body, name=name, grid=(R // tr,), in_specs=[blk] * len(ins), out_specs=[blk] * len(out_dtypes),
        out_shape=[S((R, C), dt) for dt in out_dtypes], compiler_params=_cparams(("parallel",)),
    )(*ins)


def _pair_sum(g, r, core, name):
    J, K, n = g.shape
    kh = K // 2
    tr = _pick(kh, 256)
    nb = kh // tr

    def body(core_ref, g_ref, r_ref, o_ref):
        o_ref[...] = (g_ref[...] + r_ref[...]).astype(BF16)

    return pl.pallas_call(
        body, name=name,
        grid_spec=pltpu.PrefetchScalarGridSpec(
            num_scalar_prefetch=1, grid=(J, nb),
            in_specs=[pl.BlockSpec((None, tr, n), lambda j, i, core_ref: (j, core_ref[0] * nb + i, 0)),
                      pl.BlockSpec((None, tr, n), lambda j, i, core_ref: (j, i, 0))],
            out_specs=pl.BlockSpec((None, tr, n), lambda j, i, core_ref: (j, i, 0))),
        out_shape=S((J, kh, n), BF16),
        compiler_params=_cparams(("parallel", "parallel")),
    )(core, g, r)


def _chip_sum(rc, core, name):
    J, R, C = rc.shape
    tr = _pick(R, 256)
    nb = R // tr

    def body(core_ref, r_ref, o_ref):
        acc = r_ref[0].astype(F32)
        for j in range(1, J):
            acc = acc + r_ref[j].astype(F32)
        o_ref[...] = acc

    return pl.pallas_call(
        body, name=name,
        grid_spec=pltpu.PrefetchScalarGridSpec(
            num_scalar_prefetch=1, grid=(nb,),
            in_specs=[pl.BlockSpec((J, tr, C), lambda i, core_ref: (0, i, 0))],
            out_specs=pl.BlockSpec((tr, C), lambda i, core_ref: (core_ref[0] * nb + i, 0))),
        out_shape=S((2 * R, C), F32), compiler_params=_cparams(("parallel",)),
    )(core, rc)


def _slot_sum(slots, name):
    J, R, C = slots.shape
    tr = _pick(R, 512)

    def body(r_ref, o_ref):
        acc = r_ref[0]
        for j in range(1, J):
            acc = acc + r_ref[j]
        o_ref[...] = acc

    return pl.pallas_call(
        body, name=name, grid=(R // tr,),
        in_specs=[pl.BlockSpec((J, tr, C), lambda i: (0, i, 0))], out_specs=pl.BlockSpec((tr, C), lambda i: (i, 0)),
        out_shape=S((R, C), F32), compiler_params=_cparams(("parallel",)),
    )(slots)


def _adamw(w, g, m, v, name):
    def body(w_ref, g_ref, m_ref, v_ref, d_ref, nm_ref, nv_ref):
        g_ = g_ref[...]
        nm = ADAM_B1 * m_ref[...] + (1.0 - ADAM_B1) * g_
        nv = ADAM_B2 * v_ref[...] + (1.0 - ADAM_B2) * (g_ * g_)
        m_hat = nm / (1.0 - ADAM_B1 ** ADAM_STEP)
        v_hat = nv / (1.0 - ADAM_B2 ** ADAM_STEP)
        d_ref[...] = -ADAM_LR * (m_hat / (jnp.sqrt(v_hat) + ADAM_EPS) + ADAM_WD * w_ref[...])
        nm_ref[...] = nm
        nv_ref[...] = nv
    return _rows_call(body, [w, g, m, v], [F32, F32, F32], name)


ANY = pl.BlockSpec(memory_space=pl.ANY)


def _place():
    x, y, c = lax.axis_index("x"), lax.axis_index("y"), lax.axis_index("c")
    return x, y, c


def _other_chips(x, y):
    return [(1 - x, y), (x, 1 - y), (1 - x, 1 - y)]


HBM = pl.BlockSpec(memory_space=pltpu.HBM)
SEM = pl.BlockSpec(memory_space=pltpu.SEMAPHORE)
DATAFLOW = pltpu.SideEffectType.DATAFLOW_SIDE_EFFECTING


def _in_hbm(a):
    return pltpu.with_memory_space_constraint(a, pltpu.HBM)


def _half_rows(k_rows, which):
    return pl.ds(which * (k_rows // 2), k_rows // 2)


COPIES_PER_ARRAY = {"gather": 3, "scatter": 3, "halves": 1}


def _split_copies(srcs, lands, send_sems, recv_sems, mode):
    x, y, c = _place()
    me_chip = 2 * x + y
    out = []
    for w, (src, land) in enumerate(zip(srcs, lands)):
        if mode == "halves":
            rows = _half_rows(src.shape[1], 1 - c)
            cp = pltpu.make_async_remote_copy(src_ref=src.at[:, rows, :], dst_ref=land, send_sem=send_sems.at[w],
                                              recv_sem=recv_sems.at[w], device_id=(x, y, 1 - c), device_id_type=MESH_ID)
            out.append((cp, cp))
            continue
        for j, (cx, cy) in enumerate(_other_chips(x, y)):
            them = 2 * cx + cy
            sems = dict(send_sem=send_sems.at[3 * w + j], recv_sem=recv_sems.at[3 * w + j],
                        device_id=(cx, cy, c), device_id_type=MESH_ID)
            if mode == "gather":
                rows = _half_rows(src.shape[0], c)
                go = pltpu.make_async_remote_copy(src_ref=src.at[rows], dst_ref=land.at[me_chip, rows], **sems)
                arrive = pltpu.make_async_remote_copy(src_ref=src.at[rows], dst_ref=land.at[them, rows], **sems)
            else:
                go = pltpu.make_async_remote_copy(src_ref=src.at[them], dst_ref=land.at[me_chip], **sems)
                arrive = pltpu.make_async_remote_copy(src_ref=src.at[them], dst_ref=land.at[them], **sems)
            out.append((go, arrive))
    return out


def _split_copies_start(srcs, lands, mode, name, after=None):
    nw = len(srcs)
    n_sems = COPIES_PER_ARRAY[mode] * nw
    n_in = 2 * nw + (after is not None)

    def body(*refs):
        ins, lnd = refs[:nw], refs[nw:2 * nw]
        send_sems, recv_sems, token = refs[n_in], refs[n_in + 1], refs[-1]
        for go, _ in _split_copies(ins, lnd, send_sems, recv_sems, mode):
            go.start()
        token[...] = jnp.zeros_like(token)

    thru = [pltpu.HBM(a.shape, a.dtype) for a in list(srcs) + list(lands)]
    res = pl.pallas_call(
        body, name=name, in_specs=[HBM] * (2 * nw) + [ANY] * (after is not None),
        out_specs=[SEM, SEM] + [HBM] * (2 * nw) + [pl.BlockSpec(memory_space=pltpu.VMEM)],
        out_shape=[pltpu.SemaphoreType.DMA((n_sems,)), pltpu.SemaphoreType.DMA((n_sems,))] + thru + [S((8, LANES), F32)],
        input_output_aliases={i: 2 + i for i in range(2 * nw)},
        compiler_params=pltpu.CompilerParams(has_side_effects=DATAFLOW),
    )(*[_in_hbm(a) for a in list(srcs) + list(lands)], *([after] if after is not None else []))
    return res[0], res[1], list(res[2:2 + nw]), list(res[2 + nw:2 + 2 * nw]), res[-1]


def _split_copies_wait(send_sems, recv_sems, srcs, lands, after, mode, name):
    nw = len(srcs)

    def body(*refs):
        ins, lnd = refs[:nw], refs[nw:2 * nw]
        send, recv = refs[2 * nw], refs[2 * nw + 1]
        for _, arrive in _split_copies(ins, lnd, send, recv, mode):
            arrive.wait_send()
            arrive.wait_recv()

    res = pl.pallas_call(
        body, name=name, in_specs=[HBM] * (2 * nw) + [SEM, SEM, ANY], out_specs=[HBM] * (2 * nw),
        out_shape=[pltpu.HBM(a.shape, a.dtype) for a in list(srcs) + list(lands)],
        input_output_aliases={i: i for i in range(2 * nw)},
        compiler_params=pltpu.CompilerParams(has_side_effects=DATAFLOW),
    )(*srcs, *lands, send_sems, recv_sems, after)
    return list(res[:nw]), list(res[nw:])


def _gather_forward(lands, name):
    nw = len(lands)

    def body(*refs):
        ins, outs = refs[:nw], refs[nw:2 * nw]
        send_sems, recv_sems = refs[2 * nw:]
        x, y, c = _place()
        sibling = (x, y, 1 - c)
        cps = []
        for w in range(nw):
            kr = ins[w].shape[1]
            for j, (cx, cy) in enumerate(_other_chips(x, y)):
                them = 2 * cx + cy
                sems = dict(send_sem=send_sems.at[3 * w + j], recv_sem=recv_sems.at[3 * w + j],
                            device_id=sibling, device_id_type=MESH_ID)
                mine, theirs = _half_rows(kr, c), _half_rows(kr, 1 - c)
                go = pltpu.make_async_remote_copy(src_ref=ins[w].at[them, mine], dst_ref=outs[w].at[them, mine], **sems)
                go.start()
                cps.append((go, pltpu.make_async_remote_copy(
                    src_ref=ins[w].at[them, theirs], dst_ref=outs[w].at[them, theirs], **sems)))
        for go, arrive in cps:
            arrive.wait_recv()
            go.wait_send()

    return pl.pallas_call(
        body, name=name, in_specs=[ANY] * nw, out_specs=[ANY] * nw,
        out_shape=[S(a.shape, a.dtype) for a in lands],
        scratch_shapes=[pltpu.SemaphoreType.DMA((3 * nw,)), pltpu.SemaphoreType.DMA((3 * nw,))],
        input_output_aliases={i: i for i in range(nw)},
        compiler_params=pltpu.CompilerParams(has_side_effects=True),
    )(*lands)


def _core_join_halves(halves, name):
    nw = len(halves)

    def body(*refs):
        ins, outs = refs[:nw], refs[nw:2 * nw]
        send_sems, recv_sems = refs[2 * nw:]
        x, y, c = _place()
        sibling = (x, y, 1 - c)
        rem = []
        for w in range(nw):
            r = ins[w].shape[0] // 2
            mine = pl.ds(c * r, r)
            cp = pltpu.make_async_remote_copy(
                src_ref=ins[w].at[mine], dst_ref=outs[w].at[mine], send_sem=send_sems.at[w], recv_sem=recv_sems.at[w],
                device_id=sibling, device_id_type=MESH_ID)
            cp.start()
            rem.append(cp)
        for w in range(nw):
            r = ins[w].shape[0] // 2
            theirs = outs[w].at[pl.ds((1 - c) * r, r)]
            pltpu.make_async_remote_copy(
                src_ref=theirs, dst_ref=theirs, send_sem=send_sems.at[w], recv_sem=recv_sems.at[w],
                device_id=sibling, device_id_type=MESH_ID).wait_recv()
        for cp in rem:
            cp.wait_send()

    return pl.pallas_call(
        body, name=name, in_specs=[ANY] * nw, out_specs=[ANY] * nw,
        out_shape=[S(h.shape, h.dtype) for h in halves],
        scratch_shapes=[pltpu.SemaphoreType.DMA((nw,)), pltpu.SemaphoreType.DMA((nw,))],
        input_output_aliases={i: i for i in range(nw)},
        compiler_params=pltpu.CompilerParams(has_side_effects=True),
    )(*halves)


def _broadcast_all(buf, name):
    def body(in_ref, out_ref, send_sems, recv_sems, loc_sem):
        x, y, c = _place()
        me = 4 * x + 2 * y + c
        loc = pltpu.make_async_copy(in_ref, out_ref.at[me], loc_sem)
        loc.start()
        cps = []
        for k in range(1, N_DEV):
            fx, fy, fc = (k >> 2) & 1, (k >> 1) & 1, k & 1
            px, py, pc = x ^ fx, y ^ fy, c ^ fc
            cp = pltpu.make_async_remote_copy(
                src_ref=in_ref, dst_ref=out_ref.at[me], send_sem=send_sems.at[k - 1], recv_sem=recv_sems.at[k - 1],
                device_id=(px, py, pc), device_id_type=MESH_ID)
            cp.start()
            cps.append(cp)
        for k in range(1, N_DEV):
            fx, fy, fc = (k >> 2) & 1, (k >> 1) & 1, k & 1
            px, py, pc = x ^ fx, y ^ fy, c ^ fc
            slot = out_ref.at[4 * px + 2 * py + pc]
            pltpu.make_async_remote_copy(
                src_ref=slot, dst_ref=slot, send_sem=send_sems.at[k - 1], recv_sem=recv_sems.at[k - 1],
                device_id=(px, py, pc), device_id_type=MESH_ID).wait_recv()
        for cp in cps:
            cp.wait_send()
        loc.wait()

    return pl.pallas_call(
        body, name=name, in_specs=[ANY], out_specs=ANY,
        out_shape=S((N_DEV,) + buf.shape, buf.dtype),
        scratch_shapes=[pltpu.SemaphoreType.DMA((N_DEV - 1,)), pltpu.SemaphoreType.DMA((N_DEV - 1,)),
                        pltpu.SemaphoreType.DMA],
        compiler_params=pltpu.CompilerParams(has_side_effects=True),
    )(buf)


PACK_ALIGN = 8 * LANES


def _pack(arrs):
    flat = []
    for a in arrs:
        f = a.reshape(-1).astype(F32)
        pad = (-f.shape[0]) % PACK_ALIGN
        flat.append(jnp.pad(f, (0, pad)) if pad else f)
    return jnp.concatenate(flat).reshape(-1, LANES)


def _unpack(buf, shapes):
    out, off = [], 0
    flat = buf.reshape(-1)
    for shp in shapes:
        n = math.prod(shp)
        out.append(flat[off:off + n].reshape(shp))
        off += n + ((-n) % PACK_ALIGN)
    return out


def kernel(x, positions, norm_pre, norm_post, w_in_mla, mla_q_norm, w_uq, mla_kv_norm, w_ukv, w_out_mla, w_in_sc, sc_conv, w_out_sc, w_in_gm, gm_ln_g, gm_ln_b, gm_w_s, gm_b_s, w_out_gm, w_in_cf, cf_dw, cf_dw_b, cf_ln_g, cf_ln_b, w_out_cf, loss_target, m_norm_pre, m_norm_post, m_w_in_mla, m_mla_q_norm, m_w_uq, m_mla_kv_norm, m_w_ukv, m_w_out_mla, m_w_in_sc, m_sc_conv, m_w_out_sc, m_w_in_gm, m_gm_ln_g, m_gm_ln_b, m_gm_w_s, m_gm_b_s, m_w_out_gm, m_w_in_cf, m_cf_dw, m_cf_dw_b, m_cf_ln_g, m_cf_ln_b, m_w_out_cf, v_norm_pre, v_norm_post, v_w_in_mla, v_mla_q_norm, v_w_uq, v_mla_kv_norm, v_w_ukv, v_w_out_mla, v_w_in_sc, v_sc_conv, v_w_out_sc, v_w_in_gm, v_gm_ln_g, v_gm_ln_b, v_gm_w_s, v_gm_b_s, v_w_out_gm, v_w_in_cf, v_cf_dw, v_cf_dw_b, v_cf_ln_g, v_cf_ln_b, v_w_out_cf):
    loc = dict(locals())
    wts = {n: loc[n] for n in WEIGHTS}
    mom_m = {n: loc["m_" + n] for n in WEIGHTS}
    mom_v = {n: loc["v_" + n] for n in WEIGHTS}

    T, D = x.shape[1], x.shape[2]
    xin = x.reshape(T, D)
    target = loss_target.reshape(T, D)
    q_rank, kv_rank = mla_q_norm.shape[1], mla_kv_norm.shape[1]
    H = (w_uq.shape[2] * N_CHIPS) // (NOPE_DIM + ROPE_DIM)
    hv = H * V_DIM
    c_kr = q_rank + kv_rank
    wa_cols = c_kr + ROPE_DIM
    wa_pad = wa_cols + (LANES - ROPE_DIM)
    chip = 2 * lax.axis_index("x") + lax.axis_index("y")

    c = lax.axis_index("c")
    core = c.reshape(1).astype(jnp.int32)
    gather_started = []
    for gi, names in enumerate(GROUPS):
        own = [wts[n][0].astype(BF16) for n in names]
        lands = [lax.dynamic_update_slice(lax.empty((N_CHIPS,) + s.shape, BF16), s[None], (chip, 0, 0)) for s in own]
        gather_started.append(_split_copies_start(own, lands, "gather", f"gather_start_{gi}",
                                                  after=gather_started[-1][4] if gather_started else None))
    started_token = sum(st[4][0, 0] for st in gather_started)
    gw = {}

    def gathered_weights(gi, after):
        send_sems, recv_sems, own, lands, _ = gather_started[gi]
        _, landed = _split_copies_wait(send_sems, recv_sems, own, lands, after, "gather", f"gather_wait_{gi}")
        gw.update(zip(GROUPS[gi], _gather_forward(landed, f"gather_forward_{gi}")))

    small_sh_shapes = [wts[n][0].shape for n in SMALL_SHARDED]
    slots = _broadcast_all(_pack([wts[n][0] for n in SMALL_SHARDED]), "gather_small")
    per_chip = [_unpack(slots[2 * k], small_sh_shapes) for k in range(N_CHIPS)]
    sp = {n: jnp.concatenate([per_chip[k][i] for k in range(N_CHIPS)], axis=-1) for i, n in enumerate(SMALL_SHARDED)}

    def cols_major(w4):
        return jnp.transpose(w4, (1, 0, 2)).reshape(w4.shape[1], -1)

    half = ROPE_DIM // 2
    inv_freq = ROPE_THETA ** (-jnp.arange(half, dtype=F32) / half)
    invf = jnp.concatenate([inv_freq, inv_freq, jnp.zeros((LANES - ROPE_DIM,), F32)]).reshape(1, LANES)
    tabs = _rope_tables(positions.reshape(T, 1), invf, "rope_tables")
    scale = float(NOPE_DIM + ROPE_DIM) ** -0.5

    xs = [xin]
    saved = []
    w_out = {}
    for i in range(4):
        xi = xs[-1]
        gathered_weights(LAYER_GROUPS[i][0], xi)
        h = _rms_fwd(xi[None], 0, 0, D, norm_pre[i] + started_token if i == 0 else norm_pre[i], f"pre_norm_{i}", BF16)
        if i == 0:
            w_in_full = cols_major(gw['w_in_mla'])
            w_a = jnp.pad(w_in_full[:, :wa_cols], ((0, 0), (0, wa_pad - wa_cols)))[None]
            w_z = w_in_full[:, wa_cols:][None]
            pa = _mm_nn(h, w_a, 1, "mla_in_a")
            pz = _mm_nn(h, w_z, 1, "mla_in_z")
            gathered_weights(LAYER_GROUPS[i][1], pz)
            wq = cols_major(gw['w_uq']).reshape(q_rank, H, NOPE_DIM + ROPE_DIM)
            wq = jnp.pad(wq, ((0, 0), (0, 0), (0, HEAD_PAD - NOPE_DIM - ROPE_DIM))).reshape(1, q_rank, H * HEAD_PAD)
            wkv = cols_major(gw['w_ukv'])[None]
            qn = _rms_fwd(pa, 0, 0, q_rank, mla_q_norm[0], "mla_q_norm", BF16)
            kvn = _rms_fwd(pa, 0, q_rank // kv_rank, kv_rank, mla_kv_norm[0], "mla_kv_norm", BF16)
            q3 = _mm_nn(qn, wq, 1, "mla_q_up")
            kv3 = _mm_nn(kvn, wkv, 1, "mla_kv_up")
            qf, kf, vv = _qkv_layout(q3, kv3, pa, c_kr // LANES, tabs, H, "mla_qkv_layout")
            o, lse = _attn_fwd(qf, kf, vv, H, scale, "mla_attn_fwd")
            g = _gate_fwd(o, pz, "mla_gate_fwd")
            saved.append(dict(h=h, pa=pa, pz=pz, qn=qn, kvn=kvn, qf=qf, kf=kf, vv=vv, o=o, lse=lse, g=g))
        elif i == 1:
            p3 = _mm_nn(h, gw['w_in_sc'], 4, "sc_in")
            g = _sc_fwd(p3, sp['sc_conv'], "sc_mix_fwd")
            saved.append(dict(h=h, p3=p3, g=g))
        elif i == 2:
            p3 = _mm_nn(h, gw['w_in_gm'], 3, "gm_in")
            bs_t = jnp.transpose(gm_b_s[0])
            g = _gm_fwd(p3, sp['gm_ln_g'], sp['gm_ln_b'], gm_w_s[0], bs_t, "gm_mix_fwd")
            saved.append(dict(h=h, p3=p3, g=g, bs_t=bs_t))
        else:
            p3 = _mm_nn(h, gw['w_in_cf'], 3, "cf_in")
            y1 = _cf_conv_fwd(p3, sp['cf_dw'], sp['cf_dw_b'], "cf_conv_fwd")
            g = _cf_gate_fwd(y1, p3, sp['cf_ln_g'], sp['cf_ln_b'], "cf_gate_fwd")
            saved.append(dict(h=h, p3=p3, y1=y1, g=g))
        w_out[WO_NAMES[i]] = gw[WO_NAMES[i]].reshape(1, -1, D)
        yo = _mm_nn(g, w_out[WO_NAMES[i]], 1, f"out_proj_{i}")
        saved[-1]['yo'] = yo
        xs.append(_rms_fwd(yo, 0, 0, D, norm_post[i], f"post_norm_{i}", F32, res=xi))

    dx, loss_local = _loss_head(xs[4], target, "loss_head")
    loss = lax.psum(loss_local, ("x", "y", "c"))

    big_grads = {}
    sgrad = {}
    d_npre, d_npost = [None] * 4, [None] * 4
    exch_started, scatter_started, scattered = {}, {}, {}

    def start_exchange(gi):
        full = [big_grads[n] for n in GROUPS[gi]]
        lands = [lax.empty((g_.shape[0], g_.shape[1] // 2, g_.shape[2]), g_.dtype) for g_ in full]
        exch_started[gi] = _split_copies_start(full, lands, "halves", f"grads_exchange_start_{gi}")
        return exch_started[gi][4][0, 0]

    def start_scatter(gi, after):
        send_sems, recv_sems, full, lands, _ = exch_started[gi]
        full, recv = _split_copies_wait(send_sems, recv_sems, full, lands, after, "halves", f"grads_exchange_wait_{gi}")
        pair = [_pair_sum(g_, r, core, f"pair_sum_{n}") for n, g_, r in zip(GROUPS[gi], full, recv)]
        lands = [lax.dynamic_update_slice(lax.empty(p.shape, p.dtype), lax.dynamic_slice_in_dim(p, chip, 1, axis=0),
                                          (chip, 0, 0)) for p in pair]
        scatter_started[gi] = _split_copies_start(pair, lands, "scatter", f"scatter_start_{gi}")
        return scatter_started[gi][4][0, 0]

    def finish_scatter(gi, after):
        send_sems, recv_sems, pair, lands, _ = scatter_started[gi]
        scattered[gi] = _split_copies_wait(send_sems, recv_sems, pair, lands, after, "scatter", f"scatter_wait_{gi}")[1]

    token = 0.0
    for i in (3, 2, 1, 0):
        sv = saved[i]
        h = sv['h']
        dyo, d_npost[i] = _rms_bwd(sv['yo'], 0, 0, D, norm_post[i] + token, dx, f"post_norm_bwd_{i}", BF16)
        dyo3 = dyo[None]
        wo_name = WO_NAMES[i]
        dg = _mm_nt(dyo3, w_out[wo_name], f"out_proj_dx_{i}")
        big_grads[wo_name] = _mm_tn(sv['g'], dyo3, 1, f"out_proj_dw_{i}").reshape(N_CHIPS, -1, D)
        if i == 3:
            dz, dy1, sgrad['cf_ln_g'], sgrad['cf_ln_b'] = _cf_gate_bwd(sv['y1'], sv['p3'], sp['cf_ln_g'], sp['cf_ln_b'], dg, "cf_gate_bwd")
            dp3, sgrad['cf_dw'], sgrad['cf_dw_b'] = _cf_conv_bwd(sv['p3'], sp['cf_dw'], dy1, dz, "cf_conv_bwd")
            big_grads['w_in_cf'] = _mm_tn(h, dp3, N_CHIPS, "cf_in_dw")
            dh = _mm_nt(dp3, gw['w_in_cf'], "cf_in_dx")
        elif i == 2:
            dp3, sgrad['gm_ln_g'], sgrad['gm_ln_b'], sgrad['gm_w_s'], dbs_t = _gm_bwd(
                sv['p3'], sp['gm_ln_g'], sp['gm_ln_b'], gm_w_s[0], sv['bs_t'], dg, "gm_mix_bwd")
            sgrad['gm_b_s'] = jnp.transpose(dbs_t[:, :GM_GROUPS])
            big_grads['w_in_gm'] = _mm_tn(h, dp3, N_CHIPS, "gm_in_dw")
            dh = _mm_nt(dp3, gw['w_in_gm'], "gm_in_dx")
        elif i == 1:
            dp3, sgrad['sc_conv'] = _sc_bwd(sv['p3'], sp['sc_conv'], dg, "sc_mix_bwd")
            big_grads['w_in_sc'] = _mm_tn(h, dp3, N_CHIPS, "sc_in_dw")
            dh = _mm_nt(dp3, gw['w_in_sc'], "sc_in_dx")
        else:
            do, dpz = _gate_bwd(dg, sv['o'], sv['pz'], "mla_gate_bwd")
            dqf, dkf, dv = _attn_bwd(sv['qf'], sv['kf'], sv['vv'], do, sv['o'], sv['lse'], H, scale, "mla_attn_bwd")
            finish_scatter(3, dqf)
            token = start_scatter(2, dqf)
            dq3, dkv3, dkr = _qkv_layout_bwd(dqf, dkf, dv, (tabs[0] + token, tabs[1], tabs[2]), H, "mla_qkv_layout_bwd")
            dqn = _mm_nt(dq3, wq, "mla_q_up_dx")
            dwq = _mm_tn(sv['qn'], dq3, 1, "mla_q_up_dw")
            dkvn = _mm_nt(dkv3, wkv, "mla_kv_up_dx")
            dwkv = _mm_tn(sv['kvn'], dkv3, 1, "mla_kv_up_dw")
            dwq_ = dwq[0].reshape(q_rank, H, HEAD_PAD)[:, :, :NOPE_DIM + ROPE_DIM].reshape(q_rank, N_CHIPS, -1)
            big_grads['w_uq'] = jnp.transpose(dwq_, (1, 0, 2))
            big_grads['w_ukv'] = jnp.transpose(dwkv[0].reshape(kv_rank, N_CHIPS, -1), (1, 0, 2))
            token = start_exchange(1)
            dcq, dqg = _rms_bwd(sv['pa'], 0, 0, q_rank, mla_q_norm[0] + token, dqn, "mla_q_norm_bwd", BF16)
            dckv, dkvg = _rms_bwd(sv['pa'], 0, q_rank // kv_rank, kv_rank, mla_kv_norm[0], dkvn, "mla_kv_norm_bwd", BF16)
            sgrad['mla_q_norm'], sgrad['mla_kv_norm'] = dqg, dkvg
            dpa = jnp.concatenate([dcq, dckv, dkr], axis=1)[None]
            dwa = _mm_tn(h, dpa, 1, "mla_in_a_dw")
            dwz = _mm_tn(h, dpz, 1, "mla_in_z_dw")
            dh_a = _mm_nt(dpa, w_a, "mla_in_a_dx")
            dh = _mm_nt(dpz, w_z, "mla_in_z_dx", add=dh_a)
            dw_in = jnp.concatenate([dwa[0][:, :wa_cols], dwz[0]], axis=1)
            big_grads['w_in_mla'] = jnp.transpose(dw_in.reshape(D, N_CHIPS, -1), (1, 0, 2))
            token = start_scatter(1, dh) + start_exchange(0)
        dx, d_npre[i] = _rms_bwd(xs[i][None], 0, 0, D, norm_pre[i] + token if i == 0 else norm_pre[i], dh,
                                 f"pre_norm_bwd_{i}", F32, res=dx)
        if i == 3:
            token = start_exchange(4)
        elif i == 2:
            token = start_scatter(4, dx) + start_exchange(3)
        elif i == 1:
            finish_scatter(4, dx)
            token = start_scatter(3, dx) + start_exchange(2)
    grad_x = dx.reshape(1, T, D)
    sgrad['norm_pre'] = jnp.concatenate(d_npre, axis=0)
    sgrad['norm_post'] = jnp.concatenate(d_npost, axis=0)

    grads, delta, new_m, new_v = {}, {}, {}, {}

    def update_group(gi):
        halves = [_chip_sum(r, core, f"chip_sum_{n}") for n, r in zip(GROUPS[gi], scattered[gi])]
        for n, j in zip(GROUPS[gi], _core_join_halves(halves, f"grads_core_join_{gi}")):
            shp = wts[n].shape
            two_d = (shp[1], shp[2])
            d_, m_, v_ = _adamw(wts[n].reshape(two_d), j, mom_m[n].reshape(two_d), mom_v[n].reshape(two_d), f"adamw_{n}")
            grads[n], delta[n], new_m[n], new_v[n] = j[None], d_.reshape(shp), m_.reshape(shp), v_.reshape(shp)
        return d_

    finish_scatter(2, dx)
    start_scatter(0, dx)
    for gi in (4, 3, 2):
        done = update_group(gi)
    finish_scatter(1, done)
    finish_scatter(0, done)
    update_group(1)
    update_group(0)

    small_full_shapes = [sgrad[n].reshape(wts[n].shape[:-1] + (-1,)).shape for n in SMALL]
    gslots = _broadcast_all(_pack([sgrad[n] for n in SMALL]), "grads_small_exchange")
    gsum = _unpack(_slot_sum(gslots, "grads_small_sum"), small_full_shapes)
    for n, gs in zip(SMALL, gsum):
        if n in SMALL_SHARDED:
            per = wts[n].shape[-1]
            gs = lax.dynamic_slice_in_dim(gs, chip * per, per, axis=gs.ndim - 1)
        grads[n] = gs.reshape(wts[n].shape)

    shapes = [wts[n].shape for n in SMALL]
    d_, m_, v_ = _adamw(_pack([wts[n] for n in SMALL]), _pack([grads[n] for n in SMALL]),
                        _pack([mom_m[n] for n in SMALL]), _pack([mom_v[n] for n in SMALL]), "adamw_small")
    for n, a, b, cc in zip(SMALL, _unpack(d_, shapes), _unpack(m_, shapes), _unpack(v_, shapes)):
        delta[n], new_m[n], new_v[n] = a, b, cc

    return (loss, grad_x, *[grads[n] for n in WEIGHTS], *[delta[n] for n in WEIGHTS],
            *[new_m[n] for n in WEIGHTS], *[new_v[n] for n in WEIGHTS])
```

```python
import functools
import math

import jax
import jax.numpy as jnp
from jax import lax
from jax.experimental import pallas as pl
from jax.experimental.pallas import tpu as pltpu

F32, BF16 = jnp.float32, jnp.bfloat16
S = jax.ShapeDtypeStruct
MESH_ID = pl.DeviceIdType.MESH

V7X_VMEM_BYTES = 64 * 1024 * 1024
VMEM_LIMIT = V7X_VMEM_BYTES - 8 * 1024 * 1024
LANES = 128
N_CHIPS = 4
N_DEV = 8

NORM_EPS = 1e-6
LN_EPS = 1e-5
ROPE_THETA = 10000.0
ROPE_DIM = 64
NOPE_DIM = 128
V_DIM = 128
HEAD_PAD = 256
GM_CHUNK = 128
GM_GROUPS = 8
NEG = -1e30

ADAM_LR, ADAM_B1, ADAM_B2, ADAM_EPS, ADAM_WD, ADAM_STEP = 0.001, 0.9, 0.999, 1e-08, 0.01, 10

FWD_PARAMS = ['x', 'positions', 'norm_pre', 'norm_post', 'w_in_mla', 'mla_q_norm', 'w_uq', 'mla_kv_norm', 'w_ukv',
              'w_out_mla', 'w_in_sc', 'sc_conv', 'w_out_sc', 'w_in_gm', 'gm_ln_g', 'gm_ln_b', 'gm_w_s', 'gm_b_s',
              'w_out_gm', 'w_in_cf', 'cf_dw', 'cf_dw_b', 'cf_ln_g', 'cf_ln_b', 'w_out_cf']
WEIGHTS = FWD_PARAMS[2:]
BIG = ['w_in_mla', 'w_uq', 'w_ukv', 'w_out_mla', 'w_in_sc', 'w_out_sc', 'w_in_gm', 'w_out_gm', 'w_in_cf', 'w_out_cf']
GROUPS = [['w_in_mla'], ['w_uq', 'w_ukv', 'w_out_mla'], ['w_in_sc', 'w_out_sc'], ['w_in_gm', 'w_out_gm'],
          ['w_in_cf', 'w_out_cf']]
LAYER_GROUPS = [[0, 1], [2], [3], [4]]
WO_NAMES = ['w_out_mla', 'w_out_sc', 'w_out_gm', 'w_out_cf']
SMALL = [n for n in WEIGHTS if n not in BIG]
SMALL_SHARDED = ['sc_conv', 'gm_ln_g', 'gm_ln_b', 'cf_dw', 'cf_dw_b', 'cf_ln_g', 'cf_ln_b']


def _cparams(sem=None, **kw):
    return pltpu.CompilerParams(dimension_semantics=sem, vmem_limit_bytes=VMEM_LIMIT, **kw)


def _pick(dim, pref):
    if dim <= pref:
        return dim
    t = (pref // LANES) * LANES
    while t >= LANES and dim % t:
        t -= LANES
    if t >= min(pref, 512):
        return t
    return dim if (dim <= 2048 or t < LANES) else t


def _silu(x):
    return x * jax.nn.sigmoid(x)


def _dsilu(x):
    s = jax.nn.sigmoid(x)
    return s * (1.0 + x * (1.0 - s))


def _gelu(x):
    return 0.5 * x * (1.0 + lax.erf(x * (2.0 ** -0.5)))


def _dgelu(x):
    cdf = 0.5 * (1.0 + lax.erf(x * (2.0 ** -0.5)))
    return cdf + x * jnp.exp(-0.5 * x * x) * ((2.0 * math.pi) ** -0.5)


MM_ONE_DOT = 4096


def _contract_tile(dim, divisible_by, pref_when_split):
    return dim if dim <= MM_ONE_DOT and divisible_by % dim == 0 else _pick(divisible_by, pref_when_split)


def _mm_accumulate(step, nsteps, prod, o_ref, acc, init=None):
    if nsteps == 1:
        r = prod()
        if init is not None:
            r = r + init()
        o_ref[...] = r.astype(o_ref.dtype)
        return

    @pl.when(step == 0)
    def _():
        acc[...] = jnp.zeros_like(acc) if init is None else init()

    acc[...] += prod()

    @pl.when(step == nsteps - 1)
    def _():
        o_ref[...] = acc[...].astype(o_ref.dtype)


def _mm_nn(a, w, np_out, name, out_dtype=F32):
    M, K = a.shape
    J, _, n = w.shape
    N = J * n
    W = N // np_out
    tm, tn = _pick(M, 1024), _pick(math.gcd(W, n), 1024)
    tk = _contract_tile(K, K, 2048)
    nk = K // tk

    def body(*refs):
        a_ref, w_ref, o_ref = refs[:3]
        _mm_accumulate(pl.program_id(2), nk, lambda: jnp.dot(a_ref[...], w_ref[...], preferred_element_type=F32),
                       o_ref, refs[-1])

    return pl.pallas_call(
        body, name=name, grid=(M // tm, N // tn, nk),
        in_specs=[pl.BlockSpec((tm, tk), lambda i, j, k: (i, k)),
                  pl.BlockSpec((None, tk, tn), lambda i, j, k: (j // (n // tn), k, j % (n // tn)))],
        out_specs=pl.BlockSpec((None, tm, tn), lambda i, j, k: (j // (W // tn), i, j % (W // tn))),
        out_shape=S((np_out, M, W), out_dtype),
        scratch_shapes=[pltpu.VMEM((tm, tn), F32)] if nk > 1 else [],
        compiler_params=_cparams(("parallel", "parallel", "arbitrary")),
    )(a, w)


def _mm_nt(a3, w, name, add=None, out_dtype=F32):
    NP, M, W = a3.shape
    J, K, n = w.shape
    N = NP * W
    tm, to = _pick(M, 1024), _pick(K, 1024)
    sub = _contract_tile(N, math.gcd(W, n), 2048)
    r = max(q for q in range(1, MM_ONE_DOT // sub + 1) if n % (q * sub) == 0 and N % (q * sub) == 0)
    tc = r * sub
    nc = N // tc
    has_add = add is not None

    def body(*refs):
        a_refs, w_ref = refs[:r], refs[r]
        o_ref = refs[r + 2] if has_add else refs[r + 1]

        def prod():
            tot = None
            for q in range(r):
                part = _nt(a_refs[q][...], w_ref[:, q * sub:(q + 1) * sub])
                tot = part if tot is None else tot + part
            return tot

        _mm_accumulate(pl.program_id(2), nc, prod, o_ref, refs[-1],
                       init=(lambda: refs[r + 1][...].astype(F32)) if has_add else None)

    def a_spec(q):
        return pl.BlockSpec((None, tm, sub), lambda i, j, c: ((c * r + q) // (W // sub), i, (c * r + q) % (W // sub)))

    in_specs = [a_spec(q) for q in range(r)]
    in_specs.append(pl.BlockSpec((None, to, tc), lambda i, j, c: (c // (n // tc), j, c % (n // tc))))
    ops = [a3] * r + [w]
    if has_add:
        in_specs.append(pl.BlockSpec((tm, to), lambda i, j, c: (i, j)))
        ops.append(add)
    return pl.pallas_call(
        body, name=name, grid=(M // tm, K // to, nc),
        in_specs=in_specs,
        out_specs=pl.BlockSpec((tm, to), lambda i, j, c: (i, j)),
        out_shape=S((M, K), out_dtype),
        scratch_shapes=[pltpu.VMEM((tm, to), F32)] if nc > 1 else [],
        compiler_params=_cparams(("parallel", "parallel", "arbitrary")),
    )(*ops)


def _mm_tn(a, d3, j_out, name, out_dtype=F32):
    M, K = a.shape
    NP, _, W = d3.shape
    N = NP * W
    n = N // j_out
    to, tn = _pick(K, 1024), _pick(math.gcd(W, n), 1024)
    tmc = _contract_tile(M, M, 2048)
    nm = M // tmc

    def body(*refs):
        a_ref, d_ref, o_ref = refs[:3]
        _mm_accumulate(
            pl.program_id(2), nm,
            lambda: lax.dot_general(a_ref[...], d_ref[...], (((0,), (0,)), ((), ())), preferred_element_type=F32),
            o_ref, refs[-1])

    return pl.pallas_call(
        body, name=name, grid=(K // to, N // tn, nm),
        in_specs=[pl.BlockSpec((tmc, to), lambda i, j, m: (m, i)),
                  pl.BlockSpec((None, tmc, tn), lambda i, j, m: (j // (W // tn), m, j % (W // tn)))],
        out_specs=pl.BlockSpec((None, to, tn), lambda i, j, m: (j // (n // tn), i, j % (n // tn))),
        out_shape=S((j_out, K, n), out_dtype),
        scratch_shapes=[pltpu.VMEM((to, tn), F32)] if nm > 1 else [],
        compiler_params=_cparams(("parallel", "parallel", "arbitrary")),
    )(a, d3)


def _rms_fwd(x3, piece, col_blk, width, g, name, out_dtype, res=None):
    T = x3.shape[1]
    tr = _pick(T, 256)
    has_res = res is not None

    def body(*refs):
        x_ref, g_ref = refs[0], refs[1]
        o_ref = refs[-1]
        x = x_ref[...].astype(F32)
        y = x * lax.rsqrt(jnp.mean(x * x, axis=-1, keepdims=True) + NORM_EPS) * g_ref[...]
        if has_res:
            y = refs[2][...] + y
        o_ref[...] = y.astype(o_ref.dtype)

    in_specs = [pl.BlockSpec((None, tr, width), lambda i: (piece, i, col_blk)),
                pl.BlockSpec((1, width), lambda i: (0, 0))]
    ops = [x3, g.reshape(1, width)]
    if has_res:
        in_specs.append(pl.BlockSpec((tr, width), lambda i: (i, 0)))
        ops.append(res)
    return pl.pallas_call(
        body, name=name, grid=(T // tr,), in_specs=in_specs,
        out_specs=pl.BlockSpec((tr, width), lambda i: (i, 0)),
        out_shape=S((T, width), out_dtype),
        compiler_params=_cparams(("parallel",)),
    )(*ops)


def _rms_bwd(u3, piece, col_blk, width, g, dy, name, out_dtype, res=None):
    T = u3.shape[1]
    tr = _pick(T, 256)
    has_res = res is not None

    def body(*refs):
        u_ref, g_ref, dy_ref = refs[0], refs[1], refs[2]
        du_ref, dg_ref = refs[-2], refs[-1]
        i = pl.program_id(0)
        u = u_ref[...].astype(F32)
        dy_ = dy_ref[...].astype(F32)
        r = lax.rsqrt(jnp.mean(u * u, axis=-1, keepdims=True) + NORM_EPS)
        nrm = u * r
        gdy = g_ref[...] * dy_
        du = r * (gdy - nrm * jnp.mean(gdy * nrm, axis=-1, keepdims=True))
        if has_res:
            du = du + refs[3][...]
        du_ref[...] = du.astype(du_ref.dtype)

        @pl.when(i == 0)
        def _():
            dg_ref[...] = jnp.zeros_like(dg_ref)

        dg_ref[...] += jnp.sum(dy_ * nrm, axis=0, keepdims=True)

    in_specs = [pl.BlockSpec((None, tr, width), lambda i: (piece, i, col_blk)),
                pl.BlockSpec((1, width), lambda i: (0, 0)),
                pl.BlockSpec((tr, width), lambda i: (i, 0))]
    ops = [u3, g.reshape(1, width), dy]
    if has_res:
        in_specs.append(pl.BlockSpec((tr, width), lambda i: (i, 0)))
        ops.append(res)
    return pl.pallas_call(
        body, name=name, grid=(T // tr,), in_specs=in_specs,
        out_specs=[pl.BlockSpec((tr, width), lambda i: (i, 0)), pl.BlockSpec((1, width), lambda i: (0, 0))],
        out_shape=[S((T, width), out_dtype), S((1, width), F32)],
        compiler_params=_cparams(("arbitrary",)),
    )(*ops)


def _loss_head(xl, target, name):
    T, D = xl.shape
    tr = _pick(T, 256)

    def body(x_ref, t_ref, dx_ref, l_ref):
        i = pl.program_id(0)
        err = x_ref[...] - t_ref[...]
        dx_ref[...] = err * (1.0 / D)

        @pl.when(i == 0)
        def _():
            l_ref[...] = jnp.zeros_like(l_ref)

        l_ref[...] += jnp.sum(err * err)

    dx, l = pl.pallas_call(
        body, name=name, grid=(T // tr,),
        in_specs=[pl.BlockSpec((tr, D), lambda i: (i, 0)), pl.BlockSpec((tr, D), lambda i: (i, 0))],
        out_specs=[pl.BlockSpec((tr, D), lambda i: (i, 0)), pl.BlockSpec((8, LANES), lambda i: (0, 0))],
        out_shape=[S((T, D), F32), S((8, LANES), F32)],
        compiler_params=_cparams(("arbitrary",)),
    )(xl, target)
    return dx, l[0, 0] * (0.5 / D)


def _gate_fwd(o, z3, name):
    T, W = o.shape
    tr = _pick(T, 256)

    def body(o_ref, z_ref, g_ref):
        g_ref[...] = (o_ref[...] * _silu(z_ref[...])).astype(g_ref.dtype)

    return pl.pallas_call(
        body, name=name, grid=(T // tr,),
        in_specs=[pl.BlockSpec((tr, W), lambda i: (i, 0)), pl.BlockSpec((None, tr, W), lambda i: (0, i, 0))],
        out_specs=pl.BlockSpec((tr, W), lambda i: (i, 0)),
        out_shape=S((T, W), BF16), compiler_params=_cparams(("parallel",)),
    )(o, z3)


def _gate_bwd(dg, o, z3, name):
    T, W = o.shape
    tr = _pick(T, 256)

    def body(dg_ref, o_ref, z_ref, do_ref, dz_ref):
        dg_, z = dg_ref[...], z_ref[...]
        do_ref[...] = (dg_ * _silu(z)).astype(do_ref.dtype)
        dz_ref[...] = (dg_ * o_ref[...] * _dsilu(z)).astype(dz_ref.dtype)

    return pl.pallas_call(
        body, name=name, grid=(T // tr,),
        in_specs=[pl.BlockSpec((tr, W), lambda i: (i, 0)), pl.BlockSpec((tr, W), lambda i: (i, 0)),
                  pl.BlockSpec((None, tr, W), lambda i: (0, i, 0))],
        out_specs=[pl.BlockSpec((tr, W), lambda i: (i, 0)), pl.BlockSpec((None, tr, W), lambda i: (0, i, 0))],
        out_shape=[S((T, W), BF16), S((1, T, W), BF16)], compiler_params=_cparams(("parallel",)),
    )(dg, o, z3)


def _rope_tables(pos_col, invf, name):
    T = pos_col.shape[0]
    tr = _pick(T, 512)
    half = ROPE_DIM // 2

    def body(p_ref, f_ref, c_ref, sa_ref, sb_ref):
        ang = p_ref[...].astype(F32) * f_ref[...]
        lane = lax.broadcasted_iota(jnp.int32, ang.shape, 1)
        cs, sn = jnp.cos(ang), jnp.sin(ang)
        c_ref[...] = jnp.where(lane < ROPE_DIM, cs, 0.0)
        sa_ref[...] = jnp.where(lane < half, -sn, 0.0)
        sb_ref[...] = jnp.where((lane >= half) & (lane < ROPE_DIM), sn, 0.0)

    spec = pl.BlockSpec((tr, LANES), lambda i: (i, 0))
    return pl.pallas_call(
        body, name=name, grid=(T // tr,),
        in_specs=[pl.BlockSpec((tr, 1), lambda i: (i, 0)), pl.BlockSpec((1, LANES), lambda i: (0, 0))],
        out_specs=[spec, spec, spec], out_shape=[S((T, LANES), F32)] * 3,
        compiler_params=_cparams(("parallel",)),
    )(pos_col, invf)


def _rope(t, c, sa, sb):
    half = ROPE_DIM // 2
    return t * c + pltpu.roll(t, LANES - half, 1) * sa + pltpu.roll(t, half, 1) * sb


def _rope_t(d, c, sa, sb):
    half = ROPE_DIM // 2
    return d * c + pltpu.roll(d * sa, half, 1) + pltpu.roll(d * sb, LANES - half, 1)


def _qkv_layout(q3, kv3, pa3, kr_blk, tabs, H, name):
    T = q3.shape[1]
    tr = _pick(T, 128)

    def body(q_ref, kv_ref, kr_ref, c_ref, sa_ref, sb_ref, qf_ref, kf_ref, v_ref):
        c, sa, sb = c_ref[...], sa_ref[...], sb_ref[...]
        kr = _rope(kr_ref[...], c, sa, sb).astype(BF16)
        for h in range(H):
            nope = slice(h * HEAD_PAD, h * HEAD_PAD + NOPE_DIM)
            rest = slice(h * HEAD_PAD + NOPE_DIM, (h + 1) * HEAD_PAD)
            qf_ref[:, nope] = q_ref[:, nope].astype(BF16)
            qf_ref[:, rest] = _rope(q_ref[:, rest], c, sa, sb).astype(BF16)
            kf_ref[:, nope] = kv_ref[:, nope].astype(BF16)
            kf_ref[:, rest] = kr
            v_ref[:, h * V_DIM:(h + 1) * V_DIM] = kv_ref[:, rest].astype(BF16)

    tab = pl.BlockSpec((tr, LANES), lambda i: (i, 0))
    wide = pl.BlockSpec((None, tr, H * HEAD_PAD), lambda i: (0, i, 0))
    return pl.pallas_call(
        body, name=name, grid=(T // tr,),
        in_specs=[wide, wide, pl.BlockSpec((None, tr, LANES), lambda i: (0, i, kr_blk)), tab, tab, tab],
        out_specs=[pl.BlockSpec((tr, H * HEAD_PAD), lambda i: (i, 0)), pl.BlockSpec((tr, H * HEAD_PAD), lambda i: (i, 0)),
                   pl.BlockSpec((tr, H * V_DIM), lambda i: (i, 0))],
        out_shape=[S((T, H * HEAD_PAD), BF16), S((T, H * HEAD_PAD), BF16), S((T, H * V_DIM), BF16)],
        compiler_params=_cparams(("parallel",)),
    )(q3, kv3, pa3, *tabs)


def _qkv_layout_bwd(dqf, dkf, dv, tabs, H, name):
    T = dqf.shape[0]
    tr = _pick(T, 128)

    def body(dqf_ref, dkf_ref, dv_ref, c_ref, sa_ref, sb_ref, dq_ref, dkv_ref, dkr_ref):
        c, sa, sb = c_ref[...], sa_ref[...], sb_ref[...]
        dkr = jnp.zeros((tr, LANES), F32)
        for h in range(H):
            nope = slice(h * HEAD_PAD, h * HEAD_PAD + NOPE_DIM)
            rest = slice(h * HEAD_PAD + NOPE_DIM, (h + 1) * HEAD_PAD)
            dq_ref[:, nope] = dqf_ref[:, nope]
            dq_ref[:, rest] = _rope_t(dqf_ref[:, rest].astype(F32), c, sa, sb).astype(BF16)
            dkv_ref[:, nope] = dkf_ref[:, nope]
            dkv_ref[:, rest] = dv_ref[:, h * V_DIM:(h + 1) * V_DIM]
            dkr = dkr + dkf_ref[:, rest].astype(F32)
        dkr_ref[...] = _rope_t(dkr, c, sa, sb).astype(BF16)

    tab = pl.BlockSpec((tr, LANES), lambda i: (i, 0))
    wide_in = pl.BlockSpec((tr, H * HEAD_PAD), lambda i: (i, 0))
    wide_out = pl.BlockSpec((None, tr, H * HEAD_PAD), lambda i: (0, i, 0))
    return pl.pallas_call(
        body, name=name, grid=(T // tr,),
        in_specs=[wide_in, wide_in, pl.BlockSpec((tr, H * V_DIM), lambda i: (i, 0)), tab, tab, tab],
        out_specs=[wide_out, wide_out, pl.BlockSpec((tr, LANES), lambda i: (i, 0))],
        out_shape=[S((1, T, H * HEAD_PAD), BF16), S((1, T, H * HEAD_PAD), BF16), S((T, LANES), BF16)],
        compiler_params=_cparams(("parallel",)),
    )(dqf, dkf, dv, *tabs)


ATTN_BLOCK = 512


def _nt(a, b):
    return lax.dot_general(a, b, (((1,), (1,)), ((), ())), preferred_element_type=F32)


def _tn(a, b):
    return lax.dot_general(a, b, (((0,), (0,)), ((), ())), preferred_element_type=F32)


def _causal_blocks(qi, tb, block):
    if qi > 0:
        def step(ki, carry):
            block(pl.multiple_of(ki * tb, tb), False)
            return carry
        lax.fori_loop(0, qi, step, 0)
    block(qi * tb, True)


def _attn_fwd(qf, kf, v, H, scale, name):
    T = qf.shape[0]
    tb = _pick(T, ATTN_BLOCK)
    nb = T // tb

    def body(q_ref, k_ref, v_ref, o_ref, lse_ref):
        row = lax.broadcasted_iota(jnp.int32, (tb, tb), 0)
        col = lax.broadcasted_iota(jnp.int32, (tb, tb), 1)
        for qi in range(nb):
            rows, before = pl.ds(qi * tb, tb), qi * tb
            q = q_ref[rows, :]
            s_own = jnp.where(col <= row, _nt(q, k_ref[rows, :]), NEG)
            m = jnp.max(s_own, axis=-1, keepdims=True)
            if qi > 0:
                s_pre = _nt(q, k_ref[0:before, :])
                m = jnp.maximum(m, jnp.max(s_pre, axis=-1, keepdims=True))
            p_own = jnp.exp((s_own - m) * scale)
            l = jnp.sum(p_own, axis=-1, keepdims=True)
            acc = jnp.dot(p_own.astype(BF16), v_ref[rows, :], preferred_element_type=F32)
            if qi > 0:
                p_pre = jnp.exp((s_pre - m) * scale)
                l = l + jnp.sum(p_pre, axis=-1, keepdims=True)
                acc = acc + jnp.dot(p_pre.astype(BF16), v_ref[0:before, :], preferred_element_type=F32)
            o_ref[rows, :] = acc / l
            lse_ref[rows, :] = jnp.broadcast_to(m * scale + jnp.log(l), (tb, LANES))

    return pl.pallas_call(
        body, name=name, grid=(H,),
        in_specs=[pl.BlockSpec((T, HEAD_PAD), lambda h: (0, h)), pl.BlockSpec((T, HEAD_PAD), lambda h: (0, h)),
                  pl.BlockSpec((T, V_DIM), lambda h: (0, h))],
        out_specs=[pl.BlockSpec((T, V_DIM), lambda h: (0, h)), pl.BlockSpec((T, LANES), lambda h: (0, h))],
        out_shape=[S((T, H * V_DIM), F32), S((T, H * LANES), F32)],
        compiler_params=_cparams(("parallel",)),
    )(qf, kf, v)


def _attn_bwd(qf, kf, v, do, o, lse, H, scale, name):
    T = qf.shape[0]
    tb = _pick(T, ATTN_BLOCK)
    nb = T // tb

    def body(q_ref, k_ref, v_ref, do_ref, o_ref, lse_ref, dq_ref, dk_ref, dv_ref, dq_acc, dk_acc, dv_acc):
        row = lax.broadcasted_iota(jnp.int32, (tb, tb), 0)
        col = lax.broadcasted_iota(jnp.int32, (tb, tb), 1)
        dk_acc[...] = jnp.zeros_like(dk_acc)
        dv_acc[...] = jnp.zeros_like(dv_acc)
        for qi in range(nb):
            rows = pl.ds(qi * tb, tb)
            dq_acc[...] = jnp.zeros_like(dq_acc)
            delta = jnp.sum(do_ref[rows, :].astype(F32) * o_ref[rows, :], axis=-1, keepdims=True)
            lse_q = lse_ref[rows, 0:1]

            def block(k0, masked, rows=rows, delta=delta, lse_q=lse_q):
                keys = pl.ds(k0, tb)
                q, k, do_ = q_ref[rows, :], k_ref[keys, :], do_ref[rows, :]
                s = _nt(q, k)
                if masked:
                    s = jnp.where(col <= row, s, NEG)
                p = jnp.exp(s * scale - lse_q)
                dp = _nt(do_, v_ref[keys, :])
                ds = (p * (dp - delta) * scale).astype(BF16)
                dv_acc[keys, :] += _tn(p.astype(BF16), do_)
                dk_acc[keys, :] += _tn(ds, q)
                dq_acc[...] += jnp.dot(ds, k, preferred_element_type=F32)

            _causal_blocks(qi, tb, block)
            dq_ref[rows, :] = dq_acc[...].astype(dq_ref.dtype)
        dk_ref[...] = dk_acc[...].astype(dk_ref.dtype)
        dv_ref[...] = dv_acc[...].astype(dv_ref.dtype)

    hp = pl.BlockSpec((T, HEAD_PAD), lambda h: (0, h))
    hv = pl.BlockSpec((T, V_DIM), lambda h: (0, h))
    return pl.pallas_call(
        body, name=name, grid=(H,),
        in_specs=[hp, hp, hv, hv, hv, pl.BlockSpec((T, LANES), lambda h: (0, h))],
        out_specs=[hp, hp, hv],
        out_shape=[S((T, H * HEAD_PAD), BF16), S((T, H * HEAD_PAD), BF16), S((T, H * V_DIM), BF16)],
        scratch_shapes=[pltpu.VMEM((tb, HEAD_PAD), F32), pltpu.VMEM((T, HEAD_PAD), F32), pltpu.VMEM((T, V_DIM), F32)],
        compiler_params=_cparams(("parallel",)),
    )(qf, kf, v, do, o, lse)


CONV_ROWS = 256
CONV_COLS = 128


def _conv_chunks(T):
    rc = min(CONV_ROWS, T)
    return [(r, rc) for r in range(0, T, rc)]


def _causal_conv(pad_ref, lead, w_ref, width, r0, rc):
    acc = None
    for k in range(width):
        term = w_ref[k:k + 1, :] * pad_ref[pl.ds(lead + r0 - (width - 1) + k, rc), :]
        acc = term if acc is None else acc + term
    return acc


def _anticausal_conv(pad_ref, w_ref, width, r0, rc):
    acc = None
    for k in range(width):
        term = w_ref[k:k + 1, :] * pad_ref[pl.ds(r0 + (width - 1) - k, rc), :]
        acc = term if acc is None else acc + term
    return acc


def _conv_wgrad(dpad_ref, xpad_ref, lead, width, T, dw_ref):
    for k in range(width):
        tot = None
        for r0, rc in _conv_chunks(T):
            part = jnp.sum(dpad_ref[pl.ds(r0, rc), :] * xpad_ref[pl.ds(lead + r0 - (width - 1) + k, rc), :],
                           axis=0, keepdims=True)
            tot = part if tot is None else tot + part
        dw_ref[k:k + 1, :] = tot


def _sc_fwd(p3, wconv, name):
    _, T, W = p3.shape
    width = wconv.shape[0]
    cw = min(CONV_COLS, W)
    lead = 8

    def body(p_ref, w_ref, g_ref, pad):
        pad[0:lead, :] = jnp.zeros((lead, cw), F32)
        for r0, rc in _conv_chunks(T):
            pad[pl.ds(lead + r0, rc), :] = p_ref[1, pl.ds(r0, rc), :] * p_ref[2, pl.ds(r0, rc), :]
        for r0, rc in _conv_chunks(T):
            rows = pl.ds(r0, rc)
            y = p_ref[0, rows, :] * _causal_conv(pad, lead, w_ref, width, r0, rc)
            g_ref[rows, :] = (y * _silu(p_ref[3, rows, :])).astype(g_ref.dtype)

    return pl.pallas_call(
        body, name=name, grid=(W // cw,),
        in_specs=[pl.BlockSpec((4, T, cw), lambda j: (0, 0, j)), pl.BlockSpec((width, cw), lambda j: (0, j))],
        out_specs=pl.BlockSpec((T, cw), lambda j: (0, j)),
        out_shape=S((T, W), BF16),
        scratch_shapes=[pltpu.VMEM((T + lead, cw), F32)],
        compiler_params=_cparams(("parallel",)),
    )(p3, wconv)


def _sc_bwd(p3, wconv, dg, name):
    _, T, W = p3.shape
    width = wconv.shape[0]
    cw = min(CONV_COLS, W)
    lead = 8

    def body(p_ref, w_ref, dg_ref, dp_ref, dw_ref, cupad, dvpad):
        cupad[0:lead, :] = jnp.zeros((lead, cw), F32)
        dvpad[pl.ds(T, lead), :] = jnp.zeros((lead, cw), F32)
        for r0, rc in _conv_chunks(T):
            cupad[pl.ds(lead + r0, rc), :] = p_ref[1, pl.ds(r0, rc), :] * p_ref[2, pl.ds(r0, rc), :]
        for r0, rc in _conv_chunks(T):
            rows = pl.ds(r0, rc)
            b, z, dg_ = p_ref[0, rows, :], p_ref[3, rows, :], dg_ref[rows, :]
            v = _causal_conv(cupad, lead, w_ref, width, r0, rc)
            dy = dg_ * _silu(z)
            dp_ref[3, rows, :] = (dg_ * b * v * _dsilu(z)).astype(dp_ref.dtype)
            dp_ref[0, rows, :] = (dy * v).astype(dp_ref.dtype)
            dvpad[rows, :] = dy * b
        for r0, rc in _conv_chunks(T):
            rows = pl.ds(r0, rc)
            dcu = _anticausal_conv(dvpad, w_ref, width, r0, rc)
            dp_ref[1, rows, :] = (dcu * p_ref[2, rows, :]).astype(dp_ref.dtype)
            dp_ref[2, rows, :] = (dcu * p_ref[1, rows, :]).astype(dp_ref.dtype)
        _conv_wgrad(dvpad, cupad, lead, width, T, dw_ref)

    return pl.pallas_call(
        body, name=name, grid=(W // cw,),
        in_specs=[pl.BlockSpec((4, T, cw), lambda j: (0, 0, j)), pl.BlockSpec((width, cw), lambda j: (0, j)),
                  pl.BlockSpec((T, cw), lambda j: (0, j))],
        out_specs=[pl.BlockSpec((4, T, cw), lambda j: (0, 0, j)), pl.BlockSpec((width, cw), lambda j: (0, j))],
        out_shape=[S((4, T, W), BF16), S((width, W), F32)],
        scratch_shapes=[pltpu.VMEM((T + lead, cw), F32), pltpu.VMEM((T + lead, cw), F32)],
        compiler_params=_cparams(("parallel",)),
    )(p3, wconv, dg)


def _gm_common(p_ref, lng_ref, lnb_ref):
    ug = _gelu(p_ref[0])
    vg = _gelu(p_ref[1])
    mu = jnp.mean(vg, axis=-1, keepdims=True)
    xc = vg - mu
    rstd = lax.rsqrt(jnp.mean(xc * xc, axis=-1, keepdims=True) + LN_EPS)
    xhat = xc * rstd
    vn = xhat * lng_ref[...] + lnb_ref[...]
    return ug, xhat, rstd, vn


def _gm_mix_weights(ws_ref, g):
    row = lax.broadcasted_iota(jnp.int32, (GM_CHUNK, GM_CHUNK), 0)
    col = lax.broadcasted_iota(jnp.int32, (GM_CHUNK, GM_CHUNK), 1)
    return jnp.where(col <= row, ws_ref[g], 0.0).astype(BF16)


def _gm_fwd(p3, lng, lnb, ws, bs_t, name):
    _, T, W = p3.shape
    gw = W // GM_GROUPS

    def body(p_ref, lng_ref, lnb_ref, ws_ref, bs_ref, g_ref):
        ug, _, _, vn = _gm_common(p_ref, lng_ref, lnb_ref)
        sz = _silu(p_ref[2])
        vnb = vn.astype(BF16)
        for g in range(GM_GROUPS):
            cols = slice(g * gw, (g + 1) * gw)
            s = jnp.dot(_gm_mix_weights(ws_ref, g), vnb[:, cols], preferred_element_type=F32) + bs_ref[:, g:g + 1]
            g_ref[:, cols] = (ug[:, cols] * s * sz[:, cols]).astype(g_ref.dtype)

    return pl.pallas_call(
        body, name=name, grid=(T // GM_CHUNK,),
        in_specs=[pl.BlockSpec((3, GM_CHUNK, W), lambda i: (0, i, 0)), pl.BlockSpec((1, W), lambda i: (0, 0)),
                  pl.BlockSpec((1, W), lambda i: (0, 0)),
                  pl.BlockSpec((GM_GROUPS, GM_CHUNK, GM_CHUNK), lambda i: (0, 0, 0)),
                  pl.BlockSpec((GM_CHUNK, GM_GROUPS), lambda i: (0, 0))],
        out_specs=pl.BlockSpec((GM_CHUNK, W), lambda i: (i, 0)),
        out_shape=S((T, W), BF16), compiler_params=_cparams(("parallel",)),
    )(p3, lng.reshape(1, W), lnb.reshape(1, W), ws, bs_t)


def _gm_bwd(p3, lng, lnb, ws, bs_t, dg, name):
    _, T, W = p3.shape
    gw = W // GM_GROUPS

    def body(p_ref, lng_ref, lnb_ref, ws_ref, bs_ref, dg_ref, dp_ref, dlng_ref, dlnb_ref, dws_ref, dbs_ref, dvn_s):
        i = pl.program_id(0)

        @pl.when(i == 0)
        def _():
            dlng_ref[...] = jnp.zeros_like(dlng_ref)
            dlnb_ref[...] = jnp.zeros_like(dlnb_ref)
            dws_ref[...] = jnp.zeros_like(dws_ref)
            dbs_ref[...] = jnp.zeros_like(dbs_ref)

        ug, xhat, rstd, vn = _gm_common(p_ref, lng_ref, lnb_ref)
        z = p_ref[2]
        dg_ = dg_ref[...]
        dy = dg_ * _silu(z)
        vnb = vn.astype(BF16)
        row = lax.broadcasted_iota(jnp.int32, (GM_CHUNK, GM_CHUNK), 0)
        col = lax.broadcasted_iota(jnp.int32, (GM_CHUNK, GM_CHUNK), 1)
        dbs = jnp.zeros((GM_CHUNK, LANES), F32)
        for g in range(GM_GROUPS):
            cols = slice(g * gw, (g + 1) * gw)
            wm = _gm_mix_weights(ws_ref, g)
            s = jnp.dot(wm, vnb[:, cols], preferred_element_type=F32) + bs_ref[:, g:g + 1]
            dp_ref[2, :, cols] = (dg_[:, cols] * ug[:, cols] * s * _dsilu(z[:, cols])).astype(dp_ref.dtype)
            dp_ref[0, :, cols] = (dy[:, cols] * s * _dgelu(p_ref[0, :, cols])).astype(dp_ref.dtype)
            ds = dy[:, cols] * ug[:, cols]
            dsb = ds.astype(BF16)
            dwm = lax.dot_general(dsb, vnb[:, cols], (((1,), (1,)), ((), ())), preferred_element_type=F32)
            dws_ref[g] += jnp.where(col <= row, dwm, 0.0)
            dbs = dbs + jnp.where(col == g, jnp.sum(ds, axis=-1, keepdims=True), 0.0)
            dvn_s[:, cols] = lax.dot_general(wm, dsb, (((0,), (0,)), ((), ())), preferred_element_type=F32)
        dbs_ref[...] += dbs
        dvn = dvn_s[...]
        dlng_ref[...] += jnp.sum(dvn * xhat, axis=0, keepdims=True)
        dlnb_ref[...] += jnp.sum(dvn, axis=0, keepdims=True)
        dxh = dvn * lng_ref[...]
        dvg = rstd * (dxh - jnp.mean(dxh, axis=-1, keepdims=True) - xhat * jnp.mean(dxh * xhat, axis=-1, keepdims=True))
        dp_ref[1] = (dvg * _dgelu(p_ref[1])).astype(dp_ref.dtype)

    row1 = pl.BlockSpec((1, W), lambda i: (0, 0))
    return pl.pallas_call(
        body, name=name, grid=(T // GM_CHUNK,),
        in_specs=[pl.BlockSpec((3, GM_CHUNK, W), lambda i: (0, i, 0)), row1, row1,
                  pl.BlockSpec((GM_GROUPS, GM_CHUNK, GM_CHUNK), lambda i: (0, 0, 0)),
                  pl.BlockSpec((GM_CHUNK, GM_GROUPS), lambda i: (0, 0)),
                  pl.BlockSpec((GM_CHUNK, W), lambda i: (i, 0))],
        out_specs=[pl.BlockSpec((3, GM_CHUNK, W), lambda i: (0, i, 0)), row1, row1,
                   pl.BlockSpec((GM_GROUPS, GM_CHUNK, GM_CHUNK), lambda i: (0, 0, 0)),
                   pl.BlockSpec((GM_CHUNK, LANES), lambda i: (0, 0))],
        out_shape=[S((3, T, W), BF16), S((1, W), F32), S((1, W), F32),
                   S((GM_GROUPS, GM_CHUNK, GM_CHUNK), F32), S((GM_CHUNK, LANES), F32)],
        scratch_shapes=[pltpu.VMEM((GM_CHUNK, W), F32)],
        compiler_params=_cparams(("arbitrary",)),
    )(p3, lng.reshape(1, W), lnb.reshape(1, W), ws, bs_t, dg)


def _cf_conv_fwd(p3, wdw, bdw, name):
    _, T, W = p3.shape
    width = wdw.shape[0]
    cw = min(CONV_COLS, W)
    lead = 32

    def body(p_ref, w_ref, b_ref, y_ref, pad):
        pad[0:lead, :] = jnp.zeros((lead, cw), F32)
        for r0, rc in _conv_chunks(T):
            rows = pl.ds(r0, rc)
            pad[pl.ds(lead + r0, rc), :] = p_ref[0, rows, :] * jax.nn.sigmoid(p_ref[1, rows, :])
        for r0, rc in _conv_chunks(T):
            y_ref[pl.ds(r0, rc), :] = _causal_conv(pad, lead, w_ref, width, r0, rc) + b_ref[...]

    return pl.pallas_call(
        body, name=name, grid=(W // cw,),
        in_specs=[pl.BlockSpec((2, T, cw), lambda j: (0, 0, j)), pl.BlockSpec((width, cw), lambda j: (0, j)),
                  pl.BlockSpec((1, cw), lambda j: (0, j))],
        out_specs=pl.BlockSpec((T, cw), lambda j: (0, j)),
        out_shape=S((T, W), F32),
        scratch_shapes=[pltpu.VMEM((T + lead, cw), F32)],
        compiler_params=_cparams(("parallel",)),
    )(p3, wdw, bdw.reshape(1, W))


def _cf_ln(y1_ref, lng_ref, lnb_ref):
    y1 = y1_ref[...]
    mu = jnp.mean(y1, axis=-1, keepdims=True)
    xc = y1 - mu
    rstd = lax.rsqrt(jnp.mean(xc * xc, axis=-1, keepdims=True) + LN_EPS)
    xhat = xc * rstd
    return xhat, rstd, xhat * lng_ref[...] + lnb_ref[...]


def _cf_gate_fwd(y1, p3, lng, lnb, name):
    T, W = y1.shape
    tr = _pick(T, 256)

    def body(y1_ref, z_ref, lng_ref, lnb_ref, g_ref):
        _, _, y2 = _cf_ln(y1_ref, lng_ref, lnb_ref)
        g_ref[...] = (_silu(y2) * _silu(z_ref[...])).astype(g_ref.dtype)

    row1 = pl.BlockSpec((1, W), lambda i: (0, 0))
    return pl.pallas_call(
        body, name=name, grid=(T // tr,),
        in_specs=[pl.BlockSpec((tr, W), lambda i: (i, 0)), pl.BlockSpec((None, tr, W), lambda i: (2, i, 0)), row1, row1],
        out_specs=pl.BlockSpec((tr, W), lambda i: (i, 0)),
        out_shape=S((T, W), BF16), compiler_params=_cparams(("parallel",)),
    )(y1, p3, lng.reshape(1, W), lnb.reshape(1, W))


def _cf_gate_bwd(y1, p3, lng, lnb, dg, name):
    T, W = y1.shape
    tr = _pick(T, 128)

    def body(y1_ref, z_ref, lng_ref, lnb_ref, dg_ref, dz_ref, dy1_ref, dlng_ref, dlnb_ref):
        i = pl.program_id(0)

        @pl.when(i == 0)
        def _():
            dlng_ref[...] = jnp.zeros_like(dlng_ref)
            dlnb_ref[...] = jnp.zeros_like(dlnb_ref)

        xhat, rstd, y2 = _cf_ln(y1_ref, lng_ref, lnb_ref)
        z, dg_ = z_ref[...], dg_ref[...]
        dz_ref[...] = (dg_ * _silu(y2) * _dsilu(z)).astype(dz_ref.dtype)
        dy2 = dg_ * _silu(z) * _dsilu(y2)
        dlng_ref[...] += jnp.sum(dy2 * xhat, axis=0, keepdims=True)
        dlnb_ref[...] += jnp.sum(dy2, axis=0, keepdims=True)
        dxh = dy2 * lng_ref[...]
        dy1_ref[...] = rstd * (dxh - jnp.mean(dxh, axis=-1, keepdims=True)
                               - xhat * jnp.mean(dxh * xhat, axis=-1, keepdims=True))

    row1 = pl.BlockSpec((1, W), lambda i: (0, 0))
    blk = pl.BlockSpec((tr, W), lambda i: (i, 0))
    return pl.pallas_call(
        body, name=name, grid=(T // tr,),
        in_specs=[blk, pl.BlockSpec((None, tr, W), lambda i: (2, i, 0)), row1, row1, blk],
        out_specs=[blk, blk, row1, row1],
        out_shape=[S((T, W), BF16), S((T, W), F32), S((1, W), F32), S((1, W), F32)],
        compiler_params=_cparams(("arbitrary",)),
    )(y1, p3, lng.reshape(1, W), lnb.reshape(1, W), dg)


def _cf_conv_bwd(p3, wdw, dy1, dz, name):
    _, T, W = p3.shape
    width = wdw.shape[0]
    cw = min(CONV_COLS, W)
    lead = 32

    def body(p_ref, w_ref, dy1_ref, dz_ref, dp_ref, dw_ref, db_ref, y0pad, dpad):
        y0pad[0:lead, :] = jnp.zeros((lead, cw), F32)
        dpad[pl.ds(T, lead), :] = jnp.zeros((lead, cw), F32)
        bsum = None
        for r0, rc in _conv_chunks(T):
            rows = pl.ds(r0, rc)
            y0pad[pl.ds(lead + r0, rc), :] = p_ref[0, rows, :] * jax.nn.sigmoid(p_ref[1, rows, :])
            d = dy1_ref[rows, :]
            dpad[rows, :] = d
            part = jnp.sum(d, axis=0, keepdims=True)
            bsum = part if bsum is None else bsum + part
        db_ref[...] = bsum
        for r0, rc in _conv_chunks(T):
            rows = pl.ds(r0, rc)
            dy0 = _anticausal_conv(dpad, w_ref, width, r0, rc)
            a = p_ref[0, rows, :]
            sg = jax.nn.sigmoid(p_ref[1, rows, :])
            dp_ref[0, rows, :] = (dy0 * sg).astype(dp_ref.dtype)
            dp_ref[1, rows, :] = (dy0 * a * sg * (1.0 - sg)).astype(dp_ref.dtype)
            dp_ref[2, rows, :] = dz_ref[rows, :]
        _conv_wgrad(dpad, y0pad, lead, width, T, dw_ref)

    return pl.pallas_call(
        body, name=name, grid=(W // cw,),
        in_specs=[pl.BlockSpec((2, T, cw), lambda j: (0, 0, j)), pl.BlockSpec((width, cw), lambda j: (0, j)),
                  pl.BlockSpec((T, cw), lambda j: (0, j)), pl.BlockSpec((T, cw), lambda j: (0, j))],
        out_specs=[pl.BlockSpec((3, T, cw), lambda j: (0, 0, j)), pl.BlockSpec((width, cw), lambda j: (0, j)),
                   pl.BlockSpec((1, cw), lambda j: (0, j))],
        out_shape=[S((3, T, W), BF16), S((width, W), F32), S((1, W), F32)],
        scratch_shapes=[pltpu.VMEM((T + lead, cw), F32), pltpu.VMEM((T + lead, cw), F32)],
        compiler_params=_cparams(("parallel",)),
    )(p3, wdw, dy1, dz)


def _rows_call(body, ins, out_dtypes, name, row_pref=256):
    R, C = ins[0].shape
    tr = _pick(R, row_pref) if R % 8 == 0 else R
    while tr > 8 and tr * C * 4 * (len(ins) + len(out_dtypes)) * 2 > VMEM_LIMIT // 2 and tr % 16 == 0:
        tr //= 2
    blk = pl.BlockSpec((tr, C), lambda i: (i, 0))
    return pl.pallas_call(
        body, name=name, grid=(R // tr,), in_specs=[blk] * len(ins), out_specs=[blk] * len(out_dtypes),
        out_shape=[S((R, C), dt) for dt in out_dtypes], compiler_params=_cparams(("parallel",)),
    )(*ins)


def _pair_sum(g, r, core, name):
    J, K, n = g.shape
    kh = K // 2
    tr = _pick(kh, 256)
    nb = kh // tr

    def body(core_ref, g_ref, r_ref, o_ref):
        o_ref[...] = (g_ref[...] + r_ref[...]).astype(BF16)

    return pl.pallas_call(
        body, name=name,
        grid_spec=pltpu.PrefetchScalarGridSpec(
            num_scalar_prefetch=1, grid=(J, nb),
            in_specs=[pl.BlockSpec((None, tr, n), lambda j, i, core_ref: (j, core_ref[0] * nb + i, 0)),
                      pl.BlockSpec((None, tr, n), lambda j, i, core_ref: (j, i, 0))],
            out_specs=pl.BlockSpec((None, tr, n), lambda j, i, core_ref: (j, i, 0))),
        out_shape=S((J, kh, n), BF16),
        compiler_params=_cparams(("parallel", "parallel")),
    )(core, g, r)


def _chip_sum(rc, core, name):
    J, R, C = rc.shape
    tr = _pick(R, 256)
    nb = R // tr

    def body(core_ref, r_ref, o_ref):
        acc = r_ref[0].astype(F32)
        for j in range(1, J):
            acc = acc + r_ref[j].astype(F32)
        o_ref[...] = acc

    return pl.pallas_call(
        body, name=name,
        grid_spec=pltpu.PrefetchScalarGridSpec(
            num_scalar_prefetch=1, grid=(nb,),
            in_specs=[pl.BlockSpec((J, tr, C), lambda i, core_ref: (0, i, 0))],
            out_specs=pl.BlockSpec((tr, C), lambda i, core_ref: (core_ref[0] * nb + i, 0))),
        out_shape=S((2 * R, C), F32), compiler_params=_cparams(("parallel",)),
    )(core, rc)


def _slot_sum(slots, name):
    J, R, C = slots.shape
    tr = _pick(R, 512)

    def body(r_ref, o_ref):
        acc = r_ref[0]
        for j in range(1, J):
            acc = acc + r_ref[j]
        o_ref[...] = acc

    return pl.pallas_call(
        body, name=name, grid=(R // tr,),
        in_specs=[pl.BlockSpec((J, tr, C), lambda i: (0, i, 0))], out_specs=pl.BlockSpec((tr, C), lambda i: (i, 0)),
        out_shape=S((R, C), F32), compiler_params=_cparams(("parallel",)),
    )(slots)


def _adamw(w, g, m, v, name):
    def body(w_ref, g_ref, m_ref, v_ref, d_ref, nm_ref, nv_ref):
        g_ = g_ref[...]
        nm = ADAM_B1 * m_ref[...] + (1.0 - ADAM_B1) * g_
        nv = ADAM_B2 * v_ref[...] + (1.0 - ADAM_B2) * (g_ * g_)
        m_hat = nm / (1.0 - ADAM_B1 ** ADAM_STEP)
        v_hat = nv / (1.0 - ADAM_B2 ** ADAM_STEP)
        d_ref[...] = -ADAM_LR * (m_hat / (jnp.sqrt(v_hat) + ADAM_EPS) + ADAM_WD * w_ref[...])
        nm_ref[...] = nm
        nv_ref[...] = nv
    return _rows_call(body, [w, g, m, v], [F32, F32, F32], name)


ANY = pl.BlockSpec(memory_space=pl.ANY)


def _place():
    x, y, c = lax.axis_index("x"), lax.axis_index("y"), lax.axis_index("c")
    return x, y, c


def _other_chips(x, y):
    return [(1 - x, y), (x, 1 - y), (1 - x, 1 - y)]


HBM = pl.BlockSpec(memory_space=pltpu.HBM)
SEM = pl.BlockSpec(memory_space=pltpu.SEMAPHORE)
DATAFLOW = pltpu.SideEffectType.DATAFLOW_SIDE_EFFECTING


def _in_hbm(a):
    return pltpu.with_memory_space_constraint(a, pltpu.HBM)


def _half_rows(k_rows, which):
    return pl.ds(which * (k_rows // 2), k_rows // 2)


COPIES_PER_ARRAY = {"gather": 3, "scatter": 3, "halves": 1, "forward": 3, "everyone": N_DEV - 1}


def _split_copies(srcs, lands, send_sems, recv_sems, mode):
    x, y, c = _place()
    me_chip = 2 * x + y
    out = []
    for w, (src, land) in enumerate(zip(srcs, lands)):
        if mode == "everyone":
            for k in range(1, N_DEV):
                px, py, pc = x ^ ((k >> 2) & 1), y ^ ((k >> 1) & 1), c ^ (k & 1)
                sems = dict(send_sem=send_sems.at[(N_DEV - 1) * w + k - 1], recv_sem=recv_sems.at[(N_DEV - 1) * w + k - 1],
                            device_id=(px, py, pc), device_id_type=MESH_ID)
                out.append((pltpu.make_async_remote_copy(src_ref=src, dst_ref=land.at[4 * x + 2 * y + c], **sems),
                            pltpu.make_async_remote_copy(src_ref=src, dst_ref=land.at[4 * px + 2 * py + pc], **sems)))
            continue
        if mode == "forward":
            mine, theirs = _half_rows(land.shape[1], c), _half_rows(land.shape[1], 1 - c)
            for j, (cx, cy) in enumerate(_other_chips(x, y)):
                them = 2 * cx + cy
                sems = dict(send_sem=send_sems.at[3 * w + j], recv_sem=recv_sems.at[3 * w + j],
                            device_id=(x, y, 1 - c), device_id_type=MESH_ID)
                out.append((pltpu.make_async_remote_copy(src_ref=land.at[them, mine], dst_ref=land.at[them, mine], **sems),
                            pltpu.make_async_remote_copy(src_ref=land.at[them, mine], dst_ref=land.at[them, theirs], **sems)))
            continue
        if mode == "halves":
            rows = _half_rows(src.shape[1], 1 - c)
            cp = pltpu.make_async_remote_copy(src_ref=src.at[:, rows, :], dst_ref=land, send_sem=send_sems.at[w],
                                              recv_sem=recv_sems.at[w], device_id=(x, y, 1 - c), device_id_type=MESH_ID)
            out.append((cp, cp))
            continue
        for j, (cx, cy) in enumerate(_other_chips(x, y)):
            them = 2 * cx + cy
            sems = dict(send_sem=send_sems.at[3 * w + j], recv_sem=recv_sems.at[3 * w + j],
                        device_id=(cx, cy, c), device_id_type=MESH_ID)
            if mode == "gather":
                rows = _half_rows(src.shape[0], c)
                go = pltpu.make_async_remote_copy(src_ref=src.at[rows], dst_ref=land.at[me_chip, rows], **sems)
                arrive = pltpu.make_async_remote_copy(src_ref=src.at[rows], dst_ref=land.at[them, rows], **sems)
            else:
                go = pltpu.make_async_remote_copy(src_ref=src.at[them], dst_ref=land.at[me_chip], **sems)
                arrive = pltpu.make_async_remote_copy(src_ref=src.at[them], dst_ref=land.at[them], **sems)
            out.append((go, arrive))
    return out


def _split_copies_start(srcs, lands, mode, name, after=None):
    nw = len(lands)
    ns = 0 if srcs is None else nw
    arrays = list(lands) if srcs is None else list(srcs) + list(lands)
    n_sems = COPIES_PER_ARRAY[mode] * nw
    n_in = ns + nw + (after is not None)

    def body(*refs):
        lnd = refs[ns:ns + nw]
        ins = refs[:ns] if ns else lnd
        send_sems, recv_sems, token = refs[n_in], refs[n_in + 1], refs[-1]
        for go, _ in _split_copies(ins, lnd, send_sems, recv_sems, mode):
            go.start()
        token[...] = jnp.zeros_like(token)

    thru = [pltpu.HBM(a.shape, a.dtype) for a in arrays]
    res = pl.pallas_call(
        body, name=name, in_specs=[HBM] * (ns + nw) + [ANY] * (after is not None),
        out_specs=[SEM, SEM] + [HBM] * (ns + nw) + [pl.BlockSpec(memory_space=pltpu.VMEM)],
        out_shape=[pltpu.SemaphoreType.DMA((n_sems,)), pltpu.SemaphoreType.DMA((n_sems,))] + thru + [S((8, LANES), F32)],
        input_output_aliases={i: 2 + i for i in range(ns + nw)},
        compiler_params=pltpu.CompilerParams(has_side_effects=DATAFLOW),
    )(*[_in_hbm(a) for a in arrays], *([after] if after is not None else []))
    return res[0], res[1], list(res[2:2 + ns]), list(res[2 + ns:2 + ns + nw]), res[-1]


def _split_copies_wait(send_sems, recv_sems, srcs, lands, after, mode, name):
    nw, ns = len(lands), len(srcs)
    arrays = list(srcs) + list(lands)

    def body(*refs):
        lnd = refs[ns:ns + nw]
        ins = refs[:ns] if ns else lnd
        send, recv = refs[ns + nw], refs[ns + nw + 1]
        for _, arrive in _split_copies(ins, lnd, send, recv, mode):
            arrive.wait_send()
            arrive.wait_recv()

    res = pl.pallas_call(
        body, name=name, in_specs=[HBM] * (ns + nw) + [SEM, SEM, ANY], out_specs=[HBM] * (ns + nw),
        out_shape=[pltpu.HBM(a.shape, a.dtype) for a in arrays],
        input_output_aliases={i: i for i in range(ns + nw)},
        compiler_params=pltpu.CompilerParams(has_side_effects=DATAFLOW),
    )(*arrays, send_sems, recv_sems, after)
    return list(res[:ns]), list(res[ns:])


def _gather_forward(lands, name):
    nw = len(lands)

    def body(*refs):
        ins, outs = refs[:nw], refs[nw:2 * nw]
        send_sems, recv_sems = refs[2 * nw:]
        x, y, c = _place()
        sibling = (x, y, 1 - c)
        cps = []
        for w in range(nw):
            kr = ins[w].shape[1]
            for j, (cx, cy) in enumerate(_other_chips(x, y)):
                them = 2 * cx + cy
                sems = dict(send_sem=send_sems.at[3 * w + j], recv_sem=recv_sems.at[3 * w + j],
                            device_id=sibling, device_id_type=MESH_ID)
                mine, theirs = _half_rows(kr, c), _half_rows(kr, 1 - c)
                go = pltpu.make_async_remote_copy(src_ref=ins[w].at[them, mine], dst_ref=outs[w].at[them, mine], **sems)
                go.start()
                cps.append((go, pltpu.make_async_remote_copy(
                    src_ref=ins[w].at[them, theirs], dst_ref=outs[w].at[them, theirs], **sems)))
        for go, arrive in cps:
            arrive.wait_recv()
            go.wait_send()

    return pl.pallas_call(
        body, name=name, in_specs=[ANY] * nw, out_specs=[ANY] * nw,
        out_shape=[S(a.shape, a.dtype) for a in lands],
        scratch_shapes=[pltpu.SemaphoreType.DMA((3 * nw,)), pltpu.SemaphoreType.DMA((3 * nw,))],
        input_output_aliases={i: i for i in range(nw)},
        compiler_params=pltpu.CompilerParams(has_side_effects=True),
    )(*lands)


def _core_join_halves(halves, name):
    nw = len(halves)

    def body(*refs):
        ins, outs = refs[:nw], refs[nw:2 * nw]
        send_sems, recv_sems = refs[2 * nw:]
        x, y, c = _place()
        sibling = (x, y, 1 - c)
        rem = []
        for w in range(nw):
            r = ins[w].shape[0] // 2
            mine = pl.ds(c * r, r)
            cp = pltpu.make_async_remote_copy(
                src_ref=ins[w].at[mine], dst_ref=outs[w].at[mine], send_sem=send_sems.at[w], recv_sem=recv_sems.at[w],
                device_id=sibling, device_id_type=MESH_ID)
            cp.start()
            rem.append(cp)
        for w in range(nw):
            r = ins[w].shape[0] // 2
            theirs = outs[w].at[pl.ds((1 - c) * r, r)]
            pltpu.make_async_remote_copy(
                src_ref=theirs, dst_ref=theirs, send_sem=send_sems.at[w], recv_sem=recv_sems.at[w],
                device_id=sibling, device_id_type=MESH_ID).wait_recv()
        for cp in rem:
            cp.wait_send()

    return pl.pallas_call(
        body, name=name, in_specs=[ANY] * nw, out_specs=[ANY] * nw,
        out_shape=[S(h.shape, h.dtype) for h in halves],
        scratch_shapes=[pltpu.SemaphoreType.DMA((nw,)), pltpu.SemaphoreType.DMA((nw,))],
        input_output_aliases={i: i for i in range(nw)},
        compiler_params=pltpu.CompilerParams(has_side_effects=True),
    )(*halves)


def _broadcast_all(buf, name):
    def body(in_ref, out_ref, send_sems, recv_sems, loc_sem):
        x, y, c = _place()
        me = 4 * x + 2 * y + c
        loc = pltpu.make_async_copy(in_ref, out_ref.at[me], loc_sem)
        loc.start()
        cps = []
        for k in range(1, N_DEV):
            fx, fy, fc = (k >> 2) & 1, (k >> 1) & 1, k & 1
            px, py, pc = x ^ fx, y ^ fy, c ^ fc
            cp = pltpu.make_async_remote_copy(
                src_ref=in_ref, dst_ref=out_ref.at[me], send_sem=send_sems.at[k - 1], recv_sem=recv_sems.at[k - 1],
                device_id=(px, py, pc), device_id_type=MESH_ID)
            cp.start()
            cps.append(cp)
        for k in range(1, N_DEV):
            fx, fy, fc = (k >> 2) & 1, (k >> 1) & 1, k & 1
            px, py, pc = x ^ fx, y ^ fy, c ^ fc
            slot = out_ref.at[4 * px + 2 * py + pc]
            pltpu.make_async_remote_copy(
                src_ref=slot, dst_ref=slot, send_sem=send_sems.at[k - 1], recv_sem=recv_sems.at[k - 1],
                device_id=(px, py, pc), device_id_type=MESH_ID).wait_recv()
        for cp in cps:
            cp.wait_send()
        loc.wait()

    return pl.pallas_call(
        body, name=name, in_specs=[ANY], out_specs=ANY,
        out_shape=S((N_DEV,) + buf.shape, buf.dtype),
        scratch_shapes=[pltpu.SemaphoreType.DMA((N_DEV - 1,)), pltpu.SemaphoreType.DMA((N_DEV - 1,)),
                        pltpu.SemaphoreType.DMA],
        compiler_params=pltpu.CompilerParams(has_side_effects=True),
    )(buf)


PACK_ALIGN = 8 * LANES


def _pack(arrs):
    flat = []
    for a in arrs:
        f = a.reshape(-1).astype(F32)
        pad = (-f.shape[0]) % PACK_ALIGN
        flat.append(jnp.pad(f, (0, pad)) if pad else f)
    return jnp.concatenate(flat).reshape(-1, LANES)


def _unpack(buf, shapes):
    out, off = [], 0
    flat = buf.reshape(-1)
    for shp in shapes:
        n = math.prod(shp)
        out.append(flat[off:off + n].reshape(shp))
        off += n + ((-n) % PACK_ALIGN)
    return out


def kernel(x, positions, norm_pre, norm_post, w_in_mla, mla_q_norm, w_uq, mla_kv_norm, w_ukv, w_out_mla, w_in_sc, sc_conv, w_out_sc, w_in_gm, gm_ln_g, gm_ln_b, gm_w_s, gm_b_s, w_out_gm, w_in_cf, cf_dw, cf_dw_b, cf_ln_g, cf_ln_b, w_out_cf, loss_target, m_norm_pre, m_norm_post, m_w_in_mla, m_mla_q_norm, m_w_uq, m_mla_kv_norm, m_w_ukv, m_w_out_mla, m_w_in_sc, m_sc_conv, m_w_out_sc, m_w_in_gm, m_gm_ln_g, m_gm_ln_b, m_gm_w_s, m_gm_b_s, m_w_out_gm, m_w_in_cf, m_cf_dw, m_cf_dw_b, m_cf_ln_g, m_cf_ln_b, m_w_out_cf, v_norm_pre, v_norm_post, v_w_in_mla, v_mla_q_norm, v_w_uq, v_mla_kv_norm, v_w_ukv, v_w_out_mla, v_w_in_sc, v_sc_conv, v_w_out_sc, v_w_in_gm, v_gm_ln_g, v_gm_ln_b, v_gm_w_s, v_gm_b_s, v_w_out_gm, v_w_in_cf, v_cf_dw, v_cf_dw_b, v_cf_ln_g, v_cf_ln_b, v_w_out_cf):
    loc = dict(locals())
    wts = {n: loc[n] for n in WEIGHTS}
    mom_m = {n: loc["m_" + n] for n in WEIGHTS}
    mom_v = {n: loc["v_" + n] for n in WEIGHTS}

    T, D = x.shape[1], x.shape[2]
    xin = x.reshape(T, D)
    target = loss_target.reshape(T, D)
    q_rank, kv_rank = mla_q_norm.shape[1], mla_kv_norm.shape[1]
    H = (w_uq.shape[2] * N_CHIPS) // (NOPE_DIM + ROPE_DIM)
    hv = H * V_DIM
    c_kr = q_rank + kv_rank
    wa_cols = c_kr + ROPE_DIM
    wa_pad = wa_cols + (LANES - ROPE_DIM)
    chip = 2 * lax.axis_index("x") + lax.axis_index("y")

    c = lax.axis_index("c")
    core = c.reshape(1).astype(jnp.int32)
    small_sh_shapes = [wts[n][0].shape for n in SMALL_SHARDED]
    small_slots = _broadcast_all(_pack([wts[n][0] for n in SMALL_SHARDED]), "gather_small")
    gather_started = []
    for gi, names in enumerate(GROUPS):
        own = [wts[n][0].astype(BF16) for n in names]
        lands = [lax.dynamic_update_slice(lax.empty((N_CHIPS,) + s.shape, BF16), s[None], (chip, 0, 0)) for s in own]
        gather_started.append(_split_copies_start(own, lands, "gather", f"gather_start_{gi}",
                                                  after=gather_started[-1][4] if gather_started else small_slots))
    started_token = sum(st[4][0, 0] for st in gather_started)
    gw = {}

    forward_started = {}

    def forward_ahead(gi, after):
        send_sems, recv_sems, own, lands, _ = gather_started[gi]
        _, landed = _split_copies_wait(send_sems, recv_sems, own, lands, after, "gather", f"gather_wait_{gi}")
        forward_started[gi] = _split_copies_start(None, landed, "forward", f"gather_forward_start_{gi}")
        return forward_started[gi][4][0, 0]

    def gathered_weights(gi, after):
        if gi in forward_started:
            send_sems, recv_sems, _, lands, _ = forward_started[gi]
            _, full = _split_copies_wait(send_sems, recv_sems, [], lands, after, "forward", f"gather_forward_wait_{gi}")
        else:
            send_sems, recv_sems, own, lands, _ = gather_started[gi]
            _, landed = _split_copies_wait(send_sems, recv_sems, own, lands, after, "gather", f"gather_wait_{gi}")
            full = _gather_forward(landed, f"gather_forward_{gi}")
        gw.update(zip(GROUPS[gi], full))

    per_chip = [_unpack(small_slots[2 * k], small_sh_shapes) for k in range(N_CHIPS)]
    sp = {n: jnp.concatenate([per_chip[k][i] for k in range(N_CHIPS)], axis=-1) for i, n in enumerate(SMALL_SHARDED)}

    def cols_major(w4):
        return jnp.transpose(w4, (1, 0, 2)).reshape(w4.shape[1], -1)

    half = ROPE_DIM // 2
    inv_freq = ROPE_THETA ** (-jnp.arange(half, dtype=F32) / half)
    invf = jnp.concatenate([inv_freq, inv_freq, jnp.zeros((LANES - ROPE_DIM,), F32)]).reshape(1, LANES)
    tabs = _rope_tables(positions.reshape(T, 1), invf, "rope_tables")
    scale = float(NOPE_DIM + ROPE_DIM) ** -0.5

    xs = [xin]
    saved = []
    w_out = {}
    for i in range(4):
        xi = xs[-1]
        gathered_weights(LAYER_GROUPS[i][0], gather_started[-1][4] if i == 0 else xi)
        ahead_token = 0.0
        h = _rms_fwd(xi[None], 0, 0, D, norm_pre[i] + started_token if i == 0 else norm_pre[i], f"pre_norm_{i}", BF16)
        if i == 0:
            w_in_full = cols_major(gw['w_in_mla'])
            w_a = jnp.pad(w_in_full[:, :wa_cols], ((0, 0), (0, wa_pad - wa_cols)))[None]
            w_z = w_in_full[:, wa_cols:][None]
            pa = _mm_nn(h, w_a, 1, "mla_in_a")
            pz = _mm_nn(h, w_z, 1, "mla_in_z")
            gathered_weights(LAYER_GROUPS[i][1], pz)
            wq = cols_major(gw['w_uq']).reshape(q_rank, H, NOPE_DIM + ROPE_DIM)
            wq = jnp.pad(wq, ((0, 0), (0, 0), (0, HEAD_PAD - NOPE_DIM - ROPE_DIM))).reshape(1, q_rank, H * HEAD_PAD)
            wkv = cols_major(gw['w_ukv'])[None]
            qn = _rms_fwd(pa, 0, 0, q_rank, mla_q_norm[0], "mla_q_norm", BF16)
            kvn = _rms_fwd(pa, 0, q_rank // kv_rank, kv_rank, mla_kv_norm[0], "mla_kv_norm", BF16)
            q3 = _mm_nn(qn, wq, 1, "mla_q_up")
            kv3 = _mm_nn(kvn, wkv, 1, "mla_kv_up")
            qf, kf, vv = _qkv_layout(q3, kv3, pa, c_kr // LANES, tabs, H, "mla_qkv_layout")
            o, lse = _attn_fwd(qf, kf, vv, H, scale, "mla_attn_fwd")
            ahead_token = forward_ahead(LAYER_GROUPS[1][0], o)
            g = _gate_fwd(o, pz, "mla_gate_fwd")
            saved.append(dict(h=h, pa=pa, pz=pz, qn=qn, kvn=kvn, qf=qf, kf=kf, vv=vv, o=o, lse=lse, g=g))
        elif i == 1:
            p3 = _mm_nn(h, gw['w_in_sc'], 4, "sc_in")
            ahead_token = forward_ahead(LAYER_GROUPS[2][0], p3)
            g = _sc_fwd(p3, sp['sc_conv'], "sc_mix_fwd")
            saved.append(dict(h=h, p3=p3, g=g))
        elif i == 2:
            p3 = _mm_nn(h, gw['w_in_gm'], 3, "gm_in")
            ahead_token = forward_ahead(LAYER_GROUPS[3][0], p3)
            bs_t = jnp.transpose(gm_b_s[0])
            g = _gm_fwd(p3, sp['gm_ln_g'], sp['gm_ln_b'], gm_w_s[0], bs_t, "gm_mix_fwd")
            saved.append(dict(h=h, p3=p3, g=g, bs_t=bs_t))
        else:
            p3 = _mm_nn(h, gw['w_in_cf'], 3, "cf_in")
            y1 = _cf_conv_fwd(p3, sp['cf_dw'], sp['cf_dw_b'], "cf_conv_fwd")
            g = _cf_gate_fwd(y1, p3, sp['cf_ln_g'], sp['cf_ln_b'], "cf_gate_fwd")
            saved.append(dict(h=h, p3=p3, y1=y1, g=g))
        w_out[WO_NAMES[i]] = gw[WO_NAMES[i]].reshape(1, -1, D)
        yo = _mm_nn(g, w_out[WO_NAMES[i]], 1, f"out_proj_{i}")
        saved[-1]['yo'] = yo
        xs.append(_rms_fwd(yo, 0, 0, D, norm_post[i] + ahead_token, f"post_norm_{i}", F32, res=xi))

    dx, loss_local = _loss_head(xs[4], target, "loss_head")
    loss = lax.psum(loss_local, ("x", "y", "c"))

    big_grads = {}
    sgrad = {}
    d_npre, d_npost = [None] * 4, [None] * 4
    exch_started, scatter_started, scattered = {}, {}, {}

    def start_exchange(gi):
        full = [big_grads[n] for n in GROUPS[gi]]
        lands = [lax.empty((g_.shape[0], g_.shape[1] // 2, g_.shape[2]), g_.dtype) for g_ in full]
        exch_started[gi] = _split_copies_start(full, lands, "halves", f"grads_exchange_start_{gi}")
        return exch_started[gi][4][0, 0]

    def start_scatter(gi, after):
        send_sems, recv_sems, full, lands, _ = exch_started[gi]
        full, recv = _split_copies_wait(send_sems, recv_sems, full, lands, after, "halves", f"grads_exchange_wait_{gi}")
        pair = [_pair_sum(g_, r, core, f"pair_sum_{n}") for n, g_, r in zip(GROUPS[gi], full, recv)]
        lands = [lax.dynamic_update_slice(lax.empty(p.shape, p.dtype), lax.dynamic_slice_in_dim(p, chip, 1, axis=0),
                                          (chip, 0, 0)) for p in pair]
        scatter_started[gi] = _split_copies_start(pair, lands, "scatter", f"scatter_start_{gi}")
        return scatter_started[gi][4][0, 0]

    def finish_scatter(gi, after):
        send_sems, recv_sems, pair, lands, _ = scatter_started[gi]
        scattered[gi] = _split_copies_wait(send_sems, recv_sems, pair, lands, after, "scatter", f"scatter_wait_{gi}")[1]

    token = 0.0
    for i in (3, 2, 1, 0):
        sv = saved[i]
        h = sv['h']
        dyo, d_npost[i] = _rms_bwd(sv['yo'], 0, 0, D, norm_post[i] + token, dx, f"post_norm_bwd_{i}", BF16)
        dyo3 = dyo[None]
        wo_name = WO_NAMES[i]
        dg = _mm_nt(dyo3, w_out[wo_name], f"out_proj_dx_{i}")
        big_grads[wo_name] = _mm_tn(sv['g'], dyo3, 1, f"out_proj_dw_{i}").reshape(N_CHIPS, -1, D)
        if i == 3:
            dz, dy1, sgrad['cf_ln_g'], sgrad['cf_ln_b'] = _cf_gate_bwd(sv['y1'], sv['p3'], sp['cf_ln_g'], sp['cf_ln_b'], dg, "cf_gate_bwd")
            dp3, sgrad['cf_dw'], sgrad['cf_dw_b'] = _cf_conv_bwd(sv['p3'], sp['cf_dw'], dy1, dz, "cf_conv_bwd")
            big_grads['w_in_cf'] = _mm_tn(h, dp3, N_CHIPS, "cf_in_dw")
            dh = _mm_nt(dp3, gw['w_in_cf'], "cf_in_dx")
        elif i == 2:
            dp3, sgrad['gm_ln_g'], sgrad['gm_ln_b'], sgrad['gm_w_s'], dbs_t = _gm_bwd(
                sv['p3'], sp['gm_ln_g'], sp['gm_ln_b'], gm_w_s[0], sv['bs_t'], dg, "gm_mix_bwd")
            sgrad['gm_b_s'] = jnp.transpose(dbs_t[:, :GM_GROUPS])
            big_grads['w_in_gm'] = _mm_tn(h, dp3, N_CHIPS, "gm_in_dw")
            dh = _mm_nt(dp3, gw['w_in_gm'], "gm_in_dx")
        elif i == 1:
            dp3, sgrad['sc_conv'] = _sc_bwd(sv['p3'], sp['sc_conv'], dg, "sc_mix_bwd")
            big_grads['w_in_sc'] = _mm_tn(h, dp3, N_CHIPS, "sc_in_dw")
            dh = _mm_nt(dp3, gw['w_in_sc'], "sc_in_dx")
        else:
            do, dpz = _gate_bwd(dg, sv['o'], sv['pz'], "mla_gate_bwd")
            dqf, dkf, dv = _attn_bwd(sv['qf'], sv['kf'], sv['vv'], do, sv['o'], sv['lse'], H, scale, "mla_attn_bwd")
            finish_scatter(3, dqf)
            token = start_scatter(2, dqf)
            dq3, dkv3, dkr = _qkv_layout_bwd(dqf, dkf, dv, (tabs[0] + token, tabs[1], tabs[2]), H, "mla_qkv_layout_bwd")
            dqn = _mm_nt(dq3, wq, "mla_q_up_dx")
            dwq = _mm_tn(sv['qn'], dq3, 1, "mla_q_up_dw")
            dkvn = _mm_nt(dkv3, wkv, "mla_kv_up_dx")
            dwkv = _mm_tn(sv['kvn'], dkv3, 1, "mla_kv_up_dw")
            dwq_ = dwq[0].reshape(q_rank, H, HEAD_PAD)[:, :, :NOPE_DIM + ROPE_DIM].reshape(q_rank, N_CHIPS, -1)
            big_grads['w_uq'] = jnp.transpose(dwq_, (1, 0, 2))
            big_grads['w_ukv'] = jnp.transpose(dwkv[0].reshape(kv_rank, N_CHIPS, -1), (1, 0, 2))
            token = start_exchange(1)
            dcq, dqg = _rms_bwd(sv['pa'], 0, 0, q_rank, mla_q_norm[0] + token, dqn, "mla_q_norm_bwd", BF16)
            dckv, dkvg = _rms_bwd(sv['pa'], 0, q_rank // kv_rank, kv_rank, mla_kv_norm[0], dkvn, "mla_kv_norm_bwd", BF16)
            sgrad['mla_q_norm'], sgrad['mla_kv_norm'] = dqg, dkvg
            dpa = jnp.concatenate([dcq, dckv, dkr], axis=1)[None]
            dwa = _mm_tn(h, dpa, 1, "mla_in_a_dw")
            dwz = _mm_tn(h, dpz, 1, "mla_in_z_dw")
            dh_a = _mm_nt(dpa, w_a, "mla_in_a_dx")
            dh = _mm_nt(dpz, w_z, "mla_in_z_dx", add=dh_a)
            dw_in = jnp.concatenate([dwa[0][:, :wa_cols], dwz[0]], axis=1)
            big_grads['w_in_mla'] = jnp.transpose(dw_in.reshape(D, N_CHIPS, -1), (1, 0, 2))
            token = start_scatter(1, dh) + start_exchange(0)
        dx, d_npre[i] = _rms_bwd(xs[i][None], 0, 0, D, norm_pre[i] + token if i == 0 else norm_pre[i], dh,
                                 f"pre_norm_bwd_{i}", F32, res=dx)
        if i == 3:
            token = start_exchange(4)
        elif i == 2:
            token = start_scatter(4, dx) + start_exchange(3)
        elif i == 1:
            finish_scatter(4, dx)
            token = start_scatter(3, dx) + start_exchange(2)
    grad_x = dx.reshape(1, T, D)
    sgrad['norm_pre'] = jnp.concatenate(d_npre, axis=0)
    sgrad['norm_post'] = jnp.concatenate(d_npost, axis=0)

    grads, delta, new_m, new_v = {}, {}, {}, {}

    def update_group(gi):
        halves = [_chip_sum(r, core, f"chip_sum_{n}") for n, r in zip(GROUPS[gi], scattered[gi])]
        for n, j in zip(GROUPS[gi], _core_join_halves(halves, f"grads_core_join_{gi}")):
            shp = wts[n].shape
            two_d = (shp[1], shp[2])
            d_, m_, v_ = _adamw(wts[n].reshape(two_d), j, mom_m[n].reshape(two_d), mom_v[n].reshape(two_d), f"adamw_{n}")
            grads[n], delta[n], new_m[n], new_v[n] = j[None], d_.reshape(shp), m_.reshape(shp), v_.reshape(shp)
        return d_

    small_full_shapes = [sgrad[n].reshape(wts[n].shape[:-1] + (-1,)).shape for n in SMALL]
    packed = _pack([sgrad[n] for n in SMALL])
    slots = lax.dynamic_update_slice(lax.empty((N_DEV,) + packed.shape, F32), packed[None], (2 * chip + c, 0, 0))
    small_started = _split_copies_start([packed], [slots], "everyone", "grads_small_start")
    finish_scatter(2, dx)
    start_scatter(0, dx)
    for gi in (4, 3, 2):
        done = update_group(gi)
    finish_scatter(1, done)
    finish_scatter(0, done)
    update_group(1)
    done = update_group(0)
    _, (gslots,) = _split_copies_wait(*small_started[:4], done, "everyone", "grads_small_wait")
    gsum = _unpack(_slot_sum(gslots, "grads_small_sum"), small_full_shapes)
    for n, gs in zip(SMALL, gsum):
        if n in SMALL_SHARDED:
            per = wts[n].shape[-1]
            gs = lax.dynamic_slice_in_dim(gs, chip * per, per, axis=gs.ndim - 1)
        grads[n] = gs.reshape(wts[n].shape)

    shapes = [wts[n].shape for n in SMALL]
    d_, m_, v_ = _adamw(_pack([wts[n] for n in SMALL]), _pack([grads[n] for n in SMALL]),
                        _pack([mom_m[n] for n in SMALL]), _pack([mom_v[n] for n in SMALL]), "adamw_small")
    for n, a, b, cc in zip(SMALL, _unpack(d_, shapes), _unpack(m_, shapes), _unpack(v_, shapes)):
        delta[n], new_m[n], new_v[n] = a, b, cc

    return (loss, grad_x, *[grads[n] for n in WEIGHTS], *[delta[n] for n in WEIGHTS],
            *[new_m[n] for n in WEIGHTS], *[new_v[n] for n in WEIGHTS])
```

```python
import functools
import math

import jax
import jax.numpy as jnp
from jax import lax
from jax.experimental import pallas as pl
from jax.experimental.pallas import tpu as pltpu

F32, BF16 = jnp.float32, jnp.bfloat16
S = jax.ShapeDtypeStruct
MESH_ID = pl.DeviceIdType.MESH

V7X_VMEM_BYTES = 64 * 1024 * 1024
VMEM_LIMIT = V7X_VMEM_BYTES - 8 * 1024 * 1024
LANES = 128
N_CHIPS = 4
N_DEV = 8

NORM_EPS = 1e-6
LN_EPS = 1e-5
ROPE_THETA = 10000.0
ROPE_DIM = 64
NOPE_DIM = 128
V_DIM = 128
HEAD_PAD = 256
GM_CHUNK = 128
GM_GROUPS = 8
NEG = -1e30

ADAM_LR, ADAM_B1, ADAM_B2, ADAM_EPS, ADAM_WD, ADAM_STEP = 0.001, 0.9, 0.999, 1e-08, 0.01, 10

FWD_PARAMS = ['x', 'positions', 'norm_pre', 'norm_post', 'w_in_mla', 'mla_q_norm', 'w_uq', 'mla_kv_norm', 'w_ukv',
              'w_out_mla', 'w_in_sc', 'sc_conv', 'w_out_sc', 'w_in_gm', 'gm_ln_g', 'gm_ln_b', 'gm_w_s', 'gm_b_s',
              'w_out_gm', 'w_in_cf', 'cf_dw', 'cf_dw_b', 'cf_ln_g', 'cf_ln_b', 'w_out_cf']
WEIGHTS = FWD_PARAMS[2:]
BIG = ['w_in_mla', 'w_uq', 'w_ukv', 'w_out_mla', 'w_in_sc', 'w_out_sc', 'w_in_gm', 'w_out_gm', 'w_in_cf', 'w_out_cf']
GROUPS = [['w_in_mla'], ['w_uq', 'w_ukv', 'w_out_mla'], ['w_in_sc', 'w_out_sc'], ['w_in_gm', 'w_out_gm'],
          ['w_in_cf', 'w_out_cf']]
LAYER_GROUPS = [[0, 1], [2], [3], [4]]
WO_NAMES = ['w_out_mla', 'w_out_sc', 'w_out_gm', 'w_out_cf']
SMALL = [n for n in WEIGHTS if n not in BIG]
SMALL_SHARDED = ['sc_conv', 'gm_ln_g', 'gm_ln_b', 'cf_dw', 'cf_dw_b', 'cf_ln_g', 'cf_ln_b']


def _cparams(sem=None, **kw):
    return pltpu.CompilerParams(dimension_semantics=sem, vmem_limit_bytes=VMEM_LIMIT, **kw)


def _pick(dim, pref):
    if dim <= pref:
        return dim
    t = (pref // LANES) * LANES
    while t >= LANES and dim % t:
        t -= LANES
    if t >= min(pref, 512):
        return t
    return dim if (dim <= 2048 or t < LANES) else t


def _silu(x):
    return x * jax.nn.sigmoid(x)


def _dsilu(x):
    s = jax.nn.sigmoid(x)
    return s * (1.0 + x * (1.0 - s))


def _gelu(x):
    return 0.5 * x * (1.0 + lax.erf(x * (2.0 ** -0.5)))


def _dgelu(x):
    cdf = 0.5 * (1.0 + lax.erf(x * (2.0 ** -0.5)))
    return cdf + x * jnp.exp(-0.5 * x * x) * ((2.0 * math.pi) ** -0.5)


MM_ONE_DOT = 4096


def _contract_tile(dim, divisible_by, pref_when_split):
    return dim if dim <= MM_ONE_DOT and divisible_by % dim == 0 else _pick(divisible_by, pref_when_split)


def _mm_accumulate(step, nsteps, prod, o_ref, acc, init=None):
    if nsteps == 1:
        r = prod()
        if init is not None:
            r = r + init()
        o_ref[...] = r.astype(o_ref.dtype)
        return

    @pl.when(step == 0)
    def _():
        acc[...] = jnp.zeros_like(acc) if init is None else init()

    acc[...] += prod()

    @pl.when(step == nsteps - 1)
    def _():
        o_ref[...] = acc[...].astype(o_ref.dtype)


def _mm_nn(a, w, np_out, name, out_dtype=F32):
    M, K = a.shape
    J, _, n = w.shape
    N = J * n
    W = N // np_out
    tm, tn = _pick(M, 1024), _pick(math.gcd(W, n), 1024)
    tk = _contract_tile(K, K, 2048)
    nk = K // tk

    def body(*refs):
        a_ref, w_ref, o_ref = refs[:3]
        _mm_accumulate(pl.program_id(2), nk, lambda: jnp.dot(a_ref[...], w_ref[...], preferred_element_type=F32),
                       o_ref, refs[-1])

    return pl.pallas_call(
        body, name=name, grid=(M // tm, N // tn, nk),
        in_specs=[pl.BlockSpec((tm, tk), lambda i, j, k: (i, k)),
                  pl.BlockSpec((None, tk, tn), lambda i, j, k: (j // (n // tn), k, j % (n // tn)))],
        out_specs=pl.BlockSpec((None, tm, tn), lambda i, j, k: (j // (W // tn), i, j % (W // tn))),
        out_shape=S((np_out, M, W), out_dtype),
        scratch_shapes=[pltpu.VMEM((tm, tn), F32)] if nk > 1 else [],
        compiler_params=_cparams(("parallel", "parallel", "arbitrary")),
    )(a, w)


def _mm_nt(a3, w, name, add=None, out_dtype=F32):
    NP, M, W = a3.shape
    J, K, n = w.shape
    N = NP * W
    tm, to = _pick(M, 1024), _pick(K, 1024)
    sub = _contract_tile(N, math.gcd(W, n), 2048)
    r = max(q for q in range(1, MM_ONE_DOT // sub + 1) if n % (q * sub) == 0 and N % (q * sub) == 0)
    tc = r * sub
    nc = N // tc
    has_add = add is not None

    def body(*refs):
        a_refs, w_ref = refs[:r], refs[r]
        o_ref = refs[r + 2] if has_add else refs[r + 1]

        def prod():
            tot = None
            for q in range(r):
                part = _nt(a_refs[q][...], w_ref[:, q * sub:(q + 1) * sub])
                tot = part if tot is None else tot + part
            return tot

        _mm_accumulate(pl.program_id(2), nc, prod, o_ref, refs[-1],
                       init=(lambda: refs[r + 1][...].astype(F32)) if has_add else None)

    def a_spec(q):
        return pl.BlockSpec((None, tm, sub), lambda i, j, c: ((c * r + q) // (W // sub), i, (c * r + q) % (W // sub)))

    in_specs = [a_spec(q) for q in range(r)]
    in_specs.append(pl.BlockSpec((None, to, tc), lambda i, j, c: (c // (n // tc), j, c % (n // tc))))
    ops = [a3] * r + [w]
    if has_add:
        in_specs.append(pl.BlockSpec((tm, to), lambda i, j, c: (i, j)))
        ops.append(add)
    return pl.pallas_call(
        body, name=name, grid=(M // tm, K // to, nc),
        in_specs=in_specs,
        out_specs=pl.BlockSpec((tm, to), lambda i, j, c: (i, j)),
        out_shape=S((M, K), out_dtype),
        scratch_shapes=[pltpu.VMEM((tm, to), F32)] if nc > 1 else [],
        compiler_params=_cparams(("parallel", "parallel", "arbitrary")),
    )(*ops)


def _mm_tn(a, d3, j_out, name, out_dtype=F32):
    M, K = a.shape
    NP, _, W = d3.shape
    N = NP * W
    n = N // j_out
    to, tn = _pick(K, 1024), _pick(math.gcd(W, n), 1024)
    tmc = _contract_tile(M, M, 2048)
    nm = M // tmc

    def body(*refs):
        a_ref, d_ref, o_ref = refs[:3]
        _mm_accumulate(
            pl.program_id(2), nm,
            lambda: lax.dot_general(a_ref[...], d_ref[...], (((0,), (0,)), ((), ())), preferred_element_type=F32),
            o_ref, refs[-1])

    return pl.pallas_call(
        body, name=name, grid=(K // to, N // tn, nm),
        in_specs=[pl.BlockSpec((tmc, to), lambda i, j, m: (m, i)),
                  pl.BlockSpec((None, tmc, tn), lambda i, j, m: (j // (W // tn), m, j % (W // tn)))],
        out_specs=pl.BlockSpec((None, to, tn), lambda i, j, m: (j // (n // tn), i, j % (n // tn))),
        out_shape=S((j_out, K, n), out_dtype),
        scratch_shapes=[pltpu.VMEM((to, tn), F32)] if nm > 1 else [],
        compiler_params=_cparams(("parallel", "parallel", "arbitrary")),
    )(a, d3)


def _rms_fwd(x3, piece, col_blk, width, g, name, out_dtype, res=None):
    T = x3.shape[1]
    tr = _pick(T, 256)
    has_res = res is not None

    def body(*refs):
        x_ref, g_ref = refs[0], refs[1]
        o_ref = refs[-1]
        x = x_ref[...].astype(F32)
        y = x * lax.rsqrt(jnp.mean(x * x, axis=-1, keepdims=True) + NORM_EPS) * g_ref[...]
        if has_res:
            y = refs[2][...] + y
        o_ref[...] = y.astype(o_ref.dtype)

    in_specs = [pl.BlockSpec((None, tr, width), lambda i: (piece, i, col_blk)),
                pl.BlockSpec((1, width), lambda i: (0, 0))]
    ops = [x3, g.reshape(1, width)]
    if has_res:
        in_specs.append(pl.BlockSpec((tr, width), lambda i: (i, 0)))
        ops.append(res)
    return pl.pallas_call(
        body, name=name, grid=(T // tr,), in_specs=in_specs,
        out_specs=pl.BlockSpec((tr, width), lambda i: (i, 0)),
        out_shape=S((T, width), out_dtype),
        compiler_params=_cparams(("parallel",)),
    )(*ops)


def _rms_bwd(u3, piece, col_blk, width, g, dy, name, out_dtype, res=None):
    T = u3.shape[1]
    tr = _pick(T, 256)
    has_res = res is not None

    def body(*refs):
        u_ref, g_ref, dy_ref = refs[0], refs[1], refs[2]
        du_ref, dg_ref = refs[-2], refs[-1]
        i = pl.program_id(0)
        u = u_ref[...].astype(F32)
        dy_ = dy_ref[...].astype(F32)
        r = lax.rsqrt(jnp.mean(u * u, axis=-1, keepdims=True) + NORM_EPS)
        nrm = u * r
        gdy = g_ref[...] * dy_
        du = r * (gdy - nrm * jnp.mean(gdy * nrm, axis=-1, keepdims=True))
        if has_res:
            du = du + refs[3][...]
        du_ref[...] = du.astype(du_ref.dtype)

        @pl.when(i == 0)
        def _():
            dg_ref[...] = jnp.zeros_like(dg_ref)

        dg_ref[...] += jnp.sum(dy_ * nrm, axis=0, keepdims=True)

    in_specs = [pl.BlockSpec((None, tr, width), lambda i: (piece, i, col_blk)),
                pl.BlockSpec((1, width), lambda i: (0, 0)),
                pl.BlockSpec((tr, width), lambda i: (i, 0))]
    ops = [u3, g.reshape(1, width), dy]
    if has_res:
        in_specs.append(pl.BlockSpec((tr, width), lambda i: (i, 0)))
        ops.append(res)
    return pl.pallas_call(
        body, name=name, grid=(T // tr,), in_specs=in_specs,
        out_specs=[pl.BlockSpec((tr, width), lambda i: (i, 0)), pl.BlockSpec((1, width), lambda i: (0, 0))],
        out_shape=[S((T, width), out_dtype), S((1, width), F32)],
        compiler_params=_cparams(("arbitrary",)),
    )(*ops)


def _loss_head(xl, target, name):
    T, D = xl.shape
    tr = _pick(T, 256)

    def body(x_ref, t_ref, dx_ref, l_ref):
        i = pl.program_id(0)
        err = x_ref[...] - t_ref[...]
        dx_ref[...] = err * (1.0 / D)

        @pl.when(i == 0)
        def _():
            l_ref[...] = jnp.zeros_like(l_ref)

        l_ref[...] += jnp.sum(err * err)

    dx, l = pl.pallas_call(
        body, name=name, grid=(T // tr,),
        in_specs=[pl.BlockSpec((tr, D), lambda i: (i, 0)), pl.BlockSpec((tr, D), lambda i: (i, 0))],
        out_specs=[pl.BlockSpec((tr, D), lambda i: (i, 0)), pl.BlockSpec((8, LANES), lambda i: (0, 0))],
        out_shape=[S((T, D), F32), S((8, LANES), F32)],
        compiler_params=_cparams(("arbitrary",)),
    )(xl, target)
    return dx, l[0, 0] * (0.5 / D)


def _gate_fwd(o, z3, name):
    T, W = o.shape
    tr = _pick(T, 256)

    def body(o_ref, z_ref, g_ref):
        g_ref[...] = (o_ref[...] * _silu(z_ref[...])).astype(g_ref.dtype)

    return pl.pallas_call(
        body, name=name, grid=(T // tr,),
        in_specs=[pl.BlockSpec((tr, W), lambda i: (i, 0)), pl.BlockSpec((None, tr, W), lambda i: (0, i, 0))],
        out_specs=pl.BlockSpec((tr, W), lambda i: (i, 0)),
        out_shape=S((T, W), BF16), compiler_params=_cparams(("parallel",)),
    )(o, z3)


def _gate_bwd(dg, o, z3, name):
    T, W = o.shape
    tr = _pick(T, 256)

    def body(dg_ref, o_ref, z_ref, do_ref, dz_ref):
        dg_, z = dg_ref[...], z_ref[...]
        do_ref[...] = (dg_ * _silu(z)).astype(do_ref.dtype)
        dz_ref[...] = (dg_ * o_ref[...] * _dsilu(z)).astype(dz_ref.dtype)

    return pl.pallas_call(
        body, name=name, grid=(T // tr,),
        in_specs=[pl.BlockSpec((tr, W), lambda i: (i, 0)), pl.BlockSpec((tr, W), lambda i: (i, 0)),
                  pl.BlockSpec((None, tr, W), lambda i: (0, i, 0))],
        out_specs=[pl.BlockSpec((tr, W), lambda i: (i, 0)), pl.BlockSpec((None, tr, W), lambda i: (0, i, 0))],
        out_shape=[S((T, W), BF16), S((1, T, W), BF16)], compiler_params=_cparams(("parallel",)),
    )(dg, o, z3)


def _rope_tables(pos_col, invf, name):
    T = pos_col.shape[0]
    tr = _pick(T, 512)
    half = ROPE_DIM // 2

    def body(p_ref, f_ref, c_ref, sa_ref, sb_ref):
        ang = p_ref[...].astype(F32) * f_ref[...]
        lane = lax.broadcasted_iota(jnp.int32, ang.shape, 1)
        cs, sn = jnp.cos(ang), jnp.sin(ang)
        c_ref[...] = jnp.where(lane < ROPE_DIM, cs, 0.0)
        sa_ref[...] = jnp.where(lane < half, -sn, 0.0)
        sb_ref[...] = jnp.where((lane >= half) & (lane < ROPE_DIM), sn, 0.0)

    spec = pl.BlockSpec((tr, LANES), lambda i: (i, 0))
    return pl.pallas_call(
        body, name=name, grid=(T // tr,),
        in_specs=[pl.BlockSpec((tr, 1), lambda i: (i, 0)), pl.BlockSpec((1, LANES), lambda i: (0, 0))],
        out_specs=[spec, spec, spec], out_shape=[S((T, LANES), F32)] * 3,
        compiler_params=_cparams(("parallel",)),
    )(pos_col, invf)


def _rope(t, c, sa, sb):
    half = ROPE_DIM // 2
    return t * c + pltpu.roll(t, LANES - half, 1) * sa + pltpu.roll(t, half, 1) * sb


def _rope_t(d, c, sa, sb):
    half = ROPE_DIM // 2
    return d * c + pltpu.roll(d * sa, half, 1) + pltpu.roll(d * sb, LANES - half, 1)


def _qkv_layout(q3, kv3, pa3, kr_blk, tabs, H, name):
    T = q3.shape[1]
    tr = _pick(T, 128)

    def body(q_ref, kv_ref, kr_ref, c_ref, sa_ref, sb_ref, qf_ref, kf_ref, v_ref):
        c, sa, sb = c_ref[...], sa_ref[...], sb_ref[...]
        kr = _rope(kr_ref[...], c, sa, sb).astype(BF16)
        for h in range(H):
            nope = slice(h * HEAD_PAD, h * HEAD_PAD + NOPE_DIM)
            rest = slice(h * HEAD_PAD + NOPE_DIM, (h + 1) * HEAD_PAD)
            qf_ref[:, nope] = q_ref[:, nope].astype(BF16)
            qf_ref[:, rest] = _rope(q_ref[:, rest], c, sa, sb).astype(BF16)
            kf_ref[:, nope] = kv_ref[:, nope].astype(BF16)
            kf_ref[:, rest] = kr
            v_ref[:, h * V_DIM:(h + 1) * V_DIM] = kv_ref[:, rest].astype(BF16)

    tab = pl.BlockSpec((tr, LANES), lambda i: (i, 0))
    wide = pl.BlockSpec((None, tr, H * HEAD_PAD), lambda i: (0, i, 0))
    return pl.pallas_call(
        body, name=name, grid=(T // tr,),
        in_specs=[wide, wide, pl.BlockSpec((None, tr, LANES), lambda i: (0, i, kr_blk)), tab, tab, tab],
        out_specs=[pl.BlockSpec((tr, H * HEAD_PAD), lambda i: (i, 0)), pl.BlockSpec((tr, H * HEAD_PAD), lambda i: (i, 0)),
                   pl.BlockSpec((tr, H * V_DIM), lambda i: (i, 0))],
        out_shape=[S((T, H * HEAD_PAD), BF16), S((T, H * HEAD_PAD), BF16), S((T, H * V_DIM), BF16)],
        compiler_params=_cparams(("parallel",)),
    )(q3, kv3, pa3, *tabs)


def _qkv_layout_bwd(dqf, dkf, dv, tabs, H, name):
    T = dqf.shape[0]
    tr = _pick(T, 128)

    def body(dqf_ref, dkf_ref, dv_ref, c_ref, sa_ref, sb_ref, dq_ref, dkv_ref, dkr_ref):
        c, sa, sb = c_ref[...], sa_ref[...], sb_ref[...]
        dkr = jnp.zeros((tr, LANES), F32)
        for h in range(H):
            nope = slice(h * HEAD_PAD, h * HEAD_PAD + NOPE_DIM)
            rest = slice(h * HEAD_PAD + NOPE_DIM, (h + 1) * HEAD_PAD)
            dq_ref[:, nope] = dqf_ref[:, nope]
            dq_ref[:, rest] = _rope_t(dqf_ref[:, rest].astype(F32), c, sa, sb).astype(BF16)
            dkv_ref[:, nope] = dkf_ref[:, nope]
            dkv_ref[:, rest] = dv_ref[:, h * V_DIM:(h + 1) * V_DIM]
            dkr = dkr + dkf_ref[:, rest].astype(F32)
        dkr_ref[...] = _rope_t(dkr, c, sa, sb).astype(BF16)

    tab = pl.BlockSpec((tr, LANES), lambda i: (i, 0))
    wide_in = pl.BlockSpec((tr, H * HEAD_PAD), lambda i: (i, 0))
    wide_out = pl.BlockSpec((None, tr, H * HEAD_PAD), lambda i: (0, i, 0))
    return pl.pallas_call(
        body, name=name, grid=(T // tr,),
        in_specs=[wide_in, wide_in, pl.BlockSpec((tr, H * V_DIM), lambda i: (i, 0)), tab, tab, tab],
        out_specs=[wide_out, wide_out, pl.BlockSpec((tr, LANES), lambda i: (i, 0))],
        out_shape=[S((1, T, H * HEAD_PAD), BF16), S((1, T, H * HEAD_PAD), BF16), S((T, LANES), BF16)],
        compiler_params=_cparams(("parallel",)),
    )(dqf, dkf, dv, *tabs)


ATTN_BLOCK = 512


def _nt(a, b):
    return lax.dot_general(a, b, (((1,), (1,)), ((), ())), preferred_element_type=F32)


def _tn(a, b):
    return lax.dot_general(a, b, (((0,), (0,)), ((), ())), preferred_element_type=F32)


def _causal_blocks(qi, tb, block):
    if qi > 0:
        def step(ki, carry):
            block(pl.multiple_of(ki * tb, tb), False)
            return carry
        lax.fori_loop(0, qi, step, 0)
    block(qi * tb, True)


def _attn_fwd(qf, kf, v, H, scale, name):
    T = qf.shape[0]
    tb = _pick(T, ATTN_BLOCK)
    nb = T // tb

    def body(q_ref, k_ref, v_ref, o_ref, lse_ref):
        row = lax.broadcasted_iota(jnp.int32, (tb, tb), 0)
        col = lax.broadcasted_iota(jnp.int32, (tb, tb), 1)
        for qi in range(nb):
            rows, before = pl.ds(qi * tb, tb), qi * tb
            q = q_ref[rows, :]
            s_own = jnp.where(col <= row, _nt(q, k_ref[rows, :]), NEG)
            m = jnp.max(s_own, axis=-1, keepdims=True)
            if qi > 0:
                s_pre = _nt(q, k_ref[0:before, :])
                m = jnp.maximum(m, jnp.max(s_pre, axis=-1, keepdims=True))
            p_own = jnp.exp((s_own - m) * scale)
            l = jnp.sum(p_own, axis=-1, keepdims=True)
            acc = jnp.dot(p_own.astype(BF16), v_ref[rows, :], preferred_element_type=F32)
            if qi > 0:
                p_pre = jnp.exp((s_pre - m) * scale)
                l = l + jnp.sum(p_pre, axis=-1, keepdims=True)
                acc = acc + jnp.dot(p_pre.astype(BF16), v_ref[0:before, :], preferred_element_type=F32)
            o_ref[rows, :] = acc / l
            lse_ref[rows, :] = jnp.broadcast_to(m * scale + jnp.log(l), (tb, LANES))

    return pl.pallas_call(
        body, name=name, grid=(H,),
        in_specs=[pl.BlockSpec((T, HEAD_PAD), lambda h: (0, h)), pl.BlockSpec((T, HEAD_PAD), lambda h: (0, h)),
                  pl.BlockSpec((T, V_DIM), lambda h: (0, h))],
        out_specs=[pl.BlockSpec((T, V_DIM), lambda h: (0, h)), pl.BlockSpec((T, LANES), lambda h: (0, h))],
        out_shape=[S((T, H * V_DIM), F32), S((T, H * LANES), F32)],
        compiler_params=_cparams(("parallel",)),
    )(qf, kf, v)


def _attn_bwd(qf, kf, v, do, o, lse, H, scale, name):
    T = qf.shape[0]
    tb = _pick(T, ATTN_BLOCK)
    nb = T // tb

    def body(q_ref, k_ref, v_ref, do_ref, o_ref, lse_ref, dq_ref, dk_ref, dv_ref, dq_acc, dk_acc, dv_acc):
        row = lax.broadcasted_iota(jnp.int32, (tb, tb), 0)
        col = lax.broadcasted_iota(jnp.int32, (tb, tb), 1)
        dk_acc[...] = jnp.zeros_like(dk_acc)
        dv_acc[...] = jnp.zeros_like(dv_acc)
        for qi in range(nb):
            rows = pl.ds(qi * tb, tb)
            dq_acc[...] = jnp.zeros_like(dq_acc)
            delta = jnp.sum(do_ref[rows, :].astype(F32) * o_ref[rows, :], axis=-1, keepdims=True)
            lse_q = lse_ref[rows, 0:1]

            def block(k0, masked, rows=rows, delta=delta, lse_q=lse_q):
                keys = pl.ds(k0, tb)
                q, k, do_ = q_ref[rows, :], k_ref[keys, :], do_ref[rows, :]
                s = _nt(q, k)
                if masked:
                    s = jnp.where(col <= row, s, NEG)
                p = jnp.exp(s * scale - lse_q)
                dp = _nt(do_, v_ref[keys, :])
                ds = (p * (dp - delta) * scale).astype(BF16)
                dv_acc[keys, :] += _tn(p.astype(BF16), do_)
                dk_acc[keys, :] += _tn(ds, q)
                dq_acc[...] += jnp.dot(ds, k, preferred_element_type=F32)

            _causal_blocks(qi, tb, block)
            dq_ref[rows, :] = dq_acc[...].astype(dq_ref.dtype)
        dk_ref[...] = dk_acc[...].astype(dk_ref.dtype)
        dv_ref[...] = dv_acc[...].astype(dv_ref.dtype)

    hp = pl.BlockSpec((T, HEAD_PAD), lambda h: (0, h))
    hv = pl.BlockSpec((T, V_DIM), lambda h: (0, h))
    return pl.pallas_call(
        body, name=name, grid=(H,),
        in_specs=[hp, hp, hv, hv, hv, pl.BlockSpec((T, LANES), lambda h: (0, h))],
        out_specs=[hp, hp, hv],
        out_shape=[S((T, H * HEAD_PAD), BF16), S((T, H * HEAD_PAD), BF16), S((T, H * V_DIM), BF16)],
        scratch_shapes=[pltpu.VMEM((tb, HEAD_PAD), F32), pltpu.VMEM((T, HEAD_PAD), F32), pltpu.VMEM((T, V_DIM), F32)],
        compiler_params=_cparams(("parallel",)),
    )(qf, kf, v, do, o, lse)


CONV_ROWS = 256
CONV_COLS = 128


def _conv_chunks(T):
    rc = min(CONV_ROWS, T)
    return [(r, rc) for r in range(0, T, rc)]


def _causal_conv(pad_ref, lead, w_ref, width, r0, rc):
    acc = None
    for k in range(width):
        term = w_ref[k:k + 1, :] * pad_ref[pl.ds(lead + r0 - (width - 1) + k, rc), :]
        acc = term if acc is None else acc + term
    return acc


def _anticausal_conv(pad_ref, w_ref, width, r0, rc):
    acc = None
    for k in range(width):
        term = w_ref[k:k + 1, :] * pad_ref[pl.ds(r0 + (width - 1) - k, rc), :]
        acc = term if acc is None else acc + term
    return acc


def _conv_wgrad(dpad_ref, xpad_ref, lead, width, T, dw_ref):
    for k in range(width):
        tot = None
        for r0, rc in _conv_chunks(T):
            part = jnp.sum(dpad_ref[pl.ds(r0, rc), :] * xpad_ref[pl.ds(lead + r0 - (width - 1) + k, rc), :],
                           axis=0, keepdims=True)
            tot = part if tot is None else tot + part
        dw_ref[k:k + 1, :] = tot


def _sc_fwd(p3, wconv, name):
    _, T, W = p3.shape
    width = wconv.shape[0]
    cw = min(CONV_COLS, W)
    lead = 8

    def body(p_ref, w_ref, g_ref, pad):
        pad[0:lead, :] = jnp.zeros((lead, cw), F32)
        for r0, rc in _conv_chunks(T):
            pad[pl.ds(lead + r0, rc), :] = p_ref[1, pl.ds(r0, rc), :] * p_ref[2, pl.ds(r0, rc), :]
        for r0, rc in _conv_chunks(T):
            rows = pl.ds(r0, rc)
            y = p_ref[0, rows, :] * _causal_conv(pad, lead, w_ref, width, r0, rc)
            g_ref[rows, :] = (y * _silu(p_ref[3, rows, :])).astype(g_ref.dtype)

    return pl.pallas_call(
        body, name=name, grid=(W // cw,),
        in_specs=[pl.BlockSpec((4, T, cw), lambda j: (0, 0, j)), pl.BlockSpec((width, cw), lambda j: (0, j))],
        out_specs=pl.BlockSpec((T, cw), lambda j: (0, j)),
        out_shape=S((T, W), BF16),
        scratch_shapes=[pltpu.VMEM((T + lead, cw), F32)],
        compiler_params=_cparams(("parallel",)),
    )(p3, wconv)


def _sc_bwd(p3, wconv, dg, name):
    _, T, W = p3.shape
    width = wconv.shape[0]
    cw = min(CONV_COLS, W)
    lead = 8

    def body(p_ref, w_ref, dg_ref, dp_ref, dw_ref, cupad, dvpad):
        cupad[0:lead, :] = jnp.zeros((lead, cw), F32)
        dvpad[pl.ds(T, lead), :] = jnp.zeros((lead, cw), F32)
        for r0, rc in _conv_chunks(T):
            cupad[pl.ds(lead + r0, rc), :] = p_ref[1, pl.ds(r0, rc), :] * p_ref[2, pl.ds(r0, rc), :]
        for r0, rc in _conv_chunks(T):
            rows = pl.ds(r0, rc)
            b, z, dg_ = p_ref[0, rows, :], p_ref[3, rows, :], dg_ref[rows, :]
            v = _causal_conv(cupad, lead, w_ref, width, r0, rc)
            dy = dg_ * _silu(z)
            dp_ref[3, rows, :] = (dg_ * b * v * _dsilu(z)).astype(dp_ref.dtype)
            dp_ref[0, rows, :] = (dy * v).astype(dp_ref.dtype)
            dvpad[rows, :] = dy * b
        for r0, rc in _conv_chunks(T):
            rows = pl.ds(r0, rc)
            dcu = _anticausal_conv(dvpad, w_ref, width, r0, rc)
            dp_ref[1, rows, :] = (dcu * p_ref[2, rows, :]).astype(dp_ref.dtype)
            dp_ref[2, rows, :] = (dcu * p_ref[1, rows, :]).astype(dp_ref.dtype)
        _conv_wgrad(dvpad, cupad, lead, width, T, dw_ref)

    return pl.pallas_call(
        body, name=name, grid=(W // cw,),
        in_specs=[pl.BlockSpec((4, T, cw), lambda j: (0, 0, j)), pl.BlockSpec((width, cw), lambda j: (0, j)),
                  pl.BlockSpec((T, cw), lambda j: (0, j))],
        out_specs=[pl.BlockSpec((4, T, cw), lambda j: (0, 0, j)), pl.BlockSpec((width, cw), lambda j: (0, j))],
        out_shape=[S((4, T, W), BF16), S((width, W), F32)],
        scratch_shapes=[pltpu.VMEM((T + lead, cw), F32), pltpu.VMEM((T + lead, cw), F32)],
        compiler_params=_cparams(("parallel",)),
    )(p3, wconv, dg)


def _gm_common(p_ref, lng_ref, lnb_ref):
    ug = _gelu(p_ref[0])
    vg = _gelu(p_ref[1])
    mu = jnp.mean(vg, axis=-1, keepdims=True)
    xc = vg - mu
    rstd = lax.rsqrt(jnp.mean(xc * xc, axis=-1, keepdims=True) + LN_EPS)
    xhat = xc * rstd
    vn = xhat * lng_ref[...] + lnb_ref[...]
    return ug, xhat, rstd, vn


def _gm_mix_weights(ws_ref, g):
    row = lax.broadcasted_iota(jnp.int32, (GM_CHUNK, GM_CHUNK), 0)
    col = lax.broadcasted_iota(jnp.int32, (GM_CHUNK, GM_CHUNK), 1)
    return jnp.where(col <= row, ws_ref[g], 0.0).astype(BF16)


def _gm_fwd(p3, lng, lnb, ws, bs_t, name):
    _, T, W = p3.shape
    gw = W // GM_GROUPS

    def body(p_ref, lng_ref, lnb_ref, ws_ref, bs_ref, g_ref):
        ug, _, _, vn = _gm_common(p_ref, lng_ref, lnb_ref)
        sz = _silu(p_ref[2])
        vnb = vn.astype(BF16)
        for g in range(GM_GROUPS):
            cols = slice(g * gw, (g + 1) * gw)
            s = jnp.dot(_gm_mix_weights(ws_ref, g), vnb[:, cols], preferred_element_type=F32) + bs_ref[:, g:g + 1]
            g_ref[:, cols] = (ug[:, cols] * s * sz[:, cols]).astype(g_ref.dtype)

    return pl.pallas_call(
        body, name=name, grid=(T // GM_CHUNK,),
        in_specs=[pl.BlockSpec((3, GM_CHUNK, W), lambda i: (0, i, 0)), pl.BlockSpec((1, W), lambda i: (0, 0)),
                  pl.BlockSpec((1, W), lambda i: (0, 0)),
                  pl.BlockSpec((GM_GROUPS, GM_CHUNK, GM_CHUNK), lambda i: (0, 0, 0)),
                  pl.BlockSpec((GM_CHUNK, GM_GROUPS), lambda i: (0, 0))],
        out_specs=pl.BlockSpec((GM_CHUNK, W), lambda i: (i, 0)),
        out_shape=S((T, W), BF16), compiler_params=_cparams(("parallel",)),
    )(p3, lng.reshape(1, W), lnb.reshape(1, W), ws, bs_t)


def _gm_bwd(p3, lng, lnb, ws, bs_t, dg, name):
    _, T, W = p3.shape
    gw = W // GM_GROUPS

    def body(p_ref, lng_ref, lnb_ref, ws_ref, bs_ref, dg_ref, dp_ref, dlng_ref, dlnb_ref, dws_ref, dbs_ref, dvn_s):
        i = pl.program_id(0)

        @pl.when(i == 0)
        def _():
            dlng_ref[...] = jnp.zeros_like(dlng_ref)
            dlnb_ref[...] = jnp.zeros_like(dlnb_ref)
            dws_ref[...] = jnp.zeros_like(dws_ref)
            dbs_ref[...] = jnp.zeros_like(dbs_ref)

        ug, xhat, rstd, vn = _gm_common(p_ref, lng_ref, lnb_ref)
        z = p_ref[2]
        dg_ = dg_ref[...]
        dy = dg_ * _silu(z)
        vnb = vn.astype(BF16)
        row = lax.broadcasted_iota(jnp.int32, (GM_CHUNK, GM_CHUNK), 0)
        col = lax.broadcasted_iota(jnp.int32, (GM_CHUNK, GM_CHUNK), 1)
        dbs = jnp.zeros((GM_CHUNK, LANES), F32)
        for g in range(GM_GROUPS):
            cols = slice(g * gw, (g + 1) * gw)
            wm = _gm_mix_weights(ws_ref, g)
            s = jnp.dot(wm, vnb[:, cols], preferred_element_type=F32) + bs_ref[:, g:g + 1]
            dp_ref[2, :, cols] = (dg_[:, cols] * ug[:, cols] * s * _dsilu(z[:, cols])).astype(dp_ref.dtype)
            dp_ref[0, :, cols] = (dy[:, cols] * s * _dgelu(p_ref[0, :, cols])).astype(dp_ref.dtype)
            ds = dy[:, cols] * ug[:, cols]
            dsb = ds.astype(BF16)
            dwm = lax.dot_general(dsb, vnb[:, cols], (((1,), (1,)), ((), ())), preferred_element_type=F32)
            dws_ref[g] += jnp.where(col <= row, dwm, 0.0)
            dbs = dbs + jnp.where(col == g, jnp.sum(ds, axis=-1, keepdims=True), 0.0)
            dvn_s[:, cols] = lax.dot_general(wm, dsb, (((0,), (0,)), ((), ())), preferred_element_type=F32)
        dbs_ref[...] += dbs
        dvn = dvn_s[...]
        dlng_ref[...] += jnp.sum(dvn * xhat, axis=0, keepdims=True)
        dlnb_ref[...] += jnp.sum(dvn, axis=0, keepdims=True)
        dxh = dvn * lng_ref[...]
        dvg = rstd * (dxh - jnp.mean(dxh, axis=-1, keepdims=True) - xhat * jnp.mean(dxh * xhat, axis=-1, keepdims=True))
        dp_ref[1] = (dvg * _dgelu(p_ref[1])).astype(dp_ref.dtype)

    row1 = pl.BlockSpec((1, W), lambda i: (0, 0))
    return pl.pallas_call(
        body, name=name, grid=(T // GM_CHUNK,),
        in_specs=[pl.BlockSpec((3, GM_CHUNK, W), lambda i: (0, i, 0)), row1, row1,
                  pl.BlockSpec((GM_GROUPS, GM_CHUNK, GM_CHUNK), lambda i: (0, 0, 0)),
                  pl.BlockSpec((GM_CHUNK, GM_GROUPS), lambda i: (0, 0)),
                  pl.BlockSpec((GM_CHUNK, W), lambda i: (i, 0))],
        out_specs=[pl.BlockSpec((3, GM_CHUNK, W), lambda i: (0, i, 0)), row1, row1,
                   pl.BlockSpec((GM_GROUPS, GM_CHUNK, GM_CHUNK), lambda i: (0, 0, 0)),
                   pl.BlockSpec((GM_CHUNK, LANES), lambda i: (0, 0))],
        out_shape=[S((3, T, W), BF16), S((1, W), F32), S((1, W), F32),
                   S((GM_GROUPS, GM_CHUNK, GM_CHUNK), F32), S((GM_CHUNK, LANES), F32)],
        scratch_shapes=[pltpu.VMEM((GM_CHUNK, W), F32)],
        compiler_params=_cparams(("arbitrary",)),
    )(p3, lng.reshape(1, W), lnb.reshape(1, W), ws, bs_t, dg)


def _cf_conv_fwd(p3, wdw, bdw, name):
    _, T, W = p3.shape
    width = wdw.shape[0]
    cw = min(CONV_COLS, W)
    lead = 32

    def body(p_ref, w_ref, b_ref, y_ref, pad):
        pad[0:lead, :] = jnp.zeros((lead, cw), F32)
        for r0, rc in _conv_chunks(T):
            rows = pl.ds(r0, rc)
            pad[pl.ds(lead + r0, rc), :] = p_ref[0, rows, :] * jax.nn.sigmoid(p_ref[1, rows, :])
        for r0, rc in _conv_chunks(T):
            y_ref[pl.ds(r0, rc), :] = _causal_conv(pad, lead, w_ref, width, r0, rc) + b_ref[...]

    return pl.pallas_call(
        body, name=name, grid=(W // cw,),
        in_specs=[pl.BlockSpec((2, T, cw), lambda j: (0, 0, j)), pl.BlockSpec((width, cw), lambda j: (0, j)),
                  pl.BlockSpec((1, cw), lambda j: (0, j))],
        out_specs=pl.BlockSpec((T, cw), lambda j: (0, j)),
        out_shape=S((T, W), F32),
        scratch_shapes=[pltpu.VMEM((T + lead, cw), F32)],
        compiler_params=_cparams(("parallel",)),
    )(p3, wdw, bdw.reshape(1, W))


def _cf_ln(y1_ref, lng_ref, lnb_ref):
    y1 = y1_ref[...]
    mu = jnp.mean(y1, axis=-1, keepdims=True)
    xc = y1 - mu
    rstd = lax.rsqrt(jnp.mean(xc * xc, axis=-1, keepdims=True) + LN_EPS)
    xhat = xc * rstd
    return xhat, rstd, xhat * lng_ref[...] + lnb_ref[...]


def _cf_gate_fwd(y1, p3, lng, lnb, name):
    T, W = y1.shape
    tr = _pick(T, 256)

    def body(y1_ref, z_ref, lng_ref, lnb_ref, g_ref):
        _, _, y2 = _cf_ln(y1_ref, lng_ref, lnb_ref)
        g_ref[...] = (_silu(y2) * _silu(z_ref[...])).astype(g_ref.dtype)

    row1 = pl.BlockSpec((1, W), lambda i: (0, 0))
    return pl.pallas_call(
        body, name=name, grid=(T // tr,),
        in_specs=[pl.BlockSpec((tr, W), lambda i: (i, 0)), pl.BlockSpec((None, tr, W), lambda i: (2, i, 0)), row1, row1],
        out_specs=pl.BlockSpec((tr, W), lambda i: (i, 0)),
        out_shape=S((T, W), BF16), compiler_params=_cparams(("parallel",)),
    )(y1, p3, lng.reshape(1, W), lnb.reshape(1, W))


def _cf_gate_bwd(y1, p3, lng, lnb, dg, name):
    T, W = y1.shape
    tr = _pick(T, 128)

    def body(y1_ref, z_ref, lng_ref, lnb_ref, dg_ref, dz_ref, dy1_ref, dlng_ref, dlnb_ref):
        i = pl.program_id(0)

        @pl.when(i == 0)
        def _():
            dlng_ref[...] = jnp.zeros_like(dlng_ref)
            dlnb_ref[...] = jnp.zeros_like(dlnb_ref)

        xhat, rstd, y2 = _cf_ln(y1_ref, lng_ref, lnb_ref)
        z, dg_ = z_ref[...], dg_ref[...]
        dz_ref[...] = (dg_ * _silu(y2) * _dsilu(z)).astype(dz_ref.dtype)
        dy2 = dg_ * _silu(z) * _dsilu(y2)
        dlng_ref[...] += jnp.sum(dy2 * xhat, axis=0, keepdims=True)
        dlnb_ref[...] += jnp.sum(dy2, axis=0, keepdims=True)
        dxh = dy2 * lng_ref[...]
        dy1_ref[...] = rstd * (dxh - jnp.mean(dxh, axis=-1, keepdims=True)
                               - xhat * jnp.mean(dxh * xhat, axis=-1, keepdims=True))

    row1 = pl.BlockSpec((1, W), lambda i: (0, 0))
    blk = pl.BlockSpec((tr, W), lambda i: (i, 0))
    return pl.pallas_call(
        body, name=name, grid=(T // tr,),
        in_specs=[blk, pl.BlockSpec((None, tr, W), lambda i: (2, i, 0)), row1, row1, blk],
        out_specs=[blk, blk, row1, row1],
        out_shape=[S((T, W), BF16), S((T, W), F32), S((1, W), F32), S((1, W), F32)],
        compiler_params=_cparams(("arbitrary",)),
    )(y1, p3, lng.reshape(1, W), lnb.reshape(1, W), dg)


def _cf_conv_bwd(p3, wdw, dy1, dz, name):
    _, T, W = p3.shape
    width = wdw.shape[0]
    cw = min(CONV_COLS, W)
    lead = 32

    def body(p_ref, w_ref, dy1_ref, dz_ref, dp_ref, dw_ref, db_ref, y0pad, dpad):
        y0pad[0:lead, :] = jnp.zeros((lead, cw), F32)
        dpad[pl.ds(T, lead), :] = jnp.zeros((lead, cw), F32)
        bsum = None
        for r0, rc in _conv_chunks(T):
            rows = pl.ds(r0, rc)
            y0pad[pl.ds(lead + r0, rc), :] = p_ref[0, rows, :] * jax.nn.sigmoid(p_ref[1, rows, :])
            d = dy1_ref[rows, :]
            dpad[rows, :] = d
            part = jnp.sum(d, axis=0, keepdims=True)
            bsum = part if bsum is None else bsum + part
        db_ref[...] = bsum
        for r0, rc in _conv_chunks(T):
            rows = pl.ds(r0, rc)
            dy0 = _anticausal_conv(dpad, w_ref, width, r0, rc)
            a = p_ref[0, rows, :]
            sg = jax.nn.sigmoid(p_ref[1, rows, :])
            dp_ref[0, rows, :] = (dy0 * sg).astype(dp_ref.dtype)
            dp_ref[1, rows, :] = (dy0 * a * sg * (1.0 - sg)).astype(dp_ref.dtype)
            dp_ref[2, rows, :] = dz_ref[rows, :]
        _conv_wgrad(dpad, y0pad, lead, width, T, dw_ref)

    return pl.pallas_call(
        body, name=name, grid=(W // cw,),
        in_specs=[pl.BlockSpec((2, T, cw), lambda j: (0, 0, j)), pl.BlockSpec((width, cw), lambda j: (0, j)),
                  pl.BlockSpec((T, cw), lambda j: (0, j)), pl.BlockSpec((T, cw), lambda j: (0, j))],
        out_specs=[pl.BlockSpec((3, T, cw), lambda j: (0, 0, j)), pl.BlockSpec((width, cw), lambda j: (0, j)),
                   pl.BlockSpec((1, cw), lambda j: (0, j))],
        out_shape=[S((3, T, W), BF16), S((width, W), F32), S((1, W), F32)],
        scratch_shapes=[pltpu.VMEM((T + lead, cw), F32), pltpu.VMEM((T + lead, cw), F32)],
        compiler_params=_cparams(("parallel",)),
    )(p3, wdw, dy1, dz)


def _rows_call(body, ins, out_dtypes, name, row_pref=256):
    R, C = ins[0].shape
    tr = _pick(R, row_pref) if R % 8 == 0 else R
    while tr > 8 and tr * C * 4 * (len(ins) + len(out_dtypes)) * 2 > VMEM_LIMIT // 2 and tr % 16 == 0:
        tr //= 2
    blk = pl.BlockSpec((tr, C), lambda i: (i, 0))
    return pl.pallas_call(
        body, name=name, grid=(R // tr,), in_specs=[blk] * len(ins), out_specs=[blk] * len(out_dtypes),
        out_shape=[S((R, C), dt) for dt in out_dtypes], compiler_params=_cparams(("parallel",)),
    )(*ins)


def _pair_sum(g, r, core, name):
    J, K, n = g.shape
    kh = K // 2
    tr = _pick(kh, 256)
    nb = kh // tr

    def body(core_ref, g_ref, r_ref, o_ref):
        o_ref[...] = (g_ref[...] + r_ref[...]).astype(BF16)

    return pl.pallas_call(
        body, name=name,
        grid_spec=pltpu.PrefetchScalarGridSpec(
            num_scalar_prefetch=1, grid=(J, nb),
            in_specs=[pl.BlockSpec((None, tr, n), lambda j, i, core_ref: (j, core_ref[0] * nb + i, 0)),
                      pl.BlockSpec((None, tr, n), lambda j, i, core_ref: (j, i, 0))],
            out_specs=pl.BlockSpec((None, tr, n), lambda j, i, core_ref: (j, i, 0))),
        out_shape=S((J, kh, n), BF16),
        compiler_params=_cparams(("parallel", "parallel")),
    )(core, g, r)


def _chip_sum(rc, core, name):
    J, R, C = rc.shape
    tr = _pick(R, 256)
    nb = R // tr

    def body(core_ref, r_ref, o_ref):
        acc = r_ref[0].astype(F32)
        for j in range(1, J):
            acc = acc + r_ref[j].astype(F32)
        o_ref[...] = acc

    return pl.pallas_call(
        body, name=name,
        grid_spec=pltpu.PrefetchScalarGridSpec(
            num_scalar_prefetch=1, grid=(nb,),
            in_specs=[pl.BlockSpec((J, tr, C), lambda i, core_ref: (0, i, 0))],
            out_specs=pl.BlockSpec((tr, C), lambda i, core_ref: (core_ref[0] * nb + i, 0))),
        out_shape=S((2 * R, C), F32), compiler_params=_cparams(("parallel",)),
    )(core, rc)


def _slot_sum(slots, name):
    J, R, C = slots.shape
    tr = _pick(R, 512)

    def body(r_ref, o_ref):
        acc = r_ref[0]
        for j in range(1, J):
            acc = acc + r_ref[j]
        o_ref[...] = acc

    return pl.pallas_call(
        body, name=name, grid=(R // tr,),
        in_specs=[pl.BlockSpec((J, tr, C), lambda i: (0, i, 0))], out_specs=pl.BlockSpec((tr, C), lambda i: (i, 0)),
        out_shape=S((R, C), F32), compiler_params=_cparams(("parallel",)),
    )(slots)


def _adamw(w, g, m, v, name):
    def body(w_ref, g_ref, m_ref, v_ref, d_ref, nm_ref, nv_ref):
        g_ = g_ref[...]
        nm = ADAM_B1 * m_ref[...] + (1.0 - ADAM_B1) * g_
        nv = ADAM_B2 * v_ref[...] + (1.0 - ADAM_B2) * (g_ * g_)
        m_hat = nm / (1.0 - ADAM_B1 ** ADAM_STEP)
        v_hat = nv / (1.0 - ADAM_B2 ** ADAM_STEP)
        d_ref[...] = -ADAM_LR * (m_hat / (jnp.sqrt(v_hat) + ADAM_EPS) + ADAM_WD * w_ref[...])
        nm_ref[...] = nm
        nv_ref[...] = nv
    return _rows_call(body, [w, g, m, v], [F32, F32, F32], name)


ANY = pl.BlockSpec(memory_space=pl.ANY)


def _place():
    x, y, c = lax.axis_index("x"), lax.axis_index("y"), lax.axis_index("c")
    return x, y, c


def _other_chips(x, y):
    return [(1 - x, y), (x, 1 - y), (1 - x, 1 - y)]


HBM = pl.BlockSpec(memory_space=pltpu.HBM)
SEM = pl.BlockSpec(memory_space=pltpu.SEMAPHORE)
DATAFLOW = pltpu.SideEffectType.DATAFLOW_SIDE_EFFECTING


def _in_hbm(a):
    return pltpu.with_memory_space_constraint(a, pltpu.HBM)


def _half_rows(k_rows, which):
    return pl.ds(which * (k_rows // 2), k_rows // 2)


COPIES_PER_ARRAY = {"gather": 3, "scatter": 3, "halves": 1, "forward": 3, "join": 1, "everyone": N_DEV - 1}


def _split_copies(srcs, lands, send_sems, recv_sems, mode):
    x, y, c = _place()
    me_chip = 2 * x + y
    out = []
    for w, (src, land) in enumerate(zip(srcs, lands)):
        if mode == "join":
            mine, theirs = _half_rows(land.shape[0], c), _half_rows(land.shape[0], 1 - c)
            sems = dict(send_sem=send_sems.at[w], recv_sem=recv_sems.at[w], device_id=(x, y, 1 - c), device_id_type=MESH_ID)
            out.append((pltpu.make_async_remote_copy(src_ref=land.at[mine], dst_ref=land.at[mine], **sems),
                        pltpu.make_async_remote_copy(src_ref=land.at[mine], dst_ref=land.at[theirs], **sems)))
            continue
        if mode == "everyone":
            for k in range(1, N_DEV):
                px, py, pc = x ^ ((k >> 2) & 1), y ^ ((k >> 1) & 1), c ^ (k & 1)
                sems = dict(send_sem=send_sems.at[(N_DEV - 1) * w + k - 1], recv_sem=recv_sems.at[(N_DEV - 1) * w + k - 1],
                            device_id=(px, py, pc), device_id_type=MESH_ID)
                out.append((pltpu.make_async_remote_copy(src_ref=src, dst_ref=land.at[4 * x + 2 * y + c], **sems),
                            pltpu.make_async_remote_copy(src_ref=src, dst_ref=land.at[4 * px + 2 * py + pc], **sems)))
            continue
        if mode == "forward":
            mine, theirs = _half_rows(land.shape[1], c), _half_rows(land.shape[1], 1 - c)
            for j, (cx, cy) in enumerate(_other_chips(x, y)):
                them = 2 * cx + cy
                sems = dict(send_sem=send_sems.at[3 * w + j], recv_sem=recv_sems.at[3 * w + j],
                            device_id=(x, y, 1 - c), device_id_type=MESH_ID)
                out.append((pltpu.make_async_remote_copy(src_ref=land.at[them, mine], dst_ref=land.at[them, mine], **sems),
                            pltpu.make_async_remote_copy(src_ref=land.at[them, mine], dst_ref=land.at[them, theirs], **sems)))
            continue
        if mode == "halves":
            rows = _half_rows(src.shape[1], 1 - c)
            cp = pltpu.make_async_remote_copy(src_ref=src.at[:, rows, :], dst_ref=land, send_sem=send_sems.at[w],
                                              recv_sem=recv_sems.at[w], device_id=(x, y, 1 - c), device_id_type=MESH_ID)
            out.append((cp, cp))
            continue
        for j, (cx, cy) in enumerate(_other_chips(x, y)):
            them = 2 * cx + cy
            sems = dict(send_sem=send_sems.at[3 * w + j], recv_sem=recv_sems.at[3 * w + j],
                        device_id=(cx, cy, c), device_id_type=MESH_ID)
            if mode == "gather":
                rows = _half_rows(src.shape[0], c)
                go = pltpu.make_async_remote_copy(src_ref=src.at[rows], dst_ref=land.at[me_chip, rows], **sems)
                arrive = pltpu.make_async_remote_copy(src_ref=src.at[rows], dst_ref=land.at[them, rows], **sems)
            else:
                go = pltpu.make_async_remote_copy(src_ref=src.at[them], dst_ref=land.at[me_chip], **sems)
                arrive = pltpu.make_async_remote_copy(src_ref=src.at[them], dst_ref=land.at[them], **sems)
            out.append((go, arrive))
    return out


def _split_copies_start(srcs, lands, mode, name, after=None):
    nw = len(lands)
    ns = 0 if srcs is None else nw
    arrays = list(lands) if srcs is None else list(srcs) + list(lands)
    n_sems = COPIES_PER_ARRAY[mode] * nw
    n_in = ns + nw + (after is not None)

    def body(*refs):
        lnd = refs[ns:ns + nw]
        ins = refs[:ns] if ns else lnd
        send_sems, recv_sems, token = refs[n_in], refs[n_in + 1], refs[-1]
        for go, _ in _split_copies(ins, lnd, send_sems, recv_sems, mode):
            go.start()
        token[...] = jnp.zeros_like(token)

    thru = [pltpu.HBM(a.shape, a.dtype) for a in arrays]
    res = pl.pallas_call(
        body, name=name, in_specs=[HBM] * (ns + nw) + [ANY] * (after is not None),
        out_specs=[SEM, SEM] + [HBM] * (ns + nw) + [pl.BlockSpec(memory_space=pltpu.VMEM)],
        out_shape=[pltpu.SemaphoreType.DMA((n_sems,)), pltpu.SemaphoreType.DMA((n_sems,))] + thru + [S((8, LANES), F32)],
        input_output_aliases={i: 2 + i for i in range(ns + nw)},
        compiler_params=pltpu.CompilerParams(has_side_effects=DATAFLOW),
    )(*[_in_hbm(a) for a in arrays], *([after] if after is not None else []))
    return res[0], res[1], list(res[2:2 + ns]), list(res[2 + ns:2 + ns + nw]), res[-1]


def _split_copies_wait(send_sems, recv_sems, srcs, lands, after, mode, name):
    nw, ns = len(lands), len(srcs)
    arrays = list(srcs) + list(lands)

    def body(*refs):
        lnd = refs[ns:ns + nw]
        ins = refs[:ns] if ns else lnd
        send, recv = refs[ns + nw], refs[ns + nw + 1]
        for _, arrive in _split_copies(ins, lnd, send, recv, mode):
            arrive.wait_send()
            arrive.wait_recv()

    res = pl.pallas_call(
        body, name=name, in_specs=[HBM] * (ns + nw) + [SEM, SEM, ANY], out_specs=[HBM] * (ns + nw),
        out_shape=[pltpu.HBM(a.shape, a.dtype) for a in arrays],
        input_output_aliases={i: i for i in range(ns + nw)},
        compiler_params=pltpu.CompilerParams(has_side_effects=DATAFLOW),
    )(*arrays, send_sems, recv_sems, after)
    return list(res[:ns]), list(res[ns:])


def _gather_forward(lands, name):
    nw = len(lands)

    def body(*refs):
        ins, outs = refs[:nw], refs[nw:2 * nw]
        send_sems, recv_sems = refs[2 * nw:]
        x, y, c = _place()
        sibling = (x, y, 1 - c)
        cps = []
        for w in range(nw):
            kr = ins[w].shape[1]
            for j, (cx, cy) in enumerate(_other_chips(x, y)):
                them = 2 * cx + cy
                sems = dict(send_sem=send_sems.at[3 * w + j], recv_sem=recv_sems.at[3 * w + j],
                            device_id=sibling, device_id_type=MESH_ID)
                mine, theirs = _half_rows(kr, c), _half_rows(kr, 1 - c)
                go = pltpu.make_async_remote_copy(src_ref=ins[w].at[them, mine], dst_ref=outs[w].at[them, mine], **sems)
                go.start()
                cps.append((go, pltpu.make_async_remote_copy(
                    src_ref=ins[w].at[them, theirs], dst_ref=outs[w].at[them, theirs], **sems)))
        for go, arrive in cps:
            arrive.wait_recv()
            go.wait_send()

    return pl.pallas_call(
        body, name=name, in_specs=[ANY] * nw, out_specs=[ANY] * nw,
        out_shape=[S(a.shape, a.dtype) for a in lands],
        scratch_shapes=[pltpu.SemaphoreType.DMA((3 * nw,)), pltpu.SemaphoreType.DMA((3 * nw,))],
        input_output_aliases={i: i for i in range(nw)},
        compiler_params=pltpu.CompilerParams(has_side_effects=True),
    )(*lands)


def _broadcast_all(buf, name):
    def body(in_ref, out_ref, send_sems, recv_sems, loc_sem):
        x, y, c = _place()
        me = 4 * x + 2 * y + c
        loc = pltpu.make_async_copy(in_ref, out_ref.at[me], loc_sem)
        loc.start()
        cps = []
        for k in range(1, N_DEV):
            fx, fy, fc = (k >> 2) & 1, (k >> 1) & 1, k & 1
            px, py, pc = x ^ fx, y ^ fy, c ^ fc
            cp = pltpu.make_async_remote_copy(
                src_ref=in_ref, dst_ref=out_ref.at[me], send_sem=send_sems.at[k - 1], recv_sem=recv_sems.at[k - 1],
                device_id=(px, py, pc), device_id_type=MESH_ID)
            cp.start()
            cps.append(cp)
        for k in range(1, N_DEV):
            fx, fy, fc = (k >> 2) & 1, (k >> 1) & 1, k & 1
            px, py, pc = x ^ fx, y ^ fy, c ^ fc
            slot = out_ref.at[4 * px + 2 * py + pc]
            pltpu.make_async_remote_copy(
                src_ref=slot, dst_ref=slot, send_sem=send_sems.at[k - 1], recv_sem=recv_sems.at[k - 1],
                device_id=(px, py, pc), device_id_type=MESH_ID).wait_recv()
        for cp in cps:
            cp.wait_send()
        loc.wait()

    return pl.pallas_call(
        body, name=name, in_specs=[ANY], out_specs=ANY,
        out_shape=S((N_DEV,) + buf.shape, buf.dtype),
        scratch_shapes=[pltpu.SemaphoreType.DMA((N_DEV - 1,)), pltpu.SemaphoreType.DMA((N_DEV - 1,)),
                        pltpu.SemaphoreType.DMA],
        compiler_params=pltpu.CompilerParams(has_side_effects=True),
    )(buf)


PACK_ALIGN = 8 * LANES


def _pack(arrs):
    flat = []
    for a in arrs:
        f = a.reshape(-1).astype(F32)
        pad = (-f.shape[0]) % PACK_ALIGN
        flat.append(jnp.pad(f, (0, pad)) if pad else f)
    return jnp.concatenate(flat).reshape(-1, LANES)


def _unpack(buf, shapes):
    out, off = [], 0
    flat = buf.reshape(-1)
    for shp in shapes:
        n = math.prod(shp)
        out.append(flat[off:off + n].reshape(shp))
        off += n + ((-n) % PACK_ALIGN)
    return out


def kernel(x, positions, norm_pre, norm_post, w_in_mla, mla_q_norm, w_uq, mla_kv_norm, w_ukv, w_out_mla, w_in_sc, sc_conv, w_out_sc, w_in_gm, gm_ln_g, gm_ln_b, gm_w_s, gm_b_s, w_out_gm, w_in_cf, cf_dw, cf_dw_b, cf_ln_g, cf_ln_b, w_out_cf, loss_target, m_norm_pre, m_norm_post, m_w_in_mla, m_mla_q_norm, m_w_uq, m_mla_kv_norm, m_w_ukv, m_w_out_mla, m_w_in_sc, m_sc_conv, m_w_out_sc, m_w_in_gm, m_gm_ln_g, m_gm_ln_b, m_gm_w_s, m_gm_b_s, m_w_out_gm, m_w_in_cf, m_cf_dw, m_cf_dw_b, m_cf_ln_g, m_cf_ln_b, m_w_out_cf, v_norm_pre, v_norm_post, v_w_in_mla, v_mla_q_norm, v_w_uq, v_mla_kv_norm, v_w_ukv, v_w_out_mla, v_w_in_sc, v_sc_conv, v_w_out_sc, v_w_in_gm, v_gm_ln_g, v_gm_ln_b, v_gm_w_s, v_gm_b_s, v_w_out_gm, v_w_in_cf, v_cf_dw, v_cf_dw_b, v_cf_ln_g, v_cf_ln_b, v_w_out_cf):
    loc = dict(locals())
    wts = {n: loc[n] for n in WEIGHTS}
    mom_m = {n: loc["m_" + n] for n in WEIGHTS}
    mom_v = {n: loc["v_" + n] for n in WEIGHTS}

    T, D = x.shape[1], x.shape[2]
    xin = x.reshape(T, D)
    target = loss_target.reshape(T, D)
    q_rank, kv_rank = mla_q_norm.shape[1], mla_kv_norm.shape[1]
    H = (w_uq.shape[2] * N_CHIPS) // (NOPE_DIM + ROPE_DIM)
    hv = H * V_DIM
    c_kr = q_rank + kv_rank
    wa_cols = c_kr + ROPE_DIM
    wa_pad = wa_cols + (LANES - ROPE_DIM)
    chip = 2 * lax.axis_index("x") + lax.axis_index("y")

    c = lax.axis_index("c")
    core = c.reshape(1).astype(jnp.int32)
    small_sh_shapes = [wts[n][0].shape for n in SMALL_SHARDED]
    small_slots = _broadcast_all(_pack([wts[n][0] for n in SMALL_SHARDED]), "gather_small")
    gather_started = []
    for gi, names in enumerate(GROUPS):
        own = [wts[n][0].astype(BF16) for n in names]
        lands = [lax.dynamic_update_slice(lax.empty((N_CHIPS,) + s.shape, BF16), s[None], (chip, 0, 0)) for s in own]
        gather_started.append(_split_copies_start(own, lands, "gather", f"gather_start_{gi}",
                                                  after=gather_started[-1][4] if gather_started else small_slots))
    started_token = sum(st[4][0, 0] for st in gather_started)
    gw = {}

    forward_started = {}

    def forward_ahead(gi, after):
        send_sems, recv_sems, own, lands, _ = gather_started[gi]
        _, landed = _split_copies_wait(send_sems, recv_sems, own, lands, after, "gather", f"gather_wait_{gi}")
        forward_started[gi] = _split_copies_start(None, landed, "forward", f"gather_forward_start_{gi}")
        return forward_started[gi][4][0, 0]

    def gathered_weights(gi, after):
        if gi in forward_started:
            send_sems, recv_sems, _, lands, _ = forward_started[gi]
            _, full = _split_copies_wait(send_sems, recv_sems, [], lands, after, "forward", f"gather_forward_wait_{gi}")
        else:
            send_sems, recv_sems, own, lands, _ = gather_started[gi]
            _, landed = _split_copies_wait(send_sems, recv_sems, own, lands, after, "gather", f"gather_wait_{gi}")
            full = _gather_forward(landed, f"gather_forward_{gi}")
        gw.update(zip(GROUPS[gi], full))

    per_chip = [_unpack(small_slots[2 * k], small_sh_shapes) for k in range(N_CHIPS)]
    sp = {n: jnp.concatenate([per_chip[k][i] for k in range(N_CHIPS)], axis=-1) for i, n in enumerate(SMALL_SHARDED)}

    def cols_major(w4):
        return jnp.transpose(w4, (1, 0, 2)).reshape(w4.shape[1], -1)

    half = ROPE_DIM // 2
    inv_freq = ROPE_THETA ** (-jnp.arange(half, dtype=F32) / half)
    invf = jnp.concatenate([inv_freq, inv_freq, jnp.zeros((LANES - ROPE_DIM,), F32)]).reshape(1, LANES)
    tabs = _rope_tables(positions.reshape(T, 1), invf, "rope_tables")
    scale = float(NOPE_DIM + ROPE_DIM) ** -0.5

    xs = [xin]
    saved = []
    w_out = {}
    for i in range(4):
        xi = xs[-1]
        h = _rms_fwd(xi[None], 0, 0, D, norm_pre[i] + started_token if i == 0 else norm_pre[i], f"pre_norm_{i}", BF16)
        gathered_weights(LAYER_GROUPS[i][0], h)
        ahead_token = 0.0
        if i == 0:
            w_in_full = cols_major(gw['w_in_mla'])
            w_a = jnp.pad(w_in_full[:, :wa_cols], ((0, 0), (0, wa_pad - wa_cols)))[None]
            w_z = w_in_full[:, wa_cols:][None]
            pa = _mm_nn(h, w_a, 1, "mla_in_a")
            pz = _mm_nn(h, w_z, 1, "mla_in_z")
            gathered_weights(LAYER_GROUPS[i][1], pz)
            wq = cols_major(gw['w_uq']).reshape(q_rank, H, NOPE_DIM + ROPE_DIM)
            wq = jnp.pad(wq, ((0, 0), (0, 0), (0, HEAD_PAD - NOPE_DIM - ROPE_DIM))).reshape(1, q_rank, H * HEAD_PAD)
            wkv = cols_major(gw['w_ukv'])[None]
            qn = _rms_fwd(pa, 0, 0, q_rank, mla_q_norm[0], "mla_q_norm", BF16)
            kvn = _rms_fwd(pa, 0, q_rank // kv_rank, kv_rank, mla_kv_norm[0], "mla_kv_norm", BF16)
            q3 = _mm_nn(qn, wq, 1, "mla_q_up")
            kv3 = _mm_nn(kvn, wkv, 1, "mla_kv_up")
            qf, kf, vv = _qkv_layout(q3, kv3, pa, c_kr // LANES, tabs, H, "mla_qkv_layout")
            o, lse = _attn_fwd(qf, kf, vv, H, scale, "mla_attn_fwd")
            ahead_token = forward_ahead(LAYER_GROUPS[1][0], o)
            g = _gate_fwd(o, pz, "mla_gate_fwd")
            saved.append(dict(h=h, pa=pa, pz=pz, qn=qn, kvn=kvn, qf=qf, kf=kf, vv=vv, o=o, lse=lse, g=g))
        elif i == 1:
            p3 = _mm_nn(h, gw['w_in_sc'], 4, "sc_in")
            ahead_token = forward_ahead(LAYER_GROUPS[2][0], p3)
            g = _sc_fwd(p3, sp['sc_conv'], "sc_mix_fwd")
            saved.append(dict(h=h, p3=p3, g=g))
        elif i == 2:
            p3 = _mm_nn(h, gw['w_in_gm'], 3, "gm_in")
            ahead_token = forward_ahead(LAYER_GROUPS[3][0], p3)
            bs_t = jnp.transpose(gm_b_s[0])
            g = _gm_fwd(p3, sp['gm_ln_g'], sp['gm_ln_b'], gm_w_s[0], bs_t, "gm_mix_fwd")
            saved.append(dict(h=h, p3=p3, g=g, bs_t=bs_t))
        else:
            p3 = _mm_nn(h, gw['w_in_cf'], 3, "cf_in")
            y1 = _cf_conv_fwd(p3, sp['cf_dw'], sp['cf_dw_b'], "cf_conv_fwd")
            g = _cf_gate_fwd(y1, p3, sp['cf_ln_g'], sp['cf_ln_b'], "cf_gate_fwd")
            saved.append(dict(h=h, p3=p3, y1=y1, g=g))
        w_out[WO_NAMES[i]] = gw[WO_NAMES[i]].reshape(1, -1, D)
        yo = _mm_nn(g, w_out[WO_NAMES[i]], 1, f"out_proj_{i}")
        saved[-1]['yo'] = yo
        xs.append(_rms_fwd(yo, 0, 0, D, norm_post[i] + ahead_token, f"post_norm_{i}", F32, res=xi))

    dx, loss_local = _loss_head(xs[4], target, "loss_head")
    loss = lax.psum(loss_local, ("x", "y", "c"))

    big_grads = {}
    sgrad = {}
    d_npre, d_npost = [None] * 4, [None] * 4
    exch_started, scatter_started, scattered = {}, {}, {}

    def start_exchange(gi):
        full = [big_grads[n] for n in GROUPS[gi]]
        lands = [lax.empty((g_.shape[0], g_.shape[1] // 2, g_.shape[2]), g_.dtype) for g_ in full]
        exch_started[gi] = _split_copies_start(full, lands, "halves", f"grads_exchange_start_{gi}")
        return exch_started[gi][4][0, 0]

    def start_scatter(gi, after):
        send_sems, recv_sems, full, lands, _ = exch_started[gi]
        full, recv = _split_copies_wait(send_sems, recv_sems, full, lands, after, "halves", f"grads_exchange_wait_{gi}")
        pair = [_pair_sum(g_, r, core, f"pair_sum_{n}") for n, g_, r in zip(GROUPS[gi], full, recv)]
        lands = [lax.dynamic_update_slice(lax.empty(p.shape, p.dtype), lax.dynamic_slice_in_dim(p, chip, 1, axis=0),
                                          (chip, 0, 0)) for p in pair]
        scatter_started[gi] = _split_copies_start(pair, lands, "scatter", f"scatter_start_{gi}")
        return scatter_started[gi][4][0, 0]

    def finish_scatter(gi, after):
        send_sems, recv_sems, pair, lands, _ = scatter_started[gi]
        scattered[gi] = _split_copies_wait(send_sems, recv_sems, pair, lands, after, "scatter", f"scatter_wait_{gi}")[1]

    token = 0.0
    for i in (3, 2, 1, 0):
        sv = saved[i]
        h = sv['h']
        dyo, d_npost[i] = _rms_bwd(sv['yo'], 0, 0, D, norm_post[i] + token, dx, f"post_norm_bwd_{i}", BF16)
        dyo3 = dyo[None]
        wo_name = WO_NAMES[i]
        dg = _mm_nt(dyo3, w_out[wo_name], f"out_proj_dx_{i}")
        big_grads[wo_name] = _mm_tn(sv['g'], dyo3, 1, f"out_proj_dw_{i}").reshape(N_CHIPS, -1, D)
        if i == 3:
            dz, dy1, sgrad['cf_ln_g'], sgrad['cf_ln_b'] = _cf_gate_bwd(sv['y1'], sv['p3'], sp['cf_ln_g'], sp['cf_ln_b'], dg, "cf_gate_bwd")
            dp3, sgrad['cf_dw'], sgrad['cf_dw_b'] = _cf_conv_bwd(sv['p3'], sp['cf_dw'], dy1, dz, "cf_conv_bwd")
            big_grads['w_in_cf'] = _mm_tn(h, dp3, N_CHIPS, "cf_in_dw")
            dh = _mm_nt(dp3, gw['w_in_cf'], "cf_in_dx")
        elif i == 2:
            dp3, sgrad['gm_ln_g'], sgrad['gm_ln_b'], sgrad['gm_w_s'], dbs_t = _gm_bwd(
                sv['p3'], sp['gm_ln_g'], sp['gm_ln_b'], gm_w_s[0], sv['bs_t'], dg, "gm_mix_bwd")
            sgrad['gm_b_s'] = jnp.transpose(dbs_t[:, :GM_GROUPS])
            big_grads['w_in_gm'] = _mm_tn(h, dp3, N_CHIPS, "gm_in_dw")
            dh = _mm_nt(dp3, gw['w_in_gm'], "gm_in_dx")
        elif i == 1:
            dp3, sgrad['sc_conv'] = _sc_bwd(sv['p3'], sp['sc_conv'], dg, "sc_mix_bwd")
            big_grads['w_in_sc'] = _mm_tn(h, dp3, N_CHIPS, "sc_in_dw")
            dh = _mm_nt(dp3, gw['w_in_sc'], "sc_in_dx")
        else:
            do, dpz = _gate_bwd(dg, sv['o'], sv['pz'], "mla_gate_bwd")
            dqf, dkf, dv = _attn_bwd(sv['qf'], sv['kf'], sv['vv'], do, sv['o'], sv['lse'], H, scale, "mla_attn_bwd")
            finish_scatter(3, dqf)
            token = start_scatter(2, dqf)
            dq3, dkv3, dkr = _qkv_layout_bwd(dqf, dkf, dv, (tabs[0] + token, tabs[1], tabs[2]), H, "mla_qkv_layout_bwd")
            dqn = _mm_nt(dq3, wq, "mla_q_up_dx")
            dwq = _mm_tn(sv['qn'], dq3, 1, "mla_q_up_dw")
            dkvn = _mm_nt(dkv3, wkv, "mla_kv_up_dx")
            dwkv = _mm_tn(sv['kvn'], dkv3, 1, "mla_kv_up_dw")
            dwq_ = dwq[0].reshape(q_rank, H, HEAD_PAD)[:, :, :NOPE_DIM + ROPE_DIM].reshape(q_rank, N_CHIPS, -1)
            big_grads['w_uq'] = jnp.transpose(dwq_, (1, 0, 2))
            big_grads['w_ukv'] = jnp.transpose(dwkv[0].reshape(kv_rank, N_CHIPS, -1), (1, 0, 2))
            token = start_exchange(1)
            dcq, dqg = _rms_bwd(sv['pa'], 0, 0, q_rank, mla_q_norm[0] + token, dqn, "mla_q_norm_bwd", BF16)
            dckv, dkvg = _rms_bwd(sv['pa'], 0, q_rank // kv_rank, kv_rank, mla_kv_norm[0], dkvn, "mla_kv_norm_bwd", BF16)
            sgrad['mla_q_norm'], sgrad['mla_kv_norm'] = dqg, dkvg
            dpa = jnp.concatenate([dcq, dckv, dkr], axis=1)[None]
            dwa = _mm_tn(h, dpa, 1, "mla_in_a_dw")
            dwz = _mm_tn(h, dpz, 1, "mla_in_z_dw")
            dh_a = _mm_nt(dpa, w_a, "mla_in_a_dx")
            dh = _mm_nt(dpz, w_z, "mla_in_z_dx", add=dh_a)
            dw_in = jnp.concatenate([dwa[0][:, :wa_cols], dwz[0]], axis=1)
            big_grads['w_in_mla'] = jnp.transpose(dw_in.reshape(D, N_CHIPS, -1), (1, 0, 2))
            token = start_scatter(1, dh) + start_exchange(0)
        dx, d_npre[i] = _rms_bwd(xs[i][None], 0, 0, D, norm_pre[i] + token if i == 0 else norm_pre[i], dh,
                                 f"pre_norm_bwd_{i}", F32, res=dx)
        if i == 3:
            token = start_exchange(4)
        elif i == 2:
            token = start_scatter(4, dx) + start_exchange(3)
        elif i == 1:
            finish_scatter(4, dx)
            token = start_scatter(3, dx) + start_exchange(2)
    grad_x = dx.reshape(1, T, D)
    sgrad['norm_pre'] = jnp.concatenate(d_npre, axis=0)
    sgrad['norm_post'] = jnp.concatenate(d_npost, axis=0)

    grads, delta, new_m, new_v = {}, {}, {}, {}

    join_started = {}

    def sum_group(gi):
        halves = [_chip_sum(r, core, f"chip_sum_{n}") for n, r in zip(GROUPS[gi], scattered[gi])]
        join_started[gi] = _split_copies_start(None, halves, "join", f"grads_join_start_{gi}")

    def update_group(gi, after):
        send_sems, recv_sems, _, halves, _ = join_started[gi]
        _, full = _split_copies_wait(send_sems, recv_sems, [], halves, after, "join", f"grads_join_wait_{gi}")
        for n, j in zip(GROUPS[gi], full):
            shp = wts[n].shape
            two_d = (shp[1], shp[2])
            d_, m_, v_ = _adamw(wts[n].reshape(two_d), j, mom_m[n].reshape(two_d), mom_v[n].reshape(two_d), f"adamw_{n}")
            grads[n], delta[n], new_m[n], new_v[n] = j[None], d_.reshape(shp), m_.reshape(shp), v_.reshape(shp)
        return d_

    small_full_shapes = [sgrad[n].reshape(wts[n].shape[:-1] + (-1,)).shape for n in SMALL]
    packed = _pack([sgrad[n] for n in SMALL])
    slots = lax.dynamic_update_slice(lax.empty((N_DEV,) + packed.shape, F32), packed[None], (2 * chip + c, 0, 0))
    small_started = _split_copies_start([packed], [slots], "everyone", "grads_small_start")
    finish_scatter(2, dx)
    start_scatter(0, dx)
    done = dx
    for gi in (4, 3, 2):
        sum_group(gi)
    for gi in (4, 3, 2):
        done = update_group(gi, done)
    finish_scatter(1, done)
    finish_scatter(0, done)
    sum_group(1)
    sum_group(0)
    done = update_group(1, done)
    done = update_group(0, done)
    _, (gslots,) = _split_copies_wait(*small_started[:4], done, "everyone", "grads_small_wait")
    gsum = _unpack(_slot_sum(gslots, "grads_small_sum"), small_full_shapes)
    for n, gs in zip(SMALL, gsum):
        if n in SMALL_SHARDED:
            per = wts[n].shape[-1]
            gs = lax.dynamic_slice_in_dim(gs, chip * per, per, axis=gs.ndim - 1)
        grads[n] = gs.reshape(wts[n].shape)

    shapes = [wts[n].shape for n in SMALL]
    d_, m_, v_ = _adamw(_pack([wts[n] for n in SMALL]), _pack([grads[n] for n in SMALL]),
                        _pack([mom_m[n] for n in SMALL]), _pack([mom_v[n] for n in SMALL]), "adamw_small")
    for n, a, b, cc in zip(SMALL, _unpack(d_, shapes), _unpack(m_, shapes), _unpack(v_, shapes)):
        delta[n], new_m[n], new_v[n] = a, b, cc

    return (loss, grad_x, *[grads[n] for n in WEIGHTS], *[delta[n] for n in WEIGHTS],
            *[new_m[n] for n in WEIGHTS], *[new_v[n] for n in WEIGHTS])
```

```python
import functools
import math

import jax
import jax.numpy as jnp
from jax import lax
from jax.experimental import pallas as pl
from jax.experimental.pallas import tpu as pltpu

F32, BF16 = jnp.float32, jnp.bfloat16
S = jax.ShapeDtypeStruct
MESH_ID = pl.DeviceIdType.MESH

V7X_VMEM_BYTES = 64 * 1024 * 1024
VMEM_LIMIT = V7X_VMEM_BYTES - 8 * 1024 * 1024
LANES = 128
N_CHIPS = 4
N_DEV = 8

NORM_EPS = 1e-6
LN_EPS = 1e-5
ROPE_THETA = 10000.0
ROPE_DIM = 64
NOPE_DIM = 128
V_DIM = 128
HEAD_PAD = 256
GM_CHUNK = 128
GM_GROUPS = 8
NEG = -1e30

ADAM_LR, ADAM_B1, ADAM_B2, ADAM_EPS, ADAM_WD, ADAM_STEP = 0.001, 0.9, 0.999, 1e-08, 0.01, 10

FWD_PARAMS = ['x', 'positions', 'norm_pre', 'norm_post', 'w_in_mla', 'mla_q_norm', 'w_uq', 'mla_kv_norm', 'w_ukv',
              'w_out_mla', 'w_in_sc', 'sc_conv', 'w_out_sc', 'w_in_gm', 'gm_ln_g', 'gm_ln_b', 'gm_w_s', 'gm_b_s',
              'w_out_gm', 'w_in_cf', 'cf_dw', 'cf_dw_b', 'cf_ln_g', 'cf_ln_b', 'w_out_cf']
WEIGHTS = FWD_PARAMS[2:]
BIG = ['w_in_mla', 'w_uq', 'w_ukv', 'w_out_mla', 'w_in_sc', 'w_out_sc', 'w_in_gm', 'w_out_gm', 'w_in_cf', 'w_out_cf']
GROUPS = [['w_in_mla'], ['w_uq', 'w_ukv', 'w_out_mla'], ['w_in_sc', 'w_out_sc'], ['w_in_gm', 'w_out_gm'],
          ['w_in_cf', 'w_out_cf']]
LAYER_GROUPS = [[0, 1], [2], [3], [4]]
WO_NAMES = ['w_out_mla', 'w_out_sc', 'w_out_gm', 'w_out_cf']
SMALL = [n for n in WEIGHTS if n not in BIG]
SMALL_SHARDED = ['sc_conv', 'gm_ln_g', 'gm_ln_b', 'cf_dw', 'cf_dw_b', 'cf_ln_g', 'cf_ln_b']


def _cparams(sem=None, **kw):
    return pltpu.CompilerParams(dimension_semantics=sem, vmem_limit_bytes=VMEM_LIMIT, **kw)


def _pick(dim, pref):
    if dim <= pref:
        return dim
    t = (pref // LANES) * LANES
    while t >= LANES and dim % t:
        t -= LANES
    if t >= min(pref, 512):
        return t
    return dim if (dim <= 2048 or t < LANES) else t


def _silu(x):
    return x * jax.nn.sigmoid(x)


def _dsilu(x):
    s = jax.nn.sigmoid(x)
    return s * (1.0 + x * (1.0 - s))


def _gelu(x):
    return 0.5 * x * (1.0 + lax.erf(x * (2.0 ** -0.5)))


def _dgelu(x):
    cdf = 0.5 * (1.0 + lax.erf(x * (2.0 ** -0.5)))
    return cdf + x * jnp.exp(-0.5 * x * x) * ((2.0 * math.pi) ** -0.5)


MM_ONE_DOT = 4096


def _contract_tile(dim, divisible_by, pref_when_split):
    return dim if dim <= MM_ONE_DOT and divisible_by % dim == 0 else _pick(divisible_by, pref_when_split)


def _mm_accumulate(step, nsteps, prod, o_ref, acc, init=None):
    if nsteps == 1:
        r = prod()
        if init is not None:
            r = r + init()
        o_ref[...] = r.astype(o_ref.dtype)
        return

    @pl.when(step == 0)
    def _():
        acc[...] = jnp.zeros_like(acc) if init is None else init()

    acc[...] += prod()

    @pl.when(step == nsteps - 1)
    def _():
        o_ref[...] = acc[...].astype(o_ref.dtype)


def _mm_nn(a, w, np_out, name, out_dtype=F32):
    M, K = a.shape
    J, _, n = w.shape
    N = J * n
    W = N // np_out
    tm, tn = _pick(M, 1024), _pick(math.gcd(W, n), 1024)
    tk = _contract_tile(K, K, 2048)
    nk = K // tk

    def body(*refs):
        a_ref, w_ref, o_ref = refs[:3]
        _mm_accumulate(pl.program_id(2), nk, lambda: jnp.dot(a_ref[...], w_ref[...], preferred_element_type=F32),
                       o_ref, refs[-1])

    return pl.pallas_call(
        body, name=name, grid=(M // tm, N // tn, nk),
        in_specs=[pl.BlockSpec((tm, tk), lambda i, j, k: (i, k)),
                  pl.BlockSpec((None, tk, tn), lambda i, j, k: (j // (n // tn), k, j % (n // tn)))],
        out_specs=pl.BlockSpec((None, tm, tn), lambda i, j, k: (j // (W // tn), i, j % (W // tn))),
        out_shape=S((np_out, M, W), out_dtype),
        scratch_shapes=[pltpu.VMEM((tm, tn), F32)] if nk > 1 else [],
        compiler_params=_cparams(("parallel", "parallel", "arbitrary")),
    )(a, w)


def _mm_nt(a3, w, name, add=None, out_dtype=F32):
    NP, M, W = a3.shape
    J, K, n = w.shape
    N = NP * W
    tm, to = _pick(M, 1024), _pick(K, 1024)
    sub = _contract_tile(N, math.gcd(W, n), 2048)
    r = max(q for q in range(1, MM_ONE_DOT // sub + 1) if n % (q * sub) == 0 and N % (q * sub) == 0)
    tc = r * sub
    nc = N // tc
    has_add = add is not None

    def body(*refs):
        a_refs, w_ref = refs[:r], refs[r]
        o_ref = refs[r + 2] if has_add else refs[r + 1]

        def prod():
            tot = None
            for q in range(r):
                part = _nt(a_refs[q][...], w_ref[:, q * sub:(q + 1) * sub])
                tot = part if tot is None else tot + part
            return tot

        _mm_accumulate(pl.program_id(2), nc, prod, o_ref, refs[-1],
                       init=(lambda: refs[r + 1][...].astype(F32)) if has_add else None)

    def a_spec(q):
        return pl.BlockSpec((None, tm, sub), lambda i, j, c: ((c * r + q) // (W // sub), i, (c * r + q) % (W // sub)))

    in_specs = [a_spec(q) for q in range(r)]
    in_specs.append(pl.BlockSpec((None, to, tc), lambda i, j, c: (c // (n // tc), j, c % (n // tc))))
    ops = [a3] * r + [w]
    if has_add:
        in_specs.append(pl.BlockSpec((tm, to), lambda i, j, c: (i, j)))
        ops.append(add)
    return pl.pallas_call(
        body, name=name, grid=(M // tm, K // to, nc),
        in_specs=in_specs,
        out_specs=pl.BlockSpec((tm, to), lambda i, j, c: (i, j)),
        out_shape=S((M, K), out_dtype),
        scratch_shapes=[pltpu.VMEM((tm, to), F32)] if nc > 1 else [],
        compiler_params=_cparams(("parallel", "parallel", "arbitrary")),
    )(*ops)


def _mm_tn(a, d3, j_out, name, out_dtype=F32):
    M, K = a.shape
    NP, _, W = d3.shape
    N = NP * W
    n = N // j_out
    to, tn = _pick(K, 1024), _pick(math.gcd(W, n), 1024)
    tmc = _contract_tile(M, M, 2048)
    nm = M // tmc

    def body(*refs):
        a_ref, d_ref, o_ref = refs[:3]
        _mm_accumulate(
            pl.program_id(2), nm,
            lambda: lax.dot_general(a_ref[...], d_ref[...], (((0,), (0,)), ((), ())), preferred_element_type=F32),
            o_ref, refs[-1])

    return pl.pallas_call(
        body, name=name, grid=(K // to, N // tn, nm),
        in_specs=[pl.BlockSpec((tmc, to), lambda i, j, m: (m, i)),
                  pl.BlockSpec((None, tmc, tn), lambda i, j, m: (j // (W // tn), m, j % (W // tn)))],
        out_specs=pl.BlockSpec((None, to, tn), lambda i, j, m: (j // (n // tn), i, j % (n // tn))),
        out_shape=S((j_out, K, n), out_dtype),
        scratch_shapes=[pltpu.VMEM((to, tn), F32)] if nm > 1 else [],
        compiler_params=_cparams(("parallel", "parallel", "arbitrary")),
    )(a, d3)


def _rms_fwd(x3, piece, col_blk, width, g, name, out_dtype, res=None):
    T = x3.shape[1]
    tr = _pick(T, 256)
    has_res = res is not None

    def body(*refs):
        x_ref, g_ref = refs[0], refs[1]
        o_ref = refs[-1]
        x = x_ref[...].astype(F32)
        y = x * lax.rsqrt(jnp.mean(x * x, axis=-1, keepdims=True) + NORM_EPS) * g_ref[...]
        if has_res:
            y = refs[2][...] + y
        o_ref[...] = y.astype(o_ref.dtype)

    in_specs = [pl.BlockSpec((None, tr, width), lambda i: (piece, i, col_blk)),
                pl.BlockSpec((1, width), lambda i: (0, 0))]
    ops = [x3, g.reshape(1, width)]
    if has_res:
        in_specs.append(pl.BlockSpec((tr, width), lambda i: (i, 0)))
        ops.append(res)
    return pl.pallas_call(
        body, name=name, grid=(T // tr,), in_specs=in_specs,
        out_specs=pl.BlockSpec((tr, width), lambda i: (i, 0)),
        out_shape=S((T, width), out_dtype),
        compiler_params=_cparams(("parallel",)),
    )(*ops)


def _rms_bwd(u3, piece, col_blk, width, g, dy, name, out_dtype, res=None):
    T = u3.shape[1]
    tr = _pick(T, 256)
    has_res = res is not None

    def body(*refs):
        u_ref, g_ref, dy_ref = refs[0], refs[1], refs[2]
        du_ref, dg_ref = refs[-2], refs[-1]
        i = pl.program_id(0)
        u = u_ref[...].astype(F32)
        dy_ = dy_ref[...].astype(F32)
        r = lax.rsqrt(jnp.mean(u * u, axis=-1, keepdims=True) + NORM_EPS)
        nrm = u * r
        gdy = g_ref[...] * dy_
        du = r * (gdy - nrm * jnp.mean(gdy * nrm, axis=-1, keepdims=True))
        if has_res:
            du = du + refs[3][...]
        du_ref[...] = du.astype(du_ref.dtype)

        @pl.when(i == 0)
        def _():
            dg_ref[...] = jnp.zeros_like(dg_ref)

        dg_ref[...] += jnp.sum(dy_ * nrm, axis=0, keepdims=True)

    in_specs = [pl.BlockSpec((None, tr, width), lambda i: (piece, i, col_blk)),
                pl.BlockSpec((1, width), lambda i: (0, 0)),
                pl.BlockSpec((tr, width), lambda i: (i, 0))]
    ops = [u3, g.reshape(1, width), dy]
    if has_res:
        in_specs.append(pl.BlockSpec((tr, width), lambda i: (i, 0)))
        ops.append(res)
    return pl.pallas_call(
        body, name=name, grid=(T // tr,), in_specs=in_specs,
        out_specs=[pl.BlockSpec((tr, width), lambda i: (i, 0)), pl.BlockSpec((1, width), lambda i: (0, 0))],
        out_shape=[S((T, width), out_dtype), S((1, width), F32)],
        compiler_params=_cparams(("arbitrary",)),
    )(*ops)


def _loss_head(xl, target, name):
    T, D = xl.shape
    tr = _pick(T, 256)

    def body(x_ref, t_ref, dx_ref, l_ref):
        i = pl.program_id(0)
        err = x_ref[...] - t_ref[...]
        dx_ref[...] = err * (1.0 / D)

        @pl.when(i == 0)
        def _():
            l_ref[...] = jnp.zeros_like(l_ref)

        l_ref[...] += jnp.sum(err * err)

    dx, l = pl.pallas_call(
        body, name=name, grid=(T // tr,),
        in_specs=[pl.BlockSpec((tr, D), lambda i: (i, 0)), pl.BlockSpec((tr, D), lambda i: (i, 0))],
        out_specs=[pl.BlockSpec((tr, D), lambda i: (i, 0)), pl.BlockSpec((8, LANES), lambda i: (0, 0))],
        out_shape=[S((T, D), F32), S((8, LANES), F32)],
        compiler_params=_cparams(("arbitrary",)),
    )(xl, target)
    return dx, l[0, 0] * (0.5 / D)


def _gate_fwd(o, z3, name):
    T, W = o.shape
    tr = _pick(T, 256)

    def body(o_ref, z_ref, g_ref):
        g_ref[...] = (o_ref[...] * _silu(z_ref[...])).astype(g_ref.dtype)

    return pl.pallas_call(
        body, name=name, grid=(T // tr,),
        in_specs=[pl.BlockSpec((tr, W), lambda i: (i, 0)), pl.BlockSpec((None, tr, W), lambda i: (0, i, 0))],
        out_specs=pl.BlockSpec((tr, W), lambda i: (i, 0)),
        out_shape=S((T, W), BF16), compiler_params=_cparams(("parallel",)),
    )(o, z3)


def _gate_bwd(dg, o, z3, name):
    T, W = o.shape
    tr = _pick(T, 256)

    def body(dg_ref, o_ref, z_ref, do_ref, dz_ref):
        dg_, z = dg_ref[...], z_ref[...]
        do_ref[...] = (dg_ * _silu(z)).astype(do_ref.dtype)
        dz_ref[...] = (dg_ * o_ref[...] * _dsilu(z)).astype(dz_ref.dtype)

    return pl.pallas_call(
        body, name=name, grid=(T // tr,),
        in_specs=[pl.BlockSpec((tr, W), lambda i: (i, 0)), pl.BlockSpec((tr, W), lambda i: (i, 0)),
                  pl.BlockSpec((None, tr, W), lambda i: (0, i, 0))],
        out_specs=[pl.BlockSpec((tr, W), lambda i: (i, 0)), pl.BlockSpec((None, tr, W), lambda i: (0, i, 0))],
        out_shape=[S((T, W), BF16), S((1, T, W), BF16)], compiler_params=_cparams(("parallel",)),
    )(dg, o, z3)


def _rope_tables(pos_col, invf, name):
    T = pos_col.shape[0]
    tr = _pick(T, 512)
    half = ROPE_DIM // 2

    def body(p_ref, f_ref, c_ref, sa_ref, sb_ref):
        ang = p_ref[...].astype(F32) * f_ref[...]
        lane = lax.broadcasted_iota(jnp.int32, ang.shape, 1)
        cs, sn = jnp.cos(ang), jnp.sin(ang)
        c_ref[...] = jnp.where(lane < ROPE_DIM, cs, 0.0)
        sa_ref[...] = jnp.where(lane < half, -sn, 0.0)
        sb_ref[...] = jnp.where((lane >= half) & (lane < ROPE_DIM), sn, 0.0)

    spec = pl.BlockSpec((tr, LANES), lambda i: (i, 0))
    return pl.pallas_call(
        body, name=name, grid=(T // tr,),
        in_specs=[pl.BlockSpec((tr, 1), lambda i: (i, 0)), pl.BlockSpec((1, LANES), lambda i: (0, 0))],
        out_specs=[spec, spec, spec], out_shape=[S((T, LANES), F32)] * 3,
        compiler_params=_cparams(("parallel",)),
    )(pos_col, invf)


def _rope(t, c, sa, sb):
    half = ROPE_DIM // 2
    return t * c + pltpu.roll(t, LANES - half, 1) * sa + pltpu.roll(t, half, 1) * sb


def _rope_t(d, c, sa, sb):
    half = ROPE_DIM // 2
    return d * c + pltpu.roll(d * sa, half, 1) + pltpu.roll(d * sb, LANES - half, 1)


def _qkv_layout(q3, kv3, pa3, kr_blk, tabs, H, name):
    T = q3.shape[1]
    tr = _pick(T, 128)

    def body(q_ref, kv_ref, kr_ref, c_ref, sa_ref, sb_ref, qf_ref, kf_ref, v_ref):
        c, sa, sb = c_ref[...], sa_ref[...], sb_ref[...]
        kr = _rope(kr_ref[...], c, sa, sb).astype(BF16)
        for h in range(H):
            nope = slice(h * HEAD_PAD, h * HEAD_PAD + NOPE_DIM)
            rest = slice(h * HEAD_PAD + NOPE_DIM, (h + 1) * HEAD_PAD)
            qf_ref[:, nope] = q_ref[:, nope].astype(BF16)
            qf_ref[:, rest] = _rope(q_ref[:, rest], c, sa, sb).astype(BF16)
            kf_ref[:, nope] = kv_ref[:, nope].astype(BF16)
            kf_ref[:, rest] = kr
            v_ref[:, h * V_DIM:(h + 1) * V_DIM] = kv_ref[:, rest].astype(BF16)

    tab = pl.BlockSpec((tr, LANES), lambda i: (i, 0))
    wide = pl.BlockSpec((None, tr, H * HEAD_PAD), lambda i: (0, i, 0))
    return pl.pallas_call(
        body, name=name, grid=(T // tr,),
        in_specs=[wide, wide, pl.BlockSpec((None, tr, LANES), lambda i: (0, i, kr_blk)), tab, tab, tab],
        out_specs=[pl.BlockSpec((tr, H * HEAD_PAD), lambda i: (i, 0)), pl.BlockSpec((tr, H * HEAD_PAD), lambda i: (i, 0)),
                   pl.BlockSpec((tr, H * V_DIM), lambda i: (i, 0))],
        out_shape=[S((T, H * HEAD_PAD), BF16), S((T, H * HEAD_PAD), BF16), S((T, H * V_DIM), BF16)],
        compiler_params=_cparams(("parallel",)),
    )(q3, kv3, pa3, *tabs)


def _qkv_layout_bwd(dqf, dkf, dv, tabs, H, name):
    T = dqf.shape[0]
    tr = _pick(T, 128)

    def body(dqf_ref, dkf_ref, dv_ref, c_ref, sa_ref, sb_ref, dq_ref, dkv_ref, dkr_ref):
        c, sa, sb = c_ref[...], sa_ref[...], sb_ref[...]
        dkr = jnp.zeros((tr, LANES), F32)
        for h in range(H):
            nope = slice(h * HEAD_PAD, h * HEAD_PAD + NOPE_DIM)
            rest = slice(h * HEAD_PAD + NOPE_DIM, (h + 1) * HEAD_PAD)
            dq_ref[:, nope] = dqf_ref[:, nope]
            dq_ref[:, rest] = _rope_t(dqf_ref[:, rest].astype(F32), c, sa, sb).astype(BF16)
            dkv_ref[:, nope] = dkf_ref[:, nope]
            dkv_ref[:, rest] = dv_ref[:, h * V_DIM:(h + 1) * V_DIM]
            dkr = dkr + dkf_ref[:, rest].astype(F32)
        dkr_ref[...] = _rope_t(dkr, c, sa, sb).astype(BF16)

    tab = pl.BlockSpec((tr, LANES), lambda i: (i, 0))
    wide_in = pl.BlockSpec((tr, H * HEAD_PAD), lambda i: (i, 0))
    wide_out = pl.BlockSpec((None, tr, H * HEAD_PAD), lambda i: (0, i, 0))
    return pl.pallas_call(
        body, name=name, grid=(T // tr,),
        in_specs=[wide_in, wide_in, pl.BlockSpec((tr, H * V_DIM), lambda i: (i, 0)), tab, tab, tab],
        out_specs=[wide_out, wide_out, pl.BlockSpec((tr, LANES), lambda i: (i, 0))],
        out_shape=[S((1, T, H * HEAD_PAD), BF16), S((1, T, H * HEAD_PAD), BF16), S((T, LANES), BF16)],
        compiler_params=_cparams(("parallel",)),
    )(dqf, dkf, dv, *tabs)


ATTN_BLOCK = 512


def _nt(a, b):
    return lax.dot_general(a, b, (((1,), (1,)), ((), ())), preferred_element_type=F32)


def _tn(a, b):
    return lax.dot_general(a, b, (((0,), (0,)), ((), ())), preferred_element_type=F32)


def _causal_blocks(qi, tb, block):
    if qi > 0:
        def step(ki, carry):
            block(pl.multiple_of(ki * tb, tb), False)
            return carry
        lax.fori_loop(0, qi, step, 0)
    block(qi * tb, True)


def _attn_fwd(qf, kf, v, H, scale, name):
    T = qf.shape[0]
    tb = _pick(T, ATTN_BLOCK)
    nb = T // tb

    def body(q_ref, k_ref, v_ref, o_ref, lse_ref):
        row = lax.broadcasted_iota(jnp.int32, (tb, tb), 0)
        col = lax.broadcasted_iota(jnp.int32, (tb, tb), 1)
        for qi in range(nb):
            rows, before = pl.ds(qi * tb, tb), qi * tb
            q = q_ref[rows, :]
            s_own = jnp.where(col <= row, _nt(q, k_ref[rows, :]), NEG)
            m = jnp.max(s_own, axis=-1, keepdims=True)
            if qi > 0:
                s_pre = _nt(q, k_ref[0:before, :])
                m = jnp.maximum(m, jnp.max(s_pre, axis=-1, keepdims=True))
            p_own = jnp.exp((s_own - m) * scale)
            l = jnp.sum(p_own, axis=-1, keepdims=True)
            acc = jnp.dot(p_own.astype(BF16), v_ref[rows, :], preferred_element_type=F32)
            if qi > 0:
                p_pre = jnp.exp((s_pre - m) * scale)
                l = l + jnp.sum(p_pre, axis=-1, keepdims=True)
                acc = acc + jnp.dot(p_pre.astype(BF16), v_ref[0:before, :], preferred_element_type=F32)
            o_ref[rows, :] = acc / l
            lse_ref[rows, :] = jnp.broadcast_to(m * scale + jnp.log(l), (tb, LANES))

    return pl.pallas_call(
        body, name=name, grid=(H,),
        in_specs=[pl.BlockSpec((T, HEAD_PAD), lambda h: (0, h)), pl.BlockSpec((T, HEAD_PAD), lambda h: (0, h)),
                  pl.BlockSpec((T, V_DIM), lambda h: (0, h))],
        out_specs=[pl.BlockSpec((T, V_DIM), lambda h: (0, h)), pl.BlockSpec((T, LANES), lambda h: (0, h))],
        out_shape=[S((T, H * V_DIM), F32), S((T, H * LANES), F32)],
        compiler_params=_cparams(("parallel",)),
    )(qf, kf, v)


def _attn_bwd(qf, kf, v, do, o, lse, H, scale, name):
    T = qf.shape[0]
    tb = _pick(T, ATTN_BLOCK)
    nb = T // tb

    def body(q_ref, k_ref, v_ref, do_ref, o_ref, lse_ref, dq_ref, dk_ref, dv_ref, dq_acc, dk_acc, dv_acc):
        row = lax.broadcasted_iota(jnp.int32, (tb, tb), 0)
        col = lax.broadcasted_iota(jnp.int32, (tb, tb), 1)
        dk_acc[...] = jnp.zeros_like(dk_acc)
        dv_acc[...] = jnp.zeros_like(dv_acc)
        for qi in range(nb):
            rows = pl.ds(qi * tb, tb)
            dq_acc[...] = jnp.zeros_like(dq_acc)
            delta = jnp.sum(do_ref[rows, :].astype(F32) * o_ref[rows, :], axis=-1, keepdims=True)
            lse_q = lse_ref[rows, 0:1]

            def block(k0, masked, rows=rows, delta=delta, lse_q=lse_q):
                keys = pl.ds(k0, tb)
                q, k, do_ = q_ref[rows, :], k_ref[keys, :], do_ref[rows, :]
                s = _nt(q, k)
                if masked:
                    s = jnp.where(col <= row, s, NEG)
                p = jnp.exp(s * scale - lse_q)
                dp = _nt(do_, v_ref[keys, :])
                ds = (p * (dp - delta) * scale).astype(BF16)
                dv_acc[keys, :] += _tn(p.astype(BF16), do_)
                dk_acc[keys, :] += _tn(ds, q)
                dq_acc[...] += jnp.dot(ds, k, preferred_element_type=F32)

            _causal_blocks(qi, tb, block)
            dq_ref[rows, :] = dq_acc[...].astype(dq_ref.dtype)
        dk_ref[...] = dk_acc[...].astype(dk_ref.dtype)
        dv_ref[...] = dv_acc[...].astype(dv_ref.dtype)

    hp = pl.BlockSpec((T, HEAD_PAD), lambda h: (0, h))
    hv = pl.BlockSpec((T, V_DIM), lambda h: (0, h))
    return pl.pallas_call(
        body, name=name, grid=(H,),
        in_specs=[hp, hp, hv, hv, hv, pl.BlockSpec((T, LANES), lambda h: (0, h))],
        out_specs=[hp, hp, hv],
        out_shape=[S((T, H * HEAD_PAD), BF16), S((T, H * HEAD_PAD), BF16), S((T, H * V_DIM), BF16)],
        scratch_shapes=[pltpu.VMEM((tb, HEAD_PAD), F32), pltpu.VMEM((T, HEAD_PAD), F32), pltpu.VMEM((T, V_DIM), F32)],
        compiler_params=_cparams(("parallel",)),
    )(qf, kf, v, do, o, lse)


CONV_ROWS = 256
CONV_COLS = 128


def _conv_chunks(T):
    rc = min(CONV_ROWS, T)
    return [(r, rc) for r in range(0, T, rc)]


def _causal_conv(pad_ref, lead, w_ref, width, r0, rc):
    acc = None
    for k in range(width):
        term = w_ref[k:k + 1, :] * pad_ref[pl.ds(lead + r0 - (width - 1) + k, rc), :]
        acc = term if acc is None else acc + term
    return acc


def _anticausal_conv(pad_ref, w_ref, width, r0, rc):
    acc = None
    for k in range(width):
        term = w_ref[k:k + 1, :] * pad_ref[pl.ds(r0 + (width - 1) - k, rc), :]
        acc = term if acc is None else acc + term
    return acc


def _conv_wgrad(dpad_ref, xpad_ref, lead, width, T, dw_ref):
    for k in range(width):
        tot = None
        for r0, rc in _conv_chunks(T):
            part = jnp.sum(dpad_ref[pl.ds(r0, rc), :] * xpad_ref[pl.ds(lead + r0 - (width - 1) + k, rc), :],
                           axis=0, keepdims=True)
            tot = part if tot is None else tot + part
        dw_ref[k:k + 1, :] = tot


def _sc_fwd(p3, wconv, name):
    _, T, W = p3.shape
    width = wconv.shape[0]
    cw = min(CONV_COLS, W)
    lead = 8

    def body(p_ref, w_ref, g_ref, pad):
        pad[0:lead, :] = jnp.zeros((lead, cw), F32)
        for r0, rc in _conv_chunks(T):
            pad[pl.ds(lead + r0, rc), :] = p_ref[1, pl.ds(r0, rc), :] * p_ref[2, pl.ds(r0, rc), :]
        for r0, rc in _conv_chunks(T):
            rows = pl.ds(r0, rc)
            y = p_ref[0, rows, :] * _causal_conv(pad, lead, w_ref, width, r0, rc)
            g_ref[rows, :] = (y * _silu(p_ref[3, rows, :])).astype(g_ref.dtype)

    return pl.pallas_call(
        body, name=name, grid=(W // cw,),
        in_specs=[pl.BlockSpec((4, T, cw), lambda j: (0, 0, j)), pl.BlockSpec((width, cw), lambda j: (0, j))],
        out_specs=pl.BlockSpec((T, cw), lambda j: (0, j)),
        out_shape=S((T, W), BF16),
        scratch_shapes=[pltpu.VMEM((T + lead, cw), F32)],
        compiler_params=_cparams(("parallel",)),
    )(p3, wconv)


def _sc_bwd(p3, wconv, dg, name):
    _, T, W = p3.shape
    width = wconv.shape[0]
    cw = min(CONV_COLS, W)
    lead = 8

    def body(p_ref, w_ref, dg_ref, dp_ref, dw_ref, cupad, dvpad):
        cupad[0:lead, :] = jnp.zeros((lead, cw), F32)
        dvpad[pl.ds(T, lead), :] = jnp.zeros((lead, cw), F32)
        for r0, rc in _conv_chunks(T):
            cupad[pl.ds(lead + r0, rc), :] = p_ref[1, pl.ds(r0, rc), :] * p_ref[2, pl.ds(r0, rc), :]
        for r0, rc in _conv_chunks(T):
            rows = pl.ds(r0, rc)
            b, z, dg_ = p_ref[0, rows, :], p_ref[3, rows, :], dg_ref[rows, :]
            v = _causal_conv(cupad, lead, w_ref, width, r0, rc)
            dy = dg_ * _silu(z)
            dp_ref[3, rows, :] = (dg_ * b * v * _dsilu(z)).astype(dp_ref.dtype)
            dp_ref[0, rows, :] = (dy * v).astype(dp_ref.dtype)
            dvpad[rows, :] = dy * b
        for r0, rc in _conv_chunks(T):
            rows = pl.ds(r0, rc)
            dcu = _anticausal_conv(dvpad, w_ref, width, r0, rc)
            dp_ref[1, rows, :] = (dcu * p_ref[2, rows, :]).astype(dp_ref.dtype)
            dp_ref[2, rows, :] = (dcu * p_ref[1, rows, :]).astype(dp_ref.dtype)
        _conv_wgrad(dvpad, cupad, lead, width, T, dw_ref)

    return pl.pallas_call(
        body, name=name, grid=(W // cw,),
        in_specs=[pl.BlockSpec((4, T, cw), lambda j: (0, 0, j)), pl.BlockSpec((width, cw), lambda j: (0, j)),
                  pl.BlockSpec((T, cw), lambda j: (0, j))],
        out_specs=[pl.BlockSpec((4, T, cw), lambda j: (0, 0, j)), pl.BlockSpec((width, cw), lambda j: (0, j))],
        out_shape=[S((4, T, W), BF16), S((width, W), F32)],
        scratch_shapes=[pltpu.VMEM((T + lead, cw), F32), pltpu.VMEM((T + lead, cw), F32)],
        compiler_params=_cparams(("parallel",)),
    )(p3, wconv, dg)


def _gm_common(p_ref, lng_ref, lnb_ref):
    ug = _gelu(p_ref[0])
    vg = _gelu(p_ref[1])
    mu = jnp.mean(vg, axis=-1, keepdims=True)
    xc = vg - mu
    rstd = lax.rsqrt(jnp.mean(xc * xc, axis=-1, keepdims=True) + LN_EPS)
    xhat = xc * rstd
    vn = xhat * lng_ref[...] + lnb_ref[...]
    return ug, xhat, rstd, vn


def _gm_mix_weights(ws_ref, g):
    row = lax.broadcasted_iota(jnp.int32, (GM_CHUNK, GM_CHUNK), 0)
    col = lax.broadcasted_iota(jnp.int32, (GM_CHUNK, GM_CHUNK), 1)
    return jnp.where(col <= row, ws_ref[g], 0.0).astype(BF16)


def _gm_fwd(p3, lng, lnb, ws, bs_t, name):
    _, T, W = p3.shape
    gw = W // GM_GROUPS

    def body(p_ref, lng_ref, lnb_ref, ws_ref, bs_ref, g_ref):
        ug, _, _, vn = _gm_common(p_ref, lng_ref, lnb_ref)
        sz = _silu(p_ref[2])
        vnb = vn.astype(BF16)
        for g in range(GM_GROUPS):
            cols = slice(g * gw, (g + 1) * gw)
            s = jnp.dot(_gm_mix_weights(ws_ref, g), vnb[:, cols], preferred_element_type=F32) + bs_ref[:, g:g + 1]
            g_ref[:, cols] = (ug[:, cols] * s * sz[:, cols]).astype(g_ref.dtype)

    return pl.pallas_call(
        body, name=name, grid=(T // GM_CHUNK,),
        in_specs=[pl.BlockSpec((3, GM_CHUNK, W), lambda i: (0, i, 0)), pl.BlockSpec((1, W), lambda i: (0, 0)),
                  pl.BlockSpec((1, W), lambda i: (0, 0)),
                  pl.BlockSpec((GM_GROUPS, GM_CHUNK, GM_CHUNK), lambda i: (0, 0, 0)),
                  pl.BlockSpec((GM_CHUNK, GM_GROUPS), lambda i: (0, 0))],
        out_specs=pl.BlockSpec((GM_CHUNK, W), lambda i: (i, 0)),
        out_shape=S((T, W), BF16), compiler_params=_cparams(("parallel",)),
    )(p3, lng.reshape(1, W), lnb.reshape(1, W), ws, bs_t)


def _gm_bwd(p3, lng, lnb, ws, bs_t, dg, name):
    _, T, W = p3.shape
    gw = W // GM_GROUPS

    def body(p_ref, lng_ref, lnb_ref, ws_ref, bs_ref, dg_ref, dp_ref, dlng_ref, dlnb_ref, dws_ref, dbs_ref, dvn_s):
        i = pl.program_id(0)

        @pl.when(i == 0)
        def _():
            dlng_ref[...] = jnp.zeros_like(dlng_ref)
            dlnb_ref[...] = jnp.zeros_like(dlnb_ref)
            dws_ref[...] = jnp.zeros_like(dws_ref)
            dbs_ref[...] = jnp.zeros_like(dbs_ref)

        ug, xhat, rstd, vn = _gm_common(p_ref, lng_ref, lnb_ref)
        z = p_ref[2]
        dg_ = dg_ref[...]
        dy = dg_ * _silu(z)
        vnb = vn.astype(BF16)
        row = lax.broadcasted_iota(jnp.int32, (GM_CHUNK, GM_CHUNK), 0)
        col = lax.broadcasted_iota(jnp.int32, (GM_CHUNK, GM_CHUNK), 1)
        dbs = jnp.zeros((GM_CHUNK, LANES), F32)
        for g in range(GM_GROUPS):
            cols = slice(g * gw, (g + 1) * gw)
            wm = _gm_mix_weights(ws_ref, g)
            s = jnp.dot(wm, vnb[:, cols], preferred_element_type=F32) + bs_ref[:, g:g + 1]
            dp_ref[2, :, cols] = (dg_[:, cols] * ug[:, cols] * s * _dsilu(z[:, cols])).astype(dp_ref.dtype)
            dp_ref[0, :, cols] = (dy[:, cols] * s * _dgelu(p_ref[0, :, cols])).astype(dp_ref.dtype)
            ds = dy[:, cols] * ug[:, cols]
            dsb = ds.astype(BF16)
            dwm = lax.dot_general(dsb, vnb[:, cols], (((1,), (1,)), ((), ())), preferred_element_type=F32)
            dws_ref[g] += jnp.where(col <= row, dwm, 0.0)
            dbs = dbs + jnp.where(col == g, jnp.sum(ds, axis=-1, keepdims=True), 0.0)
            dvn_s[:, cols] = lax.dot_general(wm, dsb, (((0,), (0,)), ((), ())), preferred_element_type=F32)
        dbs_ref[...] += dbs
        dvn = dvn_s[...]
        dlng_ref[...] += jnp.sum(dvn * xhat, axis=0, keepdims=True)
        dlnb_ref[...] += jnp.sum(dvn, axis=0, keepdims=True)
        dxh = dvn * lng_ref[...]
        dvg = rstd * (dxh - jnp.mean(dxh, axis=-1, keepdims=True) - xhat * jnp.mean(dxh * xhat, axis=-1, keepdims=True))
        dp_ref[1] = (dvg * _dgelu(p_ref[1])).astype(dp_ref.dtype)

    row1 = pl.BlockSpec((1, W), lambda i: (0, 0))
    return pl.pallas_call(
        body, name=name, grid=(T // GM_CHUNK,),
        in_specs=[pl.BlockSpec((3, GM_CHUNK, W), lambda i: (0, i, 0)), row1, row1,
                  pl.BlockSpec((GM_GROUPS, GM_CHUNK, GM_CHUNK), lambda i: (0, 0, 0)),
                  pl.BlockSpec((GM_CHUNK, GM_GROUPS), lambda i: (0, 0)),
                  pl.BlockSpec((GM_CHUNK, W), lambda i: (i, 0))],
        out_specs=[pl.BlockSpec((3, GM_CHUNK, W), lambda i: (0, i, 0)), row1, row1,
                   pl.BlockSpec((GM_GROUPS, GM_CHUNK, GM_CHUNK), lambda i: (0, 0, 0)),
                   pl.BlockSpec((GM_CHUNK, LANES), lambda i: (0, 0))],
        out_shape=[S((3, T, W), BF16), S((1, W), F32), S((1, W), F32),
                   S((GM_GROUPS, GM_CHUNK, GM_CHUNK), F32), S((GM_CHUNK, LANES), F32)],
        scratch_shapes=[pltpu.VMEM((GM_CHUNK, W), F32)],
        compiler_params=_cparams(("arbitrary",)),
    )(p3, lng.reshape(1, W), lnb.reshape(1, W), ws, bs_t, dg)


def _cf_conv_fwd(p3, wdw, bdw, name):
    _, T, W = p3.shape
    width = wdw.shape[0]
    cw = min(CONV_COLS, W)
    lead = 32

    def body(p_ref, w_ref, b_ref, y_ref, pad):
        pad[0:lead, :] = jnp.zeros((lead, cw), F32)
        for r0, rc in _conv_chunks(T):
            rows = pl.ds(r0, rc)
            pad[pl.ds(lead + r0, rc), :] = p_ref[0, rows, :] * jax.nn.sigmoid(p_ref[1, rows, :])
        for r0, rc in _conv_chunks(T):
            y_ref[pl.ds(r0, rc), :] = _causal_conv(pad, lead, w_ref, width, r0, rc) + b_ref[...]

    return pl.pallas_call(
        body, name=name, grid=(W // cw,),
        in_specs=[pl.BlockSpec((2, T, cw), lambda j: (0, 0, j)), pl.BlockSpec((width, cw), lambda j: (0, j)),
                  pl.BlockSpec((1, cw), lambda j: (0, j))],
        out_specs=pl.BlockSpec((T, cw), lambda j: (0, j)),
        out_shape=S((T, W), F32),
        scratch_shapes=[pltpu.VMEM((T + lead, cw), F32)],
        compiler_params=_cparams(("parallel",)),
    )(p3, wdw, bdw.reshape(1, W))


def _cf_ln(y1_ref, lng_ref, lnb_ref):
    y1 = y1_ref[...]
    mu = jnp.mean(y1, axis=-1, keepdims=True)
    xc = y1 - mu
    rstd = lax.rsqrt(jnp.mean(xc * xc, axis=-1, keepdims=True) + LN_EPS)
    xhat = xc * rstd
    return xhat, rstd, xhat * lng_ref[...] + lnb_ref[...]


def _cf_gate_fwd(y1, p3, lng, lnb, name):
    T, W = y1.shape
    tr = _pick(T, 256)

    def body(y1_ref, z_ref, lng_ref, lnb_ref, g_ref):
        _, _, y2 = _cf_ln(y1_ref, lng_ref, lnb_ref)
        g_ref[...] = (_silu(y2) * _silu(z_ref[...])).astype(g_ref.dtype)

    row1 = pl.BlockSpec((1, W), lambda i: (0, 0))
    return pl.pallas_call(
        body, name=name, grid=(T // tr,),
        in_specs=[pl.BlockSpec((tr, W), lambda i: (i, 0)), pl.BlockSpec((None, tr, W), lambda i: (2, i, 0)), row1, row1],
        out_specs=pl.BlockSpec((tr, W), lambda i: (i, 0)),
        out_shape=S((T, W), BF16), compiler_params=_cparams(("parallel",)),
    )(y1, p3, lng.reshape(1, W), lnb.reshape(1, W))


def _cf_gate_bwd(y1, p3, lng, lnb, dg, name):
    T, W = y1.shape
    tr = _pick(T, 128)

    def body(y1_ref, z_ref, lng_ref, lnb_ref, dg_ref, dz_ref, dy1_ref, dlng_ref, dlnb_ref):
        i = pl.program_id(0)

        @pl.when(i == 0)
        def _():
            dlng_ref[...] = jnp.zeros_like(dlng_ref)
            dlnb_ref[...] = jnp.zeros_like(dlnb_ref)

        xhat, rstd, y2 = _cf_ln(y1_ref, lng_ref, lnb_ref)
        z, dg_ = z_ref[...], dg_ref[...]
        dz_ref[...] = (dg_ * _silu(y2) * _dsilu(z)).astype(dz_ref.dtype)
        dy2 = dg_ * _silu(z) * _dsilu(y2)
        dlng_ref[...] += jnp.sum(dy2 * xhat, axis=0, keepdims=True)
        dlnb_ref[...] += jnp.sum(dy2, axis=0, keepdims=True)
        dxh = dy2 * lng_ref[...]
        dy1_ref[...] = rstd * (dxh - jnp.mean(dxh, axis=-1, keepdims=True)
                               - xhat * jnp.mean(dxh * xhat, axis=-1, keepdims=True))

    row1 = pl.BlockSpec((1, W), lambda i: (0, 0))
    blk = pl.BlockSpec((tr, W), lambda i: (i, 0))
    return pl.pallas_call(
        body, name=name, grid=(T // tr,),
        in_specs=[blk, pl.BlockSpec((None, tr, W), lambda i: (2, i, 0)), row1, row1, blk],
        out_specs=[blk, blk, row1, row1],
        out_shape=[S((T, W), BF16), S((T, W), F32), S((1, W), F32), S((1, W), F32)],
        compiler_params=_cparams(("arbitrary",)),
    )(y1, p3, lng.reshape(1, W), lnb.reshape(1, W), dg)


def _cf_conv_bwd(p3, wdw, dy1, dz, name):
    _, T, W = p3.shape
    width = wdw.shape[0]
    cw = min(CONV_COLS, W)
    lead = 32

    def body(p_ref, w_ref, dy1_ref, dz_ref, dp_ref, dw_ref, db_ref, y0pad, dpad):
        y0pad[0:lead, :] = jnp.zeros((lead, cw), F32)
        dpad[pl.ds(T, lead), :] = jnp.zeros((lead, cw), F32)
        bsum = None
        for r0, rc in _conv_chunks(T):
            rows = pl.ds(r0, rc)
            y0pad[pl.ds(lead + r0, rc), :] = p_ref[0, rows, :] * jax.nn.sigmoid(p_ref[1, rows, :])
            d = dy1_ref[rows, :]
            dpad[rows, :] = d
            part = jnp.sum(d, axis=0, keepdims=True)
            bsum = part if bsum is None else bsum + part
        db_ref[...] = bsum
        for r0, rc in _conv_chunks(T):
            rows = pl.ds(r0, rc)
            dy0 = _anticausal_conv(dpad, w_ref, width, r0, rc)
            a = p_ref[0, rows, :]
            sg = jax.nn.sigmoid(p_ref[1, rows, :])
            dp_ref[0, rows, :] = (dy0 * sg).astype(dp_ref.dtype)
            dp_ref[1, rows, :] = (dy0 * a * sg * (1.0 - sg)).astype(dp_ref.dtype)
            dp_ref[2, rows, :] = dz_ref[rows, :]
        _conv_wgrad(dpad, y0pad, lead, width, T, dw_ref)

    return pl.pallas_call(
        body, name=name, grid=(W // cw,),
        in_specs=[pl.BlockSpec((2, T, cw), lambda j: (0, 0, j)), pl.BlockSpec((width, cw), lambda j: (0, j)),
                  pl.BlockSpec((T, cw), lambda j: (0, j)), pl.BlockSpec((T, cw), lambda j: (0, j))],
        out_specs=[pl.BlockSpec((3, T, cw), lambda j: (0, 0, j)), pl.BlockSpec((width, cw), lambda j: (0, j)),
                   pl.BlockSpec((1, cw), lambda j: (0, j))],
        out_shape=[S((3, T, W), BF16), S((width, W), F32), S((1, W), F32)],
        scratch_shapes=[pltpu.VMEM((T + lead, cw), F32), pltpu.VMEM((T + lead, cw), F32)],
        compiler_params=_cparams(("parallel",)),
    )(p3, wdw, dy1, dz)


def _rows_call(body, ins, out_dtypes, name, row_pref=256):
    R, C = ins[0].shape
    tr = _pick(R, row_pref) if R % 8 == 0 else R
    while tr > 8 and tr * C * 4 * (len(ins) + len(out_dtypes)) * 2 > VMEM_LIMIT // 2 and tr % 16 == 0:
        tr //= 2
    blk = pl.BlockSpec((tr, C), lambda i: (i, 0))
    return pl.pallas_call(
        body, name=name, grid=(R // tr,), in_specs=[blk] * len(ins), out_specs=[blk] * len(out_dtypes),
        out_shape=[S((R, C), dt) for dt in out_dtypes], compiler_params=_cparams(("parallel",)),
    )(*ins)


def _pair_sum(g, r, core, name):
    J, K, n = g.shape
    kh = K // 2
    tr = _pick(kh, 256)
    nb = kh // tr

    def body(core_ref, g_ref, r_ref, o_ref):
        o_ref[...] = (g_ref[...] + r_ref[...]).astype(BF16)

    return pl.pallas_call(
        body, name=name,
        grid_spec=pltpu.PrefetchScalarGridSpec(
            num_scalar_prefetch=1, grid=(J, nb),
            in_specs=[pl.BlockSpec((None, tr, n), lambda j, i, core_ref: (j, core_ref[0] * nb + i, 0)),
                      pl.BlockSpec((None, tr, n), lambda j, i, core_ref: (j, i, 0))],
            out_specs=pl.BlockSpec((None, tr, n), lambda j, i, core_ref: (j, i, 0))),
        out_shape=S((J, kh, n), BF16),
        compiler_params=_cparams(("parallel", "parallel")),
    )(core, g, r)


def _chip_sum(rc, core, name):
    J, R, C = rc.shape
    tr = _pick(R, 256)
    nb = R // tr

    def body(core_ref, r_ref, o_ref):
        acc = r_ref[0].astype(F32)
        for j in range(1, J):
            acc = acc + r_ref[j].astype(F32)
        o_ref[...] = acc

    return pl.pallas_call(
        body, name=name,
        grid_spec=pltpu.PrefetchScalarGridSpec(
            num_scalar_prefetch=1, grid=(nb,),
            in_specs=[pl.BlockSpec((J, tr, C), lambda i, core_ref: (0, i, 0))],
            out_specs=pl.BlockSpec((tr, C), lambda i, core_ref: (core_ref[0] * nb + i, 0))),
        out_shape=S((2 * R, C), F32), compiler_params=_cparams(("parallel",)),
    )(core, rc)


def _slot_sum(slots, name):
    J, R, C = slots.shape
    tr = _pick(R, 512)

    def body(r_ref, o_ref):
        acc = r_ref[0]
        for j in range(1, J):
            acc = acc + r_ref[j]
        o_ref[...] = acc

    return pl.pallas_call(
        body, name=name, grid=(R // tr,),
        in_specs=[pl.BlockSpec((J, tr, C), lambda i: (0, i, 0))], out_specs=pl.BlockSpec((tr, C), lambda i: (i, 0)),
        out_shape=S((R, C), F32), compiler_params=_cparams(("parallel",)),
    )(slots)


def _adamw(w, g, m, v, name):
    def body(w_ref, g_ref, m_ref, v_ref, d_ref, nm_ref, nv_ref):
        g_ = g_ref[...]
        nm = ADAM_B1 * m_ref[...] + (1.0 - ADAM_B1) * g_
        nv = ADAM_B2 * v_ref[...] + (1.0 - ADAM_B2) * (g_ * g_)
        m_hat = nm / (1.0 - ADAM_B1 ** ADAM_STEP)
        v_hat = nv / (1.0 - ADAM_B2 ** ADAM_STEP)
        d_ref[...] = -ADAM_LR * (m_hat / (jnp.sqrt(v_hat) + ADAM_EPS) + ADAM_WD * w_ref[...])
        nm_ref[...] = nm
        nv_ref[...] = nv
    return _rows_call(body, [w, g, m, v], [F32, F32, F32], name)


ANY = pl.BlockSpec(memory_space=pl.ANY)


def _place():
    x, y, c = lax.axis_index("x"), lax.axis_index("y"), lax.axis_index("c")
    return x, y, c


def _other_chips(x, y):
    return [(1 - x, y), (x, 1 - y), (1 - x, 1 - y)]


HBM = pl.BlockSpec(memory_space=pltpu.HBM)
SEM = pl.BlockSpec(memory_space=pltpu.SEMAPHORE)
DATAFLOW = pltpu.SideEffectType.DATAFLOW_SIDE_EFFECTING


def _in_hbm(a):
    return pltpu.with_memory_space_constraint(a, pltpu.HBM)


def _half_rows(k_rows, which):
    return pl.ds(which * (k_rows // 2), k_rows // 2)


COPIES_PER_ARRAY = {"gather": 3, "scatter": 3, "halves": 1, "forward": 3, "join": 1, "everyone": N_DEV - 1}


def _split_copies(srcs, lands, send_sems, recv_sems, mode):
    x, y, c = _place()
    me_chip = 2 * x + y
    out = []
    for w, (src, land) in enumerate(zip(srcs, lands)):
        if mode == "join":
            mine, theirs = _half_rows(land.shape[0], c), _half_rows(land.shape[0], 1 - c)
            sems = dict(send_sem=send_sems.at[w], recv_sem=recv_sems.at[w], device_id=(x, y, 1 - c), device_id_type=MESH_ID)
            out.append((pltpu.make_async_remote_copy(src_ref=land.at[mine], dst_ref=land.at[mine], **sems),
                        pltpu.make_async_remote_copy(src_ref=land.at[mine], dst_ref=land.at[theirs], **sems)))
            continue
        if mode == "everyone":
            for k in range(1, N_DEV):
                px, py, pc = x ^ ((k >> 2) & 1), y ^ ((k >> 1) & 1), c ^ (k & 1)
                sems = dict(send_sem=send_sems.at[(N_DEV - 1) * w + k - 1], recv_sem=recv_sems.at[(N_DEV - 1) * w + k - 1],
                            device_id=(px, py, pc), device_id_type=MESH_ID)
                out.append((pltpu.make_async_remote_copy(src_ref=src, dst_ref=land.at[4 * x + 2 * y + c], **sems),
                            pltpu.make_async_remote_copy(src_ref=src, dst_ref=land.at[4 * px + 2 * py + pc], **sems)))
            continue
        if mode == "forward":
            mine, theirs = _half_rows(land.shape[1], c), _half_rows(land.shape[1], 1 - c)
            for j, (cx, cy) in enumerate(_other_chips(x, y)):
                them = 2 * cx + cy
                sems = dict(send_sem=send_sems.at[3 * w + j], recv_sem=recv_sems.at[3 * w + j],
                            device_id=(x, y, 1 - c), device_id_type=MESH_ID)
                out.append((pltpu.make_async_remote_copy(src_ref=land.at[them, mine], dst_ref=land.at[them, mine], **sems),
                            pltpu.make_async_remote_copy(src_ref=land.at[them, mine], dst_ref=land.at[them, theirs], **sems)))
            continue
        if mode == "halves":
            rows = _half_rows(src.shape[1], 1 - c)
            cp = pltpu.make_async_remote_copy(src_ref=src.at[:, rows, :], dst_ref=land, send_sem=send_sems.at[w],
                                              recv_sem=recv_sems.at[w], device_id=(x, y, 1 - c), device_id_type=MESH_ID)
            out.append((cp, cp))
            continue
        for j, (cx, cy) in enumerate(_other_chips(x, y)):
            them = 2 * cx + cy
            sems = dict(send_sem=send_sems.at[3 * w + j], recv_sem=recv_sems.at[3 * w + j],
                        device_id=(cx, cy, c), device_id_type=MESH_ID)
            if mode == "gather":
                rows = _half_rows(src.shape[0], c)
                go = pltpu.make_async_remote_copy(src_ref=src.at[rows], dst_ref=land.at[me_chip, rows], **sems)
                arrive = pltpu.make_async_remote_copy(src_ref=src.at[rows], dst_ref=land.at[them, rows], **sems)
            else:
                go = pltpu.make_async_remote_copy(src_ref=src.at[them], dst_ref=land.at[me_chip], **sems)
                arrive = pltpu.make_async_remote_copy(src_ref=src.at[them], dst_ref=land.at[them], **sems)
            out.append((go, arrive))
    return out


def _split_copies_start(srcs, lands, mode, name, after=None):
    nw = len(lands)
    ns = 0 if srcs is None else nw
    arrays = list(lands) if srcs is None else list(srcs) + list(lands)
    n_sems = COPIES_PER_ARRAY[mode] * nw
    n_in = ns + nw + (after is not None)

    def body(*refs):
        lnd = refs[ns:ns + nw]
        ins = refs[:ns] if ns else lnd
        send_sems, recv_sems, token = refs[n_in], refs[n_in + 1], refs[-1]
        for go, _ in _split_copies(ins, lnd, send_sems, recv_sems, mode):
            go.start()
        token[...] = jnp.zeros_like(token)

    thru = [pltpu.HBM(a.shape, a.dtype) for a in arrays]
    res = pl.pallas_call(
        body, name=name, in_specs=[HBM] * (ns + nw) + [ANY] * (after is not None),
        out_specs=[SEM, SEM] + [HBM] * (ns + nw) + [pl.BlockSpec(memory_space=pltpu.VMEM)],
        out_shape=[pltpu.SemaphoreType.DMA((n_sems,)), pltpu.SemaphoreType.DMA((n_sems,))] + thru + [S((8, LANES), F32)],
        input_output_aliases={i: 2 + i for i in range(ns + nw)},
        compiler_params=pltpu.CompilerParams(has_side_effects=DATAFLOW),
    )(*[_in_hbm(a) for a in arrays], *([after] if after is not None else []))
    return res[0], res[1], list(res[2:2 + ns]), list(res[2 + ns:2 + ns + nw]), res[-1]


def _split_copies_wait(send_sems, recv_sems, srcs, lands, after, mode, name):
    nw, ns = len(lands), len(srcs)
    arrays = list(srcs) + list(lands)
    after = list(after) if isinstance(after, (list, tuple)) else [after]

    def body(*refs):
        lnd = refs[ns:ns + nw]
        ins = refs[:ns] if ns else lnd
        send, recv = refs[ns + nw], refs[ns + nw + 1]
        for _, arrive in _split_copies(ins, lnd, send, recv, mode):
            arrive.wait_send()
            arrive.wait_recv()

    res = pl.pallas_call(
        body, name=name, in_specs=[HBM] * (ns + nw) + [SEM, SEM] + [ANY] * len(after), out_specs=[HBM] * (ns + nw),
        out_shape=[pltpu.HBM(a.shape, a.dtype) for a in arrays],
        input_output_aliases={i: i for i in range(ns + nw)},
        compiler_params=pltpu.CompilerParams(has_side_effects=DATAFLOW),
    )(*arrays, send_sems, recv_sems, *after)
    return list(res[:ns]), list(res[ns:])


def _gather_forward(lands, name):
    nw = len(lands)

    def body(*refs):
        ins, outs = refs[:nw], refs[nw:2 * nw]
        send_sems, recv_sems = refs[2 * nw:]
        x, y, c = _place()
        sibling = (x, y, 1 - c)
        cps = []
        for w in range(nw):
            kr = ins[w].shape[1]
            for j, (cx, cy) in enumerate(_other_chips(x, y)):
                them = 2 * cx + cy
                sems = dict(send_sem=send_sems.at[3 * w + j], recv_sem=recv_sems.at[3 * w + j],
                            device_id=sibling, device_id_type=MESH_ID)
                mine, theirs = _half_rows(kr, c), _half_rows(kr, 1 - c)
                go = pltpu.make_async_remote_copy(src_ref=ins[w].at[them, mine], dst_ref=outs[w].at[them, mine], **sems)
                go.start()
                cps.append((go, pltpu.make_async_remote_copy(
                    src_ref=ins[w].at[them, theirs], dst_ref=outs[w].at[them, theirs], **sems)))
        for go, arrive in cps:
            arrive.wait_recv()
            go.wait_send()

    return pl.pallas_call(
        body, name=name, in_specs=[ANY] * nw, out_specs=[ANY] * nw,
        out_shape=[S(a.shape, a.dtype) for a in lands],
        scratch_shapes=[pltpu.SemaphoreType.DMA((3 * nw,)), pltpu.SemaphoreType.DMA((3 * nw,))],
        input_output_aliases={i: i for i in range(nw)},
        compiler_params=pltpu.CompilerParams(has_side_effects=True),
    )(*lands)


def _broadcast_all(buf, name):
    def body(in_ref, out_ref, send_sems, recv_sems, loc_sem):
        x, y, c = _place()
        me = 4 * x + 2 * y + c
        loc = pltpu.make_async_copy(in_ref, out_ref.at[me], loc_sem)
        loc.start()
        cps = []
        for k in range(1, N_DEV):
            fx, fy, fc = (k >> 2) & 1, (k >> 1) & 1, k & 1
            px, py, pc = x ^ fx, y ^ fy, c ^ fc
            cp = pltpu.make_async_remote_copy(
                src_ref=in_ref, dst_ref=out_ref.at[me], send_sem=send_sems.at[k - 1], recv_sem=recv_sems.at[k - 1],
                device_id=(px, py, pc), device_id_type=MESH_ID)
            cp.start()
            cps.append(cp)
        for k in range(1, N_DEV):
            fx, fy, fc = (k >> 2) & 1, (k >> 1) & 1, k & 1
            px, py, pc = x ^ fx, y ^ fy, c ^ fc
            slot = out_ref.at[4 * px + 2 * py + pc]
            pltpu.make_async_remote_copy(
                src_ref=slot, dst_ref=slot, send_sem=send_sems.at[k - 1], recv_sem=recv_sems.at[k - 1],
                device_id=(px, py, pc), device_id_type=MESH_ID).wait_recv()
        for cp in cps:
            cp.wait_send()
        loc.wait()

    return pl.pallas_call(
        body, name=name, in_specs=[ANY], out_specs=ANY,
        out_shape=S((N_DEV,) + buf.shape, buf.dtype),
        scratch_shapes=[pltpu.SemaphoreType.DMA((N_DEV - 1,)), pltpu.SemaphoreType.DMA((N_DEV - 1,)),
                        pltpu.SemaphoreType.DMA],
        compiler_params=pltpu.CompilerParams(has_side_effects=True),
    )(buf)


PACK_ALIGN = 8 * LANES


def _pack(arrs):
    flat = []
    for a in arrs:
        f = a.reshape(-1).astype(F32)
        pad = (-f.shape[0]) % PACK_ALIGN
        flat.append(jnp.pad(f, (0, pad)) if pad else f)
    return jnp.concatenate(flat).reshape(-1, LANES)


def _unpack(buf, shapes):
    out, off = [], 0
    flat = buf.reshape(-1)
    for shp in shapes:
        n = math.prod(shp)
        out.append(flat[off:off + n].reshape(shp))
        off += n + ((-n) % PACK_ALIGN)
    return out


def kernel(x, positions, norm_pre, norm_post, w_in_mla, mla_q_norm, w_uq, mla_kv_norm, w_ukv, w_out_mla, w_in_sc, sc_conv, w_out_sc, w_in_gm, gm_ln_g, gm_ln_b, gm_w_s, gm_b_s, w_out_gm, w_in_cf, cf_dw, cf_dw_b, cf_ln_g, cf_ln_b, w_out_cf, loss_target, m_norm_pre, m_norm_post, m_w_in_mla, m_mla_q_norm, m_w_uq, m_mla_kv_norm, m_w_ukv, m_w_out_mla, m_w_in_sc, m_sc_conv, m_w_out_sc, m_w_in_gm, m_gm_ln_g, m_gm_ln_b, m_gm_w_s, m_gm_b_s, m_w_out_gm, m_w_in_cf, m_cf_dw, m_cf_dw_b, m_cf_ln_g, m_cf_ln_b, m_w_out_cf, v_norm_pre, v_norm_post, v_w_in_mla, v_mla_q_norm, v_w_uq, v_mla_kv_norm, v_w_ukv, v_w_out_mla, v_w_in_sc, v_sc_conv, v_w_out_sc, v_w_in_gm, v_gm_ln_g, v_gm_ln_b, v_gm_w_s, v_gm_b_s, v_w_out_gm, v_w_in_cf, v_cf_dw, v_cf_dw_b, v_cf_ln_g, v_cf_ln_b, v_w_out_cf):
    loc = dict(locals())
    wts = {n: loc[n] for n in WEIGHTS}
    mom_m = {n: loc["m_" + n] for n in WEIGHTS}
    mom_v = {n: loc["v_" + n] for n in WEIGHTS}

    T, D = x.shape[1], x.shape[2]
    xin = x.reshape(T, D)
    target = loss_target.reshape(T, D)
    q_rank, kv_rank = mla_q_norm.shape[1], mla_kv_norm.shape[1]
    H = (w_uq.shape[2] * N_CHIPS) // (NOPE_DIM + ROPE_DIM)
    hv = H * V_DIM
    c_kr = q_rank + kv_rank
    wa_cols = c_kr + ROPE_DIM
    wa_pad = wa_cols + (LANES - ROPE_DIM)
    chip = 2 * lax.axis_index("x") + lax.axis_index("y")

    c = lax.axis_index("c")
    core = c.reshape(1).astype(jnp.int32)
    small_sh_shapes = [wts[n][0].shape for n in SMALL_SHARDED]
    small_slots = _broadcast_all(_pack([wts[n][0] for n in SMALL_SHARDED]), "gather_small")
    gather_started = []
    for gi, names in enumerate(GROUPS):
        own = [wts[n][0].astype(BF16) for n in names]
        lands = [lax.dynamic_update_slice(lax.empty((N_CHIPS,) + s.shape, BF16), s[None], (chip, 0, 0)) for s in own]
        gather_started.append(_split_copies_start(own, lands, "gather", f"gather_start_{gi}",
                                                  after=gather_started[-1][4] if gather_started else small_slots))
    started_token = sum(st[4][0, 0] for st in gather_started)
    gw = {}

    forward_started = {}

    def forward_ahead(gi, after):
        send_sems, recv_sems, own, lands, _ = gather_started[gi]
        _, landed = _split_copies_wait(send_sems, recv_sems, own, lands, after, "gather", f"gather_wait_{gi}")
        forward_started[gi] = _split_copies_start(None, landed, "forward", f"gather_forward_start_{gi}")
        return forward_started[gi][4][0, 0]

    def gathered_weights(gi, after):
        if gi in forward_started:
            send_sems, recv_sems, _, lands, _ = forward_started[gi]
            _, full = _split_copies_wait(send_sems, recv_sems, [], lands, after, "forward", f"gather_forward_wait_{gi}")
        else:
            send_sems, recv_sems, own, lands, _ = gather_started[gi]
            _, landed = _split_copies_wait(send_sems, recv_sems, own, lands, after, "gather", f"gather_wait_{gi}")
            full = _gather_forward(landed, f"gather_forward_{gi}")
        gw.update(zip(GROUPS[gi], full))

    per_chip = [_unpack(small_slots[2 * k], small_sh_shapes) for k in range(N_CHIPS)]
    sp = {n: jnp.concatenate([per_chip[k][i] for k in range(N_CHIPS)], axis=-1) for i, n in enumerate(SMALL_SHARDED)}

    def cols_major(w4):
        return jnp.transpose(w4, (1, 0, 2)).reshape(w4.shape[1], -1)

    half = ROPE_DIM // 2
    inv_freq = ROPE_THETA ** (-jnp.arange(half, dtype=F32) / half)
    invf = jnp.concatenate([inv_freq, inv_freq, jnp.zeros((LANES - ROPE_DIM,), F32)]).reshape(1, LANES)
    tabs = _rope_tables(positions.reshape(T, 1), invf, "rope_tables")
    scale = float(NOPE_DIM + ROPE_DIM) ** -0.5

    xs = [xin]
    saved = []
    w_out = {}
    for i in range(4):
        xi = xs[-1]
        h = _rms_fwd(xi[None], 0, 0, D, norm_pre[i] + started_token if i == 0 else norm_pre[i], f"pre_norm_{i}", BF16)
        gathered_weights(LAYER_GROUPS[i][0], h)
        ahead_token = 0.0
        if i == 0:
            w_in_full = cols_major(gw['w_in_mla'])
            w_a = jnp.pad(w_in_full[:, :wa_cols], ((0, 0), (0, wa_pad - wa_cols)))[None]
            w_z = w_in_full[:, wa_cols:][None]
            pa = _mm_nn(h, w_a, 1, "mla_in_a")
            pz = _mm_nn(h, w_z, 1, "mla_in_z")
            gathered_weights(LAYER_GROUPS[i][1], pz)
            wq = cols_major(gw['w_uq']).reshape(q_rank, H, NOPE_DIM + ROPE_DIM)
            wq = jnp.pad(wq, ((0, 0), (0, 0), (0, HEAD_PAD - NOPE_DIM - ROPE_DIM))).reshape(1, q_rank, H * HEAD_PAD)
            wkv = cols_major(gw['w_ukv'])[None]
            qn = _rms_fwd(pa, 0, 0, q_rank, mla_q_norm[0], "mla_q_norm", BF16)
            kvn = _rms_fwd(pa, 0, q_rank // kv_rank, kv_rank, mla_kv_norm[0], "mla_kv_norm", BF16)
            q3 = _mm_nn(qn, wq, 1, "mla_q_up")
            kv3 = _mm_nn(kvn, wkv, 1, "mla_kv_up")
            qf, kf, vv = _qkv_layout(q3, kv3, pa, c_kr // LANES, tabs, H, "mla_qkv_layout")
            o, lse = _attn_fwd(qf, kf, vv, H, scale, "mla_attn_fwd")
            ahead_token = forward_ahead(LAYER_GROUPS[1][0], o)
            g = _gate_fwd(o, pz, "mla_gate_fwd")
            saved.append(dict(h=h, pa=pa, pz=pz, qn=qn, kvn=kvn, qf=qf, kf=kf, vv=vv, o=o, lse=lse, g=g))
        elif i == 1:
            p3 = _mm_nn(h, gw['w_in_sc'], 4, "sc_in")
            ahead_token = forward_ahead(LAYER_GROUPS[2][0], p3)
            g = _sc_fwd(p3, sp['sc_conv'], "sc_mix_fwd")
            saved.append(dict(h=h, p3=p3, g=g))
        elif i == 2:
            p3 = _mm_nn(h, gw['w_in_gm'], 3, "gm_in")
            ahead_token = forward_ahead(LAYER_GROUPS[3][0], p3)
            bs_t = jnp.transpose(gm_b_s[0])
            g = _gm_fwd(p3, sp['gm_ln_g'], sp['gm_ln_b'], gm_w_s[0], bs_t, "gm_mix_fwd")
            saved.append(dict(h=h, p3=p3, g=g, bs_t=bs_t))
        else:
            p3 = _mm_nn(h, gw['w_in_cf'], 3, "cf_in")
            y1 = _cf_conv_fwd(p3, sp['cf_dw'], sp['cf_dw_b'], "cf_conv_fwd")
            g = _cf_gate_fwd(y1, p3, sp['cf_ln_g'], sp['cf_ln_b'], "cf_gate_fwd")
            saved.append(dict(h=h, p3=p3, y1=y1, g=g))
        w_out[WO_NAMES[i]] = gw[WO_NAMES[i]].reshape(1, -1, D)
        yo = _mm_nn(g, w_out[WO_NAMES[i]], 1, f"out_proj_{i}")
        saved[-1]['yo'] = yo
        xs.append(_rms_fwd(yo, 0, 0, D, norm_post[i] + ahead_token, f"post_norm_{i}", F32, res=xi))

    dx, loss_local = _loss_head(xs[4], target, "loss_head")
    loss = lax.psum(loss_local, ("x", "y", "c"))

    big_grads = {}
    sgrad = {}
    d_npre, d_npost = [None] * 4, [None] * 4
    exch_started, scatter_started, scattered = {}, {}, {}

    def start_exchange(gi):
        full = [big_grads[n] for n in GROUPS[gi]]
        lands = [lax.empty((g_.shape[0], g_.shape[1] // 2, g_.shape[2]), g_.dtype) for g_ in full]
        exch_started[gi] = _split_copies_start(full, lands, "halves", f"grads_exchange_start_{gi}")
        return exch_started[gi][4][0, 0]

    def start_scatter(gi, after):
        send_sems, recv_sems, full, lands, _ = exch_started[gi]
        full, recv = _split_copies_wait(send_sems, recv_sems, full, lands, after, "halves", f"grads_exchange_wait_{gi}")
        pair = [_pair_sum(g_, r, core, f"pair_sum_{n}") for n, g_, r in zip(GROUPS[gi], full, recv)]
        lands = [lax.dynamic_update_slice(lax.empty(p.shape, p.dtype), lax.dynamic_slice_in_dim(p, chip, 1, axis=0),
                                          (chip, 0, 0)) for p in pair]
        scatter_started[gi] = _split_copies_start(pair, lands, "scatter", f"scatter_start_{gi}")
        return scatter_started[gi][4][0, 0]

    def finish_scatter(gi, after):
        send_sems, recv_sems, pair, lands, _ = scatter_started[gi]
        scattered[gi] = _split_copies_wait(send_sems, recv_sems, pair, lands, after, "scatter", f"scatter_wait_{gi}")[1]

    token = 0.0
    for i in (3, 2, 1, 0):
        sv = saved[i]
        h = sv['h']
        dyo, d_npost[i] = _rms_bwd(sv['yo'], 0, 0, D, norm_post[i] + token, dx, f"post_norm_bwd_{i}", BF16)
        dyo3 = dyo[None]
        wo_name = WO_NAMES[i]
        dg = _mm_nt(dyo3, w_out[wo_name], f"out_proj_dx_{i}")
        big_grads[wo_name] = _mm_tn(sv['g'], dyo3, 1, f"out_proj_dw_{i}").reshape(N_CHIPS, -1, D)
        if i == 3:
            dz, dy1, sgrad['cf_ln_g'], sgrad['cf_ln_b'] = _cf_gate_bwd(sv['y1'], sv['p3'], sp['cf_ln_g'], sp['cf_ln_b'], dg, "cf_gate_bwd")
            dp3, sgrad['cf_dw'], sgrad['cf_dw_b'] = _cf_conv_bwd(sv['p3'], sp['cf_dw'], dy1, dz, "cf_conv_bwd")
            big_grads['w_in_cf'] = _mm_tn(h, dp3, N_CHIPS, "cf_in_dw")
            dh = _mm_nt(dp3, gw['w_in_cf'], "cf_in_dx")
        elif i == 2:
            dp3, sgrad['gm_ln_g'], sgrad['gm_ln_b'], sgrad['gm_w_s'], dbs_t = _gm_bwd(
                sv['p3'], sp['gm_ln_g'], sp['gm_ln_b'], gm_w_s[0], sv['bs_t'], dg, "gm_mix_bwd")
            sgrad['gm_b_s'] = jnp.transpose(dbs_t[:, :GM_GROUPS])
            big_grads['w_in_gm'] = _mm_tn(h, dp3, N_CHIPS, "gm_in_dw")
            dh = _mm_nt(dp3, gw['w_in_gm'], "gm_in_dx")
        elif i == 1:
            dp3, sgrad['sc_conv'] = _sc_bwd(sv['p3'], sp['sc_conv'], dg, "sc_mix_bwd")
            big_grads['w_in_sc'] = _mm_tn(h, dp3, N_CHIPS, "sc_in_dw")
            dh = _mm_nt(dp3, gw['w_in_sc'], "sc_in_dx")
        else:
            do, dpz = _gate_bwd(dg, sv['o'], sv['pz'], "mla_gate_bwd")
            dqf, dkf, dv = _attn_bwd(sv['qf'], sv['kf'], sv['vv'], do, sv['o'], sv['lse'], H, scale, "mla_attn_bwd")
            finish_scatter(3, dqf)
            token = start_scatter(2, dqf)
            dq3, dkv3, dkr = _qkv_layout_bwd(dqf, dkf, dv, (tabs[0] + token, tabs[1], tabs[2]), H, "mla_qkv_layout_bwd")
            dqn = _mm_nt(dq3, wq, "mla_q_up_dx")
            dwq = _mm_tn(sv['qn'], dq3, 1, "mla_q_up_dw")
            dkvn = _mm_nt(dkv3, wkv, "mla_kv_up_dx")
            dwkv = _mm_tn(sv['kvn'], dkv3, 1, "mla_kv_up_dw")
            dwq_ = dwq[0].reshape(q_rank, H, HEAD_PAD)[:, :, :NOPE_DIM + ROPE_DIM].reshape(q_rank, N_CHIPS, -1)
            big_grads['w_uq'] = jnp.transpose(dwq_, (1, 0, 2))
            big_grads['w_ukv'] = jnp.transpose(dwkv[0].reshape(kv_rank, N_CHIPS, -1), (1, 0, 2))
            token = start_exchange(1)
            dcq, dqg = _rms_bwd(sv['pa'], 0, 0, q_rank, mla_q_norm[0] + token, dqn, "mla_q_norm_bwd", BF16)
            dckv, dkvg = _rms_bwd(sv['pa'], 0, q_rank // kv_rank, kv_rank, mla_kv_norm[0], dkvn, "mla_kv_norm_bwd", BF16)
            sgrad['mla_q_norm'], sgrad['mla_kv_norm'] = dqg, dkvg
            dpa = jnp.concatenate([dcq, dckv, dkr], axis=1)[None]
            dwa = _mm_tn(h, dpa, 1, "mla_in_a_dw")
            dwz = _mm_tn(h, dpz, 1, "mla_in_z_dw")
            dh_a = _mm_nt(dpa, w_a, "mla_in_a_dx")
            dh = _mm_nt(dpz, w_z, "mla_in_z_dx", add=dh_a)
            dw_in = jnp.concatenate([dwa[0][:, :wa_cols], dwz[0]], axis=1)
            big_grads['w_in_mla'] = jnp.transpose(dw_in.reshape(D, N_CHIPS, -1), (1, 0, 2))
            token = start_scatter(1, dh) + start_exchange(0)
        dx, d_npre[i] = _rms_bwd(xs[i][None], 0, 0, D, norm_pre[i] + token if i == 0 else norm_pre[i], dh,
                                 f"pre_norm_bwd_{i}", F32, res=dx)
        if i == 3:
            token = start_exchange(4)
        elif i == 2:
            token = start_scatter(4, dx) + start_exchange(3)
        elif i == 1:
            finish_scatter(4, dx)
            token = start_scatter(3, dx) + start_exchange(2)
    grad_x = dx.reshape(1, T, D)
    sgrad['norm_pre'] = jnp.concatenate(d_npre, axis=0)
    sgrad['norm_post'] = jnp.concatenate(d_npost, axis=0)

    grads, delta, new_m, new_v = {}, {}, {}, {}

    join_started = {}

    def sum_group(gi):
        halves = [_chip_sum(r, core, f"chip_sum_{n}") for n, r in zip(GROUPS[gi], scattered[gi])]
        join_started[gi] = _split_copies_start(None, halves, "join", f"grads_join_start_{gi}")

    def update_group(gi, after):
        send_sems, recv_sems, _, halves, _ = join_started[gi]
        _, full = _split_copies_wait(send_sems, recv_sems, [], halves, after, "join", f"grads_join_wait_{gi}")
        all_done = []
        for n, j in zip(GROUPS[gi], full):
            shp = wts[n].shape
            two_d = (shp[1], shp[2])
            d_, m_, v_ = _adamw(wts[n].reshape(two_d), j, mom_m[n].reshape(two_d), mom_v[n].reshape(two_d), f"adamw_{n}")
            grads[n], delta[n], new_m[n], new_v[n] = j[None], d_.reshape(shp), m_.reshape(shp), v_.reshape(shp)
            all_done.append(d_)
        return all_done

    small_full_shapes = [sgrad[n].reshape(wts[n].shape[:-1] + (-1,)).shape for n in SMALL]
    packed = _pack([sgrad[n] for n in SMALL])
    slots = lax.dynamic_update_slice(lax.empty((N_DEV,) + packed.shape, F32), packed[None], (2 * chip + c, 0, 0))
    small_started = _split_copies_start([packed], [slots], "everyone", "grads_small_start")
    finish_scatter(2, dx)
    start_scatter(0, dx)
    done = [dx, scatter_started[0][4]]
    for gi in (4, 3, 2):
        sum_group(gi)
    for gi in (4, 3, 2):
        done = update_group(gi, done)
    finish_scatter(1, done)
    finish_scatter(0, done)
    sum_group(1)
    sum_group(0)
    done = update_group(1, done)
    done = update_group(0, done)
    _, (gslots,) = _split_copies_wait(*small_started[:4], done, "everyone", "grads_small_wait")
    gsum = _unpack(_slot_sum(gslots, "grads_small_sum"), small_full_shapes)
    for n, gs in zip(SMALL, gsum):
        if n in SMALL_SHARDED:
            per = wts[n].shape[-1]
            gs = lax.dynamic_slice_in_dim(gs, chip * per, per, axis=gs.ndim - 1)
        grads[n] = gs.reshape(wts[n].shape)

    shapes = [wts[n].shape for n in SMALL]
    d_, m_, v_ = _adamw(_pack([wts[n] for n in SMALL]), _pack([grads[n] for n in SMALL]),
                        _pack([mom_m[n] for n in SMALL]), _pack([mom_v[n] for n in SMALL]), "adamw_small")
    for n, a, b, cc in zip(SMALL, _unpack(d_, shapes), _unpack(m_, shapes), _unpack(v_, shapes)):
        delta[n], new_m[n], new_v[n] = a, b, cc

    return (loss, grad_x, *[grads[n] for n in WEIGHTS], *[delta[n] for n in WEIGHTS],
            *[new_m[n] for n in WEIGHTS], *[new_v[n] for n in WEIGHTS])
```
